```python
import math
import jax, jax.numpy as jnp
from jax import lax
import numpy as np


D_MODEL = 1024
BATCH = 2
SEQ = 8192
DEPTH = 2

GRID_W = 64
EPS = 1e-6
Q_BLOCK = 128
ROPE_THETA = 10000.0
NEG_INF = -1e30

N_GROUPS = 4
GROUP_WIDTH = D_MODEL // N_GROUPS
MIX_WIDTH = N_GROUPS * GROUP_WIDTH
D_FF = 256 * ((8 * D_MODEL // 3 + 255) // 256)

NA_HEAD_DIM = 64
NA_HEADS = GROUP_WIDTH // NA_HEAD_DIM
NA_KH = 8
NA_KW = 16
MLA_HEADS = 4
MLA_V = GROUP_WIDTH // MLA_HEADS
MLA_NOPE = 64
MLA_ROPE = 32
MLA_Q_RANK = D_MODEL // 4
MLA_KV_RANK = D_MODEL // 8
DIFF_HEADS = 4
DIFF_V = GROUP_WIDTH // DIFF_HEADS
DIFF_QK = DIFF_V // 2
GQA_HEADS = 4
GQA_KV_HEADS = 2
GQA_DIM = GROUP_WIDTH // GQA_HEADS
T5_BUCKETS = 32
T5_MAX_DIST = 128

A_IN = 3 * NA_HEADS * NA_HEAD_DIM
B_IN = MLA_Q_RANK + MLA_KV_RANK + MLA_ROPE
C_IN = 3 * DIFF_HEADS * DIFF_V
D_IN = (GQA_HEADS + 2 * GQA_KV_HEADS) * GQA_DIM
IN_WIDTH = A_IN + B_IN + C_IN + D_IN
IN_SPLITS = [A_IN, A_IN + B_IN, A_IN + B_IN + C_IN]

kernel_name = 'hybrid_parallel_heads_encoder'


def rms_norm(x, g):
    xf = x.astype(jnp.float32)
    y = xf * lax.rsqrt(jnp.mean(xf * xf, axis=-1, keepdims=True) + EPS)
    return (y * g.astype(jnp.float32)).astype(x.dtype)


def swiglu(h, w_gate, w_up, w_down):
    return (jax.nn.silu(h @ w_gate) * (h @ w_up)) @ w_down


def rope_angles(pos, dim):
    inv = jnp.exp(-math.log(ROPE_THETA) * jnp.arange(0, dim, 2, dtype=jnp.float32) / dim)
    return pos.astype(jnp.float32)[:, None] * inv[None, :]


def apply_rope(x, ang):
    half = x.shape[-1] // 2
    xf = x.astype(jnp.float32)
    x1, x2 = xf[..., :half], xf[..., half:]
    cos, sin = jnp.cos(ang), jnp.sin(ang)
    return jnp.concatenate([x1 * cos - x2 * sin, x1 * sin + x2 * cos], axis=-1).astype(x.dtype)


def axial_rope(x, ang_row, ang_col):
    half = x.shape[-1] // 2
    return jnp.concatenate([apply_rope(x[..., :half], ang_row), apply_rope(x[..., half:], ang_col)], axis=-1)


def t5_bucket(rel):
    half = T5_BUCKETS // 2
    max_exact = half // 2
    n = jnp.abs(rel)
    large = max_exact + (jnp.log(jnp.maximum(n, 1).astype(jnp.float32) / max_exact)
                         / math.log(T5_MAX_DIST / max_exact) * (half - max_exact)).astype(jnp.int32)
    large = jnp.minimum(large, half - 1)
    return jnp.where(rel > 0, half, 0) + jnp.where(n < max_exact, n, large)


def sweep_query_blocks(block_fn, *q_arrays):
    b, s = q_arrays[0].shape[:2]
    nb = s // Q_BLOCK
    blocks = tuple(jnp.swapaxes(a.reshape((b, nb, Q_BLOCK) + a.shape[2:]), 0, 1) for a in q_arrays)
    out = lax.map(lambda a: block_fn(*a), (jnp.arange(nb, dtype=jnp.int32),) + blocks)
    out = jnp.swapaxes(out, 0, 1)
    return out.reshape((b, s) + out.shape[3:])


def neighbourhood_attention(q, k, v, rpb):
    b, s, h, d = q.shape
    rows = s // GRID_W
    kh = min(NA_KH, rows)
    n_cb = GRID_W // NA_KW
    kb = 2 * NA_KW
    r = jnp.arange(rows)
    row_idx = jnp.clip(r - kh // 2, 0, rows - kh)[:, None] + jnp.arange(kh)[None, :]
    cb_start = jnp.clip(jnp.arange(n_cb) * NA_KW - NA_KW // 2, 0, GRID_W - kb)
    col_idx = cb_start[:, None] + jnp.arange(kb)[None, :]
    q_col = jnp.arange(GRID_W).reshape(n_cb, NA_KW)
    q_start = jnp.clip(q_col - NA_KW // 2, 0, GRID_W - NA_KW)
    kc = col_idx[:, None, :]
    valid = (kc >= q_start[..., None]) & (kc < q_start[..., None] + NA_KW)
    dr = row_idx - r[:, None] + NA_KH - 1
    dc = jnp.clip(kc - q_col[..., None] + NA_KW - 1, 0, 2 * NA_KW - 2)
    bias = rpb[:, dr[:, :, None, None, None], dc[None, None]]
    bias = jnp.transpose(bias, (1, 3, 4, 0, 2, 5)).astype(jnp.float32)
    qg = q.reshape(b, rows, n_cb, NA_KW, h, d)
    kg = k.reshape(b, rows, GRID_W, h, d)[:, row_idx][:, :, :, col_idx]
    vg = v.reshape(b, rows, GRID_W, h, d)[:, row_idx][:, :, :, col_idx]
    logits = jnp.einsum('brcqhd,brkcjhd->brcqhkj', qg, kg,
                        preferred_element_type=jnp.float32) * (d ** -0.5) + bias
    logits = jnp.where(valid[:, :, None, None, :], logits, NEG_INF)
    p = jax.nn.softmax(logits.reshape(logits.shape[:-2] + (kh * kb,)), axis=-1)
    p = p.reshape(logits.shape).astype(v.dtype)
    out = jnp.einsum('brcqhkj,brkcjhd->brcqhd', p, vg)
    return out.reshape(b, s, h, d)


def dense_attention(q, k, v):
    scale = q.shape[-1] ** -0.5

    def block(i, qb):
        logits = jnp.einsum('bqhd,bkhd->bhqk', qb, k, preferred_element_type=jnp.float32) * scale
        p = jax.nn.softmax(logits, axis=-1).astype(v.dtype)
        return jnp.einsum('bhqk,bkhd->bqhd', p, v)

    return sweep_query_blocks(block, q)


def gqa_attention(q, k, v):
    b, s, hq, d = q.shape
    hkv = k.shape[2]
    qg = q.reshape(b, s, hkv, hq // hkv, d)
    scale = d ** -0.5

    def block(i, qb):
        logits = jnp.einsum('bqngd,bknd->bngqk', qb, k, preferred_element_type=jnp.float32) * scale
        p = jax.nn.softmax(logits, axis=-1).astype(v.dtype)
        return jnp.einsum('bngqk,bknd->bqngd', p, v)

    return sweep_query_blocks(block, qg).reshape(b, s, hq, d)


def diff_attention(q, k, v, lam, t5_table):
    s = q.shape[1]
    scale = q.shape[-1] ** -0.5
    kpos = jnp.arange(s, dtype=jnp.int32)

    def block(i, qb):
        qpos = i * Q_BLOCK + jnp.arange(Q_BLOCK, dtype=jnp.int32)
        bias = t5_table[t5_bucket(kpos[None, :] - qpos[:, None])]
        bias = jnp.transpose(bias, (2, 0, 1)).astype(jnp.float32)
        logits = jnp.einsum('bqhmd,bkhmd->bhmqk', qb, k,
                            preferred_element_type=jnp.float32) * scale + bias[None, :, None]
        p = jax.nn.softmax(logits, axis=-1)
        w = (p[:, :, 0] - lam * p[:, :, 1]).astype(v.dtype)
        return jnp.einsum('bhqk,bkhd->bqhd', w, v)

    return sweep_query_blocks(block, q)


def neighbourhood_mixer(p, q_norm, k_norm, rpb, beta):
    b, s, _ = p.shape
    p = p.reshape(b, s, 3, NA_HEADS, NA_HEAD_DIM)
    q = rms_norm(p[:, :, 0], q_norm)
    k = rms_norm(p[:, :, 1], k_norm)
    y = neighbourhood_attention(q, k, p[:, :, 2], rpb)
    return rms_norm(y.reshape(b, s, GROUP_WIDTH), beta)


def mla_mixer(p, q_lat_norm, w_uq, kv_lat_norm, w_ukv, q_norm, k_norm, beta, ang):
    b, s, _ = p.shape
    c_q, c_kv, k_rope = jnp.split(p, [MLA_Q_RANK, MLA_Q_RANK + MLA_KV_RANK], axis=-1)
    q = (rms_norm(c_q, q_lat_norm) @ w_uq).reshape(b, s, MLA_HEADS, MLA_NOPE + MLA_ROPE)
    kv = (rms_norm(c_kv, kv_lat_norm) @ w_ukv).reshape(b, s, MLA_HEADS, MLA_NOPE + MLA_V)
    k_nope, v = kv[..., :MLA_NOPE], kv[..., MLA_NOPE:]
    k_rope = jnp.broadcast_to(k_rope[:, :, None, :], (b, s, MLA_HEADS, MLA_ROPE))
    k = jnp.concatenate([k_nope, k_rope], axis=-1)
    q = rms_norm(q, q_norm)
    k = rms_norm(k, k_norm)
    q = jnp.concatenate([q[..., :MLA_NOPE], apply_rope(q[..., MLA_NOPE:], ang)], axis=-1)
    k = jnp.concatenate([k[..., :MLA_NOPE], apply_rope(k[..., MLA_NOPE:], ang)], axis=-1)
    y = dense_attention(q, k, v)
    return rms_norm(y.reshape(b, s, GROUP_WIDTH), beta)


def diff_mixer(p, q_norm, k_norm, lam_params, subln, t5_table, lambda_init):
    b, s, _ = p.shape
    pq, pk, pv = jnp.split(p, 3, axis=-1)
    q = rms_norm(pq.reshape(b, s, DIFF_HEADS, 2, DIFF_QK), q_norm)
    k = rms_norm(pk.reshape(b, s, DIFF_HEADS, 2, DIFF_QK), k_norm)
    v = pv.reshape(b, s, DIFF_HEADS, DIFF_V)
    lp = lam_params.astype(jnp.float32)
    lam = jnp.exp(jnp.sum(lp[0] * lp[1])) - jnp.exp(jnp.sum(lp[2] * lp[3])) + lambda_init
    y = diff_attention(q, k, v, lam, t5_table)
    y = rms_norm(y, subln) * (1.0 - lambda_init)
    return y.reshape(b, s, GROUP_WIDTH)


def gqa_mixer(p, q_norm, k_norm, beta, ang_row, ang_col):
    b, s, _ = p.shape
    nq = GQA_HEADS * GQA_DIM
    nk = GQA_KV_HEADS * GQA_DIM
    pq, pk, pv = jnp.split(p, [nq, nq + nk], axis=-1)
    q = rms_norm(pq.reshape(b, s, GQA_HEADS, GQA_DIM), q_norm)
    k = rms_norm(pk.reshape(b, s, GQA_KV_HEADS, GQA_DIM), k_norm)
    v = pv.reshape(b, s, GQA_KV_HEADS, GQA_DIM)
    q = axial_rope(q, ang_row, ang_col)
    k = axial_rope(k, ang_row, ang_col)
    y = gqa_attention(q, k, v)
    return rms_norm(y.reshape(b, s, GROUP_WIDTH), beta)


def diff_lambda_init(layer):
    return 0.8 - 0.6 * math.exp(-0.3 * layer)


def setup_inputs(seed: int = 0) -> dict:
    key = jax.random.key(seed)
    ks = iter(jax.random.split(key, 64))
    L = DEPTH

    def nrm(shape, scale):
        return scale * jax.random.normal(next(ks), shape, jnp.float32)

    def gain(shape):
        return 1.0 + nrm(shape, 0.05)

    return {
        'x': nrm((BATCH, SEQ, D_MODEL), 1.0),
        'ffn1_norm': gain((L, D_MODEL)),
        'ffn1_w_gate': nrm((L, D_MODEL, D_FF), D_MODEL ** -0.5),
        'ffn1_w_up': nrm((L, D_MODEL, D_FF), D_MODEL ** -0.5),
        'ffn1_w_down': nrm((L, D_FF, D_MODEL), D_FF ** -0.5),
        'mix_norm': gain((L, D_MODEL)),
        'w_in': nrm((L, D_MODEL, IN_WIDTH), D_MODEL ** -0.5),
        'na_q_norm': gain((L, NA_HEAD_DIM)),
        'na_k_norm': gain((L, NA_HEAD_DIM)),
        'na_rpb': nrm((L, NA_HEADS, 2 * NA_KH - 1, 2 * NA_KW - 1), 0.1),
        'na_beta': gain((L, GROUP_WIDTH)),
        'mla_q_lat_norm': gain((L, MLA_Q_RANK)),
        'mla_w_uq': nrm((L, MLA_Q_RANK, MLA_HEADS * (MLA_NOPE + MLA_ROPE)), MLA_Q_RANK ** -0.5),
        'mla_kv_lat_norm': gain((L, MLA_KV_RANK)),
        'mla_w_ukv': nrm((L, MLA_KV_RANK, MLA_HEADS * (MLA_NOPE + MLA_V)), MLA_KV_RANK ** -0.5),
        'mla_q_norm': gain((L, MLA_NOPE + MLA_ROPE)),
        'mla_k_norm': gain((L, MLA_NOPE + MLA_ROPE)),
        'mla_beta': gain((L, GROUP_WIDTH)),
        'diff_q_norm': gain((L, DIFF_QK)),
        'diff_k_norm': gain((L, DIFF_QK)),
        'diff_lambda': nrm((L, 4, DIFF_QK), 0.1),
        'diff_subln': gain((L, DIFF_V)),
        'gqa_q_norm': gain((L, GQA_DIM)),
        'gqa_k_norm': gain((L, GQA_DIM)),
        'gqa_beta': gain((L, GROUP_WIDTH)),
        'w_out': nrm((L, MIX_WIDTH, D_MODEL), MIX_WIDTH ** -0.5),
        'ffn2_norm': gain((L, D_MODEL)),
        'ffn2_w_gate': nrm((L, D_MODEL, D_FF), D_MODEL ** -0.5),
        'ffn2_w_up': nrm((L, D_MODEL, D_FF), D_MODEL ** -0.5),
        'ffn2_w_down': nrm((L, D_FF, D_MODEL), D_FF ** -0.5),
        'final_norm': gain((L, D_MODEL)),
        't5_bias': nrm((T5_BUCKETS, DIFF_HEADS), 0.1),
    }


def reference(x, ffn1_norm, ffn1_w_gate, ffn1_w_up, ffn1_w_down, mix_norm, w_in,
              na_q_norm, na_k_norm, na_rpb, na_beta,
              mla_q_lat_norm, mla_w_uq, mla_kv_lat_norm, mla_w_ukv, mla_q_norm, mla_k_norm, mla_beta,
              diff_q_norm, diff_k_norm, diff_lambda, diff_subln,
              gqa_q_norm, gqa_k_norm, gqa_beta, w_out,
              ffn2_norm, ffn2_w_gate, ffn2_w_up, ffn2_w_down, final_norm, t5_bias):
    s = x.shape[1]
    t = jnp.arange(s, dtype=jnp.int32)
    ang_seq = rope_angles(t, MLA_ROPE)[None, :, None, :]
    ang_row = rope_angles(t // GRID_W, GQA_DIM // 2)[None, :, None, :]
    ang_col = rope_angles(t % GRID_W, GQA_DIM // 2)[None, :, None, :]
    for l in range(DEPTH):
        x = x + 0.5 * swiglu(rms_norm(x, ffn1_norm[l]), ffn1_w_gate[l], ffn1_w_up[l], ffn1_w_down[l])
        h = rms_norm(x, mix_norm[l])
        pa, pb, pc, pd = jnp.split(h @ w_in[l], IN_SPLITS, axis=-1)
        ya = neighbourhood_mixer(pa, na_q_norm[l], na_k_norm[l], na_rpb[l], na_beta[l])
        yb = mla_mixer(pb, mla_q_lat_norm[l], mla_w_uq[l], mla_kv_lat_norm[l], mla_w_ukv[l],
                       mla_q_norm[l], mla_k_norm[l], mla_beta[l], ang_seq)
        yc = diff_mixer(pc, diff_q_norm[l], diff_k_norm[l], diff_lambda[l], diff_subln[l],
                        t5_bias, diff_lambda_init(l))
        yd = gqa_mixer(pd, gqa_q_norm[l], gqa_k_norm[l], gqa_beta[l], ang_row, ang_col)
        x = x + jnp.concatenate([ya, yb, yc, yd], axis=-1) @ w_out[l]
        x = x + 0.5 * swiglu(rms_norm(x, ffn2_norm[l]), ffn2_w_gate[l], ffn2_w_up[l], ffn2_w_down[l])
        x = rms_norm(x, final_norm[l])
    return x
```

```python
import functools
import math

import jax
import jax.numpy as jnp
from jax import lax
from jax.experimental import pallas as pl
from jax.experimental.pallas import tpu as pltpu

F32 = jnp.float32
BF16 = jnp.bfloat16

EPS = 1e-6
NEG_BIG = -1e30
LOG2E = 1.4426950408889634
ROPE_THETA = 10000.0
GRID_W = 64
N_LAYERS = 2

HEAD_V = 64
V_ROWS = 80
GROUP_W = 256

NA_KH, NA_KW = 8, 16
NA_Q_ROWS = 4
NA_K_ROWS = 12
T5_BUCKETS = 32
T5_MAX_DIST = 128

FFN_TM = 512
FFN_TF = 256
PROJ_TM = 512
FLASH_T = 512

VMEM_LIMIT = 56 * 1024 * 1024


def _rms_rows(x, g_col, n):
    r = lax.rsqrt(jnp.sum(x * x, axis=0, keepdims=True) * (1.0 / n) + EPS)
    return (x * r) * g_col


def _rms_lanes(x, g_row):
    r = lax.rsqrt(jnp.mean(x * x, axis=-1, keepdims=True) + EPS)
    return (x * r) * g_row


def _dot(a, b):
    return jnp.dot(a, b, preferred_element_type=F32)


def _ffn_kernel(*refs, has_mix, has_final, n_f):
    x_ref, g_ref, wg_ref, wu_ref, wd_ref = refs[:5]
    rest = refs[5:]
    if has_mix:
        y_refs, wo_ref, rest = rest[:4], rest[4], rest[5:]
    if has_final:
        gf_ref, rest = rest[0], rest[1:]
    o_ref, h_scr, res_scr, acc_scr = rest
    j = pl.program_id(1)

    @pl.when(j == 0)
    def _():
        x = x_ref[...]
        if has_mix:
            for g in range(4):
                x = x + lax.dot_general(
                    y_refs[g][0], wo_ref[g * GROUP_W:(g + 1) * GROUP_W, :],
                    (((0,), (0,)), ((), ())), preferred_element_type=F32)
        res_scr[...] = x
        h_scr[...] = _rms_lanes(x, g_ref[...]).astype(BF16)
        acc_scr[...] = jnp.zeros_like(acc_scr)

    h = h_scr[...]
    gate = _dot(h, wg_ref[...])
    up = _dot(h, wu_ref[...])
    a = (gate / (1.0 + jnp.exp(-gate))) * up
    acc_scr[...] += _dot(a.astype(BF16), wd_ref[...])

    @pl.when(j == n_f - 1)
    def _():
        out = res_scr[...] + 0.5 * acc_scr[...]
        if has_final:
            out = _rms_lanes(out, gf_ref[...])
        o_ref[...] = out


def _ffn(x, g, wg, wu, wd, mix=None, final_g=None, seq=None):
    n, d = x.shape
    d_ff = wg.shape[1]
    tm, tf = FFN_TM, FFN_TF
    n_f = d_ff // tf
    in_specs = [
        pl.BlockSpec((tm, d), lambda i, j: (i, 0)),
        pl.BlockSpec((1, d), lambda i, j: (0, 0)),
        pl.BlockSpec((d, tf), lambda i, j: (0, j)),
        pl.BlockSpec((d, tf), lambda i, j: (0, j)),
        pl.BlockSpec((tf, d), lambda i, j: (j, 0)),
    ]
    args = [x, g.reshape(1, d), wg, wu, wd]
    if mix is not None:
        ys, wo = mix
        nsb = seq // tm
        for y in ys:
            in_specs.append(pl.BlockSpec((1, GROUP_W, tm), lambda i, j: (i // nsb, 0, i % nsb)))
            args.append(y)
        in_specs.append(pl.BlockSpec(wo.shape, lambda i, j: (0, 0)))
        args.append(wo)
    if final_g is not None:
        in_specs.append(pl.BlockSpec((1, d), lambda i, j: (0, 0)))
        args.append(final_g.reshape(1, d))
    return pl.pallas_call(
        functools.partial(_ffn_kernel, has_mix=mix is not None, has_final=final_g is not None, n_f=n_f),
        grid=(n // tm, n_f),
        in_specs=in_specs,
        out_specs=pl.BlockSpec((tm, d), lambda i, j: (i, 0)),
        out_shape=jax.ShapeDtypeStruct((n, d), F32),
        scratch_shapes=[pltpu.VMEM((tm, d), BF16), pltpu.VMEM((tm, d), F32), pltpu.VMEM((tm, d), F32)],
        compiler_params=pltpu.CompilerParams(
            dimension_semantics=("parallel", "arbitrary"), vmem_limit_bytes=VMEM_LIMIT),
        name="ffn_mix" if mix is not None else "ffn",
    )(*args)


def _rope_rows(x, cos, sin):
    x1, x2 = x[:16], x[16:]
    return x1 * cos - x2 * sin, x1 * sin + x2 * cos


def _store_v(v_ref, hh, v):
    v_ref[0, hh, 0:HEAD_V, :] = v.astype(BF16)
    v_ref[0, hh, HEAD_V:V_ROWS, :] = jnp.ones((V_ROWS - HEAD_V, v.shape[1]), BF16)


def _proj_kernel(x_ref, gmix_ref, wint_ref,
                 naq_ref, nak_ref,
                 qlat_ref, wuqt_ref, kvlat_ref, wukvt_ref, mq_ref, mk_ref,
                 dq_ref, dk_ref, gq_ref, gk_ref,
                 cseq_ref, sseq_ref, crow_ref, srow_ref, ccol_ref, scol_ref,
                 qa_o, ka_o, va_o, qb_o, kb_o, vb_o, qc_o, kc_o, vc_o, qd_o, kd_o, vd_o):
    h = _rms_lanes(x_ref[...], gmix_ref[...]).astype(BF16)

    def proj(lo, hi):
        return lax.dot_general(wint_ref[lo:hi, :], h, (((1,), (1,)), ((), ())),
                               preferred_element_type=F32)

    pa = proj(0, 768)
    sa = LOG2E * 64 ** -0.5
    for hh in range(4):
        q = _rms_rows(pa[hh * 64:(hh + 1) * 64], naq_ref[...], 64) * sa
        k = _rms_rows(pa[256 + hh * 64:256 + (hh + 1) * 64], nak_ref[...], 64)
        qa_o[0, hh] = q.astype(BF16)
        ka_o[0, hh] = k.astype(BF16)
        _store_v(va_o, hh, pa[512 + hh * 64:512 + (hh + 1) * 64])

    pb = proj(768, 1184)
    cq = _rms_rows(pb[0:256], qlat_ref[...], 256).astype(BF16)
    qb = _dot(wuqt_ref[...], cq)
    ckv = _rms_rows(pb[256:384], kvlat_ref[...], 128).astype(BF16)
    kv = _dot(wukvt_ref[...], ckv)
    kr = pb[384:416]
    cseq, sseq = cseq_ref[...], sseq_ref[...]
    sb = LOG2E * 96 ** -0.5
    for hh in range(4):
        q = _rms_rows(qb[hh * 96:(hh + 1) * 96], mq_ref[...], 96) * sb
        r1, r2 = _rope_rows(q[64:96], cseq, sseq)
        qb_o[0, hh, 0:64, :] = q[0:64].astype(BF16)
        qb_o[0, hh, 64:80, :] = r1.astype(BF16)
        qb_o[0, hh, 80:96, :] = r2.astype(BF16)
        k = jnp.concatenate([kv[hh * 128:hh * 128 + 64], kr], axis=0)
        k = _rms_rows(k, mk_ref[...], 96)
        r1, r2 = _rope_rows(k[64:96], cseq, sseq)
        kb_o[0, hh, 0:64, :] = k[0:64].astype(BF16)
        kb_o[0, hh, 64:80, :] = r1.astype(BF16)
        kb_o[0, hh, 80:96, :] = r2.astype(BF16)
        _store_v(vb_o, hh, kv[hh * 128 + 64:hh * 128 + 128])

    pc = proj(1184, 1952)
    sc = LOG2E * 32 ** -0.5
    for j in range(8):
        q = _rms_rows(pc[j * 32:(j + 1) * 32], dq_ref[...], 32) * sc
        k = _rms_rows(pc[256 + j * 32:256 + (j + 1) * 32], dk_ref[...], 32)
        qc_o[0, j] = q.astype(BF16)
        kc_o[0, j] = k.astype(BF16)
    for hh in range(4):
        _store_v(vc_o, hh, pc[512 + hh * 64:512 + (hh + 1) * 64])

    pd = proj(1952, 2464)
    crow, srow, ccol, scol = crow_ref[...], srow_ref[...], ccol_ref[...], scol_ref[...]
    sd = LOG2E * 64 ** -0.5

    def axial(x, o_ref, hh):
        a1, a2 = _rope_rows(x[0:32], crow, srow)
        b1, b2 = _rope_rows(x[32:64], ccol, scol)
        o_ref[0, hh, 0:16, :] = a1.astype(BF16)
        o_ref[0, hh, 16:32, :] = a2.astype(BF16)
        o_ref[0, hh, 32:48, :] = b1.astype(BF16)
        o_ref[0, hh, 48:64, :] = b2.astype(BF16)

    for hh in range(4):
        axial(_rms_rows(pd[hh * 64:(hh + 1) * 64], gq_ref[...], 64) * sd, qd_o, hh)
    for hh in range(2):
        axial(_rms_rows(pd[256 + hh * 64:256 + (hh + 1) * 64], gk_ref[...], 64), kd_o, hh)
        _store_v(vd_o, hh, pd[384 + hh * 64:384 + (hh + 1) * 64])


def _proj(x, batch, seq, gmix, wint, cols, mats, tabs):
    n, d = x.shape
    tm = PROJ_TM
    nsb = seq // tm
    naq, nak, qlat, kvlat, mq, mk, dq, dk, gq, gk = cols
    wuqt, wukvt = mats

    def full(a):
        return pl.BlockSpec(a.shape, lambda i: (0,) * a.ndim)

    tab_spec = pl.BlockSpec((16, tm), lambda i: (0, i % nsb))
    args = [x, gmix.reshape(1, d), wint, naq, nak, qlat, wuqt, kvlat, wukvt, mq, mk, dq, dk, gq, gk, *tabs]
    in_specs = [pl.BlockSpec((tm, d), lambda i: (i, 0))] + [full(a) for a in args[1:15]] + [tab_spec] * 6

    def head_out(nh, rows):
        return (jax.ShapeDtypeStruct((batch, nh, rows, seq), BF16),
                pl.BlockSpec((1, nh, rows, tm), lambda i: (i // nsb, 0, 0, i % nsb)))

    outs = [head_out(4, 64), head_out(4, 64), head_out(4, V_ROWS),
            head_out(4, 96), head_out(4, 96), head_out(4, V_ROWS),
            head_out(8, 32), head_out(8, 32), head_out(4, V_ROWS),
            head_out(4, 64), head_out(2, 64), head_out(2, V_ROWS)]
    return pl.pallas_call(
        _proj_kernel,
        grid=(n // tm,),
        in_specs=in_specs,
        out_specs=[o[1] for o in outs],
        out_shape=[o[0] for o in outs],
        compiler_params=pltpu.CompilerParams(
            dimension_semantics=("parallel",), vmem_limit_bytes=VMEM_LIMIT),
        name="mix_proj",
    )(*args)


def _online_step(s, c, k_idx, v_blk, m_scr, acc_scr):
    m_prev = m_scr[k_idx]
    m_new = jnp.maximum(m_prev, jnp.max(s, axis=0, keepdims=True) + c)
    alpha = jnp.exp2(m_prev - m_new)
    p = jnp.exp2(s - (m_new - c)).astype(BF16)
    acc_scr[k_idx] = alpha * acc_scr[k_idx] + _dot(v_blk, p)
    m_scr[k_idx] = m_new


def _finish_head(acc):
    return acc[0:HEAD_V] / acc[HEAD_V:HEAD_V + 1]


def _flash_plain_kernel(q_ref, k_ref, v_ref, beta_ref, o_ref, m_scr, acc_scr, *, kv_map, n_k):
    ki = pl.program_id(2)

    @pl.when(ki == 0)
    def _():
        m_scr[...] = jnp.full_like(m_scr, NEG_BIG)
        acc_scr[...] = jnp.zeros_like(acc_scr)

    for hh, kh in enumerate(kv_map):
        s = _dot(k_ref[0, kh], q_ref[0, hh])
        _online_step(s, 0.0, hh, v_ref[0, kh], m_scr, acc_scr)

    @pl.when(ki == n_k - 1)
    def _():
        y = jnp.concatenate([_finish_head(acc_scr[hh]) for hh in range(len(kv_map))], axis=0)
        o_ref[0] = _rms_rows(y, beta_ref[...], GROUP_W).astype(BF16)


def _flash_plain(qt, k, vt, beta_col, kv_map):
    b, hq, dq, s = qt.shape
    hk = k.shape[1]
    t = FLASH_T
    n_k = s // t
    return pl.pallas_call(
        functools.partial(_flash_plain_kernel, kv_map=kv_map, n_k=n_k),
        grid=(b, s // t, n_k),
        in_specs=[
            pl.BlockSpec((1, hq, dq, t), lambda bi, qi, ki: (bi, 0, 0, qi)),
            pl.BlockSpec((1, hk, t, dq), lambda bi, qi, ki: (bi, 0, ki, 0)),
            pl.BlockSpec((1, hk, V_ROWS, t), lambda bi, qi, ki: (bi, 0, 0, ki)),
            pl.BlockSpec((GROUP_W, 1), lambda bi, qi, ki: (0, 0)),
        ],
        out_specs=pl.BlockSpec((1, GROUP_W, t), lambda bi, qi, ki: (bi, 0, qi)),
        out_shape=jax.ShapeDtypeStruct((b, GROUP_W, s), BF16),
        scratch_shapes=[pltpu.VMEM((hq, 1, t), F32), pltpu.VMEM((hq, V_ROWS, t), F32)],
        compiler_params=pltpu.CompilerParams(
            dimension_semantics=("parallel", "parallel", "arbitrary"), vmem_limit_bytes=VMEM_LIMIT),
        name="flash_plain",
    )(qt, k, vt, beta_col)


def _flash_diff_kernel(t5_ref, q_ref, k_ref, v_ref, bias_ref, lam_ref, subln_ref, o_ref, m_scr, acc_scr,
                       *, lambda_init, n_k):
    qi = pl.program_id(1)
    ki = pl.program_id(2)

    @pl.when(ki == 0)
    def _():
        m_scr[...] = jnp.full_like(m_scr, NEG_BIG)
        acc_scr[...] = jnp.zeros_like(acc_scr)

    near = jnp.abs(ki - qi) <= 1

    @pl.when(near)
    def _():
        for hh in range(4):
            bias = bias_ref[0, hh]
            for m in range(2):
                j = hh * 2 + m
                s = _dot(k_ref[0, j], q_ref[0, j]) + bias
                _online_step(s, 0.0, j, v_ref[0, hh], m_scr, acc_scr)

    @pl.when(jnp.logical_not(near))
    def _():
        bucket = jnp.where(ki > qi, T5_BUCKETS - 1, T5_BUCKETS // 2 - 1)
        for hh in range(4):
            c = t5_ref[bucket * 4 + hh] * LOG2E
            for m in range(2):
                j = hh * 2 + m
                s = _dot(k_ref[0, j], q_ref[0, j])
                _online_step(s, c, j, v_ref[0, hh], m_scr, acc_scr)

    @pl.when(ki == n_k - 1)
    def _():
        lp = lam_ref[...]
        lam = (jnp.exp(jnp.sum(lp[0:1] * lp[1:2], axis=1, keepdims=True))
               - jnp.exp(jnp.sum(lp[2:3] * lp[3:4], axis=1, keepdims=True)) + lambda_init)
        for hh in range(4):
            y = _finish_head(acc_scr[2 * hh]) - lam * _finish_head(acc_scr[2 * hh + 1])
            y = _rms_rows(y, subln_ref[...], HEAD_V) * (1.0 - lambda_init)
            o_ref[0, hh * HEAD_V:(hh + 1) * HEAD_V, :] = y.astype(BF16)


def _flash_diff(t5_flat, qt, k, vt, bias, lam_params, subln_col, lambda_init):
    b, hq, dq, s = qt.shape
    t = FLASH_T
    n_k = s // t
    return pl.pallas_call(
        functools.partial(_flash_diff_kernel, lambda_init=lambda_init, n_k=n_k),
        grid=(b, s // t, n_k),
        in_specs=[
            pl.BlockSpec(memory_space=pltpu.SMEM),
            pl.BlockSpec((1, hq, dq, t), lambda bi, qi, ki: (bi, 0, 0, qi)),
            pl.BlockSpec((1, hq, t, dq), lambda bi, qi, ki: (bi, 0, ki, 0)),
            pl.BlockSpec((1, 4, V_ROWS, t), lambda bi, qi, ki: (bi, 0, 0, ki)),
            pl.BlockSpec((1, 4, t, t), lambda bi, qi, ki: (jnp.clip(ki - qi, -1, 1) + 1, 0, 0, 0)),
            pl.BlockSpec(lam_params.shape, lambda bi, qi, ki: (0, 0)),
            pl.BlockSpec((HEAD_V, 1), lambda bi, qi, ki: (0, 0)),
        ],
        out_specs=pl.BlockSpec((1, GROUP_W, t), lambda bi, qi, ki: (bi, 0, qi)),
        out_shape=jax.ShapeDtypeStruct((b, GROUP_W, s), BF16),
        scratch_shapes=[pltpu.VMEM((hq, 1, t), F32), pltpu.VMEM((hq, V_ROWS, t), F32)],
        compiler_params=pltpu.CompilerParams(
            dimension_semantics=("parallel", "parallel", "arbitrary"), vmem_limit_bytes=VMEM_LIMIT),
        name="flash_diff",
    )(t5_flat, qt, k, vt, bias, lam_params, subln_col)


def _t5_bias_kernel(tab_ref, o_ref, *, t):
    d = pl.program_id(0) - 1
    hh = pl.program_id(1)
    rows = 64
    half = T5_BUCKETS // 2
    max_exact = half // 2

    def body(r, carry):
        kk = lax.broadcasted_iota(jnp.int32, (rows, t), 0) + r * rows
        qq = lax.broadcasted_iota(jnp.int32, (rows, t), 1)
        rel = d * t + kk - qq
        n = jnp.abs(rel)
        large = max_exact + (jnp.log(jnp.maximum(n, 1).astype(F32) / max_exact)
                             / math.log(T5_MAX_DIST / max_exact) * (half - max_exact)).astype(jnp.int32)
        large = jnp.minimum(large, half - 1)
        bucket = jnp.where(rel > 0, half, 0) + jnp.where(n < max_exact, n, large)
        acc = jnp.zeros((rows, t), F32)
        for bkt in range(T5_BUCKETS):
            acc = jnp.where(bucket == bkt, tab_ref[bkt * 4 + hh], acc)
        o_ref[0, 0, pl.ds(pl.multiple_of(r * rows, rows), rows), :] = acc * LOG2E
        return carry

    lax.fori_loop(0, t // rows, body, 0)


def _t5_bias(t5_flat):
    t = FLASH_T
    return pl.pallas_call(
        functools.partial(_t5_bias_kernel, t=t),
        grid=(3, 4),
        in_specs=[pl.BlockSpec(memory_space=pltpu.SMEM)],
        out_specs=pl.BlockSpec((1, 1, t, t), lambda d, h: (d, h, 0, 0)),
        out_shape=jax.ShapeDtypeStruct((3, 4, t, t), F32),
        name="t5_bias",
    )(t5_flat)


def _na_bias_kernel(rpb_ref, o_ref, *, grid_rows):
    kind = pl.program_id(0)
    hh = pl.program_id(1)
    n_dr, n_dc = 2 * NA_KH - 1, 2 * NA_KW - 1
    r0 = jnp.where(kind == 0, 0, jnp.where(kind == 1, NA_Q_ROWS, grid_rows - NA_Q_ROWS))
    start = jnp.where(kind == 2, grid_rows - NA_K_ROWS, 0)
    shape = (GRID_W, NA_Q_ROWS * GRID_W)
    lane = lax.broadcasted_iota(jnp.int32, shape, 1)
    kc = lax.broadcasted_iota(jnp.int32, shape, 0)
    qc = lane & (GRID_W - 1)
    grp = lane >> 6
    dcm = jnp.clip(kc - qc + NA_KW - 1, 0, n_dc - 1)
    qs = jnp.clip(qc - NA_KW // 2, 0, GRID_W - NA_KW)
    col_ok = (kc >= qs) & (kc < qs + NA_KW)

    def by_group(vals):
        out = vals[NA_Q_ROWS - 1]
        for i in range(NA_Q_ROWS - 2, -1, -1):
            out = jnp.where(grp == i, vals[i], out)
        return out

    def jbody(j, carry):
        krow = start + j
        bases, oks = [], []
        for i in range(NA_Q_ROWS):
            qrow = r0 + i
            lo = jnp.clip(qrow - NA_KH // 2, 0, grid_rows - NA_KH)
            oks.append(((krow >= lo) & (krow < lo + NA_KH)).astype(jnp.int32))
            dr = jnp.clip(krow - qrow + NA_KH - 1, 0, n_dr - 1)
            bases.append((hh * n_dr + dr) * n_dc)

        def dcbody(dc, acc):
            vec = by_group([rpb_ref[bases[i] + dc] for i in range(NA_Q_ROWS)])
            return jnp.where(dcm == dc, vec, acc)

        acc = lax.fori_loop(0, n_dc, dcbody, jnp.zeros(shape, F32))
        ok = col_ok & (by_group(oks) > 0)
        o_ref[0, 0, pl.ds(pl.multiple_of(j * GRID_W, GRID_W), GRID_W), :] = jnp.where(ok, acc * LOG2E, NEG_BIG)
        return carry

    lax.fori_loop(0, NA_K_ROWS, jbody, 0)


def _na_bias(rpb_flat, grid_rows):
    kt, qt = NA_K_ROWS * GRID_W, NA_Q_ROWS * GRID_W
    return pl.pallas_call(
        functools.partial(_na_bias_kernel, grid_rows=grid_rows),
        grid=(3, 4),
        in_specs=[pl.BlockSpec(memory_space=pltpu.SMEM)],
        out_specs=pl.BlockSpec((1, 1, kt, qt), lambda kd, h: (kd, h, 0, 0)),
        out_shape=jax.ShapeDtypeStruct((3, 4, kt, qt), F32),
        name="na_bias",
    )(rpb_flat)


def _na_kernel(q_ref, k0_ref, k1_ref, k2_ref, v0_ref, v1_ref, v2_ref, bias_ref, beta_ref, o_ref):
    qt = NA_Q_ROWS * GRID_W
    k_refs = (k0_ref, k1_ref, k2_ref)
    v_refs = (v0_ref, v1_ref, v2_ref)
    outs = []
    for hh in range(4):
        q = q_ref[0, hh]
        ss = [_dot(k_refs[j][0, hh], q) + bias_ref[0, hh, j * qt:(j + 1) * qt, :] for j in range(3)]
        m = jnp.maximum(jnp.maximum(jnp.max(ss[0], axis=0, keepdims=True),
                                    jnp.max(ss[1], axis=0, keepdims=True)),
                        jnp.max(ss[2], axis=0, keepdims=True))
        acc = None
        for j in range(3):
            p = jnp.exp2(ss[j] - m).astype(BF16)
            pv = _dot(v_refs[j][0, hh], p)
            acc = pv if acc is None else acc + pv
        outs.append(_finish_head(acc))
    y = jnp.concatenate(outs, axis=0)
    o_ref[0] = _rms_rows(y, beta_ref[...], GROUP_W).astype(BF16)


def _na(qt_arr, k, vt, bias, beta_col):
    b, nh, d, s = qt_arr.shape
    qt = NA_Q_ROWS * GRID_W
    kt = NA_K_ROWS * GRID_W
    n_t = s // qt
    n_win = kt // qt

    def win(j):
        return lambda bi, ti: jnp.clip(ti - 1, 0, n_t - n_win) + j

    def kind(bi, ti):
        return (jnp.where(ti == 0, 0, jnp.where(ti == n_t - 1, 2, 1)), 0, 0, 0)

    k_specs = [pl.BlockSpec((1, nh, qt, d), (lambda j: lambda bi, ti: (bi, 0, win(j)(bi, ti), 0))(j))
               for j in range(n_win)]
    v_specs = [pl.BlockSpec((1, nh, V_ROWS, qt), (lambda j: lambda bi, ti: (bi, 0, 0, win(j)(bi, ti)))(j))
               for j in range(n_win)]
    return pl.pallas_call(
        _na_kernel,
        grid=(b, n_t),
        in_specs=[pl.BlockSpec((1, nh, d, qt), lambda bi, ti: (bi, 0, 0, ti))] + k_specs + v_specs + [
            pl.BlockSpec((1, nh, kt, qt), kind),
            pl.BlockSpec((GROUP_W, 1), lambda bi, ti: (0, 0)),
        ],
        out_specs=pl.BlockSpec((1, GROUP_W, qt), lambda bi, ti: (bi, 0, ti)),
        out_shape=jax.ShapeDtypeStruct((b, GROUP_W, s), BF16),
        compiler_params=pltpu.CompilerParams(
            dimension_semantics=("parallel", "parallel"), vmem_limit_bytes=VMEM_LIMIT),
        name="na_attn",
    )(qt_arr, k, k, k, vt, vt, vt, bias, beta_col)


def _rope_tables(pos, dim):
    inv = jnp.exp(-math.log(ROPE_THETA) * jnp.arange(0, dim, 2, dtype=F32) / dim)
    ang = pos.astype(F32)[:, None] * inv[None, :]
    return jnp.cos(ang).T, jnp.sin(ang).T


def _col(v):
    return v.reshape(-1, 1).astype(F32)


def _lambda_init(layer):
    return 0.8 - 0.6 * math.exp(-0.3 * layer)


def kernel(x, ffn1_norm, ffn1_w_gate, ffn1_w_up, ffn1_w_down, mix_norm, w_in, na_q_norm, na_k_norm, na_rpb, na_beta, mla_q_lat_norm, mla_w_uq, mla_kv_lat_norm, mla_w_ukv, mla_q_norm, mla_k_norm, mla_beta, diff_q_norm, diff_k_norm, diff_lambda, diff_subln, gqa_q_norm, gqa_k_norm, gqa_beta, w_out, ffn2_norm, ffn2_w_gate, ffn2_w_up, ffn2_w_down, final_norm, t5_bias):
    batch, seq, d = x.shape
    grid_rows = seq // GRID_W
    pos = jnp.arange(seq, dtype=jnp.int32)
    tabs = (*_rope_tables(pos, 32), *_rope_tables(pos // GRID_W, 32), *_rope_tables(pos % GRID_W, 32))
    t5_flat = t5_bias.reshape(-1).astype(F32)
    t5_tiles = _t5_bias(t5_flat)

    xf = x.reshape(batch * seq, d)
    for l in range(N_LAYERS):
        xf = _ffn(xf, ffn1_norm[l], ffn1_w_gate[l].astype(BF16), ffn1_w_up[l].astype(BF16),
                  ffn1_w_down[l].astype(BF16))
        cols = tuple(_col(v[l]) for v in (na_q_norm, na_k_norm, mla_q_lat_norm, mla_kv_lat_norm,
                                          mla_q_norm, mla_k_norm, diff_q_norm, diff_k_norm,
                                          gqa_q_norm, gqa_k_norm))
        mats = (mla_w_uq[l].T.astype(BF16), mla_w_ukv[l].T.astype(BF16))
        (qa, ka, va, qb, kb, vb, qc, kc, vc, qd, kd, vd) = _proj(
            xf, batch, seq, mix_norm[l], w_in[l].T.astype(BF16), cols, mats, tabs)
        ka, kb, kc, kd = (jnp.swapaxes(k, 2, 3) for k in (ka, kb, kc, kd))

        ya = _na(qa, ka, va, _na_bias(na_rpb[l].reshape(-1).astype(F32), grid_rows), _col(na_beta[l]))
        yb = _flash_plain(qb, kb, vb, _col(mla_beta[l]), (0, 1, 2, 3))
        yc = _flash_diff(t5_flat, qc, kc, vc, t5_tiles, diff_lambda[l].astype(F32), _col(diff_subln[l]),
                         _lambda_init(l))
        yd = _flash_plain(qd, kd, vd, _col(gqa_beta[l]), (0, 0, 1, 1))

        xf = _ffn(xf, ffn2_norm[l], ffn2_w_gate[l].astype(BF16), ffn2_w_up[l].astype(BF16),
                  ffn2_w_down[l].astype(BF16), mix=((ya, yb, yc, yd), w_out[l].astype(BF16)),
                  final_g=final_norm[l], seq=seq)
    return xf.reshape(batch, seq, d)
```

```python
import functools
import math

import jax
import jax.numpy as jnp
from jax import lax
from jax.experimental import pallas as pl
from jax.experimental.pallas import tpu as pltpu

F32 = jnp.float32
BF16 = jnp.bfloat16

EPS = 1e-6
NEG_BIG = -1e30
LOG2E = 1.4426950408889634
ROPE_THETA = 10000.0
GRID_W = 64
N_LAYERS = 2

HEAD_V = 64
V_ROWS = 80
GROUP_W = 256

NA_KH, NA_KW = 8, 16
NA_Q_ROWS = 4
NA_K_ROWS = 12
T5_BUCKETS = 32
T5_MAX_DIST = 128

FFN_TM = 512
FFN_TF = 256
FLASH_T = 512
PROJ_TM = FLASH_T

VMEM_LIMIT = 56 * 1024 * 1024


def _rms_rows(x, g_col, n):
    r = lax.rsqrt(jnp.sum(x * x, axis=0, keepdims=True) * (1.0 / n) + EPS)
    return (x * r) * g_col


def _rms_lanes(x, g_row):
    r = lax.rsqrt(jnp.mean(x * x, axis=-1, keepdims=True) + EPS)
    return (x * r) * g_row


def _dot(a, b):
    return jnp.dot(a, b, preferred_element_type=F32)


def _ffn_kernel(*refs, has_mix, has_final, n_f):
    x_ref, g_ref, wg_ref, wu_ref, wd_ref = refs[:5]
    rest = refs[5:]
    if has_mix:
        y_refs, wo_ref, rest = rest[:4], rest[4], rest[5:]
    if has_final:
        gf_ref, rest = rest[0], rest[1:]
    o_ref, h_scr, res_scr, acc_scr = rest
    j = pl.program_id(1)

    @pl.when(j == 0)
    def _():
        x = x_ref[...]
        if has_mix:
            for g in range(4):
                x = x + lax.dot_general(
                    y_refs[g][0], wo_ref[g * GROUP_W:(g + 1) * GROUP_W, :],
                    (((0,), (0,)), ((), ())), preferred_element_type=F32)
        res_scr[...] = x
        h_scr[...] = _rms_lanes(x, g_ref[...]).astype(BF16)
        acc_scr[...] = jnp.zeros_like(acc_scr)

    h = h_scr[...]
    gate = _dot(h, wg_ref[...])
    up = _dot(h, wu_ref[...])
    a = (gate / (1.0 + jnp.exp(-gate))) * up
    acc_scr[...] += _dot(a.astype(BF16), wd_ref[...])

    @pl.when(j == n_f - 1)
    def _():
        out = res_scr[...] + 0.5 * acc_scr[...]
        if has_final:
            out = _rms_lanes(out, gf_ref[...])
        o_ref[...] = out


def _ffn(x, g, wg, wu, wd, mix=None, final_g=None, seq=None):
    n, d = x.shape
    d_ff = wg.shape[1]
    tm, tf = FFN_TM, FFN_TF
    n_f = d_ff // tf
    in_specs = [
        pl.BlockSpec((tm, d), lambda i, j: (i, 0)),
        pl.BlockSpec((1, d), lambda i, j: (0, 0)),
        pl.BlockSpec((d, tf), lambda i, j: (0, j)),
        pl.BlockSpec((d, tf), lambda i, j: (0, j)),
        pl.BlockSpec((tf, d), lambda i, j: (j, 0)),
    ]
    args = [x, g.reshape(1, d), wg, wu, wd]
    if mix is not None:
        ys, wo = mix
        nsb = seq // tm
        for y in ys:
            in_specs.append(pl.BlockSpec((1, GROUP_W, tm), lambda i, j: (i // nsb, 0, i % nsb)))
            args.append(y)
        in_specs.append(pl.BlockSpec(wo.shape, lambda i, j: (0, 0)))
        args.append(wo)
    if final_g is not None:
        in_specs.append(pl.BlockSpec((1, d), lambda i, j: (0, 0)))
        args.append(final_g.reshape(1, d))
    return pl.pallas_call(
        functools.partial(_ffn_kernel, has_mix=mix is not None, has_final=final_g is not None, n_f=n_f),
        grid=(n // tm, n_f),
        in_specs=in_specs,
        out_specs=pl.BlockSpec((tm, d), lambda i, j: (i, 0)),
        out_shape=jax.ShapeDtypeStruct((n, d), F32),
        scratch_shapes=[pltpu.VMEM((tm, d), BF16), pltpu.VMEM((tm, d), F32), pltpu.VMEM((tm, d), F32)],
        compiler_params=pltpu.CompilerParams(
            dimension_semantics=("parallel", "arbitrary"), vmem_limit_bytes=VMEM_LIMIT),
        name="ffn_mix" if mix is not None else "ffn",
    )(*args)


def _rope_rows(x, cos, sin):
    x1, x2 = x[:16], x[16:]
    return x1 * cos - x2 * sin, x1 * sin + x2 * cos


def _store_v(v_ref, hh, v):
    v_ref[0, hh, 0, 0:HEAD_V, :] = v.astype(BF16)
    v_ref[0, hh, 0, HEAD_V:V_ROWS, :] = jnp.ones((V_ROWS - HEAD_V, v.shape[1]), BF16)


def _proj_kernel(x_ref, gmix_ref, wint_ref,
                 naq_ref, nak_ref,
                 qlat_ref, wuqt_ref, kvlat_ref, wukvt_ref, mq_ref, mk_ref,
                 dq_ref, dk_ref, gq_ref, gk_ref,
                 cseq_ref, sseq_ref, crow_ref, srow_ref, ccol_ref, scol_ref,
                 qa_o, ka_o, va_o, qb_o, kb_o, vb_o, qc_o, kc_o, vc_o, qd_o, kd_o, vd_o):
    h = _rms_lanes(x_ref[...], gmix_ref[...]).astype(BF16)

    def proj(lo, hi):
        return lax.dot_general(wint_ref[lo:hi, :], h, (((1,), (1,)), ((), ())),
                               preferred_element_type=F32)

    pa = proj(0, 768)
    sa = LOG2E * 64 ** -0.5
    for hh in range(4):
        q = _rms_rows(pa[hh * 64:(hh + 1) * 64], naq_ref[...], 64) * sa
        k = _rms_rows(pa[256 + hh * 64:256 + (hh + 1) * 64], nak_ref[...], 64)
        qa_o[0, hh] = q.astype(BF16)
        ka_o[0, hh] = k.astype(BF16)
        _store_v(va_o, hh, pa[512 + hh * 64:512 + (hh + 1) * 64])

    pb = proj(768, 1184)
    cq = _rms_rows(pb[0:256], qlat_ref[...], 256).astype(BF16)
    qb = _dot(wuqt_ref[...], cq)
    ckv = _rms_rows(pb[256:384], kvlat_ref[...], 128).astype(BF16)
    kv = _dot(wukvt_ref[...], ckv)
    kr = pb[384:416]
    cseq, sseq = cseq_ref[...], sseq_ref[...]
    sb = LOG2E * 96 ** -0.5
    for hh in range(4):
        q = _rms_rows(qb[hh * 96:(hh + 1) * 96], mq_ref[...], 96) * sb
        r1, r2 = _rope_rows(q[64:96], cseq, sseq)
        qb_o[0, hh, 0:64, :] = q[0:64].astype(BF16)
        qb_o[0, hh, 64:80, :] = r1.astype(BF16)
        qb_o[0, hh, 80:96, :] = r2.astype(BF16)
        k = jnp.concatenate([kv[hh * 128:hh * 128 + 64], kr], axis=0)
        k = _rms_rows(k, mk_ref[...], 96)
        r1, r2 = _rope_rows(k[64:96], cseq, sseq)
        kb_o[0, hh, 0:64, :] = k[0:64].astype(BF16)
        kb_o[0, hh, 64:80, :] = r1.astype(BF16)
        kb_o[0, hh, 80:96, :] = r2.astype(BF16)
        _store_v(vb_o, hh, kv[hh * 128 + 64:hh * 128 + 128])

    pc = proj(1184, 1952)
    sc = LOG2E * 32 ** -0.5
    for j in range(8):
        q = _rms_rows(pc[j * 32:(j + 1) * 32], dq_ref[...], 32) * sc
        k = _rms_rows(pc[256 + j * 32:256 + (j + 1) * 32], dk_ref[...], 32)
        qc_o[0, j] = q.astype(BF16)
        kc_o[0, j] = k.astype(BF16)
    for hh in range(4):
        _store_v(vc_o, hh, pc[512 + hh * 64:512 + (hh + 1) * 64])

    pd = proj(1952, 2464)
    crow, srow, ccol, scol = crow_ref[...], srow_ref[...], ccol_ref[...], scol_ref[...]
    sd = LOG2E * 64 ** -0.5

    def axial(x, o_ref, hh):
        a1, a2 = _rope_rows(x[0:32], crow, srow)
        b1, b2 = _rope_rows(x[32:64], ccol, scol)
        o_ref[0, hh, 0:16, :] = a1.astype(BF16)
        o_ref[0, hh, 16:32, :] = a2.astype(BF16)
        o_ref[0, hh, 32:48, :] = b1.astype(BF16)
        o_ref[0, hh, 48:64, :] = b2.astype(BF16)

    for hh in range(4):
        axial(_rms_rows(pd[hh * 64:(hh + 1) * 64], gq_ref[...], 64) * sd, qd_o, hh)
    for hh in range(2):
        axial(_rms_rows(pd[256 + hh * 64:256 + (hh + 1) * 64], gk_ref[...], 64), kd_o, hh)
        _store_v(vd_o, hh, pd[384 + hh * 64:384 + (hh + 1) * 64])


def _proj(x, batch, seq, gmix, wint, cols, mats, tabs):
    n, d = x.shape
    tm = PROJ_TM
    nsb = seq // tm
    naq, nak, qlat, kvlat, mq, mk, dq, dk, gq, gk = cols
    wuqt, wukvt = mats

    def full(a):
        return pl.BlockSpec(a.shape, lambda i: (0,) * a.ndim)

    tab_spec = pl.BlockSpec((16, tm), lambda i: (0, i % nsb))
    args = [x, gmix.reshape(1, d), wint, naq, nak, qlat, wuqt, kvlat, wukvt, mq, mk, dq, dk, gq, gk, *tabs]
    in_specs = [pl.BlockSpec((tm, d), lambda i: (i, 0))] + [full(a) for a in args[1:15]] + [tab_spec] * 6

    def head_out(nh, rows):
        return (jax.ShapeDtypeStruct((batch, nh, rows, seq), BF16),
                pl.BlockSpec((1, nh, rows, tm), lambda i: (i // nsb, 0, 0, i % nsb)))

    def v_out(nh):
        return (jax.ShapeDtypeStruct((batch, nh, nsb, V_ROWS, tm), BF16),
                pl.BlockSpec((1, nh, 1, V_ROWS, tm), lambda i: (i // nsb, 0, i % nsb, 0, 0)))

    outs = [head_out(4, 64), head_out(4, 64), v_out(4),
            head_out(4, 96), head_out(4, 96), v_out(4),
            head_out(8, 32), head_out(8, 32), v_out(4),
            head_out(4, 64), head_out(2, 64), v_out(2)]
    return pl.pallas_call(
        _proj_kernel,
        grid=(n // tm,),
        in_specs=in_specs,
        out_specs=[o[1] for o in outs],
        out_shape=[o[0] for o in outs],
        compiler_params=pltpu.CompilerParams(
            dimension_semantics=("parallel",), vmem_limit_bytes=VMEM_LIMIT),
        name="mix_proj",
    )(*args)


def _col_max(s):
    parts = [s]
    while parts[0].shape[0] > 64:
        half = parts[0].shape[0] // 2
        parts = [jnp.maximum(p[:half], p[half:]) for p in parts]
    return jnp.max(parts[0], axis=0, keepdims=True)


def _produce(k_blk, q_blk, slot, s_scr, mc_scr):
    s = _dot(k_blk, q_blk)
    s_scr[slot] = s
    mc_scr[slot] = _col_max(s)


def _consume(s, m_cur, c, idx, v_blk, m_scr, acc_scr):
    m_prev = m_scr[idx]
    m_new = jnp.maximum(m_prev, m_cur + c)
    alpha = jnp.exp2(m_prev - m_new)
    p = jnp.exp2(s - (m_new - c)).astype(BF16)
    acc_scr[idx] = alpha * acc_scr[idx] + _dot(v_blk, p)
    m_scr[idx] = m_new


def _finish_head(acc):
    return acc[0:HEAD_V] / acc[HEAD_V:HEAD_V + 1]


def _flash_plain_kernel(q_ref, k_ref, v_ref, beta_ref, o_ref, s_scr, mc_scr, m_scr, acc_scr, *, kv_map, n_k):
    n_h = len(kv_map)
    m_scr[...] = jnp.full_like(m_scr, NEG_BIG)
    acc_scr[...] = jnp.zeros_like(acc_scr)
    _produce(k_ref[0, kv_map[0], 0], q_ref[0, 0], 0, s_scr, mc_scr)

    def body(i, carry):
        nxt = jnp.minimum(i + 1, n_k - 1)
        for hh in range(n_h):
            if hh + 1 < n_h:
                _produce(k_ref[0, kv_map[hh + 1], i], q_ref[0, hh + 1], (hh + 1) % 2, s_scr, mc_scr)
            else:
                _produce(k_ref[0, kv_map[0], nxt], q_ref[0, 0], 0, s_scr, mc_scr)
            _consume(s_scr[hh % 2], mc_scr[hh % 2], 0.0, hh, v_ref[0, kv_map[hh], i], m_scr, acc_scr)
        return carry

    lax.fori_loop(0, n_k, body, 0)
    y = jnp.concatenate([_finish_head(acc_scr[hh]) for hh in range(n_h)], axis=0)
    o_ref[0] = _rms_rows(y, beta_ref[...], GROUP_W).astype(BF16)


def _flash_plain(qt, k, vt, beta_col, kv_map):
    b, hq, dq, s = qt.shape
    hk, n_k, t = k.shape[1], k.shape[2], k.shape[3]
    assert len(kv_map) % 2 == 0
    return pl.pallas_call(
        functools.partial(_flash_plain_kernel, kv_map=kv_map, n_k=n_k),
        grid=(b, s // t),
        in_specs=[
            pl.BlockSpec((1, hq, dq, t), lambda bi, qi: (bi, 0, 0, qi)),
            pl.BlockSpec((1, hk, n_k, t, dq), lambda bi, qi: (bi, 0, 0, 0, 0)),
            pl.BlockSpec((1, hk, n_k, V_ROWS, t), lambda bi, qi: (bi, 0, 0, 0, 0)),
            pl.BlockSpec((GROUP_W, 1), lambda bi, qi: (0, 0)),
        ],
        out_specs=pl.BlockSpec((1, GROUP_W, t), lambda bi, qi: (bi, 0, qi)),
        out_shape=jax.ShapeDtypeStruct((b, GROUP_W, s), BF16),
        scratch_shapes=[pltpu.VMEM((2, t, t), F32), pltpu.VMEM((2, 1, t), F32),
                        pltpu.VMEM((hq, 1, t), F32), pltpu.VMEM((hq, V_ROWS, t), F32)],
        compiler_params=pltpu.CompilerParams(
            dimension_semantics=("parallel", "parallel"), vmem_limit_bytes=VMEM_LIMIT),
        name="flash_plain",
    )(qt, k, vt, beta_col)


def _flash_diff_kernel(t5_ref, q_ref, k_ref, v_ref, bias_ref, lam_ref, subln_ref, o_ref,
                       qpad_scr, s_scr, mc_scr, m_scr, acc_scr, *, lambda_init, n_k):
    qi = pl.program_id(1)
    n_m = 8
    dq = q_ref.shape[2]
    m_scr[...] = jnp.full_like(m_scr, NEG_BIG)
    acc_scr[...] = jnp.zeros_like(acc_scr)
    qpad_scr[...] = jnp.zeros_like(qpad_scr)
    for j in range(n_m):
        qpad_scr[j, j * dq:(j + 1) * dq, :] = q_ref[0, j]
    _produce(k_ref[0, 0], qpad_scr[0], 0, s_scr, mc_scr)

    def sweep(lo, hi, near, bucket):
        def body(i, carry):
            nxt = jnp.minimum(i + 1, n_k - 1)
            for j in range(n_m):
                hh = j // 2
                if j + 1 < n_m:
                    _produce(k_ref[0, i], qpad_scr[j + 1], (j + 1) % 2, s_scr, mc_scr)
                else:
                    _produce(k_ref[0, nxt], qpad_scr[0], 0, s_scr, mc_scr)
                if near:
                    s = s_scr[j % 2] + bias_ref[i - qi + 1, hh]
                    _consume(s, _col_max(s), 0.0, j, v_ref[0, hh, i], m_scr, acc_scr)
                else:
                    c = t5_ref[bucket * 4 + hh] * LOG2E
                    _consume(s_scr[j % 2], mc_scr[j % 2], c, j, v_ref[0, hh, i], m_scr, acc_scr)
            return carry

        lax.fori_loop(lo, hi, body, 0)

    near_lo = jnp.maximum(qi - 1, 0)
    near_hi = jnp.minimum(qi + 2, n_k)
    sweep(0, near_lo, False, T5_BUCKETS // 2 - 1)
    sweep(near_lo, near_hi, True, None)
    sweep(near_hi, n_k, False, T5_BUCKETS - 1)

    lp = lam_ref[...]
    lam = (jnp.exp(jnp.sum(lp[0:1] * lp[1:2], axis=1, keepdims=True))
           - jnp.exp(jnp.sum(lp[2:3] * lp[3:4], axis=1, keepdims=True)) + lambda_init)
    for hh in range(4):
        y = _finish_head(acc_scr[2 * hh]) - lam * _finish_head(acc_scr[2 * hh + 1])
        y = _rms_rows(y, subln_ref[...], HEAD_V) * (1.0 - lambda_init)
        o_ref[0, hh * HEAD_V:(hh + 1) * HEAD_V, :] = y.astype(BF16)


def _flash_diff(t5_flat, qt, k, vt, bias, lam_params, subln_col, lambda_init):
    b, hq, dq, s = qt.shape
    n_k, t = k.shape[1], k.shape[2]
    return pl.pallas_call(
        functools.partial(_flash_diff_kernel, lambda_init=lambda_init, n_k=n_k),
        grid=(b, s // t),
        in_specs=[
            pl.BlockSpec(memory_space=pltpu.SMEM),
            pl.BlockSpec((1, hq, dq, t), lambda bi, qi: (bi, 0, 0, qi)),
            pl.BlockSpec((1, n_k, t, hq * dq), lambda bi, qi: (bi, 0, 0, 0)),
            pl.BlockSpec((1, 4, n_k, V_ROWS, t), lambda bi, qi: (bi, 0, 0, 0, 0)),
            pl.BlockSpec(bias.shape, lambda bi, qi: (0, 0, 0, 0), pipeline_mode=pl.Buffered(1)),
            pl.BlockSpec(lam_params.shape, lambda bi, qi: (0, 0)),
            pl.BlockSpec((HEAD_V, 1), lambda bi, qi: (0, 0)),
        ],
        out_specs=pl.BlockSpec((1, GROUP_W, t), lambda bi, qi: (bi, 0, qi)),
        out_shape=jax.ShapeDtypeStruct((b, GROUP_W, s), BF16),
        scratch_shapes=[pltpu.VMEM((hq, hq * dq, t), BF16),
                        pltpu.VMEM((2, t, t), F32), pltpu.VMEM((2, 1, t), F32),
                        pltpu.VMEM((hq, 1, t), F32), pltpu.VMEM((hq, V_ROWS, t), F32)],
        compiler_params=pltpu.CompilerParams(
            dimension_semantics=("parallel", "parallel"), vmem_limit_bytes=VMEM_LIMIT),
        name="flash_diff",
    )(t5_flat, qt, k, vt, bias, lam_params, subln_col)


def _t5_bias_kernel(tab_ref, o_ref, *, t):
    d = pl.program_id(0) - 1
    hh = pl.program_id(1)
    rows = 64
    half = T5_BUCKETS // 2
    max_exact = half // 2

    def body(r, carry):
        kk = lax.broadcasted_iota(jnp.int32, (rows, t), 0) + r * rows
        qq = lax.broadcasted_iota(jnp.int32, (rows, t), 1)
        rel = d * t + kk - qq
        n = jnp.abs(rel)
        large = max_exact + (jnp.log(jnp.maximum(n, 1).astype(F32) / max_exact)
                             / math.log(T5_MAX_DIST / max_exact) * (half - max_exact)).astype(jnp.int32)
        large = jnp.minimum(large, half - 1)
        bucket = jnp.where(rel > 0, half, 0) + jnp.where(n < max_exact, n, large)
        acc = jnp.zeros((rows, t), F32)
        for bkt in range(T5_BUCKETS):
            acc = jnp.where(bucket == bkt, tab_ref[bkt * 4 + hh], acc)
        o_ref[0, 0, pl.ds(pl.multiple_of(r * rows, rows), rows), :] = acc * LOG2E
        return carry

    lax.fori_loop(0, t // rows, body, 0)


def _t5_bias(t5_flat):
    t = FLASH_T
    return pl.pallas_call(
        functools.partial(_t5_bias_kernel, t=t),
        grid=(3, 4),
        in_specs=[pl.BlockSpec(memory_space=pltpu.SMEM)],
        out_specs=pl.BlockSpec((1, 1, t, t), lambda d, h: (d, h, 0, 0)),
        out_shape=jax.ShapeDtypeStruct((3, 4, t, t), F32),
        name="t5_bias",
    )(t5_flat)


def _na_bias_kernel(rpb_ref, o_ref, *, grid_rows):
    kind = pl.program_id(0)
    hh = pl.program_id(1)
    n_dr, n_dc = 2 * NA_KH - 1, 2 * NA_KW - 1
    r0 = jnp.where(kind == 0, 0, jnp.where(kind == 1, NA_Q_ROWS, grid_rows - NA_Q_ROWS))
    start = jnp.where(kind == 2, grid_rows - NA_K_ROWS, 0)
    shape = (GRID_W, NA_Q_ROWS * GRID_W)
    lane = lax.broadcasted_iota(jnp.int32, shape, 1)
    kc = lax.broadcasted_iota(jnp.int32, shape, 0)
    qc = lane & (GRID_W - 1)
    grp = lane >> 6
    dcm = jnp.clip(kc - qc + NA_KW - 1, 0, n_dc - 1)
    qs = jnp.clip(qc - NA_KW // 2, 0, GRID_W - NA_KW)
    col_ok = (kc >= qs) & (kc < qs + NA_KW)

    def by_group(vals):
        out = vals[NA_Q_ROWS - 1]
        for i in range(NA_Q_ROWS - 2, -1, -1):
            out = jnp.where(grp == i, vals[i], out)
        return out

    def jbody(j, carry):
        krow = start + j
        bases, oks = [], []
        for i in range(NA_Q_ROWS):
            qrow = r0 + i
            lo = jnp.clip(qrow - NA_KH // 2, 0, grid_rows - NA_KH)
            oks.append(((krow >= lo) & (krow < lo + NA_KH)).astype(jnp.int32))
            dr = jnp.clip(krow - qrow + NA_KH - 1, 0, n_dr - 1)
            bases.append((hh * n_dr + dr) * n_dc)

        def dcbody(dc, acc):
            vec = by_group([rpb_ref[bases[i] + dc] for i in range(NA_Q_ROWS)])
            return jnp.where(dcm == dc, vec, acc)

        acc = lax.fori_loop(0, n_dc, dcbody, jnp.zeros(shape, F32))
        ok = col_ok & (by_group(oks) > 0)
        o_ref[0, 0, pl.ds(pl.multiple_of(j * GRID_W, GRID_W), GRID_W), :] = jnp.where(ok, acc * LOG2E, NEG_BIG)
        return carry

    lax.fori_loop(0, NA_K_ROWS, jbody, 0)


def _na_bias(rpb_flat, grid_rows):
    kt, qt = NA_K_ROWS * GRID_W, NA_Q_ROWS * GRID_W
    return pl.pallas_call(
        functools.partial(_na_bias_kernel, grid_rows=grid_rows),
        grid=(3, 4),
        in_specs=[pl.BlockSpec(memory_space=pltpu.SMEM)],
        out_specs=pl.BlockSpec((1, 1, kt, qt), lambda kd, h: (kd, h, 0, 0)),
        out_shape=jax.ShapeDtypeStruct((3, 4, kt, qt), F32),
        name="na_bias",
    )(rpb_flat)


def _na_kernel(q_ref, k0_ref, k1_ref, k2_ref, v0_ref, v1_ref, v2_ref, bias_ref, beta_ref, o_ref):
    qt = NA_Q_ROWS * GRID_W
    k_refs = (k0_ref, k1_ref, k2_ref)
    v_refs = (v0_ref, v1_ref, v2_ref)
    outs = []
    for hh in range(4):
        q = q_ref[0, hh]
        ss = [_dot(k_refs[j][0, hh], q) + bias_ref[0, hh, j * qt:(j + 1) * qt, :] for j in range(3)]
        m = jnp.maximum(jnp.maximum(jnp.max(ss[0], axis=0, keepdims=True),
                                    jnp.max(ss[1], axis=0, keepdims=True)),
                        jnp.max(ss[2], axis=0, keepdims=True))
        acc = None
        for j in range(3):
            p = jnp.exp2(ss[j] - m).astype(BF16)
            pv = _dot(v_refs[j][0, hh, 0], p)
            acc = pv if acc is None else acc + pv
        outs.append(_finish_head(acc))
    y = jnp.concatenate(outs, axis=0)
    o_ref[0] = _rms_rows(y, beta_ref[...], GROUP_W).astype(BF16)


def _na(qt_arr, k, vt, bias, beta_col):
    b, nh, d, s = qt_arr.shape
    qt = NA_Q_ROWS * GRID_W
    kt = NA_K_ROWS * GRID_W
    n_t = s // qt
    n_win = kt // qt

    def win(j):
        return lambda bi, ti: jnp.clip(ti - 1, 0, n_t - n_win) + j

    def kind(bi, ti):
        return (jnp.where(ti == 0, 0, jnp.where(ti == n_t - 1, 2, 1)), 0, 0, 0)

    k_specs = [pl.BlockSpec((1, nh, qt, d), (lambda j: lambda bi, ti: (bi, 0, win(j)(bi, ti), 0))(j))
               for j in range(n_win)]
    per = vt.shape[4] // qt
    v_specs = [pl.BlockSpec((1, nh, 1, V_ROWS, qt),
                            (lambda j: lambda bi, ti: (bi, 0, win(j)(bi, ti) // per, 0, win(j)(bi, ti) % per))(j))
               for j in range(n_win)]
    return pl.pallas_call(
        _na_kernel,
        grid=(b, n_t),
        in_specs=[pl.BlockSpec((1, nh, d, qt), lambda bi, ti: (bi, 0, 0, ti))] + k_specs + v_specs + [
            pl.BlockSpec((1, nh, kt, qt), kind),
            pl.BlockSpec((GROUP_W, 1), lambda bi, ti: (0, 0)),
        ],
        out_specs=pl.BlockSpec((1, GROUP_W, qt), lambda bi, ti: (bi, 0, ti)),
        out_shape=jax.ShapeDtypeStruct((b, GROUP_W, s), BF16),
        compiler_params=pltpu.CompilerParams(
            dimension_semantics=("parallel", "parallel"), vmem_limit_bytes=VMEM_LIMIT),
        name="na_attn",
    )(qt_arr, k, k, k, vt, vt, vt, bias, beta_col)


def _rope_tables(pos, dim):
    inv = jnp.exp(-math.log(ROPE_THETA) * jnp.arange(0, dim, 2, dtype=F32) / dim)
    ang = pos.astype(F32)[:, None] * inv[None, :]
    return jnp.cos(ang).T, jnp.sin(ang).T


def _col(v):
    return v.reshape(-1, 1).astype(F32)


def _lambda_init(layer):
    return 0.8 - 0.6 * math.exp(-0.3 * layer)


def kernel(x, ffn1_norm, ffn1_w_gate, ffn1_w_up, ffn1_w_down, mix_norm, w_in, na_q_norm, na_k_norm, na_rpb, na_beta, mla_q_lat_norm, mla_w_uq, mla_kv_lat_norm, mla_w_ukv, mla_q_norm, mla_k_norm, mla_beta, diff_q_norm, diff_k_norm, diff_lambda, diff_subln, gqa_q_norm, gqa_k_norm, gqa_beta, w_out, ffn2_norm, ffn2_w_gate, ffn2_w_up, ffn2_w_down, final_norm, t5_bias):
    batch, seq, d = x.shape
    grid_rows = seq // GRID_W
    pos = jnp.arange(seq, dtype=jnp.int32)
    tabs = (*_rope_tables(pos, 32), *_rope_tables(pos // GRID_W, 32), *_rope_tables(pos % GRID_W, 32))
    t5_flat = t5_bias.reshape(-1).astype(F32)
    t5_tiles = _t5_bias(t5_flat)

    xf = x.reshape(batch * seq, d)
    for l in range(N_LAYERS):
        xf = _ffn(xf, ffn1_norm[l], ffn1_w_gate[l].astype(BF16), ffn1_w_up[l].astype(BF16),
                  ffn1_w_down[l].astype(BF16))
        cols = tuple(_col(v[l]) for v in (na_q_norm, na_k_norm, mla_q_lat_norm, mla_kv_lat_norm,
                                          mla_q_norm, mla_k_norm, diff_q_norm, diff_k_norm,
                                          gqa_q_norm, gqa_k_norm))
        mats = (mla_w_uq[l].T.astype(BF16), mla_w_ukv[l].T.astype(BF16))
        (qa, ka, va, qb, kb, vb, qc, kc, vc, qd, kd, vd) = _proj(
            xf, batch, seq, mix_norm[l], w_in[l].T.astype(BF16), cols, mats, tabs)
        n_k = seq // FLASH_T
        ka = jnp.swapaxes(ka, 2, 3)
        kb = jnp.swapaxes(kb, 2, 3).reshape(batch, 4, n_k, FLASH_T, 96)
        kc = jnp.swapaxes(kc.reshape(batch, 256, seq), 1, 2).reshape(batch, n_k, FLASH_T, 256)
        kd = jnp.swapaxes(kd, 2, 3).reshape(batch, 2, n_k, FLASH_T, 64)

        ya = _na(qa, ka, va, _na_bias(na_rpb[l].reshape(-1).astype(F32), grid_rows), _col(na_beta[l]))
        yb = _flash_plain(qb, kb, vb, _col(mla_beta[l]), (0, 1, 2, 3))
        yc = _flash_diff(t5_flat, qc, kc, vc, t5_tiles, diff_lambda[l].astype(F32), _col(diff_subln[l]),
                         _lambda_init(l))
        yd = _flash_plain(qd, kd, vd, _col(gqa_beta[l]), (0, 0, 1, 1))

        xf = _ffn(xf, ffn2_norm[l], ffn2_w_gate[l].astype(BF16), ffn2_w_up[l].astype(BF16),
                  ffn2_w_down[l].astype(BF16), mix=((ya, yb, yc, yd), w_out[l].astype(BF16)),
                  final_g=final_norm[l], seq=seq)
    return xf.reshape(batch, seq, d)
```

```python
import functools
import math

import jax
import jax.numpy as jnp
from jax import lax
from jax.experimental import pallas as pl
from jax.experimental.pallas import tpu as pltpu

F32 = jnp.float32
BF16 = jnp.bfloat16

EPS = 1e-6
NEG_BIG = -1e30
LOG2E = 1.4426950408889634
ROPE_THETA = 10000.0
GRID_W = 64
N_LAYERS = 2

HEAD_V = 64
V_ROWS = 80
GROUP_W = 256

NA_KH, NA_KW = 8, 16
NA_Q_ROWS = 4
NA_K_ROWS = 12
T5_BUCKETS = 32
T5_MAX_DIST = 128

FFN_TM = 512
FFN_TF = 1408
FLASH_T = 512
FLASH_AHEAD = 2
FLASH_SLOTS = 4
PROJ_TM = FLASH_T

VMEM_LIMIT = 56 * 1024 * 1024


def _rms_rows(x, g_col, n):
    r = lax.rsqrt(jnp.sum(x * x, axis=0, keepdims=True) * (1.0 / n) + EPS)
    return (x * r) * g_col


def _rms_lanes(x, g_row):
    r = lax.rsqrt(jnp.mean(x * x, axis=-1, keepdims=True) + EPS)
    return (x * r) * g_row


def _dot(a, b):
    return jnp.dot(a, b, preferred_element_type=F32)


def _ffn_kernel(*refs, has_mix, has_final, n_f):
    x_ref, g_ref, wg_ref, wu_ref, wd_ref = refs[:5]
    rest = refs[5:]
    if has_mix:
        y_refs, wo_ref, rest = rest[:4], rest[4], rest[5:]
    if has_final:
        gf_ref, rest = rest[0], rest[1:]
    o_ref, h_scr, res_scr, acc_scr = rest
    j = pl.program_id(1)

    @pl.when(j == 0)
    def _():
        x = x_ref[...]
        if has_mix:
            for g in range(4):
                x = x + lax.dot_general(
                    y_refs[g][0], wo_ref[g * GROUP_W:(g + 1) * GROUP_W, :],
                    (((0,), (0,)), ((), ())), preferred_element_type=F32)
        res_scr[...] = x
        h_scr[...] = _rms_lanes(x, g_ref[...]).astype(BF16)
        acc_scr[...] = jnp.zeros_like(acc_scr)

    h = h_scr[...]
    gate = _dot(h, wg_ref[...])
    up = _dot(h, wu_ref[...])
    a = (gate / (1.0 + jnp.exp(-gate))) * up
    acc_scr[...] += _dot(a.astype(BF16), wd_ref[...])

    @pl.when(j == n_f - 1)
    def _():
        out = res_scr[...] + 0.5 * acc_scr[...]
        if has_final:
            out = _rms_lanes(out, gf_ref[...])
        o_ref[...] = out


def _ffn(x, g, wg, wu, wd, mix=None, final_g=None, seq=None):
    n, d = x.shape
    d_ff = wg.shape[1]
    tm, tf = FFN_TM, FFN_TF
    n_f = d_ff // tf
    in_specs = [
        pl.BlockSpec((tm, d), lambda i, j: (i, 0)),
        pl.BlockSpec((1, d), lambda i, j: (0, 0)),
        pl.BlockSpec((d, tf), lambda i, j: (0, j)),
        pl.BlockSpec((d, tf), lambda i, j: (0, j)),
        pl.BlockSpec((tf, d), lambda i, j: (j, 0)),
    ]
    args = [x, g.reshape(1, d), wg, wu, wd]
    if mix is not None:
        ys, wo = mix
        nsb = seq // tm
        for y in ys:
            in_specs.append(pl.BlockSpec((1, GROUP_W, tm), lambda i, j: (i // nsb, 0, i % nsb)))
            args.append(y)
        in_specs.append(pl.BlockSpec(wo.shape, lambda i, j: (0, 0)))
        args.append(wo)
    if final_g is not None:
        in_specs.append(pl.BlockSpec((1, d), lambda i, j: (0, 0)))
        args.append(final_g.reshape(1, d))
    return pl.pallas_call(
        functools.partial(_ffn_kernel, has_mix=mix is not None, has_final=final_g is not None, n_f=n_f),
        grid=(n // tm, n_f),
        in_specs=in_specs,
        out_specs=pl.BlockSpec((tm, d), lambda i, j: (i, 0)),
        out_shape=jax.ShapeDtypeStruct((n, d), F32),
        scratch_shapes=[pltpu.VMEM((tm, d), BF16), pltpu.VMEM((tm, d), F32), pltpu.VMEM((tm, d), F32)],
        compiler_params=pltpu.CompilerParams(
            dimension_semantics=("parallel", "arbitrary"), vmem_limit_bytes=VMEM_LIMIT),
        name="ffn_mix" if mix is not None else "ffn",
    )(*args)


def _rope_rows(x, cos, sin):
    x1, x2 = x[:16], x[16:]
    return x1 * cos - x2 * sin, x1 * sin + x2 * cos


def _store_v(v_ref, hh, v):
    v_ref[0, hh, 0, 0:HEAD_V, :] = v.astype(BF16)
    v_ref[0, hh, 0, HEAD_V:V_ROWS, :] = jnp.ones((V_ROWS - HEAD_V, v.shape[1]), BF16)


def _proj_kernel(x_ref, gmix_ref, wint_ref,
                 naq_ref, nak_ref,
                 qlat_ref, wuqt_ref, kvlat_ref, wukvt_ref, mq_ref, mk_ref,
                 dq_ref, dk_ref, gq_ref, gk_ref,
                 cseq_ref, sseq_ref, crow_ref, srow_ref, ccol_ref, scol_ref,
                 qa_o, ka_o, va_o, qb_o, kb_o, vb_o, qc_o, kc_o, vc_o, qd_o, kd_o, vd_o):
    h = _rms_lanes(x_ref[...], gmix_ref[...]).astype(BF16)

    def proj(lo, hi):
        return lax.dot_general(wint_ref[lo:hi, :], h, (((1,), (1,)), ((), ())),
                               preferred_element_type=F32)

    pa = proj(0, 768)
    sa = LOG2E * 64 ** -0.5
    for hh in range(4):
        q = _rms_rows(pa[hh * 64:(hh + 1) * 64], naq_ref[...], 64) * sa
        k = _rms_rows(pa[256 + hh * 64:256 + (hh + 1) * 64], nak_ref[...], 64)
        qa_o[0, hh] = q.astype(BF16)
        ka_o[0, hh] = k.astype(BF16)
        _store_v(va_o, hh, pa[512 + hh * 64:512 + (hh + 1) * 64])

    pb = proj(768, 1184)
    cq = _rms_rows(pb[0:256], qlat_ref[...], 256).astype(BF16)
    qb = _dot(wuqt_ref[...], cq)
    ckv = _rms_rows(pb[256:384], kvlat_ref[...], 128).astype(BF16)
    kv = _dot(wukvt_ref[...], ckv)
    kr = pb[384:416]
    cseq, sseq = cseq_ref[...], sseq_ref[...]
    sb = LOG2E * 96 ** -0.5
    for hh in range(4):
        q = _rms_rows(qb[hh * 96:(hh + 1) * 96], mq_ref[...], 96) * sb
        r1, r2 = _rope_rows(q[64:96], cseq, sseq)
        qb_o[0, hh, 0:64, :] = q[0:64].astype(BF16)
        qb_o[0, hh, 64:80, :] = r1.astype(BF16)
        qb_o[0, hh, 80:96, :] = r2.astype(BF16)
        k = jnp.concatenate([kv[hh * 128:hh * 128 + 64], kr], axis=0)
        k = _rms_rows(k, mk_ref[...], 96)
        r1, r2 = _rope_rows(k[64:96], cseq, sseq)
        kb_o[0, hh, 0:64, :] = k[0:64].astype(BF16)
        kb_o[0, hh, 64:80, :] = r1.astype(BF16)
        kb_o[0, hh, 80:96, :] = r2.astype(BF16)
        _store_v(vb_o, hh, kv[hh * 128 + 64:hh * 128 + 128])

    pc = proj(1184, 1952)
    sc = LOG2E * 32 ** -0.5
    for j in range(8):
        q = _rms_rows(pc[j * 32:(j + 1) * 32], dq_ref[...], 32) * sc
        k = _rms_rows(pc[256 + j * 32:256 + (j + 1) * 32], dk_ref[...], 32)
        qc_o[0, j] = q.astype(BF16)
        kc_o[0, j] = k.astype(BF16)
    for hh in range(4):
        _store_v(vc_o, hh, pc[512 + hh * 64:512 + (hh + 1) * 64])

    pd = proj(1952, 2464)
    crow, srow, ccol, scol = crow_ref[...], srow_ref[...], ccol_ref[...], scol_ref[...]
    sd = LOG2E * 64 ** -0.5

    def axial(x, o_ref, hh):
        a1, a2 = _rope_rows(x[0:32], crow, srow)
        b1, b2 = _rope_rows(x[32:64], ccol, scol)
        o_ref[0, hh, 0:16, :] = a1.astype(BF16)
        o_ref[0, hh, 16:32, :] = a2.astype(BF16)
        o_ref[0, hh, 32:48, :] = b1.astype(BF16)
        o_ref[0, hh, 48:64, :] = b2.astype(BF16)

    for hh in range(4):
        axial(_rms_rows(pd[hh * 64:(hh + 1) * 64], gq_ref[...], 64) * sd, qd_o, hh)
    for hh in range(2):
        axial(_rms_rows(pd[256 + hh * 64:256 + (hh + 1) * 64], gk_ref[...], 64), kd_o, hh)
        _store_v(vd_o, hh, pd[384 + hh * 64:384 + (hh + 1) * 64])


def _proj(x, batch, seq, gmix, wint, cols, mats, tabs):
    n, d = x.shape
    tm = PROJ_TM
    nsb = seq // tm
    naq, nak, qlat, kvlat, mq, mk, dq, dk, gq, gk = cols
    wuqt, wukvt = mats

    def full(a):
        return pl.BlockSpec(a.shape, lambda i: (0,) * a.ndim)

    tab_spec = pl.BlockSpec((16, tm), lambda i: (0, i % nsb))
    args = [x, gmix.reshape(1, d), wint, naq, nak, qlat, wuqt, kvlat, wukvt, mq, mk, dq, dk, gq, gk, *tabs]
    in_specs = [pl.BlockSpec((tm, d), lambda i: (i, 0))] + [full(a) for a in args[1:15]] + [tab_spec] * 6

    def head_out(nh, rows):
        return (jax.ShapeDtypeStruct((batch, nh, rows, seq), BF16),
                pl.BlockSpec((1, nh, rows, tm), lambda i: (i // nsb, 0, 0, i % nsb)))

    def v_out(nh):
        return (jax.ShapeDtypeStruct((batch, nh, nsb, V_ROWS, tm), BF16),
                pl.BlockSpec((1, nh, 1, V_ROWS, tm), lambda i: (i // nsb, 0, i % nsb, 0, 0)))

    outs = [head_out(4, 64), head_out(4, 64), v_out(4),
            head_out(4, 96), head_out(4, 96), v_out(4),
            head_out(8, 32), head_out(8, 32), v_out(4),
            head_out(4, 64), head_out(2, 64), v_out(2)]
    return pl.pallas_call(
        _proj_kernel,
        grid=(n // tm,),
        in_specs=in_specs,
        out_specs=[o[1] for o in outs],
        out_shape=[o[0] for o in outs],
        compiler_params=pltpu.CompilerParams(
            dimension_semantics=("parallel",), vmem_limit_bytes=VMEM_LIMIT),
        name="mix_proj",
    )(*args)


def _col_max(s):
    parts = [s]
    while parts[0].shape[0] > 64:
        half = parts[0].shape[0] // 2
        parts = [jnp.maximum(p[:half], p[half:]) for p in parts]
    return jnp.max(parts[0], axis=0, keepdims=True)


def _produce(k_blk, q_blk, slot, s_scr, mc_scr):
    s = _dot(k_blk, q_blk)
    s_scr[slot] = s
    mc_scr[slot] = _col_max(s)


def _consume(s, m_cur, c, idx, v_blk, m_scr, acc_scr):
    m_prev = m_scr[idx]
    m_new = jnp.maximum(m_prev, m_cur + c)
    alpha = jnp.exp2(m_prev - m_new)
    p = jnp.exp2(s - (m_new - c)).astype(BF16)
    acc_scr[idx] = alpha * acc_scr[idx] + _dot(v_blk, p)
    m_scr[idx] = m_new


def _finish_head(acc):
    return acc[0:HEAD_V] / acc[HEAD_V:HEAD_V + 1]


def _flash_plain_kernel(q_ref, k_ref, v_ref, beta_ref, o_ref, s_scr, mc_scr, m_scr, acc_scr, *, kv_map, n_k):
    n_h = len(kv_map)
    m_scr[...] = jnp.full_like(m_scr, NEG_BIG)
    acc_scr[...] = jnp.zeros_like(acc_scr)
    def produce(i, hh):
        _produce(k_ref[0, kv_map[hh], i], q_ref[0, hh], hh % FLASH_SLOTS, s_scr, mc_scr)

    for hh in range(FLASH_AHEAD):
        produce(0, hh)

    def body(i, carry):
        nxt = jnp.minimum(i + 1, n_k - 1)
        for hh in range(n_h):
            ahead = hh + FLASH_AHEAD
            produce(i if ahead < n_h else nxt, ahead % n_h)
            slot = hh % FLASH_SLOTS
            _consume(s_scr[slot], mc_scr[slot], 0.0, hh, v_ref[0, kv_map[hh], i], m_scr, acc_scr)
        return carry

    lax.fori_loop(0, n_k, body, 0)
    y = jnp.concatenate([_finish_head(acc_scr[hh]) for hh in range(n_h)], axis=0)
    o_ref[0] = _rms_rows(y, beta_ref[...], GROUP_W).astype(BF16)


def _flash_plain(qt, k, vt, beta_col, kv_map):
    b, hq, dq, s = qt.shape
    hk, n_k, t = k.shape[1], k.shape[2], k.shape[3]
    assert len(kv_map) % 2 == 0
    return pl.pallas_call(
        functools.partial(_flash_plain_kernel, kv_map=kv_map, n_k=n_k),
        grid=(b, s // t),
        in_specs=[
            pl.BlockSpec((1, hq, dq, t), lambda bi, qi: (bi, 0, 0, qi)),
            pl.BlockSpec((1, hk, n_k, t, dq), lambda bi, qi: (bi, 0, 0, 0, 0)),
            pl.BlockSpec((1, hk, n_k, V_ROWS, t), lambda bi, qi: (bi, 0, 0, 0, 0)),
            pl.BlockSpec((GROUP_W, 1), lambda bi, qi: (0, 0)),
        ],
        out_specs=pl.BlockSpec((1, GROUP_W, t), lambda bi, qi: (bi, 0, qi)),
        out_shape=jax.ShapeDtypeStruct((b, GROUP_W, s), BF16),
        scratch_shapes=[pltpu.VMEM((FLASH_SLOTS, t, t), F32), pltpu.VMEM((FLASH_SLOTS, 1, t), F32),
                        pltpu.VMEM((hq, 1, t), F32), pltpu.VMEM((hq, V_ROWS, t), F32)],
        compiler_params=pltpu.CompilerParams(
            dimension_semantics=("parallel", "parallel"), vmem_limit_bytes=VMEM_LIMIT),
        name="flash_plain",
    )(qt, k, vt, beta_col)


def _flash_diff_kernel(t5_ref, q_ref, k_ref, v_ref, bias_ref, lam_ref, subln_ref, o_ref,
                       qpad_scr, s_scr, mc_scr, m_scr, acc_scr, *, lambda_init, n_k):
    qi = pl.program_id(1)
    n_m = 8
    dq = q_ref.shape[2]
    m_scr[...] = jnp.full_like(m_scr, NEG_BIG)
    acc_scr[...] = jnp.zeros_like(acc_scr)
    qpad_scr[...] = jnp.zeros_like(qpad_scr)
    for j in range(n_m):
        qpad_scr[j, j * dq:(j + 1) * dq, :] = q_ref[0, j]
    def produce(i, j):
        _produce(k_ref[0, i], qpad_scr[j], j % FLASH_SLOTS, s_scr, mc_scr)

    for j in range(FLASH_AHEAD):
        produce(0, j)

    def sweep(lo, hi, near, bucket):
        def body(i, carry):
            nxt = jnp.minimum(i + 1, n_k - 1)
            for j in range(n_m):
                hh = j // 2
                ahead = j + FLASH_AHEAD
                produce(i if ahead < n_m else nxt, ahead % n_m)
                slot = j % FLASH_SLOTS
                if near:
                    s = s_scr[slot] + bias_ref[i - qi + 1, hh]
                    _consume(s, _col_max(s), 0.0, j, v_ref[0, hh, i], m_scr, acc_scr)
                else:
                    c = t5_ref[bucket * 4 + hh] * LOG2E
                    _consume(s_scr[slot], mc_scr[slot], c, j, v_ref[0, hh, i], m_scr, acc_scr)
            return carry

        lax.fori_loop(lo, hi, body, 0)

    near_lo = jnp.maximum(qi - 1, 0)
    near_hi = jnp.minimum(qi + 2, n_k)
    sweep(0, near_lo, False, T5_BUCKETS // 2 - 1)
    sweep(near_lo, near_hi, True, None)
    sweep(near_hi, n_k, False, T5_BUCKETS - 1)

    lp = lam_ref[...]
    lam = (jnp.exp(jnp.sum(lp[0:1] * lp[1:2], axis=1, keepdims=True))
           - jnp.exp(jnp.sum(lp[2:3] * lp[3:4], axis=1, keepdims=True)) + lambda_init)
    for hh in range(4):
        y = _finish_head(acc_scr[2 * hh]) - lam * _finish_head(acc_scr[2 * hh + 1])
        y = _rms_rows(y, subln_ref[...], HEAD_V) * (1.0 - lambda_init)
        o_ref[0, hh * HEAD_V:(hh + 1) * HEAD_V, :] = y.astype(BF16)


def _flash_diff(t5_flat, qt, k, vt, bias, lam_params, subln_col, lambda_init):
    b, hq, dq, s = qt.shape
    n_k, t = k.shape[1], k.shape[2]
    return pl.pallas_call(
        functools.partial(_flash_diff_kernel, lambda_init=lambda_init, n_k=n_k),
        grid=(b, s // t),
        in_specs=[
            pl.BlockSpec(memory_space=pltpu.SMEM),
            pl.BlockSpec((1, hq, dq, t), lambda bi, qi: (bi, 0, 0, qi)),
            pl.BlockSpec((1, n_k, t, hq * dq), lambda bi, qi: (bi, 0, 0, 0)),
            pl.BlockSpec((1, 4, n_k, V_ROWS, t), lambda bi, qi: (bi, 0, 0, 0, 0)),
            pl.BlockSpec(bias.shape, lambda bi, qi: (0, 0, 0, 0), pipeline_mode=pl.Buffered(1)),
            pl.BlockSpec(lam_params.shape, lambda bi, qi: (0, 0)),
            pl.BlockSpec((HEAD_V, 1), lambda bi, qi: (0, 0)),
        ],
        out_specs=pl.BlockSpec((1, GROUP_W, t), lambda bi, qi: (bi, 0, qi)),
        out_shape=jax.ShapeDtypeStruct((b, GROUP_W, s), BF16),
        scratch_shapes=[pltpu.VMEM((hq, hq * dq, t), BF16),
                        pltpu.VMEM((FLASH_SLOTS, t, t), F32), pltpu.VMEM((FLASH_SLOTS, 1, t), F32),
                        pltpu.VMEM((hq, 1, t), F32), pltpu.VMEM((hq, V_ROWS, t), F32)],
        compiler_params=pltpu.CompilerParams(
            dimension_semantics=("parallel", "parallel"), vmem_limit_bytes=VMEM_LIMIT),
        name="flash_diff",
    )(t5_flat, qt, k, vt, bias, lam_params, subln_col)


def _t5_bias_kernel(tab_ref, o_ref, *, t):
    d = pl.program_id(0) - 1
    hh = pl.program_id(1)
    rows = 64
    half = T5_BUCKETS // 2
    max_exact = half // 2

    def body(r, carry):
        kk = lax.broadcasted_iota(jnp.int32, (rows, t), 0) + r * rows
        qq = lax.broadcasted_iota(jnp.int32, (rows, t), 1)
        rel = d * t + kk - qq
        n = jnp.abs(rel)
        large = max_exact + (jnp.log(jnp.maximum(n, 1).astype(F32) / max_exact)
                             / math.log(T5_MAX_DIST / max_exact) * (half - max_exact)).astype(jnp.int32)
        large = jnp.minimum(large, half - 1)
        bucket = jnp.where(rel > 0, half, 0) + jnp.where(n < max_exact, n, large)
        acc = jnp.zeros((rows, t), F32)
        for bkt in range(T5_BUCKETS):
            acc = jnp.where(bucket == bkt, tab_ref[bkt * 4 + hh], acc)
        o_ref[0, 0, pl.ds(pl.multiple_of(r * rows, rows), rows), :] = acc * LOG2E
        return carry

    lax.fori_loop(0, t // rows, body, 0)


def _t5_bias(t5_flat):
    t = FLASH_T
    return pl.pallas_call(
        functools.partial(_t5_bias_kernel, t=t),
        grid=(3, 4),
        in_specs=[pl.BlockSpec(memory_space=pltpu.SMEM)],
        out_specs=pl.BlockSpec((1, 1, t, t), lambda d, h: (d, h, 0, 0)),
        out_shape=jax.ShapeDtypeStruct((3, 4, t, t), F32),
        name="t5_bias",
    )(t5_flat)


def _na_bias_kernel(rpb_ref, o_ref, *, grid_rows):
    kind = pl.program_id(0)
    hh = pl.program_id(1)
    n_dr, n_dc = 2 * NA_KH - 1, 2 * NA_KW - 1
    r0 = jnp.where(kind == 0, 0, jnp.where(kind == 1, NA_Q_ROWS, grid_rows - NA_Q_ROWS))
    start = jnp.where(kind == 2, grid_rows - NA_K_ROWS, 0)
    shape = (GRID_W, NA_Q_ROWS * GRID_W)
    lane = lax.broadcasted_iota(jnp.int32, shape, 1)
    kc = lax.broadcasted_iota(jnp.int32, shape, 0)
    qc = lane & (GRID_W - 1)
    grp = lane >> 6
    dcm = jnp.clip(kc - qc + NA_KW - 1, 0, n_dc - 1)
    qs = jnp.clip(qc - NA_KW // 2, 0, GRID_W - NA_KW)
    col_ok = (kc >= qs) & (kc < qs + NA_KW)

    def by_group(vals):
        out = vals[NA_Q_ROWS - 1]
        for i in range(NA_Q_ROWS - 2, -1, -1):
            out = jnp.where(grp == i, vals[i], out)
        return out

    def jbody(j, carry):
        krow = start + j
        bases, oks = [], []
        for i in range(NA_Q_ROWS):
            qrow = r0 + i
            lo = jnp.clip(qrow - NA_KH // 2, 0, grid_rows - NA_KH)
            oks.append(((krow >= lo) & (krow < lo + NA_KH)).astype(jnp.int32))
            dr = jnp.clip(krow - qrow + NA_KH - 1, 0, n_dr - 1)
            bases.append((hh * n_dr + dr) * n_dc)

        def dcbody(dc, acc):
            vec = by_group([rpb_ref[bases[i] + dc] for i in range(NA_Q_ROWS)])
            return jnp.where(dcm == dc, vec, acc)

        acc = lax.fori_loop(0, n_dc, dcbody, jnp.zeros(shape, F32))
        ok = col_ok & (by_group(oks) > 0)
        o_ref[0, 0, pl.ds(pl.multiple_of(j * GRID_W, GRID_W), GRID_W), :] = jnp.where(ok, acc * LOG2E, NEG_BIG)
        return carry

    lax.fori_loop(0, NA_K_ROWS, jbody, 0)


def _na_bias(rpb_flat, grid_rows):
    kt, qt = NA_K_ROWS * GRID_W, NA_Q_ROWS * GRID_W
    return pl.pallas_call(
        functools.partial(_na_bias_kernel, grid_rows=grid_rows),
        grid=(3, 4),
        in_specs=[pl.BlockSpec(memory_space=pltpu.SMEM)],
        out_specs=pl.BlockSpec((1, 1, kt, qt), lambda kd, h: (kd, h, 0, 0)),
        out_shape=jax.ShapeDtypeStruct((3, 4, kt, qt), F32),
        name="na_bias",
    )(rpb_flat)


def _na_kernel(q_ref, k0_ref, k1_ref, k2_ref, v0_ref, v1_ref, v2_ref, bias_ref, beta_ref, o_ref):
    qt = NA_Q_ROWS * GRID_W
    k_refs = (k0_ref, k1_ref, k2_ref)
    v_refs = (v0_ref, v1_ref, v2_ref)
    outs = []
    for hh in range(4):
        q = q_ref[0, hh]
        ss = [_dot(k_refs[j][0, hh], q) + bias_ref[0, hh, j * qt:(j + 1) * qt, :] for j in range(3)]
        m = jnp.maximum(jnp.maximum(jnp.max(ss[0], axis=0, keepdims=True),
                                    jnp.max(ss[1], axis=0, keepdims=True)),
                        jnp.max(ss[2], axis=0, keepdims=True))
        acc = None
        for j in range(3):
            p = jnp.exp2(ss[j] - m).astype(BF16)
            pv = _dot(v_refs[j][0, hh, 0], p)
            acc = pv if acc is None else acc + pv
        outs.append(_finish_head(acc))
    y = jnp.concatenate(outs, axis=0)
    o_ref[0] = _rms_rows(y, beta_ref[...], GROUP_W).astype(BF16)


def _na(qt_arr, k, vt, bias, beta_col):
    b, nh, d, s = qt_arr.shape
    qt = NA_Q_ROWS * GRID_W
    kt = NA_K_ROWS * GRID_W
    n_t = s // qt
    n_win = kt // qt

    def win(j):
        return lambda bi, ti: jnp.clip(ti - 1, 0, n_t - n_win) + j

    def kind(bi, ti):
        return (jnp.where(ti == 0, 0, jnp.where(ti == n_t - 1, 2, 1)), 0, 0, 0)

    k_specs = [pl.BlockSpec((1, nh, qt, d), (lambda j: lambda bi, ti: (bi, 0, win(j)(bi, ti), 0))(j))
               for j in range(n_win)]
    per = vt.shape[4] // qt
    v_specs = [pl.BlockSpec((1, nh, 1, V_ROWS, qt),
                            (lambda j: lambda bi, ti: (bi, 0, win(j)(bi, ti) // per, 0, win(j)(bi, ti) % per))(j))
               for j in range(n_win)]
    return pl.pallas_call(
        _na_kernel,
        grid=(b, n_t),
        in_specs=[pl.BlockSpec((1, nh, d, qt), lambda bi, ti: (bi, 0, 0, ti))] + k_specs + v_specs + [
            pl.BlockSpec((1, nh, kt, qt), kind),
            pl.BlockSpec((GROUP_W, 1), lambda bi, ti: (0, 0)),
        ],
        out_specs=pl.BlockSpec((1, GROUP_W, qt), lambda bi, ti: (bi, 0, ti)),
        out_shape=jax.ShapeDtypeStruct((b, GROUP_W, s), BF16),
        compiler_params=pltpu.CompilerParams(
            dimension_semantics=("parallel", "parallel"), vmem_limit_bytes=VMEM_LIMIT),
        name="na_attn",
    )(qt_arr, k, k, k, vt, vt, vt, bias, beta_col)


def _rope_tables(pos, dim):
    inv = jnp.exp(-math.log(ROPE_THETA) * jnp.arange(0, dim, 2, dtype=F32) / dim)
    ang = pos.astype(F32)[:, None] * inv[None, :]
    return jnp.cos(ang).T, jnp.sin(ang).T


def _col(v):
    return v.reshape(-1, 1).astype(F32)


def _lambda_init(layer):
    return 0.8 - 0.6 * math.exp(-0.3 * layer)


def kernel(x, ffn1_norm, ffn1_w_gate, ffn1_w_up, ffn1_w_down, mix_norm, w_in, na_q_norm, na_k_norm, na_rpb, na_beta, mla_q_lat_norm, mla_w_uq, mla_kv_lat_norm, mla_w_ukv, mla_q_norm, mla_k_norm, mla_beta, diff_q_norm, diff_k_norm, diff_lambda, diff_subln, gqa_q_norm, gqa_k_norm, gqa_beta, w_out, ffn2_norm, ffn2_w_gate, ffn2_w_up, ffn2_w_down, final_norm, t5_bias):
    batch, seq, d = x.shape
    grid_rows = seq // GRID_W
    pos = jnp.arange(seq, dtype=jnp.int32)
    tabs = (*_rope_tables(pos, 32), *_rope_tables(pos // GRID_W, 32), *_rope_tables(pos % GRID_W, 32))
    t5_flat = t5_bias.reshape(-1).astype(F32)
    t5_tiles = _t5_bias(t5_flat)

    xf = x.reshape(batch * seq, d)
    for l in range(N_LAYERS):
        xf = _ffn(xf, ffn1_norm[l], ffn1_w_gate[l].astype(BF16), ffn1_w_up[l].astype(BF16),
                  ffn1_w_down[l].astype(BF16))
        cols = tuple(_col(v[l]) for v in (na_q_norm, na_k_norm, mla_q_lat_norm, mla_kv_lat_norm,
                                          mla_q_norm, mla_k_norm, diff_q_norm, diff_k_norm,
                                          gqa_q_norm, gqa_k_norm))
        mats = (mla_w_uq[l].T.astype(BF16), mla_w_ukv[l].T.astype(BF16))
        (qa, ka, va, qb, kb, vb, qc, kc, vc, qd, kd, vd) = _proj(
            xf, batch, seq, mix_norm[l], w_in[l].T.astype(BF16), cols, mats, tabs)
        n_k = seq // FLASH_T
        ka = jnp.swapaxes(ka, 2, 3)
        kb = jnp.swapaxes(kb, 2, 3).reshape(batch, 4, n_k, FLASH_T, 96)
        kc = jnp.swapaxes(kc.reshape(batch, 256, seq), 1, 2).reshape(batch, n_k, FLASH_T, 256)
        kd = jnp.swapaxes(kd, 2, 3).reshape(batch, 2, n_k, FLASH_T, 64)

        ya = _na(qa, ka, va, _na_bias(na_rpb[l].reshape(-1).astype(F32), grid_rows), _col(na_beta[l]))
        yb = _flash_plain(qb, kb, vb, _col(mla_beta[l]), (0, 1, 2, 3))
        yc = _flash_diff(t5_flat, qc, kc, vc, t5_tiles, diff_lambda[l].astype(F32), _col(diff_subln[l]),
                         _lambda_init(l))
        yd = _flash_plain(qd, kd, vd, _col(gqa_beta[l]), (0, 0, 1, 1))

        xf = _ffn(xf, ffn2_norm[l], ffn2_w_gate[l].astype(BF16), ffn2_w_up[l].astype(BF16),
                  ffn2_w_down[l].astype(BF16), mix=((ya, yb, yc, yd), w_out[l].astype(BF16)),
                  final_g=final_norm[l], seq=seq)
    return xf.reshape(batch, seq, d)
```

```python
import functools
import math

import jax
import jax.numpy as jnp
from jax import lax
from jax.experimental import pallas as pl
from jax.experimental.pallas import tpu as pltpu

F32 = jnp.float32
BF16 = jnp.bfloat16

EPS = 1e-6
NEG_BIG = -1e30
LOG2E = 1.4426950408889634
ROPE_THETA = 10000.0
GRID_W = 64
N_LAYERS = 2

HEAD_V = 64
V_ROWS = 80
GROUP_W = 256

NA_KH, NA_KW = 8, 16
NA_Q_ROWS = 4
NA_K_ROWS = 12
T5_BUCKETS = 32
T5_MAX_DIST = 128

FFN_TM = 512
FFN_TF = 1408
FLASH_T = 512
FLASH_AHEAD = 2
FLASH_SLOTS = 4
SAFE_LOG2_RANGE = 100.0
PROJ_TM = FLASH_T

VMEM_LIMIT = 56 * 1024 * 1024


def _rms_rows(x, g_col, n):
    r = lax.rsqrt(jnp.sum(x * x, axis=0, keepdims=True) * (1.0 / n) + EPS)
    return (x * r) * g_col


def _rms_lanes(x, g_row):
    r = lax.rsqrt(jnp.mean(x * x, axis=-1, keepdims=True) + EPS)
    return (x * r) * g_row


def _dot(a, b):
    return jnp.dot(a, b, preferred_element_type=F32)


def _ffn_kernel(*refs, has_mix, has_final, n_f):
    x_ref, g_ref, wg_ref, wu_ref, wd_ref = refs[:5]
    rest = refs[5:]
    if has_mix:
        y_refs, wo_ref, rest = rest[:4], rest[4], rest[5:]
    if has_final:
        gf_ref, rest = rest[0], rest[1:]
    o_ref, h_scr, res_scr, acc_scr = rest
    j = pl.program_id(1)

    @pl.when(j == 0)
    def _():
        x = x_ref[...]
        if has_mix:
            for g in range(4):
                x = x + lax.dot_general(
                    y_refs[g][0], wo_ref[g * GROUP_W:(g + 1) * GROUP_W, :],
                    (((0,), (0,)), ((), ())), preferred_element_type=F32)
        res_scr[...] = x
        h_scr[...] = _rms_lanes(x, g_ref[...]).astype(BF16)
        acc_scr[...] = jnp.zeros_like(acc_scr)

    h = h_scr[...]
    gate = _dot(h, wg_ref[...])
    up = _dot(h, wu_ref[...])
    a = (gate / (1.0 + jnp.exp(-gate))) * up
    acc_scr[...] += _dot(a.astype(BF16), wd_ref[...])

    @pl.when(j == n_f - 1)
    def _():
        out = res_scr[...] + 0.5 * acc_scr[...]
        if has_final:
            out = _rms_lanes(out, gf_ref[...])
        o_ref[...] = out


def _ffn(x, g, wg, wu, wd, mix=None, final_g=None, seq=None):
    n, d = x.shape
    d_ff = wg.shape[1]
    tm, tf = FFN_TM, FFN_TF
    n_f = d_ff // tf
    in_specs = [
        pl.BlockSpec((tm, d), lambda i, j: (i, 0)),
        pl.BlockSpec((1, d), lambda i, j: (0, 0)),
        pl.BlockSpec((d, tf), lambda i, j: (0, j)),
        pl.BlockSpec((d, tf), lambda i, j: (0, j)),
        pl.BlockSpec((tf, d), lambda i, j: (j, 0)),
    ]
    args = [x, g.reshape(1, d), wg, wu, wd]
    if mix is not None:
        ys, wo = mix
        nsb = seq // tm
        for y in ys:
            in_specs.append(pl.BlockSpec((1, GROUP_W, tm), lambda i, j: (i // nsb, 0, i % nsb)))
            args.append(y)
        in_specs.append(pl.BlockSpec(wo.shape, lambda i, j: (0, 0)))
        args.append(wo)
    if final_g is not None:
        in_specs.append(pl.BlockSpec((1, d), lambda i, j: (0, 0)))
        args.append(final_g.reshape(1, d))
    return pl.pallas_call(
        functools.partial(_ffn_kernel, has_mix=mix is not None, has_final=final_g is not None, n_f=n_f),
        grid=(n // tm, n_f),
        in_specs=in_specs,
        out_specs=pl.BlockSpec((tm, d), lambda i, j: (i, 0)),
        out_shape=jax.ShapeDtypeStruct((n, d), F32),
        scratch_shapes=[pltpu.VMEM((tm, d), BF16), pltpu.VMEM((tm, d), F32), pltpu.VMEM((tm, d), F32)],
        compiler_params=pltpu.CompilerParams(
            dimension_semantics=("parallel", "arbitrary"), vmem_limit_bytes=VMEM_LIMIT),
        name="ffn_mix" if mix is not None else "ffn",
    )(*args)


def _rope_rows(x, cos, sin):
    x1, x2 = x[:16], x[16:]
    return x1 * cos - x2 * sin, x1 * sin + x2 * cos


def _store_v(v_ref, hh, v):
    v_ref[0, hh, 0, 0:HEAD_V, :] = v.astype(BF16)
    v_ref[0, hh, 0, HEAD_V:V_ROWS, :] = jnp.ones((V_ROWS - HEAD_V, v.shape[1]), BF16)


def _proj_kernel(x_ref, gmix_ref, wint_ref,
                 naq_ref, nak_ref,
                 qlat_ref, wuqt_ref, kvlat_ref, wukvt_ref, mq_ref, mk_ref,
                 dq_ref, dk_ref, gq_ref, gk_ref,
                 cseq_ref, sseq_ref, crow_ref, srow_ref, ccol_ref, scol_ref,
                 qa_o, ka_o, va_o, qb_o, kb_o, vb_o, qc_o, kc_o, vc_o, qd_o, kd_o, vd_o):
    h = _rms_lanes(x_ref[...], gmix_ref[...]).astype(BF16)

    def proj(lo, hi):
        return lax.dot_general(wint_ref[lo:hi, :], h, (((1,), (1,)), ((), ())),
                               preferred_element_type=F32)

    pa = proj(0, 768)
    sa = LOG2E * 64 ** -0.5
    for hh in range(4):
        q = _rms_rows(pa[hh * 64:(hh + 1) * 64], naq_ref[...], 64) * sa
        k = _rms_rows(pa[256 + hh * 64:256 + (hh + 1) * 64], nak_ref[...], 64)
        qa_o[0, hh] = q.astype(BF16)
        ka_o[0, hh] = k.astype(BF16)
        _store_v(va_o, hh, pa[512 + hh * 64:512 + (hh + 1) * 64])

    pb = proj(768, 1184)
    cq = _rms_rows(pb[0:256], qlat_ref[...], 256).astype(BF16)
    qb = _dot(wuqt_ref[...], cq)
    ckv = _rms_rows(pb[256:384], kvlat_ref[...], 128).astype(BF16)
    kv = _dot(wukvt_ref[...], ckv)
    kr = pb[384:416]
    cseq, sseq = cseq_ref[...], sseq_ref[...]
    sb = LOG2E * 96 ** -0.5
    for hh in range(4):
        q = _rms_rows(qb[hh * 96:(hh + 1) * 96], mq_ref[...], 96) * sb
        r1, r2 = _rope_rows(q[64:96], cseq, sseq)
        qb_o[0, hh, 0:64, :] = q[0:64].astype(BF16)
        qb_o[0, hh, 64:80, :] = r1.astype(BF16)
        qb_o[0, hh, 80:96, :] = r2.astype(BF16)
        k = jnp.concatenate([kv[hh * 128:hh * 128 + 64], kr], axis=0)
        k = _rms_rows(k, mk_ref[...], 96)
        r1, r2 = _rope_rows(k[64:96], cseq, sseq)
        kb_o[0, hh, 0:64, :] = k[0:64].astype(BF16)
        kb_o[0, hh, 64:80, :] = r1.astype(BF16)
        kb_o[0, hh, 80:96, :] = r2.astype(BF16)
        _store_v(vb_o, hh, kv[hh * 128 + 64:hh * 128 + 128])

    pc = proj(1184, 1952)
    sc = LOG2E * 32 ** -0.5
    for j in range(8):
        q = _rms_rows(pc[j * 32:(j + 1) * 32], dq_ref[...], 32) * sc
        k = _rms_rows(pc[256 + j * 32:256 + (j + 1) * 32], dk_ref[...], 32)
        qc_o[0, j] = q.astype(BF16)
        kc_o[0, j] = k.astype(BF16)
    for hh in range(4):
        _store_v(vc_o, hh, pc[512 + hh * 64:512 + (hh + 1) * 64])

    pd = proj(1952, 2464)
    crow, srow, ccol, scol = crow_ref[...], srow_ref[...], ccol_ref[...], scol_ref[...]
    sd = LOG2E * 64 ** -0.5

    def axial(x, o_ref, hh):
        a1, a2 = _rope_rows(x[0:32], crow, srow)
        b1, b2 = _rope_rows(x[32:64], ccol, scol)
        o_ref[0, hh, 0:16, :] = a1.astype(BF16)
        o_ref[0, hh, 16:32, :] = a2.astype(BF16)
        o_ref[0, hh, 32:48, :] = b1.astype(BF16)
        o_ref[0, hh, 48:64, :] = b2.astype(BF16)

    for hh in range(4):
        axial(_rms_rows(pd[hh * 64:(hh + 1) * 64], gq_ref[...], 64) * sd, qd_o, hh)
    for hh in range(2):
        axial(_rms_rows(pd[256 + hh * 64:256 + (hh + 1) * 64], gk_ref[...], 64), kd_o, hh)
        _store_v(vd_o, hh, pd[384 + hh * 64:384 + (hh + 1) * 64])


def _proj(x, batch, seq, gmix, wint, cols, mats, tabs):
    n, d = x.shape
    tm = PROJ_TM
    nsb = seq // tm
    naq, nak, qlat, kvlat, mq, mk, dq, dk, gq, gk = cols
    wuqt, wukvt = mats

    def full(a):
        return pl.BlockSpec(a.shape, lambda i: (0,) * a.ndim)

    tab_spec = pl.BlockSpec((16, tm), lambda i: (0, i % nsb))
    args = [x, gmix.reshape(1, d), wint, naq, nak, qlat, wuqt, kvlat, wukvt, mq, mk, dq, dk, gq, gk, *tabs]
    in_specs = [pl.BlockSpec((tm, d), lambda i: (i, 0))] + [full(a) for a in args[1:15]] + [tab_spec] * 6

    def head_out(nh, rows):
        return (jax.ShapeDtypeStruct((batch, nh, rows, seq), BF16),
                pl.BlockSpec((1, nh, rows, tm), lambda i: (i // nsb, 0, 0, i % nsb)))

    def v_out(nh):
        return (jax.ShapeDtypeStruct((batch, nh, nsb, V_ROWS, tm), BF16),
                pl.BlockSpec((1, nh, 1, V_ROWS, tm), lambda i: (i // nsb, 0, i % nsb, 0, 0)))

    outs = [head_out(4, 64), head_out(4, 64), v_out(4),
            head_out(4, 96), head_out(4, 96), v_out(4),
            head_out(8, 32), head_out(8, 32), v_out(4),
            head_out(4, 64), head_out(2, 64), v_out(2)]
    return pl.pallas_call(
        _proj_kernel,
        grid=(n // tm,),
        in_specs=in_specs,
        out_specs=[o[1] for o in outs],
        out_shape=[o[0] for o in outs],
        compiler_params=pltpu.CompilerParams(
            dimension_semantics=("parallel",), vmem_limit_bytes=VMEM_LIMIT),
        name="mix_proj",
    )(*args)


def _col_max(s):
    parts = [s]
    while parts[0].shape[0] > 64:
        half = parts[0].shape[0] // 2
        parts = [jnp.maximum(p[:half], p[half:]) for p in parts]
    return jnp.max(parts[0], axis=0, keepdims=True)


def _produce(k_blk, q_blk, slot, s_scr, mc_scr):
    s = _dot(k_blk, q_blk)
    s_scr[slot] = s
    mc_scr[slot] = _col_max(s)


def _consume(s, m_cur, c, idx, v_blk, m_scr, acc_scr):
    m_prev = m_scr[idx]
    m_new = jnp.maximum(m_prev, m_cur + c)
    alpha = jnp.exp2(m_prev - m_new)
    p = jnp.exp2(s - (m_new - c)).astype(BF16)
    acc_scr[idx] = alpha * acc_scr[idx] + _dot(v_blk, p)
    m_scr[idx] = m_new


def _accumulate(s, r, idx, v_blk, acc_scr):
    p = jnp.exp2(s - r).astype(BF16)
    acc_scr[idx] += _dot(v_blk, p)


def _finish_head(acc):
    return acc[0:HEAD_V] / acc[HEAD_V:HEAD_V + 1]


def _key_abs_max(tile_fn, n_k, width):
    def body(i, m):
        return jnp.maximum(m, jnp.max(jnp.abs(tile_fn(i).astype(F32)), axis=0, keepdims=True))

    m = lax.fori_loop(0, n_k, body, jnp.zeros((1, width), F32))
    return jnp.broadcast_to(m, (8, width)).astype(BF16)


def _logit_bound(kmax8, q_blk):
    return _dot(kmax8, jnp.abs(q_blk))[0:1] + 1.0


def _flash_plain_kernel(q_ref, k_ref, v_ref, beta_ref, o_ref, kmax_scr, s_scr, mc_scr, m_scr, acc_scr,
                        *, kv_map, n_k):
    n_h = len(kv_map)
    dq = q_ref.shape[2]

    @pl.when(pl.program_id(1) == 0)
    def _():
        for kh in sorted(set(kv_map)):
            kmax_scr[kh] = _key_abs_max(lambda i, kh=kh: k_ref[0, kh, i], n_k, dq)

    acc_scr[...] = jnp.zeros_like(acc_scr)
    bounds = [_logit_bound(kmax_scr[kv_map[hh]], q_ref[0, hh]) for hh in range(n_h)]
    worst = functools.reduce(jnp.maximum, bounds)
    safe = jnp.max(worst) < SAFE_LOG2_RANGE

    @pl.when(safe)
    def _():
        offs = [b - SAFE_LOG2_RANGE for b in bounds]

        def body(i, carry):
            s_cur = _dot(k_ref[0, kv_map[0], i], q_ref[0, 0])
            for hh in range(n_h):
                if hh + 1 < n_h:
                    s_nxt = _dot(k_ref[0, kv_map[hh + 1], i], q_ref[0, hh + 1])
                _accumulate(s_cur, offs[hh], hh, v_ref[0, kv_map[hh], i], acc_scr)
                s_cur = s_nxt
            return carry

        lax.fori_loop(0, n_k, body, 0)

    @pl.when(jnp.logical_not(safe))
    def _():
        m_scr[...] = jnp.full_like(m_scr, NEG_BIG)

        def produce(i, hh):
            _produce(k_ref[0, kv_map[hh], i], q_ref[0, hh], hh % FLASH_SLOTS, s_scr, mc_scr)

        for hh in range(FLASH_AHEAD):
            produce(0, hh)

        def body(i, carry):
            nxt = jnp.minimum(i + 1, n_k - 1)
            for hh in range(n_h):
                ahead = hh + FLASH_AHEAD
                produce(i if ahead < n_h else nxt, ahead % n_h)
                slot = hh % FLASH_SLOTS
                _consume(s_scr[slot], mc_scr[slot], 0.0, hh, v_ref[0, kv_map[hh], i], m_scr, acc_scr)
            return carry

        lax.fori_loop(0, n_k, body, 0)

    y = jnp.concatenate([_finish_head(acc_scr[hh]) for hh in range(n_h)], axis=0)
    o_ref[0] = _rms_rows(y, beta_ref[...], GROUP_W).astype(BF16)


def _flash_plain(qt, k, vt, beta_col, kv_map):
    b, hq, dq, s = qt.shape
    hk, n_k, t = k.shape[1], k.shape[2], k.shape[3]
    assert len(kv_map) % FLASH_SLOTS == 0
    return pl.pallas_call(
        functools.partial(_flash_plain_kernel, kv_map=kv_map, n_k=n_k),
        grid=(b, s // t),
        in_specs=[
            pl.BlockSpec((1, hq, dq, t), lambda bi, qi: (bi, 0, 0, qi)),
            pl.BlockSpec((1, hk, n_k, t, dq), lambda bi, qi: (bi, 0, 0, 0, 0)),
            pl.BlockSpec((1, hk, n_k, V_ROWS, t), lambda bi, qi: (bi, 0, 0, 0, 0)),
            pl.BlockSpec((GROUP_W, 1), lambda bi, qi: (0, 0)),
        ],
        out_specs=pl.BlockSpec((1, GROUP_W, t), lambda bi, qi: (bi, 0, qi)),
        out_shape=jax.ShapeDtypeStruct((b, GROUP_W, s), BF16),
        scratch_shapes=[pltpu.VMEM((hk, 8, dq), BF16),
                        pltpu.VMEM((FLASH_SLOTS, t, t), F32), pltpu.VMEM((FLASH_SLOTS, 1, t), F32),
                        pltpu.VMEM((hq, 1, t), F32), pltpu.VMEM((hq, V_ROWS, t), F32)],
        compiler_params=pltpu.CompilerParams(
            dimension_semantics=("arbitrary", "arbitrary"), vmem_limit_bytes=VMEM_LIMIT),
        name="flash_plain",
    )(qt, k, vt, beta_col)


def _flash_diff_kernel(t5_ref, q_ref, k_ref, v_ref, bias_ref, lam_ref, subln_ref, o_ref,
                       kmax_scr, qpad_scr, s_scr, mc_scr, m_scr, acc_scr, *, lambda_init, n_k):
    qi = pl.program_id(1)
    n_m = 8
    dq = q_ref.shape[2]

    @pl.when(qi == 0)
    def _():
        kmax_scr[...] = _key_abs_max(lambda i: k_ref[0, i], n_k, n_m * dq)

    acc_scr[...] = jnp.zeros_like(acc_scr)
    qpad_scr[...] = jnp.zeros_like(qpad_scr)
    for j in range(n_m):
        qpad_scr[j, j * dq:(j + 1) * dq, :] = q_ref[0, j]

    bias_max = lax.fori_loop(0, T5_BUCKETS * 4, lambda i, m: jnp.maximum(m, jnp.abs(t5_ref[i])), 0.0) * LOG2E
    bounds = [_logit_bound(kmax_scr[...], qpad_scr[j]) + bias_max for j in range(n_m)]
    worst = functools.reduce(jnp.maximum, bounds)
    safe = jnp.max(worst) < SAFE_LOG2_RANGE
    near_lo = jnp.maximum(qi - 1, 0)
    near_hi = jnp.minimum(qi + 2, n_k)
    far_left, far_right = T5_BUCKETS // 2 - 1, T5_BUCKETS - 1

    @pl.when(safe)
    def _():
        offs = [b - SAFE_LOG2_RANGE for b in bounds]

        def sweep(lo, hi, bucket):
            def body(i, carry):
                s_cur = _dot(k_ref[0, i], qpad_scr[0])
                for j in range(n_m):
                    hh = j // 2
                    if j + 1 < n_m:
                        s_nxt = _dot(k_ref[0, i], qpad_scr[j + 1])
                    if bucket is None:
                        _accumulate(s_cur + bias_ref[i - qi + 1, hh], offs[j], j, v_ref[0, hh, i], acc_scr)
                    else:
                        c = t5_ref[bucket * 4 + hh] * LOG2E
                        _accumulate(s_cur, offs[j] - c, j, v_ref[0, hh, i], acc_scr)
                    s_cur = s_nxt
                return carry

            lax.fori_loop(lo, hi, body, 0)

        sweep(0, near_lo, far_left)
        sweep(near_lo, near_hi, None)
        sweep(near_hi, n_k, far_right)

    @pl.when(jnp.logical_not(safe))
    def _():
        m_scr[...] = jnp.full_like(m_scr, NEG_BIG)

        def produce(i, j):
            _produce(k_ref[0, i], qpad_scr[j], j % FLASH_SLOTS, s_scr, mc_scr)

        for j in range(FLASH_AHEAD):
            produce(0, j)

        def sweep(lo, hi, bucket):
            def body(i, carry):
                nxt = jnp.minimum(i + 1, n_k - 1)
                for j in range(n_m):
                    hh = j // 2
                    ahead = j + FLASH_AHEAD
                    produce(i if ahead < n_m else nxt, ahead % n_m)
                    slot = j % FLASH_SLOTS
                    if bucket is None:
                        s = s_scr[slot] + bias_ref[i - qi + 1, hh]
                        _consume(s, _col_max(s), 0.0, j, v_ref[0, hh, i], m_scr, acc_scr)
                    else:
                        c = t5_ref[bucket * 4 + hh] * LOG2E
                        _consume(s_scr[slot], mc_scr[slot], c, j, v_ref[0, hh, i], m_scr, acc_scr)
                return carry

            lax.fori_loop(lo, hi, body, 0)

        sweep(0, near_lo, far_left)
        sweep(near_lo, near_hi, None)
        sweep(near_hi, n_k, far_right)

    lp = lam_ref[...]
    lam = (jnp.exp(jnp.sum(lp[0:1] * lp[1:2], axis=1, keepdims=True))
           - jnp.exp(jnp.sum(lp[2:3] * lp[3:4], axis=1, keepdims=True)) + lambda_init)
    for hh in range(4):
        y = _finish_head(acc_scr[2 * hh]) - lam * _finish_head(acc_scr[2 * hh + 1])
        y = _rms_rows(y, subln_ref[...], HEAD_V) * (1.0 - lambda_init)
        o_ref[0, hh * HEAD_V:(hh + 1) * HEAD_V, :] = y.astype(BF16)


def _flash_diff(t5_flat, qt, k, vt, bias, lam_params, subln_col, lambda_init):
    b, hq, dq, s = qt.shape
    n_k, t = k.shape[1], k.shape[2]
    return pl.pallas_call(
        functools.partial(_flash_diff_kernel, lambda_init=lambda_init, n_k=n_k),
        grid=(b, s // t),
        in_specs=[
            pl.BlockSpec(memory_space=pltpu.SMEM),
            pl.BlockSpec((1, hq, dq, t), lambda bi, qi: (bi, 0, 0, qi)),
            pl.BlockSpec((1, n_k, t, hq * dq), lambda bi, qi: (bi, 0, 0, 0)),
            pl.BlockSpec((1, 4, n_k, V_ROWS, t), lambda bi, qi: (bi, 0, 0, 0, 0)),
            pl.BlockSpec(bias.shape, lambda bi, qi: (0, 0, 0, 0), pipeline_mode=pl.Buffered(1)),
            pl.BlockSpec(lam_params.shape, lambda bi, qi: (0, 0)),
            pl.BlockSpec((HEAD_V, 1), lambda bi, qi: (0, 0)),
        ],
        out_specs=pl.BlockSpec((1, GROUP_W, t), lambda bi, qi: (bi, 0, qi)),
        out_shape=jax.ShapeDtypeStruct((b, GROUP_W, s), BF16),
        scratch_shapes=[pltpu.VMEM((8, hq * dq), BF16), pltpu.VMEM((hq, hq * dq, t), BF16),
                        pltpu.VMEM((FLASH_SLOTS, t, t), F32), pltpu.VMEM((FLASH_SLOTS, 1, t), F32),
                        pltpu.VMEM((hq, 1, t), F32), pltpu.VMEM((hq, V_ROWS, t), F32)],
        compiler_params=pltpu.CompilerParams(
            dimension_semantics=("arbitrary", "arbitrary"), vmem_limit_bytes=VMEM_LIMIT),
        name="flash_diff",
    )(t5_flat, qt, k, vt, bias, lam_params, subln_col)


def _t5_bias_kernel(tab_ref, o_ref, *, t):
    d = pl.program_id(0) - 1
    hh = pl.program_id(1)
    rows = 64
    half = T5_BUCKETS // 2
    max_exact = half // 2

    def body(r, carry):
        kk = lax.broadcasted_iota(jnp.int32, (rows, t), 0) + r * rows
        qq = lax.broadcasted_iota(jnp.int32, (rows, t), 1)
        rel = d * t + kk - qq
        n = jnp.abs(rel)
        large = max_exact + (jnp.log(jnp.maximum(n, 1).astype(F32) / max_exact)
                             / math.log(T5_MAX_DIST / max_exact) * (half - max_exact)).astype(jnp.int32)
        large = jnp.minimum(large, half - 1)
        bucket = jnp.where(rel > 0, half, 0) + jnp.where(n < max_exact, n, large)
        acc = jnp.zeros((rows, t), F32)
        for bkt in range(T5_BUCKETS):
            acc = jnp.where(bucket == bkt, tab_ref[bkt * 4 + hh], acc)
        o_ref[0, 0, pl.ds(pl.multiple_of(r * rows, rows), rows), :] = acc * LOG2E
        return carry

    lax.fori_loop(0, t // rows, body, 0)


def _t5_bias(t5_flat):
    t = FLASH_T
    return pl.pallas_call(
        functools.partial(_t5_bias_kernel, t=t),
        grid=(3, 4),
        in_specs=[pl.BlockSpec(memory_space=pltpu.SMEM)],
        out_specs=pl.BlockSpec((1, 1, t, t), lambda d, h: (d, h, 0, 0)),
        out_shape=jax.ShapeDtypeStruct((3, 4, t, t), F32),
        name="t5_bias",
    )(t5_flat)


def _na_bias_kernel(rpb_ref, o_ref, *, grid_rows):
    kind = pl.program_id(0)
    hh = pl.program_id(1)
    n_dr, n_dc = 2 * NA_KH - 1, 2 * NA_KW - 1
    r0 = jnp.where(kind == 0, 0, jnp.where(kind == 1, NA_Q_ROWS, grid_rows - NA_Q_ROWS))
    start = jnp.where(kind == 2, grid_rows - NA_K_ROWS, 0)
    shape = (GRID_W, NA_Q_ROWS * GRID_W)
    lane = lax.broadcasted_iota(jnp.int32, shape, 1)
    kc = lax.broadcasted_iota(jnp.int32, shape, 0)
    qc = lane & (GRID_W - 1)
    grp = lane >> 6
    dcm = jnp.clip(kc - qc + NA_KW - 1, 0, n_dc - 1)
    qs = jnp.clip(qc - NA_KW // 2, 0, GRID_W - NA_KW)
    col_ok = (kc >= qs) & (kc < qs + NA_KW)

    def by_group(vals):
        out = vals[NA_Q_ROWS - 1]
        for i in range(NA_Q_ROWS - 2, -1, -1):
            out = jnp.where(grp == i, vals[i], out)
        return out

    def jbody(j, carry):
        krow = start + j
        bases, oks = [], []
        for i in range(NA_Q_ROWS):
            qrow = r0 + i
            lo = jnp.clip(qrow - NA_KH // 2, 0, grid_rows - NA_KH)
            oks.append(((krow >= lo) & (krow < lo + NA_KH)).astype(jnp.int32))
            dr = jnp.clip(krow - qrow + NA_KH - 1, 0, n_dr - 1)
            bases.append((hh * n_dr + dr) * n_dc)

        def dcbody(dc, acc):
            vec = by_group([rpb_ref[bases[i] + dc] for i in range(NA_Q_ROWS)])
            return jnp.where(dcm == dc, vec, acc)

        acc = lax.fori_loop(0, n_dc, dcbody, jnp.zeros(shape, F32))
        ok = col_ok & (by_group(oks) > 0)
        o_ref[0, 0, pl.ds(pl.multiple_of(j * GRID_W, GRID_W), GRID_W), :] = jnp.where(ok, acc * LOG2E, NEG_BIG)
        return carry

    lax.fori_loop(0, NA_K_ROWS, jbody, 0)


def _na_bias(rpb_flat, grid_rows):
    kt, qt = NA_K_ROWS * GRID_W, NA_Q_ROWS * GRID_W
    return pl.pallas_call(
        functools.partial(_na_bias_kernel, grid_rows=grid_rows),
        grid=(3, 4),
        in_specs=[pl.BlockSpec(memory_space=pltpu.SMEM)],
        out_specs=pl.BlockSpec((1, 1, kt, qt), lambda kd, h: (kd, h, 0, 0)),
        out_shape=jax.ShapeDtypeStruct((3, 4, kt, qt), F32),
        name="na_bias",
    )(rpb_flat)


def _na_kernel(q_ref, k0_ref, k1_ref, k2_ref, v0_ref, v1_ref, v2_ref, bias_ref, beta_ref, o_ref):
    qt = NA_Q_ROWS * GRID_W
    k_refs = (k0_ref, k1_ref, k2_ref)
    v_refs = (v0_ref, v1_ref, v2_ref)
    outs = []
    for hh in range(4):
        q = q_ref[0, hh]
        ss = [_dot(k_refs[j][0, hh], q) + bias_ref[0, hh, j * qt:(j + 1) * qt, :] for j in range(3)]
        m = jnp.maximum(jnp.maximum(jnp.max(ss[0], axis=0, keepdims=True),
                                    jnp.max(ss[1], axis=0, keepdims=True)),
                        jnp.max(ss[2], axis=0, keepdims=True))
        acc = None
        for j in range(3):
            p = jnp.exp2(ss[j] - m).astype(BF16)
            pv = _dot(v_refs[j][0, hh, 0], p)
            acc = pv if acc is None else acc + pv
        outs.append(_finish_head(acc))
    y = jnp.concatenate(outs, axis=0)
    o_ref[0] = _rms_rows(y, beta_ref[...], GROUP_W).astype(BF16)


def _na(qt_arr, k, vt, bias, beta_col):
    b, nh, d, s = qt_arr.shape
    qt = NA_Q_ROWS * GRID_W
    kt = NA_K_ROWS * GRID_W
    n_t = s // qt
    n_win = kt // qt

    def win(j):
        return lambda bi, ti: jnp.clip(ti - 1, 0, n_t - n_win) + j

    def kind(bi, ti):
        return (jnp.where(ti == 0, 0, jnp.where(ti == n_t - 1, 2, 1)), 0, 0, 0)

    k_specs = [pl.BlockSpec((1, nh, qt, d), (lambda j: lambda bi, ti: (bi, 0, win(j)(bi, ti), 0))(j))
               for j in range(n_win)]
    per = vt.shape[4] // qt
    v_specs = [pl.BlockSpec((1, nh, 1, V_ROWS, qt),
                            (lambda j: lambda bi, ti: (bi, 0, win(j)(bi, ti) // per, 0, win(j)(bi, ti) % per))(j))
               for j in range(n_win)]
    return pl.pallas_call(
        _na_kernel,
        grid=(b, n_t),
        in_specs=[pl.BlockSpec((1, nh, d, qt), lambda bi, ti: (bi, 0, 0, ti))] + k_specs + v_specs + [
            pl.BlockSpec((1, nh, kt, qt), kind),
            pl.BlockSpec((GROUP_W, 1), lambda bi, ti: (0, 0)),
        ],
        out_specs=pl.BlockSpec((1, GROUP_W, qt), lambda bi, ti: (bi, 0, ti)),
        out_shape=jax.ShapeDtypeStruct((b, GROUP_W, s), BF16),
        compiler_params=pltpu.CompilerParams(
            dimension_semantics=("parallel", "parallel"), vmem_limit_bytes=VMEM_LIMIT),
        name="na_attn",
    )(qt_arr, k, k, k, vt, vt, vt, bias, beta_col)


def _rope_tables(pos, dim):
    inv = jnp.exp(-math.log(ROPE_THETA) * jnp.arange(0, dim, 2, dtype=F32) / dim)
    ang = pos.astype(F32)[:, None] * inv[None, :]
    return jnp.cos(ang).T, jnp.sin(ang).T


def _col(v):
    return v.reshape(-1, 1).astype(F32)


def _lambda_init(layer):
    return 0.8 - 0.6 * math.exp(-0.3 * layer)


def kernel(x, ffn1_norm, ffn1_w_gate, ffn1_w_up, ffn1_w_down, mix_norm, w_in, na_q_norm, na_k_norm, na_rpb, na_beta, mla_q_lat_norm, mla_w_uq, mla_kv_lat_norm, mla_w_ukv, mla_q_norm, mla_k_norm, mla_beta, diff_q_norm, diff_k_norm, diff_lambda, diff_subln, gqa_q_norm, gqa_k_norm, gqa_beta, w_out, ffn2_norm, ffn2_w_gate, ffn2_w_up, ffn2_w_down, final_norm, t5_bias):
    batch, seq, d = x.shape
    grid_rows = seq // GRID_W
    pos = jnp.arange(seq, dtype=jnp.int32)
    tabs = (*_rope_tables(pos, 32), *_rope_tables(pos // GRID_W, 32), *_rope_tables(pos % GRID_W, 32))
    t5_flat = t5_bias.reshape(-1).astype(F32)
    t5_tiles = _t5_bias(t5_flat)

    xf = x.reshape(batch * seq, d)
    for l in range(N_LAYERS):
        xf = _ffn(xf, ffn1_norm[l], ffn1_w_gate[l].astype(BF16), ffn1_w_up[l].astype(BF16),
                  ffn1_w_down[l].astype(BF16))
        cols = tuple(_col(v[l]) for v in (na_q_norm, na_k_norm, mla_q_lat_norm, mla_kv_lat_norm,
                                          mla_q_norm, mla_k_norm, diff_q_norm, diff_k_norm,
                                          gqa_q_norm, gqa_k_norm))
        mats = (mla_w_uq[l].T.astype(BF16), mla_w_ukv[l].T.astype(BF16))
        (qa, ka, va, qb, kb, vb, qc, kc, vc, qd, kd, vd) = _proj(
            xf, batch, seq, mix_norm[l], w_in[l].T.astype(BF16), cols, mats, tabs)
        n_k = seq // FLASH_T
        ka = jnp.swapaxes(ka, 2, 3)
        kb = jnp.swapaxes(kb, 2, 3).reshape(batch, 4, n_k, FLASH_T, 96)
        kc = jnp.swapaxes(kc.reshape(batch, 256, seq), 1, 2).reshape(batch, n_k, FLASH_T, 256)
        kd = jnp.swapaxes(kd, 2, 3).reshape(batch, 2, n_k, FLASH_T, 64)

        ya = _na(qa, ka, va, _na_bias(na_rpb[l].reshape(-1).astype(F32), grid_rows), _col(na_beta[l]))
        yb = _flash_plain(qb, kb, vb, _col(mla_beta[l]), (0, 1, 2, 3))
        yc = _flash_diff(t5_flat, qc, kc, vc, t5_tiles, diff_lambda[l].astype(F32), _col(diff_subln[l]),
                         _lambda_init(l))
        yd = _flash_plain(qd, kd, vd, _col(gqa_beta[l]), (0, 0, 1, 1))

        xf = _ffn(xf, ffn2_norm[l], ffn2_w_gate[l].astype(BF16), ffn2_w_up[l].astype(BF16),
                  ffn2_w_down[l].astype(BF16), mix=((ya, yb, yc, yd), w_out[l].astype(BF16)),
                  final_g=final_norm[l], seq=seq)
    return xf.reshape(batch, seq, d)
```

```python
import functools
import math

import jax
import jax.numpy as jnp
from jax import lax
from jax.experimental import pallas as pl
from jax.experimental.pallas import tpu as pltpu

F32 = jnp.float32
BF16 = jnp.bfloat16

EPS = 1e-6
NEG_BIG = -1e30
LOG2E = 1.4426950408889634
ROPE_THETA = 10000.0
GRID_W = 64
N_LAYERS = 2

HEAD_V = 64
V_ROWS = 80
GROUP_W = 256

NA_KH, NA_KW = 8, 16
NA_Q_ROWS = 4
NA_K_ROWS = 12
T5_BUCKETS = 32
T5_MAX_DIST = 128

FFN_TM = 512
FFN_TF = 1408
FLASH_T = 512
FLASH_AHEAD = 2
FLASH_SLOTS = 4
PROJ_TM = FLASH_T

VMEM_LIMIT = 56 * 1024 * 1024


def _rms_rows(x, g_col, n):
    r = lax.rsqrt(jnp.sum(x * x, axis=0, keepdims=True) * (1.0 / n) + EPS)
    return (x * r) * g_col


def _rms_lanes(x, g_row):
    r = lax.rsqrt(jnp.mean(x * x, axis=-1, keepdims=True) + EPS)
    return (x * r) * g_row


def _dot(a, b):
    return jnp.dot(a, b, preferred_element_type=F32)


def _ffn_kernel(*refs, has_mix, has_final, n_f):
    x_ref, g_ref, wg_ref, wu_ref, wd_ref = refs[:5]
    rest = refs[5:]
    if has_mix:
        y_refs, wo_ref, rest = rest[:4], rest[4], rest[5:]
    if has_final:
        gf_ref, rest = rest[0], rest[1:]
    o_ref, h_scr, res_scr, acc_scr = rest
    j = pl.program_id(1)

    @pl.when(j == 0)
    def _():
        x = x_ref[...]
        if has_mix:
            for g in range(4):
                x = x + lax.dot_general(
                    y_refs[g][0], wo_ref[g * GROUP_W:(g + 1) * GROUP_W, :],
                    (((0,), (0,)), ((), ())), preferred_element_type=F32)
        res_scr[...] = x
        h_scr[...] = _rms_lanes(x, g_ref[...]).astype(BF16)
        acc_scr[...] = jnp.zeros_like(acc_scr)

    h = h_scr[...]
    gate = _dot(h, wg_ref[...])
    up = _dot(h, wu_ref[...])
    a = (gate / (1.0 + jnp.exp(-gate))) * up
    acc_scr[...] += _dot(a.astype(BF16), wd_ref[...])

    @pl.when(j == n_f - 1)
    def _():
        out = res_scr[...] + 0.5 * acc_scr[...]
        if has_final:
            out = _rms_lanes(out, gf_ref[...])
        o_ref[...] = out


def _ffn(x, g, wg, wu, wd, mix=None, final_g=None, seq=None):
    n, d = x.shape
    d_ff = wg.shape[1]
    tm, tf = FFN_TM, FFN_TF
    n_f = d_ff // tf
    in_specs = [
        pl.BlockSpec((tm, d), lambda i, j: (i, 0)),
        pl.BlockSpec((1, d), lambda i, j: (0, 0)),
        pl.BlockSpec((d, tf), lambda i, j: (0, j)),
        pl.BlockSpec((d, tf), lambda i, j: (0, j)),
        pl.BlockSpec((tf, d), lambda i, j: (j, 0)),
    ]
    args = [x, g.reshape(1, d), wg, wu, wd]
    if mix is not None:
        ys, wo = mix
        nsb = seq // tm
        for y in ys:
            in_specs.append(pl.BlockSpec((1, GROUP_W, tm), lambda i, j: (i // nsb, 0, i % nsb)))
            args.append(y)
        in_specs.append(pl.BlockSpec(wo.shape, lambda i, j: (0, 0)))
        args.append(wo)
    if final_g is not None:
        in_specs.append(pl.BlockSpec((1, d), lambda i, j: (0, 0)))
        args.append(final_g.reshape(1, d))
    return pl.pallas_call(
        functools.partial(_ffn_kernel, has_mix=mix is not None, has_final=final_g is not None, n_f=n_f),
        grid=(n // tm, n_f),
        in_specs=in_specs,
        out_specs=pl.BlockSpec((tm, d), lambda i, j: (i, 0)),
        out_shape=jax.ShapeDtypeStruct((n, d), F32),
        scratch_shapes=[pltpu.VMEM((tm, d), BF16), pltpu.VMEM((tm, d), F32), pltpu.VMEM((tm, d), F32)],
        compiler_params=pltpu.CompilerParams(
            dimension_semantics=("parallel", "arbitrary"), vmem_limit_bytes=VMEM_LIMIT),
        name="ffn_mix" if mix is not None else "ffn",
    )(*args)


def _rope_rows(x, cos, sin):
    x1, x2 = x[:16], x[16:]
    return x1 * cos - x2 * sin, x1 * sin + x2 * cos


def _store_v(v_ref, hh, v):
    v_ref[0, hh, 0, 0:HEAD_V, :] = v.astype(BF16)
    v_ref[0, hh, 0, HEAD_V:V_ROWS, :] = jnp.ones((V_ROWS - HEAD_V, v.shape[1]), BF16)


def _proj_kernel(x_ref, gmix_ref, wint_ref,
                 naq_ref, nak_ref,
                 qlat_ref, wuqt_ref, kvlat_ref, wukvt_ref, mq_ref, mk_ref,
                 dq_ref, dk_ref, gq_ref, gk_ref,
                 cseq_ref, sseq_ref, crow_ref, srow_ref, ccol_ref, scol_ref,
                 qa_o, ka_o, va_o, qb_o, kb_o, vb_o, qc_o, kc_o, vc_o, qd_o, kd_o, vd_o):
    h = _rms_lanes(x_ref[...], gmix_ref[...]).astype(BF16)

    def proj(lo, hi):
        return lax.dot_general(wint_ref[lo:hi, :], h, (((1,), (1,)), ((), ())),
                               preferred_element_type=F32)

    pa = proj(0, 768)
    sa = LOG2E * 64 ** -0.5
    for hh in range(4):
        q = _rms_rows(pa[hh * 64:(hh + 1) * 64], naq_ref[...], 64) * sa
        k = _rms_rows(pa[256 + hh * 64:256 + (hh + 1) * 64], nak_ref[...], 64)
        qa_o[0, hh] = q.astype(BF16)
        ka_o[0, hh] = k.astype(BF16)
        _store_v(va_o, hh, pa[512 + hh * 64:512 + (hh + 1) * 64])

    pb = proj(768, 1184)
    cq = _rms_rows(pb[0:256], qlat_ref[...], 256).astype(BF16)
    qb = _dot(wuqt_ref[...], cq)
    ckv = _rms_rows(pb[256:384], kvlat_ref[...], 128).astype(BF16)
    kv = _dot(wukvt_ref[...], ckv)
    kr = pb[384:416]
    cseq, sseq = cseq_ref[...], sseq_ref[...]
    sb = LOG2E * 96 ** -0.5
    for hh in range(4):
        q = _rms_rows(qb[hh * 96:(hh + 1) * 96], mq_ref[...], 96) * sb
        r1, r2 = _rope_rows(q[64:96], cseq, sseq)
        qb_o[0, hh, 0:64, :] = q[0:64].astype(BF16)
        qb_o[0, hh, 64:80, :] = r1.astype(BF16)
        qb_o[0, hh, 80:96, :] = r2.astype(BF16)
        k = jnp.concatenate([kv[hh * 128:hh * 128 + 64], kr], axis=0)
        k = _rms_rows(k, mk_ref[...], 96)
        r1, r2 = _rope_rows(k[64:96], cseq, sseq)
        kb_o[0, hh, 0:64, :] = k[0:64].astype(BF16)
        kb_o[0, hh, 64:80, :] = r1.astype(BF16)
        kb_o[0, hh, 80:96, :] = r2.astype(BF16)
        _store_v(vb_o, hh, kv[hh * 128 + 64:hh * 128 + 128])

    pc = proj(1184, 1952)
    sc = LOG2E * 32 ** -0.5
    for j in range(8):
        q = _rms_rows(pc[j * 32:(j + 1) * 32], dq_ref[...], 32) * sc
        k = _rms_rows(pc[256 + j * 32:256 + (j + 1) * 32], dk_ref[...], 32)
        qc_o[0, j] = q.astype(BF16)
        kc_o[0, j] = k.astype(BF16)
    for hh in range(4):
        _store_v(vc_o, hh, pc[512 + hh * 64:512 + (hh + 1) * 64])

    pd = proj(1952, 2464)
    crow, srow, ccol, scol = crow_ref[...], srow_ref[...], ccol_ref[...], scol_ref[...]
    sd = LOG2E * 64 ** -0.5

    def axial(x, o_ref, hh):
        a1, a2 = _rope_rows(x[0:32], crow, srow)
        b1, b2 = _rope_rows(x[32:64], ccol, scol)
        o_ref[0, hh, 0:16, :] = a1.astype(BF16)
        o_ref[0, hh, 16:32, :] = a2.astype(BF16)
        o_ref[0, hh, 32:48, :] = b1.astype(BF16)
        o_ref[0, hh, 48:64, :] = b2.astype(BF16)

    for hh in range(4):
        axial(_rms_rows(pd[hh * 64:(hh + 1) * 64], gq_ref[...], 64) * sd, qd_o, hh)
    for hh in range(2):
        axial(_rms_rows(pd[256 + hh * 64:256 + (hh + 1) * 64], gk_ref[...], 64), kd_o, hh)
        _store_v(vd_o, hh, pd[384 + hh * 64:384 + (hh + 1) * 64])


def _proj(x, batch, seq, gmix, wint, cols, mats, tabs):
    n, d = x.shape
    tm = PROJ_TM
    nsb = seq // tm
    naq, nak, qlat, kvlat, mq, mk, dq, dk, gq, gk = cols
    wuqt, wukvt = mats

    def full(a):
        return pl.BlockSpec(a.shape, lambda i: (0,) * a.ndim)

    tab_spec = pl.BlockSpec((16, tm), lambda i: (0, i % nsb))
    args = [x, gmix.reshape(1, d), wint, naq, nak, qlat, wuqt, kvlat, wukvt, mq, mk, dq, dk, gq, gk, *tabs]
    in_specs = [pl.BlockSpec((tm, d), lambda i: (i, 0))] + [full(a) for a in args[1:15]] + [tab_spec] * 6

    def head_out(nh, rows):
        return (jax.ShapeDtypeStruct((batch, nh, rows, seq), BF16),
                pl.BlockSpec((1, nh, rows, tm), lambda i: (i // nsb, 0, 0, i % nsb)))

    def v_out(nh):
        return (jax.ShapeDtypeStruct((batch, nh, nsb, V_ROWS, tm), BF16),
                pl.BlockSpec((1, nh, 1, V_ROWS, tm), lambda i: (i // nsb, 0, i % nsb, 0, 0)))

    outs = [head_out(4, 64), head_out(4, 64), v_out(4),
            head_out(4, 96), head_out(4, 96), v_out(4),
            head_out(8, 32), head_out(8, 32), v_out(4),
            head_out(4, 64), head_out(2, 64), v_out(2)]
    return pl.pallas_call(
        _proj_kernel,
        grid=(n // tm,),
        in_specs=in_specs,
        out_specs=[o[1] for o in outs],
        out_shape=[o[0] for o in outs],
        compiler_params=pltpu.CompilerParams(
            dimension_semantics=("parallel",), vmem_limit_bytes=VMEM_LIMIT),
        name="mix_proj",
    )(*args)


def _col_max(s):
    parts = [s]
    while parts[0].shape[0] > 64:
        half = parts[0].shape[0] // 2
        parts = [jnp.maximum(p[:half], p[half:]) for p in parts]
    return jnp.max(parts[0], axis=0, keepdims=True)


def _produce(k_blk, q_blk, slot, s_scr, mc_scr):
    s = _dot(k_blk, q_blk)
    s_scr[slot] = s
    mc_scr[slot] = _col_max(s)


def _consume(s, m_cur, c, idx, v_blk, m_scr, acc_scr):
    m_prev = m_scr[idx]
    m_new = jnp.maximum(m_prev, m_cur + c)
    alpha = jnp.exp2(m_prev - m_new)
    p = jnp.exp2(s - (m_new - c)).astype(BF16)
    acc_scr[idx] = alpha * acc_scr[idx] + _dot(v_blk, p)
    m_scr[idx] = m_new


def _finish_head(acc):
    return acc[0:HEAD_V] / acc[HEAD_V:HEAD_V + 1]


def _flash_plain_kernel(q_ref, k_ref, v_ref, beta_ref, o_ref, s_scr, mc_scr, m_scr, acc_scr, *, kv_map, n_k):
    n_h = len(kv_map)
    m_scr[...] = jnp.full_like(m_scr, NEG_BIG)
    acc_scr[...] = jnp.zeros_like(acc_scr)

    def produce(i, hh):
        _produce(k_ref[0, kv_map[hh], i], q_ref[0, hh], hh % FLASH_SLOTS, s_scr, mc_scr)

    for hh in range(FLASH_AHEAD):
        produce(0, hh)

    def body(i, carry):
        nxt = jnp.minimum(i + 1, n_k - 1)
        for hh in range(n_h):
            ahead = hh + FLASH_AHEAD
            produce(i if ahead < n_h else nxt, ahead % n_h)
            slot = hh % FLASH_SLOTS
            _consume(s_scr[slot], mc_scr[slot], 0.0, hh, v_ref[0, kv_map[hh], i], m_scr, acc_scr)
        return carry

    lax.fori_loop(0, n_k, body, 0)
    y = jnp.concatenate([_finish_head(acc_scr[hh]) for hh in range(n_h)], axis=0)
    o_ref[0] = _rms_rows(y, beta_ref[...], GROUP_W).astype(BF16)


def _flash_plain(qt, k, vt, beta_col, kv_map):
    b, hq, dq, s = qt.shape
    hk, n_k, t = k.shape[1], k.shape[2], k.shape[3]
    assert len(kv_map) % FLASH_SLOTS == 0
    return pl.pallas_call(
        functools.partial(_flash_plain_kernel, kv_map=kv_map, n_k=n_k),
        grid=(b, s // t),
        in_specs=[
            pl.BlockSpec((1, hq, dq, t), lambda bi, qi: (bi, 0, 0, qi)),
            pl.BlockSpec((1, hk, n_k, t, dq), lambda bi, qi: (bi, 0, 0, 0, 0)),
            pl.BlockSpec((1, hk, n_k, V_ROWS, t), lambda bi, qi: (bi, 0, 0, 0, 0)),
            pl.BlockSpec((GROUP_W, 1), lambda bi, qi: (0, 0)),
        ],
        out_specs=pl.BlockSpec((1, GROUP_W, t), lambda bi, qi: (bi, 0, qi)),
        out_shape=jax.ShapeDtypeStruct((b, GROUP_W, s), BF16),
        scratch_shapes=[pltpu.VMEM((FLASH_SLOTS, t, t), F32), pltpu.VMEM((FLASH_SLOTS, 1, t), F32),
                        pltpu.VMEM((hq, 1, t), F32), pltpu.VMEM((hq, V_ROWS, t), F32)],
        compiler_params=pltpu.CompilerParams(
            dimension_semantics=("parallel", "parallel"), vmem_limit_bytes=VMEM_LIMIT),
        name="flash_plain",
    )(qt, k, vt, beta_col)


def _flash_diff_kernel(t5_ref, q_ref, k_ref, v_ref, bias_ref, lam_ref, subln_ref, o_ref,
                       qpad_scr, s_scr, mc_scr, m_scr, acc_scr, *, lambda_init, n_k):
    qi = pl.program_id(1)
    n_m = 8
    dq = q_ref.shape[2]
    m_scr[...] = jnp.full_like(m_scr, NEG_BIG)
    acc_scr[...] = jnp.zeros_like(acc_scr)
    qpad_scr[...] = jnp.zeros_like(qpad_scr)
    for j in range(n_m):
        qpad_scr[j, (j % 2) * dq:(j % 2 + 1) * dq, :] = q_ref[0, j]

    def produce(i, j):
        _produce(k_ref[0, j // 2, i], qpad_scr[j], j % FLASH_SLOTS, s_scr, mc_scr)

    for j in range(FLASH_AHEAD):
        produce(0, j)

    def sweep(lo, hi, near, bucket):
        def body(i, carry):
            nxt = jnp.minimum(i + 1, n_k - 1)
            for j in range(n_m):
                hh = j // 2
                ahead = j + FLASH_AHEAD
                produce(i if ahead < n_m else nxt, ahead % n_m)
                slot = j % FLASH_SLOTS
                if near:
                    s = s_scr[slot] + bias_ref[i - qi + 1, hh]
                    _consume(s, _col_max(s), 0.0, j, v_ref[0, hh, i], m_scr, acc_scr)
                else:
                    c = t5_ref[bucket * 4 + hh] * LOG2E
                    _consume(s_scr[slot], mc_scr[slot], c, j, v_ref[0, hh, i], m_scr, acc_scr)
            return carry

        lax.fori_loop(lo, hi, body, 0)

    near_lo = jnp.maximum(qi - 1, 0)
    near_hi = jnp.minimum(qi + 2, n_k)
    sweep(0, near_lo, False, T5_BUCKETS // 2 - 1)
    sweep(near_lo, near_hi, True, None)
    sweep(near_hi, n_k, False, T5_BUCKETS - 1)

    lp = lam_ref[...]
    lam = (jnp.exp(jnp.sum(lp[0:1] * lp[1:2], axis=1, keepdims=True))
           - jnp.exp(jnp.sum(lp[2:3] * lp[3:4], axis=1, keepdims=True)) + lambda_init)
    for hh in range(4):
        y = _finish_head(acc_scr[2 * hh]) - lam * _finish_head(acc_scr[2 * hh + 1])
        y = _rms_rows(y, subln_ref[...], HEAD_V) * (1.0 - lambda_init)
        o_ref[0, hh * HEAD_V:(hh + 1) * HEAD_V, :] = y.astype(BF16)


def _flash_diff(t5_flat, qt, k, vt, bias, lam_params, subln_col, lambda_init):
    b, hq, dq, s = qt.shape
    n_k, t = k.shape[2], k.shape[3]
    return pl.pallas_call(
        functools.partial(_flash_diff_kernel, lambda_init=lambda_init, n_k=n_k),
        grid=(b, s // t),
        in_specs=[
            pl.BlockSpec(memory_space=pltpu.SMEM),
            pl.BlockSpec((1, hq, dq, t), lambda bi, qi: (bi, 0, 0, qi)),
            pl.BlockSpec((1, 4, n_k, t, 2 * dq), lambda bi, qi: (bi, 0, 0, 0, 0)),
            pl.BlockSpec((1, 4, n_k, V_ROWS, t), lambda bi, qi: (bi, 0, 0, 0, 0)),
            pl.BlockSpec(bias.shape, lambda bi, qi: (0, 0, 0, 0), pipeline_mode=pl.Buffered(1)),
            pl.BlockSpec(lam_params.shape, lambda bi, qi: (0, 0)),
            pl.BlockSpec((HEAD_V, 1), lambda bi, qi: (0, 0)),
        ],
        out_specs=pl.BlockSpec((1, GROUP_W, t), lambda bi, qi: (bi, 0, qi)),
        out_shape=jax.ShapeDtypeStruct((b, GROUP_W, s), BF16),
        scratch_shapes=[pltpu.VMEM((hq, 2 * dq, t), BF16),
                        pltpu.VMEM((FLASH_SLOTS, t, t), F32), pltpu.VMEM((FLASH_SLOTS, 1, t), F32),
                        pltpu.VMEM((hq, 1, t), F32), pltpu.VMEM((hq, V_ROWS, t), F32)],
        compiler_params=pltpu.CompilerParams(
            dimension_semantics=("parallel", "parallel"), vmem_limit_bytes=VMEM_LIMIT),
        name="flash_diff",
    )(t5_flat, qt, k, vt, bias, lam_params, subln_col)


def _t5_bias_kernel(tab_ref, o_ref, *, t):
    d = pl.program_id(0) - 1
    hh = pl.program_id(1)
    rows = 64
    half = T5_BUCKETS // 2
    max_exact = half // 2

    def body(r, carry):
        kk = lax.broadcasted_iota(jnp.int32, (rows, t), 0) + r * rows
        qq = lax.broadcasted_iota(jnp.int32, (rows, t), 1)
        rel = d * t + kk - qq
        n = jnp.abs(rel)
        large = max_exact + (jnp.log(jnp.maximum(n, 1).astype(F32) / max_exact)
                             / math.log(T5_MAX_DIST / max_exact) * (half - max_exact)).astype(jnp.int32)
        large = jnp.minimum(large, half - 1)
        bucket = jnp.where(rel > 0, half, 0) + jnp.where(n < max_exact, n, large)
        acc = jnp.zeros((rows, t), F32)
        for bkt in range(T5_BUCKETS):
            acc = jnp.where(bucket == bkt, tab_ref[bkt * 4 + hh], acc)
        o_ref[0, 0, pl.ds(pl.multiple_of(r * rows, rows), rows), :] = acc * LOG2E
        return carry

    lax.fori_loop(0, t // rows, body, 0)


def _t5_bias(t5_flat):
    t = FLASH_T
    return pl.pallas_call(
        functools.partial(_t5_bias_kernel, t=t),
        grid=(3, 4),
        in_specs=[pl.BlockSpec(memory_space=pltpu.SMEM)],
        out_specs=pl.BlockSpec((1, 1, t, t), lambda d, h: (d, h, 0, 0)),
        out_shape=jax.ShapeDtypeStruct((3, 4, t, t), F32),
        name="t5_bias",
    )(t5_flat)


def _na_window(kind, j, i, grid_rows):
    r0 = (0, NA_Q_ROWS, grid_rows - NA_Q_ROWS)[kind]
    start = (0, 0, grid_rows - NA_K_ROWS)[kind]
    krow, qrow = start + j, r0 + i
    lo = min(max(qrow - NA_KH // 2, 0), grid_rows - NA_KH)
    return lo <= krow < lo + NA_KH, krow - qrow + NA_KH - 1


def _na_bias_kernel(rpb_ref, o_ref, t_scr, *, grid_rows):
    hh = pl.program_id(0)
    n_dr, n_dc = 2 * NA_KH - 1, 2 * NA_KW - 1
    shape = (GRID_W, NA_Q_ROWS * GRID_W)
    lane = lax.broadcasted_iota(jnp.int32, shape, 1)
    kc = lax.broadcasted_iota(jnp.int32, shape, 0)
    qc = lane % GRID_W
    grp = lane // GRID_W
    dcm = jnp.clip(kc - qc + NA_KW - 1, 0, n_dc - 1)
    qs = jnp.clip(qc - NA_KW // 2, 0, GRID_W - NA_KW)
    col_ok = (kc >= qs) & (kc < qs + NA_KW)

    for dr in range(n_dr):
        base = (hh * n_dr + dr) * n_dc
        t_scr[dr] = lax.fori_loop(
            0, n_dc, lambda dc, acc: jnp.where(dcm == dc, rpb_ref[base + dc], acc), jnp.zeros(shape, F32)) * LOG2E

    for kind in range(3):
        for j in range(NA_K_ROWS):
            wins = [_na_window(kind, j, i, grid_rows) for i in range(NA_Q_ROWS)]
            blk = jnp.full(shape, NEG_BIG, F32)
            for i, (inside, dr) in enumerate(wins):
                if inside:
                    blk = jnp.where((grp == i) & col_ok, t_scr[dr], blk)
            o_ref[kind, 0, j * GRID_W:(j + 1) * GRID_W, :] = blk


def _na_bias(rpb_flat, grid_rows):
    kt, qt = NA_K_ROWS * GRID_W, NA_Q_ROWS * GRID_W
    return pl.pallas_call(
        functools.partial(_na_bias_kernel, grid_rows=grid_rows),
        grid=(4,),
        in_specs=[pl.BlockSpec(memory_space=pltpu.SMEM)],
        out_specs=pl.BlockSpec((3, 1, kt, qt), lambda h: (0, h, 0, 0)),
        out_shape=jax.ShapeDtypeStruct((3, 4, kt, qt), F32),
        scratch_shapes=[pltpu.VMEM((2 * NA_KH - 1, GRID_W, qt), F32)],
        name="na_bias",
    )(rpb_flat)


def _na_kernel(q_ref, k0_ref, k1_ref, k2_ref, v0_ref, v1_ref, v2_ref, bias_ref, beta_ref, o_ref):
    qt = NA_Q_ROWS * GRID_W
    k_refs = (k0_ref, k1_ref, k2_ref)
    v_refs = (v0_ref, v1_ref, v2_ref)

    def logits(hh):
        q = q_ref[0, hh]
        return [_dot(k_refs[j][0, hh], q) + bias_ref[0, hh, j * qt:(j + 1) * qt, :] for j in range(3)]

    outs = []
    ss_next = logits(0)
    for hh in range(4):
        ss = ss_next
        if hh + 1 < 4:
            ss_next = logits(hh + 1)
        m = functools.reduce(jnp.maximum, [_col_max(s) for s in ss])
        acc = None
        for j in range(3):
            p = jnp.exp2(ss[j] - m).astype(BF16)
            pv = _dot(v_refs[j][0, hh, 0], p)
            acc = pv if acc is None else acc + pv
        outs.append(_finish_head(acc))
    y = jnp.concatenate(outs, axis=0)
    o_ref[0] = _rms_rows(y, beta_ref[...], GROUP_W).astype(BF16)


def _na(qt_arr, k, vt, bias, beta_col):
    b, nh, d, s = qt_arr.shape
    qt = NA_Q_ROWS * GRID_W
    kt = NA_K_ROWS * GRID_W
    n_t = s // qt
    n_win = kt // qt

    def win(j):
        return lambda bi, ti: jnp.clip(ti - 1, 0, n_t - n_win) + j

    def kind(bi, ti):
        return (jnp.where(ti == 0, 0, jnp.where(ti == n_t - 1, 2, 1)), 0, 0, 0)

    k_specs = [pl.BlockSpec((1, nh, qt, d), (lambda j: lambda bi, ti: (bi, 0, win(j)(bi, ti), 0))(j))
               for j in range(n_win)]
    per = vt.shape[4] // qt
    v_specs = [pl.BlockSpec((1, nh, 1, V_ROWS, qt),
                            (lambda j: lambda bi, ti: (bi, 0, win(j)(bi, ti) // per, 0, win(j)(bi, ti) % per))(j))
               for j in range(n_win)]
    return pl.pallas_call(
        _na_kernel,
        grid=(b, n_t),
        in_specs=[pl.BlockSpec((1, nh, d, qt), lambda bi, ti: (bi, 0, 0, ti))] + k_specs + v_specs + [
            pl.BlockSpec((1, nh, kt, qt), kind),
            pl.BlockSpec((GROUP_W, 1), lambda bi, ti: (0, 0)),
        ],
        out_specs=pl.BlockSpec((1, GROUP_W, qt), lambda bi, ti: (bi, 0, ti)),
        out_shape=jax.ShapeDtypeStruct((b, GROUP_W, s), BF16),
        compiler_params=pltpu.CompilerParams(
            dimension_semantics=("parallel", "parallel"), vmem_limit_bytes=VMEM_LIMIT),
        name="na_attn",
    )(qt_arr, k, k, k, vt, vt, vt, bias, beta_col)


def _rope_tables(pos, dim):
    inv = jnp.exp(-math.log(ROPE_THETA) * jnp.arange(0, dim, 2, dtype=F32) / dim)
    ang = pos.astype(F32)[:, None] * inv[None, :]
    return jnp.cos(ang).T, jnp.sin(ang).T


def _col(v):
    return v.reshape(-1, 1).astype(F32)


def _lambda_init(layer):
    return 0.8 - 0.6 * math.exp(-0.3 * layer)


def kernel(x, ffn1_norm, ffn1_w_gate, ffn1_w_up, ffn1_w_down, mix_norm, w_in, na_q_norm, na_k_norm, na_rpb, na_beta, mla_q_lat_norm, mla_w_uq, mla_kv_lat_norm, mla_w_ukv, mla_q_norm, mla_k_norm, mla_beta, diff_q_norm, diff_k_norm, diff_lambda, diff_subln, gqa_q_norm, gqa_k_norm, gqa_beta, w_out, ffn2_norm, ffn2_w_gate, ffn2_w_up, ffn2_w_down, final_norm, t5_bias):
    batch, seq, d = x.shape
    grid_rows = seq // GRID_W
    pos = jnp.arange(seq, dtype=jnp.int32)
    tabs = (*_rope_tables(pos, 32), *_rope_tables(pos // GRID_W, 32), *_rope_tables(pos % GRID_W, 32))
    t5_flat = t5_bias.reshape(-1).astype(F32)
    t5_tiles = _t5_bias(t5_flat)

    xf = x.reshape(batch * seq, d)
    for l in range(N_LAYERS):
        xf = _ffn(xf, ffn1_norm[l], ffn1_w_gate[l].astype(BF16), ffn1_w_up[l].astype(BF16),
                  ffn1_w_down[l].astype(BF16))
        cols = tuple(_col(v[l]) for v in (na_q_norm, na_k_norm, mla_q_lat_norm, mla_kv_lat_norm,
                                          mla_q_norm, mla_k_norm, diff_q_norm, diff_k_norm,
                                          gqa_q_norm, gqa_k_norm))
        mats = (mla_w_uq[l].T.astype(BF16), mla_w_ukv[l].T.astype(BF16))
        (qa, ka, va, qb, kb, vb, qc, kc, vc, qd, kd, vd) = _proj(
            xf, batch, seq, mix_norm[l], w_in[l].T.astype(BF16), cols, mats, tabs)
        n_k = seq // FLASH_T
        ka = jnp.swapaxes(ka, 2, 3)
        kb = jnp.swapaxes(kb, 2, 3).reshape(batch, 4, n_k, FLASH_T, 96)
        kc = jnp.swapaxes(kc.reshape(batch, 4, 64, seq), 2, 3).reshape(batch, 4, n_k, FLASH_T, 64)
        kd = jnp.swapaxes(kd, 2, 3).reshape(batch, 2, n_k, FLASH_T, 64)

        ya = _na(qa, ka, va, _na_bias(na_rpb[l].reshape(-1).astype(F32), grid_rows), _col(na_beta[l]))
        yb = _flash_plain(qb, kb, vb, _col(mla_beta[l]), (0, 1, 2, 3))
        yc = _flash_diff(t5_flat, qc, kc, vc, t5_tiles, diff_lambda[l].astype(F32), _col(diff_subln[l]),
                         _lambda_init(l))
        yd = _flash_plain(qd, kd, vd, _col(gqa_beta[l]), (0, 0, 1, 1))

        xf = _ffn(xf, ffn2_norm[l], ffn2_w_gate[l].astype(BF16), ffn2_w_up[l].astype(BF16),
                  ffn2_w_down[l].astype(BF16), mix=((ya, yb, yc, yd), w_out[l].astype(BF16)),
                  final_g=final_norm[l], seq=seq)
    return xf.reshape(batch, seq, d)
```

```python
import functools
import math

import jax
import jax.numpy as jnp
from jax import lax
from jax.experimental import pallas as pl
from jax.experimental.pallas import tpu as pltpu

F32 = jnp.float32
BF16 = jnp.bfloat16

EPS = 1e-6
NEG_BIG = -1e30
LOG2E = 1.4426950408889634
ROPE_THETA = 10000.0
GRID_W = 64
N_LAYERS = 2

HEAD_V = 64
V_ROWS = 80
GROUP_W = 256

NA_KH, NA_KW = 8, 16
NA_Q_ROWS = 4
NA_K_ROWS = 12
T5_BUCKETS = 32
T5_MAX_DIST = 128

FFN_TM = 1024
FFN_CHUNK = 1024
FLASH_T = 512
FLASH_AHEAD = 2
FLASH_SLOTS = 4
PROJ_TM = FLASH_T

VMEM_LIMIT = 56 * 1024 * 1024


def _rms_rows(x, g_col, n):
    r = lax.rsqrt(jnp.sum(x * x, axis=0, keepdims=True) * (1.0 / n) + EPS)
    return (x * r) * g_col


def _rms_lanes(x, g_row):
    r = lax.rsqrt(jnp.mean(x * x, axis=-1, keepdims=True) + EPS)
    return (x * r) * g_row


def _dot(a, b):
    return jnp.dot(a, b, preferred_element_type=F32)


def _ffn_kernel(*refs, has_mix, has_final, chunks):
    x_ref, g_ref, wg_ref, wu_ref, wd_ref = refs[:5]
    rest = refs[5:]
    if has_mix:
        y_refs, wo_ref, rest = rest[:4], rest[4], rest[5:]
    if has_final:
        gf_ref, rest = rest[0], rest[1:]
    (o_ref,) = rest

    x = x_ref[...]
    if has_mix:
        for g in range(4):
            x = x + lax.dot_general(
                y_refs[g][0], wo_ref[g * GROUP_W:(g + 1) * GROUP_W, :],
                (((0,), (0,)), ((), ())), preferred_element_type=F32)
    o_ref[...] = x
    h = _rms_lanes(x, g_ref[...]).astype(BF16)
    acc = None
    for lo, hi in chunks:
        gate = _dot(h, wg_ref[:, lo:hi])
        up = _dot(h, wu_ref[:, lo:hi])
        a = (gate / (1.0 + jnp.exp(-gate))) * up
        part = _dot(a.astype(BF16), wd_ref[lo:hi, :])
        acc = part if acc is None else acc + part
    out = o_ref[...] + 0.5 * acc
    if has_final:
        out = _rms_lanes(out, gf_ref[...])
    o_ref[...] = out


def _ffn(x, g, wg, wu, wd, mix=None, final_g=None, seq=None):
    n, d = x.shape
    d_ff = wg.shape[1]
    tm = FFN_TM
    chunks = tuple((lo, min(lo + FFN_CHUNK, d_ff)) for lo in range(0, d_ff, FFN_CHUNK))
    once = pl.Buffered(1)
    in_specs = [
        pl.BlockSpec((tm, d), lambda i: (i, 0)),
        pl.BlockSpec((1, d), lambda i: (0, 0)),
        pl.BlockSpec((d, d_ff), lambda i: (0, 0), pipeline_mode=once),
        pl.BlockSpec((d, d_ff), lambda i: (0, 0), pipeline_mode=once),
        pl.BlockSpec((d_ff, d), lambda i: (0, 0), pipeline_mode=once),
    ]
    args = [x, g.reshape(1, d), wg, wu, wd]
    if mix is not None:
        ys, wo = mix
        nsb = seq // tm
        for y in ys:
            in_specs.append(pl.BlockSpec((1, GROUP_W, tm), lambda i: (i // nsb, 0, i % nsb)))
            args.append(y)
        in_specs.append(pl.BlockSpec(wo.shape, lambda i: (0, 0), pipeline_mode=once))
        args.append(wo)
    if final_g is not None:
        in_specs.append(pl.BlockSpec((1, d), lambda i: (0, 0)))
        args.append(final_g.reshape(1, d))
    return pl.pallas_call(
        functools.partial(_ffn_kernel, has_mix=mix is not None, has_final=final_g is not None, chunks=chunks),
        grid=(n // tm,),
        in_specs=in_specs,
        out_specs=pl.BlockSpec((tm, d), lambda i: (i, 0)),
        out_shape=jax.ShapeDtypeStruct((n, d), F32),
        compiler_params=pltpu.CompilerParams(
            dimension_semantics=("parallel",), vmem_limit_bytes=VMEM_LIMIT),
        name="ffn_mix" if mix is not None else "ffn",
    )(*args)


def _rope_rows(x, cos, sin):
    x1, x2 = x[:16], x[16:]
    return x1 * cos - x2 * sin, x1 * sin + x2 * cos


def _store_v(v_ref, hh, v):
    v_ref[0, hh, 0, 0:HEAD_V, :] = v.astype(BF16)
    v_ref[0, hh, 0, HEAD_V:V_ROWS, :] = jnp.ones((V_ROWS - HEAD_V, v.shape[1]), BF16)


def _proj_kernel(x_ref, gmix_ref, wint_ref,
                 naq_ref, nak_ref,
                 qlat_ref, wuqt_ref, kvlat_ref, wukvt_ref, mq_ref, mk_ref,
                 dq_ref, dk_ref, gq_ref, gk_ref,
                 cseq_ref, sseq_ref, crow_ref, srow_ref, ccol_ref, scol_ref,
                 qa_o, ka_o, va_o, qb_o, kb_o, vb_o, qc_o, kc_o, vc_o, qd_o, kd_o, vd_o):
    h = _rms_lanes(x_ref[...], gmix_ref[...]).astype(BF16)

    def proj(lo, hi):
        return lax.dot_general(wint_ref[lo:hi, :], h, (((1,), (1,)), ((), ())),
                               preferred_element_type=F32)

    pa = proj(0, 768)
    sa = LOG2E * 64 ** -0.5
    for hh in range(4):
        q = _rms_rows(pa[hh * 64:(hh + 1) * 64], naq_ref[...], 64) * sa
        k = _rms_rows(pa[256 + hh * 64:256 + (hh + 1) * 64], nak_ref[...], 64)
        qa_o[0, hh] = q.astype(BF16)
        ka_o[0, hh] = k.astype(BF16)
        _store_v(va_o, hh, pa[512 + hh * 64:512 + (hh + 1) * 64])

    pb = proj(768, 1184)
    cq = _rms_rows(pb[0:256], qlat_ref[...], 256).astype(BF16)
    qb = _dot(wuqt_ref[...], cq)
    ckv = _rms_rows(pb[256:384], kvlat_ref[...], 128).astype(BF16)
    kv = _dot(wukvt_ref[...], ckv)
    kr = pb[384:416]
    cseq, sseq = cseq_ref[...], sseq_ref[...]
    sb = LOG2E * 96 ** -0.5
    for hh in range(4):
        q = _rms_rows(qb[hh * 96:(hh + 1) * 96], mq_ref[...], 96) * sb
        r1, r2 = _rope_rows(q[64:96], cseq, sseq)
        qb_o[0, hh, 0:64, :] = q[0:64].astype(BF16)
        qb_o[0, hh, 64:80, :] = r1.astype(BF16)
        qb_o[0, hh, 80:96, :] = r2.astype(BF16)
        k = jnp.concatenate([kv[hh * 128:hh * 128 + 64], kr], axis=0)
        k = _rms_rows(k, mk_ref[...], 96)
        r1, r2 = _rope_rows(k[64:96], cseq, sseq)
        kb_o[0, hh, 0:64, :] = k[0:64].astype(BF16)
        kb_o[0, hh, 64:80, :] = r1.astype(BF16)
        kb_o[0, hh, 80:96, :] = r2.astype(BF16)
        _store_v(vb_o, hh, kv[hh * 128 + 64:hh * 128 + 128])

    pc = proj(1184, 1952)
    sc = LOG2E * 32 ** -0.5
    for j in range(8):
        q = _rms_rows(pc[j * 32:(j + 1) * 32], dq_ref[...], 32) * sc
        k = _rms_rows(pc[256 + j * 32:256 + (j + 1) * 32], dk_ref[...], 32)
        qc_o[0, j] = q.astype(BF16)
        kc_o[0, j] = k.astype(BF16)
    for hh in range(4):
        _store_v(vc_o, hh, pc[512 + hh * 64:512 + (hh + 1) * 64])

    pd = proj(1952, 2464)
    crow, srow, ccol, scol = crow_ref[...], srow_ref[...], ccol_ref[...], scol_ref[...]
    sd = LOG2E * 64 ** -0.5

    def axial(x, o_ref, hh):
        a1, a2 = _rope_rows(x[0:32], crow, srow)
        b1, b2 = _rope_rows(x[32:64], ccol, scol)
        o_ref[0, hh, 0:16, :] = a1.astype(BF16)
        o_ref[0, hh, 16:32, :] = a2.astype(BF16)
        o_ref[0, hh, 32:48, :] = b1.astype(BF16)
        o_ref[0, hh, 48:64, :] = b2.astype(BF16)

    for hh in range(4):
        axial(_rms_rows(pd[hh * 64:(hh + 1) * 64], gq_ref[...], 64) * sd, qd_o, hh)
    for hh in range(2):
        axial(_rms_rows(pd[256 + hh * 64:256 + (hh + 1) * 64], gk_ref[...], 64), kd_o, hh)
        _store_v(vd_o, hh, pd[384 + hh * 64:384 + (hh + 1) * 64])


def _proj(x, batch, seq, gmix, wint, cols, mats, tabs):
    n, d = x.shape
    tm = PROJ_TM
    nsb = seq // tm
    naq, nak, qlat, kvlat, mq, mk, dq, dk, gq, gk = cols
    wuqt, wukvt = mats

    def full(a):
        return pl.BlockSpec(a.shape, lambda i: (0,) * a.ndim)

    tab_spec = pl.BlockSpec((16, tm), lambda i: (0, i % nsb))
    args = [x, gmix.reshape(1, d), wint, naq, nak, qlat, wuqt, kvlat, wukvt, mq, mk, dq, dk, gq, gk, *tabs]
    in_specs = [pl.BlockSpec((tm, d), lambda i: (i, 0))] + [full(a) for a in args[1:15]] + [tab_spec] * 6

    def head_out(nh, rows):
        return (jax.ShapeDtypeStruct((batch, nh, rows, seq), BF16),
                pl.BlockSpec((1, nh, rows, tm), lambda i: (i // nsb, 0, 0, i % nsb)))

    def v_out(nh):
        return (jax.ShapeDtypeStruct((batch, nh, nsb, V_ROWS, tm), BF16),
                pl.BlockSpec((1, nh, 1, V_ROWS, tm), lambda i: (i // nsb, 0, i % nsb, 0, 0)))

    outs = [head_out(4, 64), head_out(4, 64), v_out(4),
            head_out(4, 96), head_out(4, 96), v_out(4),
            head_out(8, 32), head_out(8, 32), v_out(4),
            head_out(4, 64), head_out(2, 64), v_out(2)]
    return pl.pallas_call(
        _proj_kernel,
        grid=(n // tm,),
        in_specs=in_specs,
        out_specs=[o[1] for o in outs],
        out_shape=[o[0] for o in outs],
        compiler_params=pltpu.CompilerParams(
            dimension_semantics=("parallel",), vmem_limit_bytes=VMEM_LIMIT),
        name="mix_proj",
    )(*args)


def _col_max(s):
    parts = [s]
    while parts[0].shape[0] > 64:
        half = parts[0].shape[0] // 2
        parts = [jnp.maximum(p[:half], p[half:]) for p in parts]
    return jnp.max(parts[0], axis=0, keepdims=True)


def _produce(k_blk, q_blk, slot, s_scr, mc_scr):
    s = _dot(k_blk, q_blk)
    s_scr[slot] = s
    mc_scr[slot] = _col_max(s)


def _consume(s, m_cur, c, idx, v_blk, m_scr, acc_scr):
    m_prev = m_scr[idx]
    m_new = jnp.maximum(m_prev, m_cur + c)
    alpha = jnp.exp2(m_prev - m_new)
    p = jnp.exp2(s - (m_new - c)).astype(BF16)
    acc_scr[idx] = alpha * acc_scr[idx] + _dot(v_blk, p)
    m_scr[idx] = m_new


def _finish_head(acc):
    return acc[0:HEAD_V] / acc[HEAD_V:HEAD_V + 1]


def _flash_plain_kernel(q_ref, k_ref, v_ref, beta_ref, o_ref, s_scr, mc_scr, m_scr, acc_scr, *, kv_map, n_k):
    n_h = len(kv_map)
    m_scr[...] = jnp.full_like(m_scr, NEG_BIG)
    acc_scr[...] = jnp.zeros_like(acc_scr)

    def produce(i, hh):
        _produce(k_ref[0, kv_map[hh], i], q_ref[0, hh], hh % FLASH_SLOTS, s_scr, mc_scr)

    for hh in range(FLASH_AHEAD):
        produce(0, hh)

    def body(i, carry):
        nxt = jnp.minimum(i + 1, n_k - 1)
        for hh in range(n_h):
            ahead = hh + FLASH_AHEAD
            produce(i if ahead < n_h else nxt, ahead % n_h)
            slot = hh % FLASH_SLOTS
            _consume(s_scr[slot], mc_scr[slot], 0.0, hh, v_ref[0, kv_map[hh], i], m_scr, acc_scr)
        return carry

    lax.fori_loop(0, n_k, body, 0)
    y = jnp.concatenate([_finish_head(acc_scr[hh]) for hh in range(n_h)], axis=0)
    o_ref[0] = _rms_rows(y, beta_ref[...], GROUP_W).astype(BF16)


def _flash_plain(qt, k, vt, beta_col, kv_map):
    b, hq, dq, s = qt.shape
    hk, n_k, t = k.shape[1], k.shape[2], k.shape[3]
    assert len(kv_map) % FLASH_SLOTS == 0
    return pl.pallas_call(
        functools.partial(_flash_plain_kernel, kv_map=kv_map, n_k=n_k),
        grid=(b, s // t),
        in_specs=[
            pl.BlockSpec((1, hq, dq, t), lambda bi, qi: (bi, 0, 0, qi)),
            pl.BlockSpec((1, hk, n_k, t, dq), lambda bi, qi: (bi, 0, 0, 0, 0)),
            pl.BlockSpec((1, hk, n_k, V_ROWS, t), lambda bi, qi: (bi, 0, 0, 0, 0)),
            pl.BlockSpec((GROUP_W, 1), lambda bi, qi: (0, 0)),
        ],
        out_specs=pl.BlockSpec((1, GROUP_W, t), lambda bi, qi: (bi, 0, qi)),
        out_shape=jax.ShapeDtypeStruct((b, GROUP_W, s), BF16),
        scratch_shapes=[pltpu.VMEM((FLASH_SLOTS, t, t), F32), pltpu.VMEM((FLASH_SLOTS, 1, t), F32),
                        pltpu.VMEM((hq, 1, t), F32), pltpu.VMEM((hq, V_ROWS, t), F32)],
        compiler_params=pltpu.CompilerParams(
            dimension_semantics=("parallel", "parallel"), vmem_limit_bytes=VMEM_LIMIT),
        name="flash_plain",
    )(qt, k, vt, beta_col)


def _flash_diff_kernel(t5_ref, q_ref, k_ref, v_ref, bias_ref, lam_ref, subln_ref, o_ref,
                       qpad_scr, s_scr, mc_scr, m_scr, acc_scr, *, lambda_init, n_k):
    qi = pl.program_id(1)
    n_m = 8
    dq = q_ref.shape[2]
    m_scr[...] = jnp.full_like(m_scr, NEG_BIG)
    acc_scr[...] = jnp.zeros_like(acc_scr)
    qpad_scr[...] = jnp.zeros_like(qpad_scr)
    for j in range(n_m):
        qpad_scr[j, (j % 2) * dq:(j % 2 + 1) * dq, :] = q_ref[0, j]

    def produce(i, j):
        _produce(k_ref[0, j // 2, i], qpad_scr[j], j % FLASH_SLOTS, s_scr, mc_scr)

    for j in range(FLASH_AHEAD):
        produce(0, j)

    def sweep(lo, hi, near, bucket):
        def body(i, carry):
            nxt = jnp.minimum(i + 1, n_k - 1)
            for j in range(n_m):
                hh = j // 2
                ahead = j + FLASH_AHEAD
                produce(i if ahead < n_m else nxt, ahead % n_m)
                slot = j % FLASH_SLOTS
                if near:
                    s = s_scr[slot] + bias_ref[i - qi + 1, hh]
                    _consume(s, _col_max(s), 0.0, j, v_ref[0, hh, i], m_scr, acc_scr)
                else:
                    c = t5_ref[bucket * 4 + hh] * LOG2E
                    _consume(s_scr[slot], mc_scr[slot], c, j, v_ref[0, hh, i], m_scr, acc_scr)
            return carry

        lax.fori_loop(lo, hi, body, 0)

    near_lo = jnp.maximum(qi - 1, 0)
    near_hi = jnp.minimum(qi + 2, n_k)
    sweep(0, near_lo, False, T5_BUCKETS // 2 - 1)
    sweep(near_lo, near_hi, True, None)
    sweep(near_hi, n_k, False, T5_BUCKETS - 1)

    lp = lam_ref[...]
    lam = (jnp.exp(jnp.sum(lp[0:1] * lp[1:2], axis=1, keepdims=True))
           - jnp.exp(jnp.sum(lp[2:3] * lp[3:4], axis=1, keepdims=True)) + lambda_init)
    for hh in range(4):
        y = _finish_head(acc_scr[2 * hh]) - lam * _finish_head(acc_scr[2 * hh + 1])
        y = _rms_rows(y, subln_ref[...], HEAD_V) * (1.0 - lambda_init)
        o_ref[0, hh * HEAD_V:(hh + 1) * HEAD_V, :] = y.astype(BF16)


def _flash_diff(t5_flat, qt, k, vt, bias, lam_params, subln_col, lambda_init):
    b, hq, dq, s = qt.shape
    n_k, t = k.shape[2], k.shape[3]
    return pl.pallas_call(
        functools.partial(_flash_diff_kernel, lambda_init=lambda_init, n_k=n_k),
        grid=(b, s // t),
        in_specs=[
            pl.BlockSpec(memory_space=pltpu.SMEM),
            pl.BlockSpec((1, hq, dq, t), lambda bi, qi: (bi, 0, 0, qi)),
            pl.BlockSpec((1, 4, n_k, t, 2 * dq), lambda bi, qi: (bi, 0, 0, 0, 0)),
            pl.BlockSpec((1, 4, n_k, V_ROWS, t), lambda bi, qi: (bi, 0, 0, 0, 0)),
            pl.BlockSpec(bias.shape, lambda bi, qi: (0, 0, 0, 0), pipeline_mode=pl.Buffered(1)),
            pl.BlockSpec(lam_params.shape, lambda bi, qi: (0, 0)),
            pl.BlockSpec((HEAD_V, 1), lambda bi, qi: (0, 0)),
        ],
        out_specs=pl.BlockSpec((1, GROUP_W, t), lambda bi, qi: (bi, 0, qi)),
        out_shape=jax.ShapeDtypeStruct((b, GROUP_W, s), BF16),
        scratch_shapes=[pltpu.VMEM((hq, 2 * dq, t), BF16),
                        pltpu.VMEM((FLASH_SLOTS, t, t), F32), pltpu.VMEM((FLASH_SLOTS, 1, t), F32),
                        pltpu.VMEM((hq, 1, t), F32), pltpu.VMEM((hq, V_ROWS, t), F32)],
        compiler_params=pltpu.CompilerParams(
            dimension_semantics=("parallel", "parallel"), vmem_limit_bytes=VMEM_LIMIT),
        name="flash_diff",
    )(t5_flat, qt, k, vt, bias, lam_params, subln_col)


def _t5_bias_kernel(tab_ref, o_ref, *, t):
    d = pl.program_id(0) - 1
    hh = pl.program_id(1)
    rows = 64
    half = T5_BUCKETS // 2
    max_exact = half // 2

    def body(r, carry):
        kk = lax.broadcasted_iota(jnp.int32, (rows, t), 0) + r * rows
        qq = lax.broadcasted_iota(jnp.int32, (rows, t), 1)
        rel = d * t + kk - qq
        n = jnp.abs(rel)
        large = max_exact + (jnp.log(jnp.maximum(n, 1).astype(F32) / max_exact)
                             / math.log(T5_MAX_DIST / max_exact) * (half - max_exact)).astype(jnp.int32)
        large = jnp.minimum(large, half - 1)
        bucket = jnp.where(rel > 0, half, 0) + jnp.where(n < max_exact, n, large)
        acc = jnp.zeros((rows, t), F32)
        for bkt in range(T5_BUCKETS):
            acc = jnp.where(bucket == bkt, tab_ref[bkt * 4 + hh], acc)
        o_ref[0, 0, pl.ds(pl.multiple_of(r * rows, rows), rows), :] = acc * LOG2E
        return carry

    lax.fori_loop(0, t // rows, body, 0)


def _t5_bias(t5_flat):
    t = FLASH_T
    return pl.pallas_call(
        functools.partial(_t5_bias_kernel, t=t),
        grid=(3, 4),
        in_specs=[pl.BlockSpec(memory_space=pltpu.SMEM)],
        out_specs=pl.BlockSpec((1, 1, t, t), lambda d, h: (d, h, 0, 0)),
        out_shape=jax.ShapeDtypeStruct((3, 4, t, t), F32),
        name="t5_bias",
    )(t5_flat)


def _na_window(kind, j, i, grid_rows):
    r0 = (0, NA_Q_ROWS, grid_rows - NA_Q_ROWS)[kind]
    start = (0, 0, grid_rows - NA_K_ROWS)[kind]
    krow, qrow = start + j, r0 + i
    lo = min(max(qrow - NA_KH // 2, 0), grid_rows - NA_KH)
    return lo <= krow < lo + NA_KH, krow - qrow + NA_KH - 1


def _na_bias_kernel(rpb_ref, o_ref, t_scr, *, grid_rows):
    hh = pl.program_id(0)
    n_dr, n_dc = 2 * NA_KH - 1, 2 * NA_KW - 1
    shape = (GRID_W, NA_Q_ROWS * GRID_W)
    lane = lax.broadcasted_iota(jnp.int32, shape, 1)
    kc = lax.broadcasted_iota(jnp.int32, shape, 0)
    qc = lane % GRID_W
    grp = lane // GRID_W
    dcm = jnp.clip(kc - qc + NA_KW - 1, 0, n_dc - 1)
    qs = jnp.clip(qc - NA_KW // 2, 0, GRID_W - NA_KW)
    col_ok = (kc >= qs) & (kc < qs + NA_KW)

    for dr in range(n_dr):
        base = (hh * n_dr + dr) * n_dc
        t_scr[dr] = lax.fori_loop(
            0, n_dc, lambda dc, acc: jnp.where(dcm == dc, rpb_ref[base + dc], acc), jnp.zeros(shape, F32)) * LOG2E

    for kind in range(3):
        for j in range(NA_K_ROWS):
            wins = [_na_window(kind, j, i, grid_rows) for i in range(NA_Q_ROWS)]
            blk = jnp.full(shape, NEG_BIG, F32)
            for i, (inside, dr) in enumerate(wins):
                if inside:
                    blk = jnp.where((grp == i) & col_ok, t_scr[dr], blk)
            o_ref[kind, 0, j * GRID_W:(j + 1) * GRID_W, :] = blk


def _na_bias(rpb_flat, grid_rows):
    kt, qt = NA_K_ROWS * GRID_W, NA_Q_ROWS * GRID_W
    return pl.pallas_call(
        functools.partial(_na_bias_kernel, grid_rows=grid_rows),
        grid=(4,),
        in_specs=[pl.BlockSpec(memory_space=pltpu.SMEM)],
        out_specs=pl.BlockSpec((3, 1, kt, qt), lambda h: (0, h, 0, 0)),
        out_shape=jax.ShapeDtypeStruct((3, 4, kt, qt), F32),
        scratch_shapes=[pltpu.VMEM((2 * NA_KH - 1, GRID_W, qt), F32)],
        name="na_bias",
    )(rpb_flat)


def _na_kernel(q_ref, k0_ref, k1_ref, k2_ref, v0_ref, v1_ref, v2_ref, bias_ref, beta_ref, o_ref):
    qt = NA_Q_ROWS * GRID_W
    k_refs = (k0_ref, k1_ref, k2_ref)
    v_refs = (v0_ref, v1_ref, v2_ref)

    def logits(hh):
        q = q_ref[0, hh]
        return [_dot(k_refs[j][0, hh], q) + bias_ref[0, hh, j * qt:(j + 1) * qt, :] for j in range(3)]

    outs = []
    ss_next = logits(0)
    for hh in range(4):
        ss = ss_next
        if hh + 1 < 4:
            ss_next = logits(hh + 1)
        m = functools.reduce(jnp.maximum, [_col_max(s) for s in ss])
        acc = None
        for j in range(3):
            p = jnp.exp2(ss[j] - m).astype(BF16)
            pv = _dot(v_refs[j][0, hh, 0], p)
            acc = pv if acc is None else acc + pv
        outs.append(_finish_head(acc))
    y = jnp.concatenate(outs, axis=0)
    o_ref[0] = _rms_rows(y, beta_ref[...], GROUP_W).astype(BF16)


def _na(qt_arr, k, vt, bias, beta_col):
    b, nh, d, s = qt_arr.shape
    qt = NA_Q_ROWS * GRID_W
    kt = NA_K_ROWS * GRID_W
    n_t = s // qt
    n_win = kt // qt

    def win(j):
        return lambda bi, ti: jnp.clip(ti - 1, 0, n_t - n_win) + j

    def kind(bi, ti):
        return (jnp.where(ti == 0, 0, jnp.where(ti == n_t - 1, 2, 1)), 0, 0, 0)

    k_specs = [pl.BlockSpec((1, nh, qt, d), (lambda j: lambda bi, ti: (bi, 0, win(j)(bi, ti), 0))(j))
               for j in range(n_win)]
    per = vt.shape[4] // qt
    v_specs = [pl.BlockSpec((1, nh, 1, V_ROWS, qt),
                            (lambda j: lambda bi, ti: (bi, 0, win(j)(bi, ti) // per, 0, win(j)(bi, ti) % per))(j))
               for j in range(n_win)]
    return pl.pallas_call(
        _na_kernel,
        grid=(b, n_t),
        in_specs=[pl.BlockSpec((1, nh, d, qt), lambda bi, ti: (bi, 0, 0, ti))] + k_specs + v_specs + [
            pl.BlockSpec((1, nh, kt, qt), kind),
            pl.BlockSpec((GROUP_W, 1), lambda bi, ti: (0, 0)),
        ],
        out_specs=pl.BlockSpec((1, GROUP_W, qt), lambda bi, ti: (bi, 0, ti)),
        out_shape=jax.ShapeDtypeStruct((b, GROUP_W, s), BF16),
        compiler_params=pltpu.CompilerParams(
            dimension_semantics=("parallel", "parallel"), vmem_limit_bytes=VMEM_LIMIT),
        name="na_attn",
    )(qt_arr, k, k, k, vt, vt, vt, bias, beta_col)


def _rope_tables(pos, dim):
    inv = jnp.exp(-math.log(ROPE_THETA) * jnp.arange(0, dim, 2, dtype=F32) / dim)
    ang = pos.astype(F32)[:, None] * inv[None, :]
    return jnp.cos(ang).T, jnp.sin(ang).T


def _col(v):
    return v.reshape(-1, 1).astype(F32)


def _lambda_init(layer):
    return 0.8 - 0.6 * math.exp(-0.3 * layer)


def kernel(x, ffn1_norm, ffn1_w_gate, ffn1_w_up, ffn1_w_down, mix_norm, w_in, na_q_norm, na_k_norm, na_rpb, na_beta, mla_q_lat_norm, mla_w_uq, mla_kv_lat_norm, mla_w_ukv, mla_q_norm, mla_k_norm, mla_beta, diff_q_norm, diff_k_norm, diff_lambda, diff_subln, gqa_q_norm, gqa_k_norm, gqa_beta, w_out, ffn2_norm, ffn2_w_gate, ffn2_w_up, ffn2_w_down, final_norm, t5_bias):
    batch, seq, d = x.shape
    grid_rows = seq // GRID_W
    pos = jnp.arange(seq, dtype=jnp.int32)
    tabs = (*_rope_tables(pos, 32), *_rope_tables(pos // GRID_W, 32), *_rope_tables(pos % GRID_W, 32))
    t5_flat = t5_bias.reshape(-1).astype(F32)
    t5_tiles = _t5_bias(t5_flat)

    xf = x.reshape(batch * seq, d)
    for l in range(N_LAYERS):
        xf = _ffn(xf, ffn1_norm[l], ffn1_w_gate[l].astype(BF16), ffn1_w_up[l].astype(BF16),
                  ffn1_w_down[l].astype(BF16))
        cols = tuple(_col(v[l]) for v in (na_q_norm, na_k_norm, mla_q_lat_norm, mla_kv_lat_norm,
                                          mla_q_norm, mla_k_norm, diff_q_norm, diff_k_norm,
                                          gqa_q_norm, gqa_k_norm))
        mats = (mla_w_uq[l].T.astype(BF16), mla_w_ukv[l].T.astype(BF16))
        (qa, ka, va, qb, kb, vb, qc, kc, vc, qd, kd, vd) = _proj(
            xf, batch, seq, mix_norm[l], w_in[l].T.astype(BF16), cols, mats, tabs)
        n_k = seq // FLASH_T
        ka = jnp.swapaxes(ka, 2, 3)
        kb = jnp.swapaxes(kb, 2, 3).reshape(batch, 4, n_k, FLASH_T, 96)
        kc = jnp.swapaxes(kc.reshape(batch, 4, 64, seq), 2, 3).reshape(batch, 4, n_k, FLASH_T, 64)
        kd = jnp.swapaxes(kd, 2, 3).reshape(batch, 2, n_k, FLASH_T, 64)

        ya = _na(qa, ka, va, _na_bias(na_rpb[l].reshape(-1).astype(F32), grid_rows), _col(na_beta[l]))
        yb = _flash_plain(qb, kb, vb, _col(mla_beta[l]), (0, 1, 2, 3))
        yc = _flash_diff(t5_flat, qc, kc, vc, t5_tiles, diff_lambda[l].astype(F32), _col(diff_subln[l]),
                         _lambda_init(l))
        yd = _flash_plain(qd, kd, vd, _col(gqa_beta[l]), (0, 0, 1, 1))

        xf = _ffn(xf, ffn2_norm[l], ffn2_w_gate[l].astype(BF16), ffn2_w_up[l].astype(BF16),
                  ffn2_w_down[l].astype(BF16), mix=((ya, yb, yc, yd), w_out[l].astype(BF16)),
                  final_g=final_norm[l], seq=seq)
    return xf.reshape(batch, seq, d)
```

```python
import functools
import math

import jax
import jax.numpy as jnp
from jax import lax
from jax.experimental import pallas as pl
from jax.experimental.pallas import tpu as pltpu

F32 = jnp.float32
BF16 = jnp.bfloat16

EPS = 1e-6
NEG_BIG = -1e30
LOG2E = 1.4426950408889634
ROPE_THETA = 10000.0
GRID_W = 64
N_LAYERS = 2

HEAD_V = 64
GROUP_W = 256

NA_KH, NA_KW = 8, 16
NA_Q_ROWS = 4
NA_K_ROWS = 12
T5_BUCKETS = 32
T5_MAX_DIST = 128

FFN_TM = 1024
FFN_CHUNK = 1024
FLASH_T = 512
FLASH_AHEAD = 2
FLASH_SLOTS = 4
PROJ_TM = FLASH_T

VMEM_LIMIT = 56 * 1024 * 1024


def _rms_rows(x, g_col, n):
    r = lax.rsqrt(jnp.sum(x * x, axis=0, keepdims=True) * (1.0 / n) + EPS)
    return (x * r) * g_col


def _rms_lanes(x, g_row):
    r = lax.rsqrt(jnp.mean(x * x, axis=-1, keepdims=True) + EPS)
    return (x * r) * g_row


def _dot(a, b):
    return jnp.dot(a, b, preferred_element_type=F32)


def _ffn_kernel(*refs, has_mix, has_final, chunks):
    x_ref, g_ref, wg_ref, wu_ref, wd_ref = refs[:5]
    rest = refs[5:]
    if has_mix:
        y_refs, wo_ref, rest = rest[:4], rest[4], rest[5:]
    if has_final:
        gf_ref, rest = rest[0], rest[1:]
    (o_ref,) = rest

    x = x_ref[...]
    if has_mix:
        for g in range(4):
            x = x + lax.dot_general(
                y_refs[g][0], wo_ref[g * GROUP_W:(g + 1) * GROUP_W, :],
                (((0,), (0,)), ((), ())), preferred_element_type=F32)
    o_ref[...] = x
    h = _rms_lanes(x, g_ref[...]).astype(BF16)
    acc = None
    for lo, hi in chunks:
        gate = _dot(h, wg_ref[:, lo:hi])
        up = _dot(h, wu_ref[:, lo:hi])
        a = (gate / (1.0 + jnp.exp(-gate))) * up
        part = _dot(a.astype(BF16), wd_ref[lo:hi, :])
        acc = part if acc is None else acc + part
    out = o_ref[...] + 0.5 * acc
    if has_final:
        out = _rms_lanes(out, gf_ref[...])
    o_ref[...] = out


def _ffn(x, g, wg, wu, wd, mix=None, final_g=None, seq=None):
    n, d = x.shape
    d_ff = wg.shape[1]
    tm = FFN_TM
    chunks = tuple((lo, min(lo + FFN_CHUNK, d_ff)) for lo in range(0, d_ff, FFN_CHUNK))
    once = pl.Buffered(1)
    in_specs = [
        pl.BlockSpec((tm, d), lambda i: (i, 0)),
        pl.BlockSpec((1, d), lambda i: (0, 0)),
        pl.BlockSpec((d, d_ff), lambda i: (0, 0), pipeline_mode=once),
        pl.BlockSpec((d, d_ff), lambda i: (0, 0), pipeline_mode=once),
        pl.BlockSpec((d_ff, d), lambda i: (0, 0), pipeline_mode=once),
    ]
    args = [x, g.reshape(1, d), wg, wu, wd]
    if mix is not None:
        ys, wo = mix
        nsb = seq // tm
        for y in ys:
            in_specs.append(pl.BlockSpec((1, GROUP_W, tm), lambda i: (i // nsb, 0, i % nsb)))
            args.append(y)
        in_specs.append(pl.BlockSpec(wo.shape, lambda i: (0, 0), pipeline_mode=once))
        args.append(wo)
    if final_g is not None:
        in_specs.append(pl.BlockSpec((1, d), lambda i: (0, 0)))
        args.append(final_g.reshape(1, d))
    return pl.pallas_call(
        functools.partial(_ffn_kernel, has_mix=mix is not None, has_final=final_g is not None, chunks=chunks),
        grid=(n // tm,),
        in_specs=in_specs,
        out_specs=pl.BlockSpec((tm, d), lambda i: (i, 0)),
        out_shape=jax.ShapeDtypeStruct((n, d), F32),
        compiler_params=pltpu.CompilerParams(
            dimension_semantics=("parallel",), vmem_limit_bytes=VMEM_LIMIT),
        name="ffn_mix" if mix is not None else "ffn",
    )(*args)


def _rope_rows(x, cos, sin):
    x1, x2 = x[:16], x[16:]
    return x1 * cos - x2 * sin, x1 * sin + x2 * cos


def _store_v(v_ref, hh, v):
    v_ref[0, hh, 0] = v.astype(BF16)


def _proj_kernel(x_ref, gmix_ref, wint_ref,
                 naq_ref, nak_ref,
                 qlat_ref, wuqt_ref, kvlat_ref, wukvt_ref, mq_ref, mk_ref,
                 dq_ref, dk_ref, gq_ref, gk_ref,
                 cseq_ref, sseq_ref, crow_ref, srow_ref, ccol_ref, scol_ref,
                 qa_o, ka_o, va_o, qb_o, kb_o, vb_o, qc_o, kc_o, vc_o, qd_o, kd_o, vd_o):
    h = _rms_lanes(x_ref[...], gmix_ref[...]).astype(BF16)

    def proj(lo, hi):
        return lax.dot_general(wint_ref[lo:hi, :], h, (((1,), (1,)), ((), ())),
                               preferred_element_type=F32)

    pa = proj(0, 768)
    sa = LOG2E * 64 ** -0.5
    for hh in range(4):
        q = _rms_rows(pa[hh * 64:(hh + 1) * 64], naq_ref[...], 64) * sa
        k = _rms_rows(pa[256 + hh * 64:256 + (hh + 1) * 64], nak_ref[...], 64)
        qa_o[0, hh] = q.astype(BF16)
        ka_o[0, hh] = k.astype(BF16)
        _store_v(va_o, hh, pa[512 + hh * 64:512 + (hh + 1) * 64])

    pb = proj(768, 1184)
    cq = _rms_rows(pb[0:256], qlat_ref[...], 256).astype(BF16)
    qb = _dot(wuqt_ref[...], cq)
    ckv = _rms_rows(pb[256:384], kvlat_ref[...], 128).astype(BF16)
    kv = _dot(wukvt_ref[...], ckv)
    kr = pb[384:416]
    cseq, sseq = cseq_ref[...], sseq_ref[...]
    sb = LOG2E * 96 ** -0.5
    for hh in range(4):
        q = _rms_rows(qb[hh * 96:(hh + 1) * 96], mq_ref[...], 96) * sb
        r1, r2 = _rope_rows(q[64:96], cseq, sseq)
        qb_o[0, hh, 0:64, :] = q[0:64].astype(BF16)
        qb_o[0, hh, 64:80, :] = r1.astype(BF16)
        qb_o[0, hh, 80:96, :] = r2.astype(BF16)
        k = jnp.concatenate([kv[hh * 128:hh * 128 + 64], kr], axis=0)
        k = _rms_rows(k, mk_ref[...], 96)
        r1, r2 = _rope_rows(k[64:96], cseq, sseq)
        kb_o[0, hh, 0:64, :] = k[0:64].astype(BF16)
        kb_o[0, hh, 64:80, :] = r1.astype(BF16)
        kb_o[0, hh, 80:96, :] = r2.astype(BF16)
        _store_v(vb_o, hh, kv[hh * 128 + 64:hh * 128 + 128])

    pc = proj(1184, 1952)
    sc = LOG2E * 32 ** -0.5
    for j in range(8):
        q = _rms_rows(pc[j * 32:(j + 1) * 32], dq_ref[...], 32) * sc
        k = _rms_rows(pc[256 + j * 32:256 + (j + 1) * 32], dk_ref[...], 32)
        qc_o[0, j] = q.astype(BF16)
        kc_o[0, j] = k.astype(BF16)
    for hh in range(4):
        _store_v(vc_o, hh, pc[512 + hh * 64:512 + (hh + 1) * 64])

    pd = proj(1952, 2464)
    crow, srow, ccol, scol = crow_ref[...], srow_ref[...], ccol_ref[...], scol_ref[...]
    sd = LOG2E * 64 ** -0.5

    def axial(x, o_ref, hh):
        a1, a2 = _rope_rows(x[0:32], crow, srow)
        b1, b2 = _rope_rows(x[32:64], ccol, scol)
        o_ref[0, hh, 0:16, :] = a1.astype(BF16)
        o_ref[0, hh, 16:32, :] = a2.astype(BF16)
        o_ref[0, hh, 32:48, :] = b1.astype(BF16)
        o_ref[0, hh, 48:64, :] = b2.astype(BF16)

    for hh in range(4):
        axial(_rms_rows(pd[hh * 64:(hh + 1) * 64], gq_ref[...], 64) * sd, qd_o, hh)
    for hh in range(2):
        axial(_rms_rows(pd[256 + hh * 64:256 + (hh + 1) * 64], gk_ref[...], 64), kd_o, hh)
        _store_v(vd_o, hh, pd[384 + hh * 64:384 + (hh + 1) * 64])


def _proj(x, batch, seq, gmix, wint, cols, mats, tabs):
    n, d = x.shape
    tm = PROJ_TM
    nsb = seq // tm
    naq, nak, qlat, kvlat, mq, mk, dq, dk, gq, gk = cols
    wuqt, wukvt = mats

    def full(a):
        return pl.BlockSpec(a.shape, lambda i: (0,) * a.ndim)

    tab_spec = pl.BlockSpec((16, tm), lambda i: (0, i % nsb))
    args = [x, gmix.reshape(1, d), wint, naq, nak, qlat, wuqt, kvlat, wukvt, mq, mk, dq, dk, gq, gk, *tabs]
    in_specs = [pl.BlockSpec((tm, d), lambda i: (i, 0))] + [full(a) for a in args[1:15]] + [tab_spec] * 6

    def head_out(nh, rows):
        return (jax.ShapeDtypeStruct((batch, nh, rows, seq), BF16),
                pl.BlockSpec((1, nh, rows, tm), lambda i: (i // nsb, 0, 0, i % nsb)))

    def v_out(nh):
        return (jax.ShapeDtypeStruct((batch, nh, nsb, HEAD_V, tm), BF16),
                pl.BlockSpec((1, nh, 1, HEAD_V, tm), lambda i: (i // nsb, 0, i % nsb, 0, 0)))

    outs = [head_out(4, 64), head_out(4, 64), v_out(4),
            head_out(4, 96), head_out(4, 96), v_out(4),
            head_out(8, 32), head_out(8, 32), v_out(4),
            head_out(4, 64), head_out(2, 64), v_out(2)]
    return pl.pallas_call(
        _proj_kernel,
        grid=(n // tm,),
        in_specs=in_specs,
        out_specs=[o[1] for o in outs],
        out_shape=[o[0] for o in outs],
        compiler_params=pltpu.CompilerParams(
            dimension_semantics=("parallel",), vmem_limit_bytes=VMEM_LIMIT),
        name="mix_proj",
    )(*args)


def _col_max(s):
    parts = [s]
    while parts[0].shape[0] > 64:
        half = parts[0].shape[0] // 2
        parts = [jnp.maximum(p[:half], p[half:]) for p in parts]
    return jnp.max(parts[0], axis=0, keepdims=True)


def _produce(k_blk, q_blk, slot, s_scr, mc_scr):
    s = _dot(k_blk, q_blk)
    s_scr[slot] = s
    mc_scr[slot] = _col_max(s)


def _col_sum(p):
    parts = [p]
    while parts[0].shape[0] > 64:
        half = parts[0].shape[0] // 2
        parts = [q[:half] + q[half:] for q in parts]
    return jnp.sum(parts[0], axis=0, keepdims=True)


def _consume(s, m_cur, c, idx, v_blk, m_scr, l_scr, acc_scr):
    m_prev = m_scr[idx]
    m_new = jnp.maximum(m_prev, m_cur + c)
    alpha = jnp.exp2(m_prev - m_new)
    p = jnp.exp2(s - (m_new - c))
    l_scr[idx] = alpha * l_scr[idx] + _col_sum(p)
    acc_scr[idx] = alpha * acc_scr[idx] + _dot(v_blk, p.astype(BF16))
    m_scr[idx] = m_new


def _flash_plain_kernel(q_ref, k_ref, v_ref, beta_ref, o_ref, s_scr, mc_scr, m_scr, l_scr, acc_scr, *, kv_map, n_k):
    n_h = len(kv_map)
    m_scr[...] = jnp.full_like(m_scr, NEG_BIG)
    l_scr[...] = jnp.zeros_like(l_scr)
    acc_scr[...] = jnp.zeros_like(acc_scr)

    def produce(i, hh):
        _produce(k_ref[0, kv_map[hh], i], q_ref[0, hh], hh % FLASH_SLOTS, s_scr, mc_scr)

    for hh in range(FLASH_AHEAD):
        produce(0, hh)

    def body(i, carry):
        nxt = jnp.minimum(i + 1, n_k - 1)
        for hh in range(n_h):
            ahead = hh + FLASH_AHEAD
            produce(i if ahead < n_h else nxt, ahead % n_h)
            slot = hh % FLASH_SLOTS
            _consume(s_scr[slot], mc_scr[slot], 0.0, hh, v_ref[0, kv_map[hh], i], m_scr, l_scr, acc_scr)
        return carry

    lax.fori_loop(0, n_k, body, 0)
    y = jnp.concatenate([acc_scr[hh] / l_scr[hh] for hh in range(n_h)], axis=0)
    o_ref[0] = _rms_rows(y, beta_ref[...], GROUP_W).astype(BF16)


def _flash_plain(qt, k, vt, beta_col, kv_map):
    b, hq, dq, s = qt.shape
    hk, n_k, t = k.shape[1], k.shape[2], k.shape[3]
    assert len(kv_map) % FLASH_SLOTS == 0
    return pl.pallas_call(
        functools.partial(_flash_plain_kernel, kv_map=kv_map, n_k=n_k),
        grid=(b, s // t),
        in_specs=[
            pl.BlockSpec((1, hq, dq, t), lambda bi, qi: (bi, 0, 0, qi)),
            pl.BlockSpec((1, hk, n_k, t, dq), lambda bi, qi: (bi, 0, 0, 0, 0)),
            pl.BlockSpec((1, hk, n_k, HEAD_V, t), lambda bi, qi: (bi, 0, 0, 0, 0)),
            pl.BlockSpec((GROUP_W, 1), lambda bi, qi: (0, 0)),
        ],
        out_specs=pl.BlockSpec((1, GROUP_W, t), lambda bi, qi: (bi, 0, qi)),
        out_shape=jax.ShapeDtypeStruct((b, GROUP_W, s), BF16),
        scratch_shapes=[pltpu.VMEM((FLASH_SLOTS, t, t), F32), pltpu.VMEM((FLASH_SLOTS, 1, t), F32),
                        pltpu.VMEM((hq, 1, t), F32), pltpu.VMEM((hq, 1, t), F32), pltpu.VMEM((hq, HEAD_V, t), F32)],
        compiler_params=pltpu.CompilerParams(
            dimension_semantics=("parallel", "parallel"), vmem_limit_bytes=VMEM_LIMIT),
        name="flash_plain",
    )(qt, k, vt, beta_col)


def _flash_diff_kernel(t5_ref, q_ref, k_ref, v_ref, bias_ref, lam_ref, subln_ref, o_ref,
                       qpad_scr, s_scr, mc_scr, m_scr, l_scr, acc_scr, *, lambda_init, n_k):
    qi = pl.program_id(1)
    n_m = 8
    dq = q_ref.shape[2]
    m_scr[...] = jnp.full_like(m_scr, NEG_BIG)
    l_scr[...] = jnp.zeros_like(l_scr)
    acc_scr[...] = jnp.zeros_like(acc_scr)
    qpad_scr[...] = jnp.zeros_like(qpad_scr)
    for j in range(n_m):
        qpad_scr[j, (j % 2) * dq:(j % 2 + 1) * dq, :] = q_ref[0, j]

    def produce(i, j):
        _produce(k_ref[0, j // 2, i], qpad_scr[j], j % FLASH_SLOTS, s_scr, mc_scr)

    for j in range(FLASH_AHEAD):
        produce(0, j)

    def sweep(lo, hi, near, bucket):
        def body(i, carry):
            nxt = jnp.minimum(i + 1, n_k - 1)
            for j in range(n_m):
                hh = j // 2
                ahead = j + FLASH_AHEAD
                produce(i if ahead < n_m else nxt, ahead % n_m)
                slot = j % FLASH_SLOTS
                if near:
                    s = s_scr[slot] + bias_ref[i - qi + 1, hh]
                    _consume(s, _col_max(s), 0.0, j, v_ref[0, hh, i], m_scr, l_scr, acc_scr)
                else:
                    c = t5_ref[bucket * 4 + hh] * LOG2E
                    _consume(s_scr[slot], mc_scr[slot], c, j, v_ref[0, hh, i], m_scr, l_scr, acc_scr)
            return carry

        lax.fori_loop(lo, hi, body, 0)

    near_lo = jnp.maximum(qi - 1, 0)
    near_hi = jnp.minimum(qi + 2, n_k)
    sweep(0, near_lo, False, T5_BUCKETS // 2 - 1)
    sweep(near_lo, near_hi, True, None)
    sweep(near_hi, n_k, False, T5_BUCKETS - 1)

    lp = lam_ref[...]
    lam = (jnp.exp(jnp.sum(lp[0:1] * lp[1:2], axis=1, keepdims=True))
           - jnp.exp(jnp.sum(lp[2:3] * lp[3:4], axis=1, keepdims=True)) + lambda_init)
    for hh in range(4):
        y = acc_scr[2 * hh] / l_scr[2 * hh] - lam * (acc_scr[2 * hh + 1] / l_scr[2 * hh + 1])
        y = _rms_rows(y, subln_ref[...], HEAD_V) * (1.0 - lambda_init)
        o_ref[0, hh * HEAD_V:(hh + 1) * HEAD_V, :] = y.astype(BF16)


def _flash_diff(t5_flat, qt, k, vt, bias, lam_params, subln_col, lambda_init):
    b, hq, dq, s = qt.shape
    n_k, t = k.shape[2], k.shape[3]
    return pl.pallas_call(
        functools.partial(_flash_diff_kernel, lambda_init=lambda_init, n_k=n_k),
        grid=(b, s // t),
        in_specs=[
            pl.BlockSpec(memory_space=pltpu.SMEM),
            pl.BlockSpec((1, hq, dq, t), lambda bi, qi: (bi, 0, 0, qi)),
            pl.BlockSpec((1, 4, n_k, t, 2 * dq), lambda bi, qi: (bi, 0, 0, 0, 0)),
            pl.BlockSpec((1, 4, n_k, HEAD_V, t), lambda bi, qi: (bi, 0, 0, 0, 0)),
            pl.BlockSpec(bias.shape, lambda bi, qi: (0, 0, 0, 0), pipeline_mode=pl.Buffered(1)),
            pl.BlockSpec(lam_params.shape, lambda bi, qi: (0, 0)),
            pl.BlockSpec((HEAD_V, 1), lambda bi, qi: (0, 0)),
        ],
        out_specs=pl.BlockSpec((1, GROUP_W, t), lambda bi, qi: (bi, 0, qi)),
        out_shape=jax.ShapeDtypeStruct((b, GROUP_W, s), BF16),
        scratch_shapes=[pltpu.VMEM((hq, 2 * dq, t), BF16),
                        pltpu.VMEM((FLASH_SLOTS, t, t), F32), pltpu.VMEM((FLASH_SLOTS, 1, t), F32),
                        pltpu.VMEM((hq, 1, t), F32), pltpu.VMEM((hq, 1, t), F32), pltpu.VMEM((hq, HEAD_V, t), F32)],
        compiler_params=pltpu.CompilerParams(
            dimension_semantics=("parallel", "parallel"), vmem_limit_bytes=VMEM_LIMIT),
        name="flash_diff",
    )(t5_flat, qt, k, vt, bias, lam_params, subln_col)


def _t5_bias_kernel(tab_ref, o_ref, *, t):
    d = pl.program_id(0) - 1
    hh = pl.program_id(1)
    rows = 64
    half = T5_BUCKETS // 2
    max_exact = half // 2

    def body(r, carry):
        kk = lax.broadcasted_iota(jnp.int32, (rows, t), 0) + r * rows
        qq = lax.broadcasted_iota(jnp.int32, (rows, t), 1)
        rel = d * t + kk - qq
        n = jnp.abs(rel)
        large = max_exact + (jnp.log(jnp.maximum(n, 1).astype(F32) / max_exact)
                             / math.log(T5_MAX_DIST / max_exact) * (half - max_exact)).astype(jnp.int32)
        large = jnp.minimum(large, half - 1)
        bucket = jnp.where(rel > 0, half, 0) + jnp.where(n < max_exact, n, large)
        acc = jnp.zeros((rows, t), F32)
        for bkt in range(T5_BUCKETS):
            acc = jnp.where(bucket == bkt, tab_ref[bkt * 4 + hh], acc)
        o_ref[0, 0, pl.ds(pl.multiple_of(r * rows, rows), rows), :] = acc * LOG2E
        return carry

    lax.fori_loop(0, t // rows, body, 0)


def _t5_bias(t5_flat):
    t = FLASH_T
    return pl.pallas_call(
        functools.partial(_t5_bias_kernel, t=t),
        grid=(3, 4),
        in_specs=[pl.BlockSpec(memory_space=pltpu.SMEM)],
        out_specs=pl.BlockSpec((1, 1, t, t), lambda d, h: (d, h, 0, 0)),
        out_shape=jax.ShapeDtypeStruct((3, 4, t, t), F32),
        name="t5_bias",
    )(t5_flat)


def _na_window(kind, j, i, grid_rows):
    r0 = (0, NA_Q_ROWS, grid_rows - NA_Q_ROWS)[kind]
    start = (0, 0, grid_rows - NA_K_ROWS)[kind]
    krow, qrow = start + j, r0 + i
    lo = min(max(qrow - NA_KH // 2, 0), grid_rows - NA_KH)
    return lo <= krow < lo + NA_KH, krow - qrow + NA_KH - 1


def _na_bias_kernel(rpb_ref, o_ref, t_scr, *, grid_rows):
    hh = pl.program_id(0)
    n_dr, n_dc = 2 * NA_KH - 1, 2 * NA_KW - 1
    shape = (GRID_W, NA_Q_ROWS * GRID_W)
    lane = lax.broadcasted_iota(jnp.int32, shape, 1)
    kc = lax.broadcasted_iota(jnp.int32, shape, 0)
    qc = lane % GRID_W
    grp = lane // GRID_W
    dcm = jnp.clip(kc - qc + NA_KW - 1, 0, n_dc - 1)
    qs = jnp.clip(qc - NA_KW // 2, 0, GRID_W - NA_KW)
    col_ok = (kc >= qs) & (kc < qs + NA_KW)

    for dr in range(n_dr):
        base = (hh * n_dr + dr) * n_dc
        t_scr[dr] = lax.fori_loop(
            0, n_dc, lambda dc, acc: jnp.where(dcm == dc, rpb_ref[base + dc], acc), jnp.zeros(shape, F32)) * LOG2E

    for kind in range(3):
        for j in range(NA_K_ROWS):
            wins = [_na_window(kind, j, i, grid_rows) for i in range(NA_Q_ROWS)]
            blk = jnp.full(shape, NEG_BIG, F32)
            for i, (inside, dr) in enumerate(wins):
                if inside:
                    blk = jnp.where((grp == i) & col_ok, t_scr[dr], blk)
            o_ref[kind, 0, j * GRID_W:(j + 1) * GRID_W, :] = blk


def _na_bias(rpb_flat, grid_rows):
    kt, qt = NA_K_ROWS * GRID_W, NA_Q_ROWS * GRID_W
    return pl.pallas_call(
        functools.partial(_na_bias_kernel, grid_rows=grid_rows),
        grid=(4,),
        in_specs=[pl.BlockSpec(memory_space=pltpu.SMEM)],
        out_specs=pl.BlockSpec((3, 1, kt, qt), lambda h: (0, h, 0, 0)),
        out_shape=jax.ShapeDtypeStruct((3, 4, kt, qt), F32),
        scratch_shapes=[pltpu.VMEM((2 * NA_KH - 1, GRID_W, qt), F32)],
        name="na_bias",
    )(rpb_flat)


def _na_kernel(q_ref, k0_ref, k1_ref, k2_ref, v0_ref, v1_ref, v2_ref, bias_ref, beta_ref, o_ref):
    qt = NA_Q_ROWS * GRID_W
    k_refs = (k0_ref, k1_ref, k2_ref)
    v_refs = (v0_ref, v1_ref, v2_ref)

    def logits(hh):
        q = q_ref[0, hh]
        return [_dot(k_refs[j][0, hh], q) + bias_ref[0, hh, j * qt:(j + 1) * qt, :] for j in range(3)]

    outs = []
    ss_next = logits(0)
    for hh in range(4):
        ss = ss_next
        if hh + 1 < 4:
            ss_next = logits(hh + 1)
        m = functools.reduce(jnp.maximum, [_col_max(s) for s in ss])
        acc = l = None
        for j in range(3):
            p = jnp.exp2(ss[j] - m)
            pv = _dot(v_refs[j][0, hh, 0], p.astype(BF16))
            acc = pv if acc is None else acc + pv
            l = _col_sum(p) if l is None else l + _col_sum(p)
        outs.append(acc / l)
    y = jnp.concatenate(outs, axis=0)
    o_ref[0] = _rms_rows(y, beta_ref[...], GROUP_W).astype(BF16)


def _na(qt_arr, k, vt, bias, beta_col):
    b, nh, d, s = qt_arr.shape
    qt = NA_Q_ROWS * GRID_W
    kt = NA_K_ROWS * GRID_W
    n_t = s // qt
    n_win = kt // qt

    def win(j):
        return lambda bi, ti: jnp.clip(ti - 1, 0, n_t - n_win) + j

    def kind(bi, ti):
        return (jnp.where(ti == 0, 0, jnp.where(ti == n_t - 1, 2, 1)), 0, 0, 0)

    k_specs = [pl.BlockSpec((1, nh, qt, d), (lambda j: lambda bi, ti: (bi, 0, win(j)(bi, ti), 0))(j))
               for j in range(n_win)]
    per = vt.shape[4] // qt
    v_specs = [pl.BlockSpec((1, nh, 1, HEAD_V, qt),
                            (lambda j: lambda bi, ti: (bi, 0, win(j)(bi, ti) // per, 0, win(j)(bi, ti) % per))(j))
               for j in range(n_win)]
    return pl.pallas_call(
        _na_kernel,
        grid=(b, n_t),
        in_specs=[pl.BlockSpec((1, nh, d, qt), lambda bi, ti: (bi, 0, 0, ti))] + k_specs + v_specs + [
            pl.BlockSpec((1, nh, kt, qt), kind),
            pl.BlockSpec((GROUP_W, 1), lambda bi, ti: (0, 0)),
        ],
        out_specs=pl.BlockSpec((1, GROUP_W, qt), lambda bi, ti: (bi, 0, ti)),
        out_shape=jax.ShapeDtypeStruct((b, GROUP_W, s), BF16),
        compiler_params=pltpu.CompilerParams(
            dimension_semantics=("parallel", "parallel"), vmem_limit_bytes=VMEM_LIMIT),
        name="na_attn",
    )(qt_arr, k, k, k, vt, vt, vt, bias, beta_col)


def _rope_tables(pos, dim):
    inv = jnp.exp(-math.log(ROPE_THETA) * jnp.arange(0, dim, 2, dtype=F32) / dim)
    ang = pos.astype(F32)[:, None] * inv[None, :]
    return jnp.cos(ang).T, jnp.sin(ang).T


def _col(v):
    return v.reshape(-1, 1).astype(F32)


def _lambda_init(layer):
    return 0.8 - 0.6 * math.exp(-0.3 * layer)


def kernel(x, ffn1_norm, ffn1_w_gate, ffn1_w_up, ffn1_w_down, mix_norm, w_in, na_q_norm, na_k_norm, na_rpb, na_beta, mla_q_lat_norm, mla_w_uq, mla_kv_lat_norm, mla_w_ukv, mla_q_norm, mla_k_norm, mla_beta, diff_q_norm, diff_k_norm, diff_lambda, diff_subln, gqa_q_norm, gqa_k_norm, gqa_beta, w_out, ffn2_norm, ffn2_w_gate, ffn2_w_up, ffn2_w_down, final_norm, t5_bias):
    batch, seq, d = x.shape
    grid_rows = seq // GRID_W
    pos = jnp.arange(seq, dtype=jnp.int32)
    tabs = (*_rope_tables(pos, 32), *_rope_tables(pos // GRID_W, 32), *_rope_tables(pos % GRID_W, 32))
    t5_flat = t5_bias.reshape(-1).astype(F32)
    t5_tiles = _t5_bias(t5_flat)

    xf = x.reshape(batch * seq, d)
    for l in range(N_LAYERS):
        xf = _ffn(xf, ffn1_norm[l], ffn1_w_gate[l].astype(BF16), ffn1_w_up[l].astype(BF16),
                  ffn1_w_down[l].astype(BF16))
        cols = tuple(_col(v[l]) for v in (na_q_norm, na_k_norm, mla_q_lat_norm, mla_kv_lat_norm,
                                          mla_q_norm, mla_k_norm, diff_q_norm, diff_k_norm,
                                          gqa_q_norm, gqa_k_norm))
        mats = (mla_w_uq[l].T.astype(BF16), mla_w_ukv[l].T.astype(BF16))
        (qa, ka, va, qb, kb, vb, qc, kc, vc, qd, kd, vd) = _proj(
            xf, batch, seq, mix_norm[l], w_in[l].T.astype(BF16), cols, mats, tabs)
        n_k = seq // FLASH_T
        ka = jnp.swapaxes(ka, 2, 3)
        kb = jnp.swapaxes(kb, 2, 3).reshape(batch, 4, n_k, FLASH_T, 96)
        kc = jnp.swapaxes(kc.reshape(batch, 4, 64, seq), 2, 3).reshape(batch, 4, n_k, FLASH_T, 64)
        kd = jnp.swapaxes(kd, 2, 3).reshape(batch, 2, n_k, FLASH_T, 64)

        ya = _na(qa, ka, va, _na_bias(na_rpb[l].reshape(-1).astype(F32), grid_rows), _col(na_beta[l]))
        yb = _flash_plain(qb, kb, vb, _col(mla_beta[l]), (0, 1, 2, 3))
        yc = _flash_diff(t5_flat, qc, kc, vc, t5_tiles, diff_lambda[l].astype(F32), _col(diff_subln[l]),
                         _lambda_init(l))
        yd = _flash_plain(qd, kd, vd, _col(gqa_beta[l]), (0, 0, 1, 1))

        xf = _ffn(xf, ffn2_norm[l], ffn2_w_gate[l].astype(BF16), ffn2_w_up[l].astype(BF16),
                  ffn2_w_down[l].astype(BF16), mix=((ya, yb, yc, yd), w_out[l].astype(BF16)),
                  final_g=final_norm[l], seq=seq)
    return xf.reshape(batch, seq, d)
```

```python
import functools
import math

import jax
import jax.numpy as jnp
from jax import lax
from jax.experimental import pallas as pl
from jax.experimental.pallas import tpu as pltpu

F32 = jnp.float32
BF16 = jnp.bfloat16

EPS = 1e-6
NEG_BIG = -1e30
LOG2E = 1.4426950408889634
ROPE_THETA = 10000.0
GRID_W = 64
N_LAYERS = 2

HEAD_V = 64
V_ROWS = 80
GROUP_W = 256

NA_KH, NA_KW = 8, 16
NA_Q_ROWS = 4
NA_K_ROWS = 12
T5_BUCKETS = 32
T5_MAX_DIST = 128

FFN_TM = 1024
FFN_CHUNK = 1024
FLASH_T = 512
FLASH_AHEAD = 3
FLASH_SLOTS = 4
PROJ_TM = FLASH_T

VMEM_LIMIT = 56 * 1024 * 1024


def _rms_rows(x, g_col, n):
    r = lax.rsqrt(jnp.sum(x * x, axis=0, keepdims=True) * (1.0 / n) + EPS)
    return (x * r) * g_col


def _rms_lanes(x, g_row):
    r = lax.rsqrt(jnp.mean(x * x, axis=-1, keepdims=True) + EPS)
    return (x * r) * g_row


def _dot(a, b):
    return jnp.dot(a, b, preferred_element_type=F32)


def _ffn_kernel(*refs, has_mix, has_final, chunks):
    x_ref, g_ref, wg_ref, wu_ref, wd_ref = refs[:5]
    rest = refs[5:]
    if has_mix:
        y_refs, wo_ref, rest = rest[:4], rest[4], rest[5:]
    if has_final:
        gf_ref, rest = rest[0], rest[1:]
    (o_ref,) = rest

    x = x_ref[...]
    if has_mix:
        for g in range(4):
            x = x + lax.dot_general(
                y_refs[g][0], wo_ref[g * GROUP_W:(g + 1) * GROUP_W, :],
                (((0,), (0,)), ((), ())), preferred_element_type=F32)
    o_ref[...] = x
    h = _rms_lanes(x, g_ref[...]).astype(BF16)
    acc = None
    for lo, hi in chunks:
        gate = _dot(h, wg_ref[:, lo:hi])
        up = _dot(h, wu_ref[:, lo:hi])
        a = (gate / (1.0 + jnp.exp(-gate))) * up
        part = _dot(a.astype(BF16), wd_ref[lo:hi, :])
        acc = part if acc is None else acc + part
    out = o_ref[...] + 0.5 * acc
    if has_final:
        out = _rms_lanes(out, gf_ref[...])
    o_ref[...] = out


def _ffn(x, g, wg, wu, wd, mix=None, final_g=None, seq=None):
    n, d = x.shape
    d_ff = wg.shape[1]
    tm = FFN_TM
    chunks = tuple((lo, min(lo + FFN_CHUNK, d_ff)) for lo in range(0, d_ff, FFN_CHUNK))
    once = pl.Buffered(1)
    in_specs = [
        pl.BlockSpec((tm, d), lambda i: (i, 0)),
        pl.BlockSpec((1, d), lambda i: (0, 0)),
        pl.BlockSpec((d, d_ff), lambda i: (0, 0), pipeline_mode=once),
        pl.BlockSpec((d, d_ff), lambda i: (0, 0), pipeline_mode=once),
        pl.BlockSpec((d_ff, d), lambda i: (0, 0), pipeline_mode=once),
    ]
    args = [x, g.reshape(1, d), wg, wu, wd]
    if mix is not None:
        ys, wo = mix
        nsb = seq // tm
        for y in ys:
            in_specs.append(pl.BlockSpec((1, GROUP_W, tm), lambda i: (i // nsb, 0, i % nsb)))
            args.append(y)
        in_specs.append(pl.BlockSpec(wo.shape, lambda i: (0, 0), pipeline_mode=once))
        args.append(wo)
    if final_g is not None:
        in_specs.append(pl.BlockSpec((1, d), lambda i: (0, 0)))
        args.append(final_g.reshape(1, d))
    return pl.pallas_call(
        functools.partial(_ffn_kernel, has_mix=mix is not None, has_final=final_g is not None, chunks=chunks),
        grid=(n // tm,),
        in_specs=in_specs,
        out_specs=pl.BlockSpec((tm, d), lambda i: (i, 0)),
        out_shape=jax.ShapeDtypeStruct((n, d), F32),
        compiler_params=pltpu.CompilerParams(
            dimension_semantics=("parallel",), vmem_limit_bytes=VMEM_LIMIT),
        name="ffn_mix" if mix is not None else "ffn",
    )(*args)


def _rope_rows(x, cos, sin):
    x1, x2 = x[:16], x[16:]
    return x1 * cos - x2 * sin, x1 * sin + x2 * cos


def _store_v(v_ref, hh, v):
    v_ref[0, hh, 0, 0:HEAD_V, :] = v.astype(BF16)
    v_ref[0, hh, 0, HEAD_V:V_ROWS, :] = jnp.ones((V_ROWS - HEAD_V, v.shape[1]), BF16)


def _proj_kernel(x_ref, gmix_ref, wint_ref,
                 naq_ref, nak_ref,
                 qlat_ref, wuqt_ref, kvlat_ref, wukvt_ref, mq_ref, mk_ref,
                 dq_ref, dk_ref, gq_ref, gk_ref,
                 cseq_ref, sseq_ref, crow_ref, srow_ref, ccol_ref, scol_ref,
                 qa_o, ka_o, va_o, qb_o, kb_o, vb_o, qc_o, kc_o, vc_o, qd_o, kd_o, vd_o):
    h = _rms_lanes(x_ref[...], gmix_ref[...]).astype(BF16)

    def proj(lo, hi):
        return lax.dot_general(wint_ref[lo:hi, :], h, (((1,), (1,)), ((), ())),
                               preferred_element_type=F32)

    pa = proj(0, 768)
    sa = LOG2E * 64 ** -0.5
    for hh in range(4):
        q = _rms_rows(pa[hh * 64:(hh + 1) * 64], naq_ref[...], 64) * sa
        k = _rms_rows(pa[256 + hh * 64:256 + (hh + 1) * 64], nak_ref[...], 64)
        qa_o[0, hh] = q.astype(BF16)
        ka_o[0, hh] = k.astype(BF16)
        _store_v(va_o, hh, pa[512 + hh * 64:512 + (hh + 1) * 64])

    pb = proj(768, 1184)
    cq = _rms_rows(pb[0:256], qlat_ref[...], 256).astype(BF16)
    qb = _dot(wuqt_ref[...], cq)
    ckv = _rms_rows(pb[256:384], kvlat_ref[...], 128).astype(BF16)
    kv = _dot(wukvt_ref[...], ckv)
    kr = pb[384:416]
    cseq, sseq = cseq_ref[...], sseq_ref[...]
    sb = LOG2E * 96 ** -0.5
    for hh in range(4):
        q = _rms_rows(qb[hh * 96:(hh + 1) * 96], mq_ref[...], 96) * sb
        r1, r2 = _rope_rows(q[64:96], cseq, sseq)
        qb_o[0, hh, 0:64, :] = q[0:64].astype(BF16)
        qb_o[0, hh, 64:80, :] = r1.astype(BF16)
        qb_o[0, hh, 80:96, :] = r2.astype(BF16)
        k = jnp.concatenate([kv[hh * 128:hh * 128 + 64], kr], axis=0)
        k = _rms_rows(k, mk_ref[...], 96)
        r1, r2 = _rope_rows(k[64:96], cseq, sseq)
        kb_o[0, hh, 0:64, :] = k[0:64].astype(BF16)
        kb_o[0, hh, 64:80, :] = r1.astype(BF16)
        kb_o[0, hh, 80:96, :] = r2.astype(BF16)
        _store_v(vb_o, hh, kv[hh * 128 + 64:hh * 128 + 128])

    pc = proj(1184, 1952)
    sc = LOG2E * 32 ** -0.5
    for j in range(8):
        q = _rms_rows(pc[j * 32:(j + 1) * 32], dq_ref[...], 32) * sc
        k = _rms_rows(pc[256 + j * 32:256 + (j + 1) * 32], dk_ref[...], 32)
        qc_o[0, j] = q.astype(BF16)
        kc_o[0, j] = k.astype(BF16)
    for hh in range(4):
        _store_v(vc_o, hh, pc[512 + hh * 64:512 + (hh + 1) * 64])

    pd = proj(1952, 2464)
    crow, srow, ccol, scol = crow_ref[...], srow_ref[...], ccol_ref[...], scol_ref[...]
    sd = LOG2E * 64 ** -0.5

    def axial(x, o_ref, hh):
        a1, a2 = _rope_rows(x[0:32], crow, srow)
        b1, b2 = _rope_rows(x[32:64], ccol, scol)
        o_ref[0, hh, 0:16, :] = a1.astype(BF16)
        o_ref[0, hh, 16:32, :] = a2.astype(BF16)
        o_ref[0, hh, 32:48, :] = b1.astype(BF16)
        o_ref[0, hh, 48:64, :] = b2.astype(BF16)

    for hh in range(4):
        axial(_rms_rows(pd[hh * 64:(hh + 1) * 64], gq_ref[...], 64) * sd, qd_o, hh)
    for hh in range(2):
        axial(_rms_rows(pd[256 + hh * 64:256 + (hh + 1) * 64], gk_ref[...], 64), kd_o, hh)
        _store_v(vd_o, hh, pd[384 + hh * 64:384 + (hh + 1) * 64])


def _proj(x, batch, seq, gmix, wint, cols, mats, tabs):
    n, d = x.shape
    tm = PROJ_TM
    nsb = seq // tm
    naq, nak, qlat, kvlat, mq, mk, dq, dk, gq, gk = cols
    wuqt, wukvt = mats

    def full(a):
        return pl.BlockSpec(a.shape, lambda i: (0,) * a.ndim)

    tab_spec = pl.BlockSpec((16, tm), lambda i: (0, i % nsb))
    args = [x, gmix.reshape(1, d), wint, naq, nak, qlat, wuqt, kvlat, wukvt, mq, mk, dq, dk, gq, gk, *tabs]
    in_specs = [pl.BlockSpec((tm, d), lambda i: (i, 0))] + [full(a) for a in args[1:15]] + [tab_spec] * 6

    def head_out(nh, rows):
        return (jax.ShapeDtypeStruct((batch, nh, rows, seq), BF16),
                pl.BlockSpec((1, nh, rows, tm), lambda i: (i // nsb, 0, 0, i % nsb)))

    def v_out(nh):
        return (jax.ShapeDtypeStruct((batch, nh, nsb, V_ROWS, tm), BF16),
                pl.BlockSpec((1, nh, 1, V_ROWS, tm), lambda i: (i // nsb, 0, i % nsb, 0, 0)))

    outs = [head_out(4, 64), head_out(4, 64), v_out(4),
            head_out(4, 96), head_out(4, 96), v_out(4),
            head_out(8, 32), head_out(8, 32), v_out(4),
            head_out(4, 64), head_out(2, 64), v_out(2)]
    return pl.pallas_call(
        _proj_kernel,
        grid=(n // tm,),
        in_specs=in_specs,
        out_specs=[o[1] for o in outs],
        out_shape=[o[0] for o in outs],
        compiler_params=pltpu.CompilerParams(
            dimension_semantics=("parallel",), vmem_limit_bytes=VMEM_LIMIT),
        name="mix_proj",
    )(*args)


def _col_max(s):
    parts = [s]
    while parts[0].shape[0] > 64:
        half = parts[0].shape[0] // 2
        parts = [jnp.maximum(p[:half], p[half:]) for p in parts]
    return jnp.max(parts[0], axis=0, keepdims=True)


def _produce(k_blk, q_blk, slot, s_scr, mc_scr):
    s = _dot(k_blk, q_blk)
    s_scr[slot] = s
    mc_scr[slot] = _col_max(s)


def _consume(s, m_cur, c, idx, v_blk, m_scr, acc_scr):
    m_prev = m_scr[idx]
    m_new = jnp.maximum(m_prev, m_cur + c)
    alpha = jnp.exp2(m_prev - m_new)
    p = jnp.exp2(s - (m_new - c)).astype(BF16)
    acc_scr[idx] = alpha * acc_scr[idx] + _dot(v_blk, p)
    m_scr[idx] = m_new


def _finish_head(acc):
    return acc[0:HEAD_V] / acc[HEAD_V:HEAD_V + 1]


def _flash_plain_kernel(q_ref, k_ref, v_ref, beta_ref, o_ref, s_scr, mc_scr, m_scr, acc_scr, *, kv_map, n_k):
    n_h = len(kv_map)
    m_scr[...] = jnp.full_like(m_scr, NEG_BIG)
    acc_scr[...] = jnp.zeros_like(acc_scr)

    def produce(i, hh):
        _produce(k_ref[0, kv_map[hh], i], q_ref[0, hh], hh % FLASH_SLOTS, s_scr, mc_scr)

    for hh in range(FLASH_AHEAD):
        produce(0, hh)

    def body(i, carry):
        nxt = jnp.minimum(i + 1, n_k - 1)
        for hh in range(n_h):
            ahead = hh + FLASH_AHEAD
            produce(i if ahead < n_h else nxt, ahead % n_h)
            slot = hh % FLASH_SLOTS
            _consume(s_scr[slot], mc_scr[slot], 0.0, hh, v_ref[0, kv_map[hh], i], m_scr, acc_scr)
        return carry

    lax.fori_loop(0, n_k, body, 0)
    y = jnp.concatenate([_finish_head(acc_scr[hh]) for hh in range(n_h)], axis=0)
    o_ref[0] = _rms_rows(y, beta_ref[...], GROUP_W).astype(BF16)


def _flash_plain(qt, k, vt, beta_col, kv_map):
    b, hq, dq, s = qt.shape
    hk, n_k, t = k.shape[1], k.shape[2], k.shape[3]
    assert len(kv_map) % FLASH_SLOTS == 0
    return pl.pallas_call(
        functools.partial(_flash_plain_kernel, kv_map=kv_map, n_k=n_k),
        grid=(b, s // t),
        in_specs=[
            pl.BlockSpec((1, hq, dq, t), lambda bi, qi: (bi, 0, 0, qi)),
            pl.BlockSpec((1, hk, n_k, t, dq), lambda bi, qi: (bi, 0, 0, 0, 0)),
            pl.BlockSpec((1, hk, n_k, V_ROWS, t), lambda bi, qi: (bi, 0, 0, 0, 0)),
            pl.BlockSpec((GROUP_W, 1), lambda bi, qi: (0, 0)),
        ],
        out_specs=pl.BlockSpec((1, GROUP_W, t), lambda bi, qi: (bi, 0, qi)),
        out_shape=jax.ShapeDtypeStruct((b, GROUP_W, s), BF16),
        scratch_shapes=[pltpu.VMEM((FLASH_SLOTS, t, t), F32), pltpu.VMEM((FLASH_SLOTS, 1, t), F32),
                        pltpu.VMEM((hq, 1, t), F32), pltpu.VMEM((hq, V_ROWS, t), F32)],
        compiler_params=pltpu.CompilerParams(
            dimension_semantics=("parallel", "parallel"), vmem_limit_bytes=VMEM_LIMIT),
        name="flash_plain",
    )(qt, k, vt, beta_col)


def _flash_diff_kernel(t5_ref, q_ref, k_ref, v_ref, bias_ref, lam_ref, subln_ref, o_ref,
                       qpad_scr, s_scr, mc_scr, m_scr, acc_scr, *, lambda_init, n_k):
    qi = pl.program_id(1)
    n_m = 8
    dq = q_ref.shape[2]
    m_scr[...] = jnp.full_like(m_scr, NEG_BIG)
    acc_scr[...] = jnp.zeros_like(acc_scr)
    qpad_scr[...] = jnp.zeros_like(qpad_scr)
    for j in range(n_m):
        qpad_scr[j, (j % 2) * dq:(j % 2 + 1) * dq, :] = q_ref[0, j]

    def produce(i, j):
        _produce(k_ref[0, j // 2, i], qpad_scr[j], j % FLASH_SLOTS, s_scr, mc_scr)

    for j in range(FLASH_AHEAD):
        produce(0, j)

    def sweep(lo, hi, near, bucket):
        def body(i, carry):
            nxt = jnp.minimum(i + 1, n_k - 1)
            for j in range(n_m):
                hh = j // 2
                ahead = j + FLASH_AHEAD
                produce(i if ahead < n_m else nxt, ahead % n_m)
                slot = j % FLASH_SLOTS
                if near:
                    s = s_scr[slot] + bias_ref[i - qi + 1, hh]
                    _consume(s, _col_max(s), 0.0, j, v_ref[0, hh, i], m_scr, acc_scr)
                else:
                    c = t5_ref[bucket * 4 + hh] * LOG2E
                    _consume(s_scr[slot], mc_scr[slot], c, j, v_ref[0, hh, i], m_scr, acc_scr)
            return carry

        lax.fori_loop(lo, hi, body, 0)

    near_lo = jnp.maximum(qi - 1, 0)
    near_hi = jnp.minimum(qi + 2, n_k)
    sweep(0, near_lo, False, T5_BUCKETS // 2 - 1)
    sweep(near_lo, near_hi, True, None)
    sweep(near_hi, n_k, False, T5_BUCKETS - 1)

    lp = lam_ref[...]
    lam = (jnp.exp(jnp.sum(lp[0:1] * lp[1:2], axis=1, keepdims=True))
           - jnp.exp(jnp.sum(lp[2:3] * lp[3:4], axis=1, keepdims=True)) + lambda_init)
    for hh in range(4):
        y = _finish_head(acc_scr[2 * hh]) - lam * _finish_head(acc_scr[2 * hh + 1])
        y = _rms_rows(y, subln_ref[...], HEAD_V) * (1.0 - lambda_init)
        o_ref[0, hh * HEAD_V:(hh + 1) * HEAD_V, :] = y.astype(BF16)


def _flash_diff(t5_flat, qt, k, vt, bias, lam_params, subln_col, lambda_init):
    b, hq, dq, s = qt.shape
    n_k, t = k.shape[2], k.shape[3]
    return pl.pallas_call(
        functools.partial(_flash_diff_kernel, lambda_init=lambda_init, n_k=n_k),
        grid=(b, s // t),
        in_specs=[
            pl.BlockSpec(memory_space=pltpu.SMEM),
            pl.BlockSpec((1, hq, dq, t), lambda bi, qi: (bi, 0, 0, qi)),
            pl.BlockSpec((1, 4, n_k, t, 2 * dq), lambda bi, qi: (bi, 0, 0, 0, 0)),
            pl.BlockSpec((1, 4, n_k, V_ROWS, t), lambda bi, qi: (bi, 0, 0, 0, 0)),
            pl.BlockSpec(bias.shape, lambda bi, qi: (0, 0, 0, 0), pipeline_mode=pl.Buffered(1)),
            pl.BlockSpec(lam_params.shape, lambda bi, qi: (0, 0)),
            pl.BlockSpec((HEAD_V, 1), lambda bi, qi: (0, 0)),
        ],
        out_specs=pl.BlockSpec((1, GROUP_W, t), lambda bi, qi: (bi, 0, qi)),
        out_shape=jax.ShapeDtypeStruct((b, GROUP_W, s), BF16),
        scratch_shapes=[pltpu.VMEM((hq, 2 * dq, t), BF16),
                        pltpu.VMEM((FLASH_SLOTS, t, t), F32), pltpu.VMEM((FLASH_SLOTS, 1, t), F32),
                        pltpu.VMEM((hq, 1, t), F32), pltpu.VMEM((hq, V_ROWS, t), F32)],
        compiler_params=pltpu.CompilerParams(
            dimension_semantics=("parallel", "parallel"), vmem_limit_bytes=VMEM_LIMIT),
        name="flash_diff",
    )(t5_flat, qt, k, vt, bias, lam_params, subln_col)


def _t5_bias_kernel(tab_ref, o_ref, *, t):
    d = pl.program_id(0) - 1
    rows = 16
    half = T5_BUCKETS // 2
    max_exact = half // 2

    def body(r, carry):
        kk = lax.broadcasted_iota(jnp.int32, (rows, t), 0) + r * rows
        qq = lax.broadcasted_iota(jnp.int32, (rows, t), 1)
        rel = d * t + kk - qq
        n = jnp.abs(rel)
        large = max_exact + (jnp.log(jnp.maximum(n, 1).astype(F32) / max_exact)
                             / math.log(T5_MAX_DIST / max_exact) * (half - max_exact)).astype(jnp.int32)
        large = jnp.minimum(large, half - 1)
        bucket = jnp.where(rel > 0, half, 0) + jnp.where(n < max_exact, n, large)
        accs = [jnp.zeros((rows, t), F32) for _ in range(4)]
        for bkt in range(T5_BUCKETS):
            hit = bucket == bkt
            accs = [jnp.where(hit, tab_ref[bkt * 4 + hh], accs[hh]) for hh in range(4)]
        for hh in range(4):
            o_ref[0, hh, pl.ds(pl.multiple_of(r * rows, rows), rows), :] = accs[hh] * LOG2E
        return carry

    lax.fori_loop(0, t // rows, body, 0)


def _t5_bias(t5_flat):
    t = FLASH_T
    return pl.pallas_call(
        functools.partial(_t5_bias_kernel, t=t),
        grid=(3,),
        in_specs=[pl.BlockSpec(memory_space=pltpu.SMEM)],
        out_specs=pl.BlockSpec((1, 4, t, t), lambda d: (d, 0, 0, 0)),
        out_shape=jax.ShapeDtypeStruct((3, 4, t, t), F32),
        name="t5_bias",
    )(t5_flat)


def _na_window(kind, j, i, grid_rows):
    r0 = (0, NA_Q_ROWS, grid_rows - NA_Q_ROWS)[kind]
    start = (0, 0, grid_rows - NA_K_ROWS)[kind]
    krow, qrow = start + j, r0 + i
    lo = min(max(qrow - NA_KH // 2, 0), grid_rows - NA_KH)
    return lo <= krow < lo + NA_KH, krow - qrow + NA_KH - 1


def _na_bias_kernel(rpb_ref, o_ref, t_scr, *, grid_rows):
    hh = pl.program_id(0)
    n_dr, n_dc = 2 * NA_KH - 1, 2 * NA_KW - 1
    shape = (GRID_W, NA_Q_ROWS * GRID_W)
    lane = lax.broadcasted_iota(jnp.int32, shape, 1)
    kc = lax.broadcasted_iota(jnp.int32, shape, 0)
    qc = lane % GRID_W
    grp = lane // GRID_W
    dcm = jnp.clip(kc - qc + NA_KW - 1, 0, n_dc - 1)
    qs = jnp.clip(qc - NA_KW // 2, 0, GRID_W - NA_KW)
    col_ok = (kc >= qs) & (kc < qs + NA_KW)

    for dr in range(n_dr):
        base = (hh * n_dr + dr) * n_dc
        t_scr[dr] = lax.fori_loop(
            0, n_dc, lambda dc, acc: jnp.where(dcm == dc, rpb_ref[base + dc], acc), jnp.zeros(shape, F32)) * LOG2E

    for kind in range(3):
        for j in range(NA_K_ROWS):
            wins = [_na_window(kind, j, i, grid_rows) for i in range(NA_Q_ROWS)]
            blk = jnp.full(shape, NEG_BIG, F32)
            for i, (inside, dr) in enumerate(wins):
                if inside:
                    blk = jnp.where((grp == i) & col_ok, t_scr[dr], blk)
            o_ref[kind, 0, j * GRID_W:(j + 1) * GRID_W, :] = blk


def _na_bias(rpb_flat, grid_rows):
    kt, qt = NA_K_ROWS * GRID_W, NA_Q_ROWS * GRID_W
    return pl.pallas_call(
        functools.partial(_na_bias_kernel, grid_rows=grid_rows),
        grid=(4,),
        in_specs=[pl.BlockSpec(memory_space=pltpu.SMEM)],
        out_specs=pl.BlockSpec((3, 1, kt, qt), lambda h: (0, h, 0, 0)),
        out_shape=jax.ShapeDtypeStruct((3, 4, kt, qt), F32),
        scratch_shapes=[pltpu.VMEM((2 * NA_KH - 1, GRID_W, qt), F32)],
        name="na_bias",
    )(rpb_flat)


def _na_kernel(q_ref, k0_ref, k1_ref, k2_ref, v0_ref, v1_ref, v2_ref, bias_ref, beta_ref, o_ref):
    qt = NA_Q_ROWS * GRID_W
    k_refs = (k0_ref, k1_ref, k2_ref)
    v_refs = (v0_ref, v1_ref, v2_ref)

    def logits(hh):
        q = q_ref[0, hh]
        return [_dot(k_refs[j][0, hh], q) + bias_ref[0, hh, j * qt:(j + 1) * qt, :] for j in range(3)]

    outs = []
    ss_next = logits(0)
    for hh in range(4):
        ss = ss_next
        if hh + 1 < 4:
            ss_next = logits(hh + 1)
        m = functools.reduce(jnp.maximum, [_col_max(s) for s in ss])
        acc = None
        for j in range(3):
            p = jnp.exp2(ss[j] - m).astype(BF16)
            pv = _dot(v_refs[j][0, hh, 0], p)
            acc = pv if acc is None else acc + pv
        outs.append(_finish_head(acc))
    y = jnp.concatenate(outs, axis=0)
    o_ref[0] = _rms_rows(y, beta_ref[...], GROUP_W).astype(BF16)


def _na(qt_arr, k, vt, bias, beta_col):
    b, nh, d, s = qt_arr.shape
    qt = NA_Q_ROWS * GRID_W
    kt = NA_K_ROWS * GRID_W
    n_t = s // qt
    n_win = kt // qt

    def win(j):
        return lambda bi, ti: jnp.clip(ti - 1, 0, n_t - n_win) + j

    def kind(bi, ti):
        return (jnp.where(ti == 0, 0, jnp.where(ti == n_t - 1, 2, 1)), 0, 0, 0)

    k_specs = [pl.BlockSpec((1, nh, qt, d), (lambda j: lambda bi, ti: (bi, 0, win(j)(bi, ti), 0))(j))
               for j in range(n_win)]
    per = vt.shape[4] // qt
    v_specs = [pl.BlockSpec((1, nh, 1, V_ROWS, qt),
                            (lambda j: lambda bi, ti: (bi, 0, win(j)(bi, ti) // per, 0, win(j)(bi, ti) % per))(j))
               for j in range(n_win)]
    return pl.pallas_call(
        _na_kernel,
        grid=(b, n_t),
        in_specs=[pl.BlockSpec((1, nh, d, qt), lambda bi, ti: (bi, 0, 0, ti))] + k_specs + v_specs + [
            pl.BlockSpec((1, nh, kt, qt), kind),
            pl.BlockSpec((GROUP_W, 1), lambda bi, ti: (0, 0)),
        ],
        out_specs=pl.BlockSpec((1, GROUP_W, qt), lambda bi, ti: (bi, 0, ti)),
        out_shape=jax.ShapeDtypeStruct((b, GROUP_W, s), BF16),
        compiler_params=pltpu.CompilerParams(
            dimension_semantics=("parallel", "parallel"), vmem_limit_bytes=VMEM_LIMIT),
        name="na_attn",
    )(qt_arr, k, k, k, vt, vt, vt, bias, beta_col)


def _rope_tables(pos, dim):
    inv = jnp.exp(-math.log(ROPE_THETA) * jnp.arange(0, dim, 2, dtype=F32) / dim)
    ang = pos.astype(F32)[:, None] * inv[None, :]
    return jnp.cos(ang).T, jnp.sin(ang).T


def _col(v):
    return v.reshape(-1, 1).astype(F32)


def _lambda_init(layer):
    return 0.8 - 0.6 * math.exp(-0.3 * layer)


def kernel(x, ffn1_norm, ffn1_w_gate, ffn1_w_up, ffn1_w_down, mix_norm, w_in, na_q_norm, na_k_norm, na_rpb, na_beta, mla_q_lat_norm, mla_w_uq, mla_kv_lat_norm, mla_w_ukv, mla_q_norm, mla_k_norm, mla_beta, diff_q_norm, diff_k_norm, diff_lambda, diff_subln, gqa_q_norm, gqa_k_norm, gqa_beta, w_out, ffn2_norm, ffn2_w_gate, ffn2_w_up, ffn2_w_down, final_norm, t5_bias):
    batch, seq, d = x.shape
    grid_rows = seq // GRID_W
    pos = jnp.arange(seq, dtype=jnp.int32)
    tabs = (*_rope_tables(pos, 32), *_rope_tables(pos // GRID_W, 32), *_rope_tables(pos % GRID_W, 32))
    t5_flat = t5_bias.reshape(-1).astype(F32)
    t5_tiles = _t5_bias(t5_flat)

    xf = x.reshape(batch * seq, d)
    for l in range(N_LAYERS):
        xf = _ffn(xf, ffn1_norm[l], ffn1_w_gate[l].astype(BF16), ffn1_w_up[l].astype(BF16),
                  ffn1_w_down[l].astype(BF16))
        cols = tuple(_col(v[l]) for v in (na_q_norm, na_k_norm, mla_q_lat_norm, mla_kv_lat_norm,
                                          mla_q_norm, mla_k_norm, diff_q_norm, diff_k_norm,
                                          gqa_q_norm, gqa_k_norm))
        mats = (mla_w_uq[l].T.astype(BF16), mla_w_ukv[l].T.astype(BF16))
        (qa, ka, va, qb, kb, vb, qc, kc, vc, qd, kd, vd) = _proj(
            xf, batch, seq, mix_norm[l], w_in[l].T.astype(BF16), cols, mats, tabs)
        n_k = seq // FLASH_T
        ka = jnp.swapaxes(ka, 2, 3)
        kb = jnp.swapaxes(kb, 2, 3).reshape(batch, 4, n_k, FLASH_T, 96)
        kc = jnp.swapaxes(kc.reshape(batch, 4, 64, seq), 2, 3).reshape(batch, 4, n_k, FLASH_T, 64)
        kd = jnp.swapaxes(kd, 2, 3).reshape(batch, 2, n_k, FLASH_T, 64)

        ya = _na(qa, ka, va, _na_bias(na_rpb[l].reshape(-1).astype(F32), grid_rows), _col(na_beta[l]))
        yb = _flash_plain(qb, kb, vb, _col(mla_beta[l]), (0, 1, 2, 3))
        yc = _flash_diff(t5_flat, qc, kc, vc, t5_tiles, diff_lambda[l].astype(F32), _col(diff_subln[l]),
                         _lambda_init(l))
        yd = _flash_plain(qd, kd, vd, _col(gqa_beta[l]), (0, 0, 1, 1))

        xf = _ffn(xf, ffn2_norm[l], ffn2_w_gate[l].astype(BF16), ffn2_w_up[l].astype(BF16),
                  ffn2_w_down[l].astype(BF16), mix=((ya, yb, yc, yd), w_out[l].astype(BF16)),
                  final_g=final_norm[l], seq=seq)
    return xf.reshape(batch, seq, d)
```

```python
import functools
import math

import jax
import jax.numpy as jnp
from jax import lax
from jax.experimental import pallas as pl
from jax.experimental.pallas import tpu as pltpu

F32 = jnp.float32
BF16 = jnp.bfloat16

EPS = 1e-6
NEG_BIG = -1e30
LOG2E = 1.4426950408889634
ROPE_THETA = 10000.0
GRID_W = 64
N_LAYERS = 2

HEAD_V = 64
V_ROWS = 80
GROUP_W = 256

NA_KH, NA_KW = 8, 16
NA_Q_ROWS = 4
NA_K_ROWS = 12
T5_BUCKETS = 32
T5_MAX_DIST = 128

FFN_TM = 1024
FFN_CHUNK = 1024
FLASH_T = 512
FLASH_AHEAD = 2
FLASH_SLOTS = 4
FLASH_UNROLL = 8
FLASH_DIFF_UNROLL = 4
PROJ_TM = FLASH_T

VMEM_LIMIT = 56 * 1024 * 1024


def _rms_rows(x, g_col, n):
    r = lax.rsqrt(jnp.sum(x * x, axis=0, keepdims=True) * (1.0 / n) + EPS)
    return (x * r) * g_col


def _rms_lanes(x, g_row):
    r = lax.rsqrt(jnp.mean(x * x, axis=-1, keepdims=True) + EPS)
    return (x * r) * g_row


def _dot(a, b):
    return jnp.dot(a, b, preferred_element_type=F32)


def _ffn_kernel(*refs, has_mix, has_final, chunks):
    x_ref, g_ref, wg_ref, wu_ref, wd_ref = refs[:5]
    rest = refs[5:]
    if has_mix:
        y_refs, wo_ref, rest = rest[:4], rest[4], rest[5:]
    if has_final:
        gf_ref, rest = rest[0], rest[1:]
    (o_ref,) = rest

    x = x_ref[...]
    if has_mix:
        for g in range(4):
            x = x + lax.dot_general(
                y_refs[g][0], wo_ref[g * GROUP_W:(g + 1) * GROUP_W, :],
                (((0,), (0,)), ((), ())), preferred_element_type=F32)
    o_ref[...] = x
    h = _rms_lanes(x, g_ref[...]).astype(BF16)
    acc = None
    for lo, hi in chunks:
        gate = _dot(h, wg_ref[:, lo:hi])
        up = _dot(h, wu_ref[:, lo:hi])
        a = (gate / (1.0 + jnp.exp(-gate))) * up
        part = _dot(a.astype(BF16), wd_ref[lo:hi, :])
        acc = part if acc is None else acc + part
    out = o_ref[...] + 0.5 * acc
    if has_final:
        out = _rms_lanes(out, gf_ref[...])
    o_ref[...] = out


def _ffn(x, g, wg, wu, wd, mix=None, final_g=None, seq=None):
    n, d = x.shape
    d_ff = wg.shape[1]
    tm = FFN_TM
    chunks = tuple((lo, min(lo + FFN_CHUNK, d_ff)) for lo in range(0, d_ff, FFN_CHUNK))
    once = pl.Buffered(1)
    in_specs = [
        pl.BlockSpec((tm, d), lambda i: (i, 0)),
        pl.BlockSpec((1, d), lambda i: (0, 0)),
        pl.BlockSpec((d, d_ff), lambda i: (0, 0), pipeline_mode=once),
        pl.BlockSpec((d, d_ff), lambda i: (0, 0), pipeline_mode=once),
        pl.BlockSpec((d_ff, d), lambda i: (0, 0), pipeline_mode=once),
    ]
    args = [x, g.reshape(1, d), wg, wu, wd]
    if mix is not None:
        ys, wo = mix
        nsb = seq // tm
        for y in ys:
            in_specs.append(pl.BlockSpec((1, GROUP_W, tm), lambda i: (i // nsb, 0, i % nsb)))
            args.append(y)
        in_specs.append(pl.BlockSpec(wo.shape, lambda i: (0, 0), pipeline_mode=once))
        args.append(wo)
    if final_g is not None:
        in_specs.append(pl.BlockSpec((1, d), lambda i: (0, 0)))
        args.append(final_g.reshape(1, d))
    return pl.pallas_call(
        functools.partial(_ffn_kernel, has_mix=mix is not None, has_final=final_g is not None, chunks=chunks),
        grid=(n // tm,),
        in_specs=in_specs,
        out_specs=pl.BlockSpec((tm, d), lambda i: (i, 0)),
        out_shape=jax.ShapeDtypeStruct((n, d), F32),
        compiler_params=pltpu.CompilerParams(
            dimension_semantics=("parallel",), vmem_limit_bytes=VMEM_LIMIT),
        name="ffn_mix" if mix is not None else "ffn",
    )(*args)


def _rope_rows(x, cos, sin):
    x1, x2 = x[:16], x[16:]
    return x1 * cos - x2 * sin, x1 * sin + x2 * cos


def _store_v(v_ref, hh, v):
    v_ref[0, hh, 0, 0:HEAD_V, :] = v.astype(BF16)
    v_ref[0, hh, 0, HEAD_V:V_ROWS, :] = jnp.ones((V_ROWS - HEAD_V, v.shape[1]), BF16)


def _proj_kernel(x_ref, gmix_ref, wint_ref,
                 naq_ref, nak_ref,
                 qlat_ref, wuqt_ref, kvlat_ref, wukvt_ref, mq_ref, mk_ref,
                 dq_ref, dk_ref, gq_ref, gk_ref,
                 cseq_ref, sseq_ref, crow_ref, srow_ref, ccol_ref, scol_ref,
                 qa_o, ka_o, va_o, qb_o, kb_o, vb_o, qc_o, kc_o, vc_o, qd_o, kd_o, vd_o):
    h = _rms_lanes(x_ref[...], gmix_ref[...]).astype(BF16)

    def proj(lo, hi):
        return lax.dot_general(wint_ref[lo:hi, :], h, (((1,), (1,)), ((), ())),
                               preferred_element_type=F32)

    pa = proj(0, 768)
    sa = LOG2E * 64 ** -0.5
    for hh in range(4):
        q = _rms_rows(pa[hh * 64:(hh + 1) * 64], naq_ref[...], 64) * sa
        k = _rms_rows(pa[256 + hh * 64:256 + (hh + 1) * 64], nak_ref[...], 64)
        qa_o[0, hh] = q.astype(BF16)
        ka_o[0, hh] = k.astype(BF16)
        _store_v(va_o, hh, pa[512 + hh * 64:512 + (hh + 1) * 64])

    pb = proj(768, 1184)
    cq = _rms_rows(pb[0:256], qlat_ref[...], 256).astype(BF16)
    qb = _dot(wuqt_ref[...], cq)
    ckv = _rms_rows(pb[256:384], kvlat_ref[...], 128).astype(BF16)
    kv = _dot(wukvt_ref[...], ckv)
    kr = pb[384:416]
    cseq, sseq = cseq_ref[...], sseq_ref[...]
    sb = LOG2E * 96 ** -0.5
    for hh in range(4):
        q = _rms_rows(qb[hh * 96:(hh + 1) * 96], mq_ref[...], 96) * sb
        r1, r2 = _rope_rows(q[64:96], cseq, sseq)
        qb_o[0, hh, 0:64, :] = q[0:64].astype(BF16)
        qb_o[0, hh, 64:80, :] = r1.astype(BF16)
        qb_o[0, hh, 80:96, :] = r2.astype(BF16)
        k = jnp.concatenate([kv[hh * 128:hh * 128 + 64], kr], axis=0)
        k = _rms_rows(k, mk_ref[...], 96)
        r1, r2 = _rope_rows(k[64:96], cseq, sseq)
        kb_o[0, hh, 0:64, :] = k[0:64].astype(BF16)
        kb_o[0, hh, 64:80, :] = r1.astype(BF16)
        kb_o[0, hh, 80:96, :] = r2.astype(BF16)
        _store_v(vb_o, hh, kv[hh * 128 + 64:hh * 128 + 128])

    pc = proj(1184, 1952)
    sc = LOG2E * 32 ** -0.5
    for j in range(8):
        q = _rms_rows(pc[j * 32:(j + 1) * 32], dq_ref[...], 32) * sc
        k = _rms_rows(pc[256 + j * 32:256 + (j + 1) * 32], dk_ref[...], 32)
        qc_o[0, j] = q.astype(BF16)
        kc_o[0, j] = k.astype(BF16)
    for hh in range(4):
        _store_v(vc_o, hh, pc[512 + hh * 64:512 + (hh + 1) * 64])

    pd = proj(1952, 2464)
    crow, srow, ccol, scol = crow_ref[...], srow_ref[...], ccol_ref[...], scol_ref[...]
    sd = LOG2E * 64 ** -0.5

    def axial(x, o_ref, hh):
        a1, a2 = _rope_rows(x[0:32], crow, srow)
        b1, b2 = _rope_rows(x[32:64], ccol, scol)
        o_ref[0, hh, 0:16, :] = a1.astype(BF16)
        o_ref[0, hh, 16:32, :] = a2.astype(BF16)
        o_ref[0, hh, 32:48, :] = b1.astype(BF16)
        o_ref[0, hh, 48:64, :] = b2.astype(BF16)

    for hh in range(4):
        axial(_rms_rows(pd[hh * 64:(hh + 1) * 64], gq_ref[...], 64) * sd, qd_o, hh)
    for hh in range(2):
        axial(_rms_rows(pd[256 + hh * 64:256 + (hh + 1) * 64], gk_ref[...], 64), kd_o, hh)
        _store_v(vd_o, hh, pd[384 + hh * 64:384 + (hh + 1) * 64])


def _proj(x, batch, seq, gmix, wint, cols, mats, tabs):
    n, d = x.shape
    tm = PROJ_TM
    nsb = seq // tm
    naq, nak, qlat, kvlat, mq, mk, dq, dk, gq, gk = cols
    wuqt, wukvt = mats

    def full(a):
        return pl.BlockSpec(a.shape, lambda i: (0,) * a.ndim)

    tab_spec = pl.BlockSpec((16, tm), lambda i: (0, i % nsb))
    args = [x, gmix.reshape(1, d), wint, naq, nak, qlat, wuqt, kvlat, wukvt, mq, mk, dq, dk, gq, gk, *tabs]
    in_specs = [pl.BlockSpec((tm, d), lambda i: (i, 0))] + [full(a) for a in args[1:15]] + [tab_spec] * 6

    def head_out(nh, rows):
        return (jax.ShapeDtypeStruct((batch, nh, rows, seq), BF16),
                pl.BlockSpec((1, nh, rows, tm), lambda i: (i // nsb, 0, 0, i % nsb)))

    def v_out(nh):
        return (jax.ShapeDtypeStruct((batch, nh, nsb, V_ROWS, tm), BF16),
                pl.BlockSpec((1, nh, 1, V_ROWS, tm), lambda i: (i // nsb, 0, i % nsb, 0, 0)))

    outs = [head_out(4, 64), head_out(4, 64), v_out(4),
            head_out(4, 96), head_out(4, 96), v_out(4),
            head_out(8, 32), head_out(8, 32), v_out(4),
            head_out(4, 64), head_out(2, 64), v_out(2)]
    return pl.pallas_call(
        _proj_kernel,
        grid=(n // tm,),
        in_specs=in_specs,
        out_specs=[o[1] for o in outs],
        out_shape=[o[0] for o in outs],
        compiler_params=pltpu.CompilerParams(
            dimension_semantics=("parallel",), vmem_limit_bytes=VMEM_LIMIT),
        name="mix_proj",
    )(*args)


def _col_max(s):
    parts = [s]
    while parts[0].shape[0] > 64:
        half = parts[0].shape[0] // 2
        parts = [jnp.maximum(p[:half], p[half:]) for p in parts]
    return jnp.max(parts[0], axis=0, keepdims=True)


def _produce(k_blk, q_blk, slot, s_scr, mc_scr):
    s = _dot(k_blk, q_blk)
    s_scr[slot] = s
    mc_scr[slot] = _col_max(s)


def _consume(s, m_cur, c, idx, v_blk, m_scr, acc_scr):
    m_prev = m_scr[idx]
    m_new = jnp.maximum(m_prev, m_cur + c)
    alpha = jnp.exp2(m_prev - m_new)
    p = jnp.exp2(s - (m_new - c)).astype(BF16)
    acc_scr[idx] = alpha * acc_scr[idx] + _dot(v_blk, p)
    m_scr[idx] = m_new


def _finish_head(acc):
    return acc[0:HEAD_V] / acc[HEAD_V:HEAD_V + 1]


def _flash_plain_kernel(q_ref, k_ref, v_ref, beta_ref, o_ref, s_scr, mc_scr, m_scr, acc_scr, *, kv_map, n_k):
    n_h = len(kv_map)
    m_scr[...] = jnp.full_like(m_scr, NEG_BIG)
    acc_scr[...] = jnp.zeros_like(acc_scr)

    def produce(i, hh):
        _produce(k_ref[0, kv_map[hh], i], q_ref[0, hh], hh % FLASH_SLOTS, s_scr, mc_scr)

    for hh in range(FLASH_AHEAD):
        produce(0, hh)

    def body(i, carry):
        nxt = jnp.minimum(i + 1, n_k - 1)
        for hh in range(n_h):
            ahead = hh + FLASH_AHEAD
            produce(i if ahead < n_h else nxt, ahead % n_h)
            slot = hh % FLASH_SLOTS
            _consume(s_scr[slot], mc_scr[slot], 0.0, hh, v_ref[0, kv_map[hh], i], m_scr, acc_scr)
        return carry

    lax.fori_loop(0, n_k, body, 0, unroll=FLASH_UNROLL)
    y = jnp.concatenate([_finish_head(acc_scr[hh]) for hh in range(n_h)], axis=0)
    o_ref[0] = _rms_rows(y, beta_ref[...], GROUP_W).astype(BF16)


def _flash_plain(qt, k, vt, beta_col, kv_map):
    b, hq, dq, s = qt.shape
    hk, n_k, t = k.shape[1], k.shape[2], k.shape[3]
    assert len(kv_map) % FLASH_SLOTS == 0
    return pl.pallas_call(
        functools.partial(_flash_plain_kernel, kv_map=kv_map, n_k=n_k),
        grid=(b, s // t),
        in_specs=[
            pl.BlockSpec((1, hq, dq, t), lambda bi, qi: (bi, 0, 0, qi)),
            pl.BlockSpec((1, hk, n_k, t, dq), lambda bi, qi: (bi, 0, 0, 0, 0)),
            pl.BlockSpec((1, hk, n_k, V_ROWS, t), lambda bi, qi: (bi, 0, 0, 0, 0)),
            pl.BlockSpec((GROUP_W, 1), lambda bi, qi: (0, 0)),
        ],
        out_specs=pl.BlockSpec((1, GROUP_W, t), lambda bi, qi: (bi, 0, qi)),
        out_shape=jax.ShapeDtypeStruct((b, GROUP_W, s), BF16),
        scratch_shapes=[pltpu.VMEM((FLASH_SLOTS, t, t), F32), pltpu.VMEM((FLASH_SLOTS, 1, t), F32),
                        pltpu.VMEM((hq, 1, t), F32), pltpu.VMEM((hq, V_ROWS, t), F32)],
        compiler_params=pltpu.CompilerParams(
            dimension_semantics=("parallel", "parallel"), vmem_limit_bytes=VMEM_LIMIT),
        name="flash_plain",
    )(qt, k, vt, beta_col)


def _flash_diff_kernel(t5_ref, q_ref, k_ref, v_ref, bias_ref, lam_ref, subln_ref, o_ref,
                       qpad_scr, s_scr, mc_scr, m_scr, acc_scr, *, lambda_init, n_k):
    qi = pl.program_id(1)
    n_m = 8
    dq = q_ref.shape[2]
    m_scr[...] = jnp.full_like(m_scr, NEG_BIG)
    acc_scr[...] = jnp.zeros_like(acc_scr)
    qpad_scr[...] = jnp.zeros_like(qpad_scr)
    for j in range(n_m):
        qpad_scr[j, (j % 2) * dq:(j % 2 + 1) * dq, :] = q_ref[0, j]

    def produce(i, j):
        _produce(k_ref[0, j // 2, i], qpad_scr[j], j % FLASH_SLOTS, s_scr, mc_scr)

    near_lo = jnp.maximum(qi - 1, 0)
    n_near = jnp.minimum(qi + 2, n_k) - near_lo
    far_left, far_right = T5_BUCKETS // 2 - 1, T5_BUCKETS - 1

    def tile_at(t):
        far = t - n_near
        return jnp.where(t < n_near, near_lo + t, jnp.where(far < near_lo, far, far + n_near))

    for j in range(FLASH_AHEAD):
        produce(tile_at(0), j)

    def visit(t, near):
        i = tile_at(t)
        nxt = tile_at(jnp.minimum(t + 1, n_k - 1))
        for j in range(n_m):
            hh = j // 2
            ahead = j + FLASH_AHEAD
            produce(i if ahead < n_m else nxt, ahead % n_m)
            slot = j % FLASH_SLOTS
            if near:
                s = s_scr[slot] + bias_ref[i - qi + 1, hh]
                _consume(s, _col_max(s), 0.0, j, v_ref[0, hh, i], m_scr, acc_scr)
            else:
                c = jnp.where(i < qi, t5_ref[far_left * 4 + hh], t5_ref[far_right * 4 + hh]) * LOG2E
                _consume(s_scr[slot], mc_scr[slot], c, j, v_ref[0, hh, i], m_scr, acc_scr)

    def near_body(t, carry):
        visit(t, True)
        return carry

    def far_group(g, carry):
        for u in range(FLASH_DIFF_UNROLL):
            visit(n_near + g * FLASH_DIFF_UNROLL + u, False)
        return carry

    def far_body(t, carry):
        visit(t, False)
        return carry

    n_groups = (n_k - n_near) // FLASH_DIFF_UNROLL
    lax.fori_loop(0, n_near, near_body, 0)
    lax.fori_loop(0, n_groups, far_group, 0)
    lax.fori_loop(n_near + n_groups * FLASH_DIFF_UNROLL, n_k, far_body, 0)

    lp = lam_ref[...]
    lam = (jnp.exp(jnp.sum(lp[0:1] * lp[1:2], axis=1, keepdims=True))
           - jnp.exp(jnp.sum(lp[2:3] * lp[3:4], axis=1, keepdims=True)) + lambda_init)
    for hh in range(4):
        y = _finish_head(acc_scr[2 * hh]) - lam * _finish_head(acc_scr[2 * hh + 1])
        y = _rms_rows(y, subln_ref[...], HEAD_V) * (1.0 - lambda_init)
        o_ref[0, hh * HEAD_V:(hh + 1) * HEAD_V, :] = y.astype(BF16)


def _flash_diff(t5_flat, qt, k, vt, bias, lam_params, subln_col, lambda_init):
    b, hq, dq, s = qt.shape
    n_k, t = k.shape[2], k.shape[3]
    return pl.pallas_call(
        functools.partial(_flash_diff_kernel, lambda_init=lambda_init, n_k=n_k),
        grid=(b, s // t),
        in_specs=[
            pl.BlockSpec(memory_space=pltpu.SMEM),
            pl.BlockSpec((1, hq, dq, t), lambda bi, qi: (bi, 0, 0, qi)),
            pl.BlockSpec((1, 4, n_k, t, 2 * dq), lambda bi, qi: (bi, 0, 0, 0, 0)),
            pl.BlockSpec((1, 4, n_k, V_ROWS, t), lambda bi, qi: (bi, 0, 0, 0, 0)),
            pl.BlockSpec(bias.shape, lambda bi, qi: (0, 0, 0, 0), pipeline_mode=pl.Buffered(1)),
            pl.BlockSpec(lam_params.shape, lambda bi, qi: (0, 0)),
            pl.BlockSpec((HEAD_V, 1), lambda bi, qi: (0, 0)),
        ],
        out_specs=pl.BlockSpec((1, GROUP_W, t), lambda bi, qi: (bi, 0, qi)),
        out_shape=jax.ShapeDtypeStruct((b, GROUP_W, s), BF16),
        scratch_shapes=[pltpu.VMEM((hq, 2 * dq, t), BF16),
                        pltpu.VMEM((FLASH_SLOTS, t, t), F32), pltpu.VMEM((FLASH_SLOTS, 1, t), F32),
                        pltpu.VMEM((hq, 1, t), F32), pltpu.VMEM((hq, V_ROWS, t), F32)],
        compiler_params=pltpu.CompilerParams(
            dimension_semantics=("parallel", "parallel"), vmem_limit_bytes=VMEM_LIMIT),
        name="flash_diff",
    )(t5_flat, qt, k, vt, bias, lam_params, subln_col)


def _t5_bias_kernel(tab_ref, o_ref, *, t):
    d = pl.program_id(0) - 1
    rows = 16
    half = T5_BUCKETS // 2
    max_exact = half // 2

    def body(r, carry):
        kk = lax.broadcasted_iota(jnp.int32, (rows, t), 0) + r * rows
        qq = lax.broadcasted_iota(jnp.int32, (rows, t), 1)
        rel = d * t + kk - qq
        n = jnp.abs(rel)
        large = max_exact + (jnp.log(jnp.maximum(n, 1).astype(F32) / max_exact)
                             / math.log(T5_MAX_DIST / max_exact) * (half - max_exact)).astype(jnp.int32)
        large = jnp.minimum(large, half - 1)
        bucket = jnp.where(rel > 0, half, 0) + jnp.where(n < max_exact, n, large)
        accs = [jnp.zeros((rows, t), F32) for _ in range(4)]
        for bkt in range(T5_BUCKETS):
            hit = bucket == bkt
            accs = [jnp.where(hit, tab_ref[bkt * 4 + hh], accs[hh]) for hh in range(4)]
        for hh in range(4):
            o_ref[0, hh, pl.ds(pl.multiple_of(r * rows, rows), rows), :] = accs[hh] * LOG2E
        return carry

    lax.fori_loop(0, t // rows, body, 0)


def _t5_bias(t5_flat):
    t = FLASH_T
    return pl.pallas_call(
        functools.partial(_t5_bias_kernel, t=t),
        grid=(3,),
        in_specs=[pl.BlockSpec(memory_space=pltpu.SMEM)],
        out_specs=pl.BlockSpec((1, 4, t, t), lambda d: (d, 0, 0, 0)),
        out_shape=jax.ShapeDtypeStruct((3, 4, t, t), F32),
        name="t5_bias",
    )(t5_flat)


def _na_window(kind, j, i, grid_rows):
    r0 = (0, NA_Q_ROWS, grid_rows - NA_Q_ROWS)[kind]
    start = (0, 0, grid_rows - NA_K_ROWS)[kind]
    krow, qrow = start + j, r0 + i
    lo = min(max(qrow - NA_KH // 2, 0), grid_rows - NA_KH)
    return lo <= krow < lo + NA_KH, krow - qrow + NA_KH - 1


def _na_bias_kernel(rpb_ref, o_ref, t_scr, *, grid_rows):
    hh = pl.program_id(0)
    n_dr, n_dc = 2 * NA_KH - 1, 2 * NA_KW - 1
    shape = (GRID_W, NA_Q_ROWS * GRID_W)
    lane = lax.broadcasted_iota(jnp.int32, shape, 1)
    kc = lax.broadcasted_iota(jnp.int32, shape, 0)
    qc = lane % GRID_W
    grp = lane // GRID_W
    dcm = jnp.clip(kc - qc + NA_KW - 1, 0, n_dc - 1)
    qs = jnp.clip(qc - NA_KW // 2, 0, GRID_W - NA_KW)
    col_ok = (kc >= qs) & (kc < qs + NA_KW)

    for dr in range(n_dr):
        base = (hh * n_dr + dr) * n_dc
        t_scr[dr] = lax.fori_loop(
            0, n_dc, lambda dc, acc: jnp.where(dcm == dc, rpb_ref[base + dc], acc), jnp.zeros(shape, F32)) * LOG2E

    for kind in range(3):
        for j in range(NA_K_ROWS):
            wins = [_na_window(kind, j, i, grid_rows) for i in range(NA_Q_ROWS)]
            blk = jnp.full(shape, NEG_BIG, F32)
            for i, (inside, dr) in enumerate(wins):
                if inside:
                    blk = jnp.where((grp == i) & col_ok, t_scr[dr], blk)
            o_ref[kind, 0, j * GRID_W:(j + 1) * GRID_W, :] = blk


def _na_bias(rpb_flat, grid_rows):
    kt, qt = NA_K_ROWS * GRID_W, NA_Q_ROWS * GRID_W
    return pl.pallas_call(
        functools.partial(_na_bias_kernel, grid_rows=grid_rows),
        grid=(4,),
        in_specs=[pl.BlockSpec(memory_space=pltpu.SMEM)],
        out_specs=pl.BlockSpec((3, 1, kt, qt), lambda h: (0, h, 0, 0)),
        out_shape=jax.ShapeDtypeStruct((3, 4, kt, qt), F32),
        scratch_shapes=[pltpu.VMEM((2 * NA_KH - 1, GRID_W, qt), F32)],
        name="na_bias",
    )(rpb_flat)


def _na_kernel(q_ref, k0_ref, k1_ref, k2_ref, v0_ref, v1_ref, v2_ref, bias_ref, beta_ref, o_ref):
    qt = NA_Q_ROWS * GRID_W
    k_refs = (k0_ref, k1_ref, k2_ref)
    v_refs = (v0_ref, v1_ref, v2_ref)

    def logits(hh):
        q = q_ref[0, hh]
        return [_dot(k_refs[j][0, hh], q) + bias_ref[0, hh, j * qt:(j + 1) * qt, :] for j in range(3)]

    outs = []
    ss_next = logits(0)
    for hh in range(4):
        ss = ss_next
        if hh + 1 < 4:
            ss_next = logits(hh + 1)
        m = functools.reduce(jnp.maximum, [_col_max(s) for s in ss])
        acc = None
        for j in range(3):
            p = jnp.exp2(ss[j] - m).astype(BF16)
            pv = _dot(v_refs[j][0, hh, 0], p)
            acc = pv if acc is None else acc + pv
        outs.append(_finish_head(acc))
    y = jnp.concatenate(outs, axis=0)
    o_ref[0] = _rms_rows(y, beta_ref[...], GROUP_W).astype(BF16)


def _na(qt_arr, k, vt, bias, beta_col):
    b, nh, d, s = qt_arr.shape
    qt = NA_Q_ROWS * GRID_W
    kt = NA_K_ROWS * GRID_W
    n_t = s // qt
    n_win = kt // qt

    def win(j):
        return lambda bi, ti: jnp.clip(ti - 1, 0, n_t - n_win) + j

    def kind(bi, ti):
        return (jnp.where(ti == 0, 0, jnp.where(ti == n_t - 1, 2, 1)), 0, 0, 0)

    k_specs = [pl.BlockSpec((1, nh, qt, d), (lambda j: lambda bi, ti: (bi, 0, win(j)(bi, ti), 0))(j))
               for j in range(n_win)]
    per = vt.shape[4] // qt
    v_specs = [pl.BlockSpec((1, nh, 1, V_ROWS, qt),
                            (lambda j: lambda bi, ti: (bi, 0, win(j)(bi, ti) // per, 0, win(j)(bi, ti) % per))(j))
               for j in range(n_win)]
    return pl.pallas_call(
        _na_kernel,
        grid=(b, n_t),
        in_specs=[pl.BlockSpec((1, nh, d, qt), lambda bi, ti: (bi, 0, 0, ti))] + k_specs + v_specs + [
            pl.BlockSpec((1, nh, kt, qt), kind),
            pl.BlockSpec((GROUP_W, 1), lambda bi, ti: (0, 0)),
        ],
        out_specs=pl.BlockSpec((1, GROUP_W, qt), lambda bi, ti: (bi, 0, ti)),
        out_shape=jax.ShapeDtypeStruct((b, GROUP_W, s), BF16),
        compiler_params=pltpu.CompilerParams(
            dimension_semantics=("parallel", "parallel"), vmem_limit_bytes=VMEM_LIMIT),
        name="na_attn",
    )(qt_arr, k, k, k, vt, vt, vt, bias, beta_col)


def _rope_tables(pos, dim):
    inv = jnp.exp(-math.log(ROPE_THETA) * jnp.arange(0, dim, 2, dtype=F32) / dim)
    ang = pos.astype(F32)[:, None] * inv[None, :]
    return jnp.cos(ang).T, jnp.sin(ang).T


def _col(v):
    return v.reshape(-1, 1).astype(F32)


def _lambda_init(layer):
    return 0.8 - 0.6 * math.exp(-0.3 * layer)


def kernel(x, ffn1_norm, ffn1_w_gate, ffn1_w_up, ffn1_w_down, mix_norm, w_in, na_q_norm, na_k_norm, na_rpb, na_beta, mla_q_lat_norm, mla_w_uq, mla_kv_lat_norm, mla_w_ukv, mla_q_norm, mla_k_norm, mla_beta, diff_q_norm, diff_k_norm, diff_lambda, diff_subln, gqa_q_norm, gqa_k_norm, gqa_beta, w_out, ffn2_norm, ffn2_w_gate, ffn2_w_up, ffn2_w_down, final_norm, t5_bias):
    batch, seq, d = x.shape
    grid_rows = seq // GRID_W
    pos = jnp.arange(seq, dtype=jnp.int32)
    tabs = (*_rope_tables(pos, 32), *_rope_tables(pos // GRID_W, 32), *_rope_tables(pos % GRID_W, 32))
    t5_flat = t5_bias.reshape(-1).astype(F32)
    t5_tiles = _t5_bias(t5_flat)

    xf = x.reshape(batch * seq, d)
    for l in range(N_LAYERS):
        xf = _ffn(xf, ffn1_norm[l], ffn1_w_gate[l].astype(BF16), ffn1_w_up[l].astype(BF16),
                  ffn1_w_down[l].astype(BF16))
        cols = tuple(_col(v[l]) for v in (na_q_norm, na_k_norm, mla_q_lat_norm, mla_kv_lat_norm,
                                          mla_q_norm, mla_k_norm, diff_q_norm, diff_k_norm,
                                          gqa_q_norm, gqa_k_norm))
        mats = (mla_w_uq[l].T.astype(BF16), mla_w_ukv[l].T.astype(BF16))
        (qa, ka, va, qb, kb, vb, qc, kc, vc, qd, kd, vd) = _proj(
            xf, batch, seq, mix_norm[l], w_in[l].T.astype(BF16), cols, mats, tabs)
        n_k = seq // FLASH_T
        ka = jnp.swapaxes(ka, 2, 3)
        kb = jnp.swapaxes(kb, 2, 3).reshape(batch, 4, n_k, FLASH_T, 96)
        kc = jnp.swapaxes(kc.reshape(batch, 4, 64, seq), 2, 3).reshape(batch, 4, n_k, FLASH_T, 64)
        kd = jnp.swapaxes(kd, 2, 3).reshape(batch, 2, n_k, FLASH_T, 64)

        ya = _na(qa, ka, va, _na_bias(na_rpb[l].reshape(-1).astype(F32), grid_rows), _col(na_beta[l]))
        yb = _flash_plain(qb, kb, vb, _col(mla_beta[l]), (0, 1, 2, 3))
        yc = _flash_diff(t5_flat, qc, kc, vc, t5_tiles, diff_lambda[l].astype(F32), _col(diff_subln[l]),
                         _lambda_init(l))
        yd = _flash_plain(qd, kd, vd, _col(gqa_beta[l]), (0, 0, 1, 1))

        xf = _ffn(xf, ffn2_norm[l], ffn2_w_gate[l].astype(BF16), ffn2_w_up[l].astype(BF16),
                  ffn2_w_down[l].astype(BF16), mix=((ya, yb, yc, yd), w_out[l].astype(BF16)),
                  final_g=final_norm[l], seq=seq)
    return xf.reshape(batch, seq, d)
```

```python
import functools
import math

import jax
import jax.numpy as jnp
from jax import lax
from jax.experimental import pallas as pl
from jax.experimental.pallas import tpu as pltpu

F32 = jnp.float32
BF16 = jnp.bfloat16

EPS = 1e-6
NEG_BIG = -1e30
LOG2E = 1.4426950408889634
ROPE_THETA = 10000.0
GRID_W = 64
N_LAYERS = 2

HEAD_V = 64
V_ROWS = 80
GROUP_W = 256

NA_KH, NA_KW = 8, 16
NA_Q_ROWS = 4
NA_K_ROWS = 12
T5_BUCKETS = 32
T5_MAX_DIST = 128

FFN_TM = 1024
FFN_CHUNK = 1024
FLASH_T = 512
FLASH_AHEAD = 2
FLASH_SLOTS = 4
FLASH_SKEW = 128
FLASH_UNROLL = 8
FLASH_DIFF_UNROLL = 4
PROJ_TM = FLASH_T

VMEM_LIMIT = 56 * 1024 * 1024


def _rms_rows(x, g_col, n):
    r = lax.rsqrt(jnp.sum(x * x, axis=0, keepdims=True) * (1.0 / n) + EPS)
    return (x * r) * g_col


def _rms_lanes(x, g_row):
    r = lax.rsqrt(jnp.mean(x * x, axis=-1, keepdims=True) + EPS)
    return (x * r) * g_row


def _dot(a, b):
    return jnp.dot(a, b, preferred_element_type=F32)


def _ffn_kernel(*refs, has_mix, has_final, chunks):
    x_ref, g_ref, wg_ref, wu_ref, wd_ref = refs[:5]
    rest = refs[5:]
    if has_mix:
        y_refs, wo_ref, rest = rest[:4], rest[4], rest[5:]
    if has_final:
        gf_ref, rest = rest[0], rest[1:]
    (o_ref,) = rest

    x = x_ref[...]
    if has_mix:
        for g in range(4):
            x = x + lax.dot_general(
                y_refs[g][0], wo_ref[g * GROUP_W:(g + 1) * GROUP_W, :],
                (((0,), (0,)), ((), ())), preferred_element_type=F32)
    o_ref[...] = x
    h = _rms_lanes(x, g_ref[...]).astype(BF16)
    acc = None
    for lo, hi in chunks:
        gate = _dot(h, wg_ref[:, lo:hi])
        up = _dot(h, wu_ref[:, lo:hi])
        a = (gate / (1.0 + jnp.exp(-gate))) * up
        part = _dot(a.astype(BF16), wd_ref[lo:hi, :])
        acc = part if acc is None else acc + part
    out = o_ref[...] + 0.5 * acc
    if has_final:
        out = _rms_lanes(out, gf_ref[...])
    o_ref[...] = out


def _ffn(x, g, wg, wu, wd, mix=None, final_g=None, seq=None):
    n, d = x.shape
    d_ff = wg.shape[1]
    tm = FFN_TM
    chunks = tuple((lo, min(lo + FFN_CHUNK, d_ff)) for lo in range(0, d_ff, FFN_CHUNK))
    once = pl.Buffered(1)
    in_specs = [
        pl.BlockSpec((tm, d), lambda i: (i, 0)),
        pl.BlockSpec((1, d), lambda i: (0, 0)),
        pl.BlockSpec((d, d_ff), lambda i: (0, 0), pipeline_mode=once),
        pl.BlockSpec((d, d_ff), lambda i: (0, 0), pipeline_mode=once),
        pl.BlockSpec((d_ff, d), lambda i: (0, 0), pipeline_mode=once),
    ]
    args = [x, g.reshape(1, d), wg, wu, wd]
    if mix is not None:
        ys, wo = mix
        nsb = seq // tm
        for y in ys:
            in_specs.append(pl.BlockSpec((1, GROUP_W, tm), lambda i: (i // nsb, 0, i % nsb)))
            args.append(y)
        in_specs.append(pl.BlockSpec(wo.shape, lambda i: (0, 0), pipeline_mode=once))
        args.append(wo)
    if final_g is not None:
        in_specs.append(pl.BlockSpec((1, d), lambda i: (0, 0)))
        args.append(final_g.reshape(1, d))
    return pl.pallas_call(
        functools.partial(_ffn_kernel, has_mix=mix is not None, has_final=final_g is not None, chunks=chunks),
        grid=(n // tm,),
        in_specs=in_specs,
        out_specs=pl.BlockSpec((tm, d), lambda i: (i, 0)),
        out_shape=jax.ShapeDtypeStruct((n, d), F32),
        compiler_params=pltpu.CompilerParams(
            dimension_semantics=("parallel",), vmem_limit_bytes=VMEM_LIMIT),
        name="ffn_mix" if mix is not None else "ffn",
    )(*args)


def _rope_rows(x, cos, sin):
    x1, x2 = x[:16], x[16:]
    return x1 * cos - x2 * sin, x1 * sin + x2 * cos


def _store_v(v_ref, hh, v):
    v_ref[0, hh, 0, 0:HEAD_V, :] = v.astype(BF16)
    v_ref[0, hh, 0, HEAD_V:V_ROWS, :] = jnp.ones((V_ROWS - HEAD_V, v.shape[1]), BF16)


def _proj_kernel(x_ref, gmix_ref, wint_ref,
                 naq_ref, nak_ref,
                 qlat_ref, wuqt_ref, kvlat_ref, wukvt_ref, mq_ref, mk_ref,
                 dq_ref, dk_ref, gq_ref, gk_ref,
                 cseq_ref, sseq_ref, crow_ref, srow_ref, ccol_ref, scol_ref,
                 qa_o, ka_o, va_o, qb_o, kb_o, vb_o, qc_o, kc_o, vc_o, qd_o, kd_o, vd_o):
    h = _rms_lanes(x_ref[...], gmix_ref[...]).astype(BF16)

    def proj(lo, hi):
        return lax.dot_general(wint_ref[lo:hi, :], h, (((1,), (1,)), ((), ())),
                               preferred_element_type=F32)

    pa = proj(0, 768)
    sa = LOG2E * 64 ** -0.5
    for hh in range(4):
        q = _rms_rows(pa[hh * 64:(hh + 1) * 64], naq_ref[...], 64) * sa
        k = _rms_rows(pa[256 + hh * 64:256 + (hh + 1) * 64], nak_ref[...], 64)
        qa_o[0, hh] = q.astype(BF16)
        ka_o[0, hh] = k.astype(BF16)
        _store_v(va_o, hh, pa[512 + hh * 64:512 + (hh + 1) * 64])

    pb = proj(768, 1184)
    cq = _rms_rows(pb[0:256], qlat_ref[...], 256).astype(BF16)
    qb = _dot(wuqt_ref[...], cq)
    ckv = _rms_rows(pb[256:384], kvlat_ref[...], 128).astype(BF16)
    kv = _dot(wukvt_ref[...], ckv)
    kr = pb[384:416]
    cseq, sseq = cseq_ref[...], sseq_ref[...]
    sb = LOG2E * 96 ** -0.5
    for hh in range(4):
        q = _rms_rows(qb[hh * 96:(hh + 1) * 96], mq_ref[...], 96) * sb
        r1, r2 = _rope_rows(q[64:96], cseq, sseq)
        qb_o[0, hh, 0:64, :] = q[0:64].astype(BF16)
        qb_o[0, hh, 64:80, :] = r1.astype(BF16)
        qb_o[0, hh, 80:96, :] = r2.astype(BF16)
        k = jnp.concatenate([kv[hh * 128:hh * 128 + 64], kr], axis=0)
        k = _rms_rows(k, mk_ref[...], 96)
        r1, r2 = _rope_rows(k[64:96], cseq, sseq)
        kb_o[0, hh, 0:64, :] = k[0:64].astype(BF16)
        kb_o[0, hh, 64:80, :] = r1.astype(BF16)
        kb_o[0, hh, 80:96, :] = r2.astype(BF16)
        _store_v(vb_o, hh, kv[hh * 128 + 64:hh * 128 + 128])

    pc = proj(1184, 1952)
    sc = LOG2E * 32 ** -0.5
    for j in range(8):
        q = _rms_rows(pc[j * 32:(j + 1) * 32], dq_ref[...], 32) * sc
        k = _rms_rows(pc[256 + j * 32:256 + (j + 1) * 32], dk_ref[...], 32)
        qc_o[0, j] = q.astype(BF16)
        kc_o[0, j] = k.astype(BF16)
    for hh in range(4):
        _store_v(vc_o, hh, pc[512 + hh * 64:512 + (hh + 1) * 64])

    pd = proj(1952, 2464)
    crow, srow, ccol, scol = crow_ref[...], srow_ref[...], ccol_ref[...], scol_ref[...]
    sd = LOG2E * 64 ** -0.5

    def axial(x, o_ref, hh):
        a1, a2 = _rope_rows(x[0:32], crow, srow)
        b1, b2 = _rope_rows(x[32:64], ccol, scol)
        o_ref[0, hh, 0:16, :] = a1.astype(BF16)
        o_ref[0, hh, 16:32, :] = a2.astype(BF16)
        o_ref[0, hh, 32:48, :] = b1.astype(BF16)
        o_ref[0, hh, 48:64, :] = b2.astype(BF16)

    for hh in range(4):
        axial(_rms_rows(pd[hh * 64:(hh + 1) * 64], gq_ref[...], 64) * sd, qd_o, hh)
    for hh in range(2):
        axial(_rms_rows(pd[256 + hh * 64:256 + (hh + 1) * 64], gk_ref[...], 64), kd_o, hh)
        _store_v(vd_o, hh, pd[384 + hh * 64:384 + (hh + 1) * 64])


def _proj(x, batch, seq, gmix, wint, cols, mats, tabs):
    n, d = x.shape
    tm = PROJ_TM
    nsb = seq // tm
    naq, nak, qlat, kvlat, mq, mk, dq, dk, gq, gk = cols
    wuqt, wukvt = mats

    def full(a):
        return pl.BlockSpec(a.shape, lambda i: (0,) * a.ndim)

    tab_spec = pl.BlockSpec((16, tm), lambda i: (0, i % nsb))
    args = [x, gmix.reshape(1, d), wint, naq, nak, qlat, wuqt, kvlat, wukvt, mq, mk, dq, dk, gq, gk, *tabs]
    in_specs = [pl.BlockSpec((tm, d), lambda i: (i, 0))] + [full(a) for a in args[1:15]] + [tab_spec] * 6

    def head_out(nh, rows):
        return (jax.ShapeDtypeStruct((batch, nh, rows, seq), BF16),
                pl.BlockSpec((1, nh, rows, tm), lambda i: (i // nsb, 0, 0, i % nsb)))

    def v_out(nh):
        return (jax.ShapeDtypeStruct((batch, nh, nsb, V_ROWS, tm), BF16),
                pl.BlockSpec((1, nh, 1, V_ROWS, tm), lambda i: (i // nsb, 0, i % nsb, 0, 0)))

    outs = [head_out(4, 64), head_out(4, 64), v_out(4),
            head_out(4, 96), head_out(4, 96), v_out(4),
            head_out(8, 32), head_out(8, 32), v_out(4),
            head_out(4, 64), head_out(2, 64), v_out(2)]
    return pl.pallas_call(
        _proj_kernel,
        grid=(n // tm,),
        in_specs=in_specs,
        out_specs=[o[1] for o in outs],
        out_shape=[o[0] for o in outs],
        compiler_params=pltpu.CompilerParams(
            dimension_semantics=("parallel",), vmem_limit_bytes=VMEM_LIMIT),
        name="mix_proj",
    )(*args)


def _col_max(s):
    parts = [s]
    while parts[0].shape[0] > 64:
        half = parts[0].shape[0] // 2
        parts = [jnp.maximum(p[:half], p[half:]) for p in parts]
    return jnp.max(parts[0], axis=0, keepdims=True)


def _produce(k_blk, q_blk, slot, s_scr, mc_scr):
    s = _dot(k_blk, q_blk)
    s_scr[slot, :, 0:s.shape[1]] = s
    mc_scr[slot] = _col_max(s)


def _consume(s, m_cur, c, idx, v_blk, m_scr, acc_scr):
    m_prev = m_scr[idx]
    m_new = jnp.maximum(m_prev, m_cur + c)
    alpha = jnp.exp2(m_prev - m_new)
    p = jnp.exp2(s - (m_new - c)).astype(BF16)
    acc_scr[idx] = alpha * acc_scr[idx] + _dot(v_blk, p)
    m_scr[idx] = m_new


def _finish_head(acc):
    return acc[0:HEAD_V] / acc[HEAD_V:HEAD_V + 1]


def _flash_plain_kernel(q_ref, k_ref, v_ref, beta_ref, o_ref, s_scr, mc_scr, m_scr, acc_scr, *, kv_map, n_k):
    n_h = len(kv_map)
    t = q_ref.shape[3]
    m_scr[...] = jnp.full_like(m_scr, NEG_BIG)
    acc_scr[...] = jnp.zeros_like(acc_scr)

    def produce(i, hh):
        _produce(k_ref[0, kv_map[hh], i], q_ref[0, hh], hh % FLASH_SLOTS, s_scr, mc_scr)

    for hh in range(FLASH_AHEAD):
        produce(0, hh)

    def body(i, carry):
        nxt = jnp.minimum(i + 1, n_k - 1)
        for hh in range(n_h):
            ahead = hh + FLASH_AHEAD
            produce(i if ahead < n_h else nxt, ahead % n_h)
            slot = hh % FLASH_SLOTS
            _consume(s_scr[slot, :, 0:t], mc_scr[slot], 0.0, hh, v_ref[0, kv_map[hh], i], m_scr, acc_scr)
        return carry

    lax.fori_loop(0, n_k, body, 0, unroll=FLASH_UNROLL)
    y = jnp.concatenate([_finish_head(acc_scr[hh]) for hh in range(n_h)], axis=0)
    o_ref[0] = _rms_rows(y, beta_ref[...], GROUP_W).astype(BF16)


def _flash_plain(qt, k, vt, beta_col, kv_map):
    b, hq, dq, s = qt.shape
    hk, n_k, t = k.shape[1], k.shape[2], k.shape[3]
    assert len(kv_map) % FLASH_SLOTS == 0
    return pl.pallas_call(
        functools.partial(_flash_plain_kernel, kv_map=kv_map, n_k=n_k),
        grid=(b, s // t),
        in_specs=[
            pl.BlockSpec((1, hq, dq, t), lambda bi, qi: (bi, 0, 0, qi)),
            pl.BlockSpec((1, hk, n_k, t, dq), lambda bi, qi: (bi, 0, 0, 0, 0)),
            pl.BlockSpec((1, hk, n_k, V_ROWS, t), lambda bi, qi: (bi, 0, 0, 0, 0)),
            pl.BlockSpec((GROUP_W, 1), lambda bi, qi: (0, 0)),
        ],
        out_specs=pl.BlockSpec((1, GROUP_W, t), lambda bi, qi: (bi, 0, qi)),
        out_shape=jax.ShapeDtypeStruct((b, GROUP_W, s), BF16),
        scratch_shapes=[pltpu.VMEM((FLASH_SLOTS, t, t + FLASH_SKEW), F32), pltpu.VMEM((FLASH_SLOTS, 1, t), F32),
                        pltpu.VMEM((hq, 1, t), F32), pltpu.VMEM((hq, V_ROWS, t), F32)],
        compiler_params=pltpu.CompilerParams(
            dimension_semantics=("parallel", "parallel"), vmem_limit_bytes=VMEM_LIMIT),
        name="flash_plain",
    )(qt, k, vt, beta_col)


def _flash_diff_kernel(t5_ref, q_ref, k_ref, v_ref, bias_ref, lam_ref, subln_ref, o_ref,
                       qpad_scr, s_scr, mc_scr, m_scr, acc_scr, *, lambda_init, n_k):
    qi = pl.program_id(1)
    n_m = 8
    dq, tq = q_ref.shape[2], q_ref.shape[3]
    m_scr[...] = jnp.full_like(m_scr, NEG_BIG)
    acc_scr[...] = jnp.zeros_like(acc_scr)
    qpad_scr[...] = jnp.zeros_like(qpad_scr)
    for j in range(n_m):
        qpad_scr[j, (j % 2) * dq:(j % 2 + 1) * dq, :] = q_ref[0, j]

    def produce(i, j):
        _produce(k_ref[0, j // 2, i], qpad_scr[j], j % FLASH_SLOTS, s_scr, mc_scr)

    near_lo = jnp.maximum(qi - 1, 0)
    n_near = jnp.minimum(qi + 2, n_k) - near_lo
    far_left, far_right = T5_BUCKETS // 2 - 1, T5_BUCKETS - 1

    def tile_at(t):
        far = t - n_near
        return jnp.where(t < n_near, near_lo + t, jnp.where(far < near_lo, far, far + n_near))

    for j in range(FLASH_AHEAD):
        produce(tile_at(0), j)

    def visit(t, near):
        i = tile_at(t)
        nxt = tile_at(jnp.minimum(t + 1, n_k - 1))
        for j in range(n_m):
            hh = j // 2
            ahead = j + FLASH_AHEAD
            produce(i if ahead < n_m else nxt, ahead % n_m)
            slot = j % FLASH_SLOTS
            if near:
                s = s_scr[slot, :, 0:tq] + bias_ref[i - qi + 1, hh]
                _consume(s, _col_max(s), 0.0, j, v_ref[0, hh, i], m_scr, acc_scr)
            else:
                c = jnp.where(i < qi, t5_ref[far_left * 4 + hh], t5_ref[far_right * 4 + hh]) * LOG2E
                _consume(s_scr[slot, :, 0:tq], mc_scr[slot], c, j, v_ref[0, hh, i], m_scr, acc_scr)

    def near_body(t, carry):
        visit(t, True)
        return carry

    def far_group(g, carry):
        for u in range(FLASH_DIFF_UNROLL):
            visit(n_near + g * FLASH_DIFF_UNROLL + u, False)
        return carry

    def far_body(t, carry):
        visit(t, False)
        return carry

    n_groups = (n_k - n_near) // FLASH_DIFF_UNROLL
    lax.fori_loop(0, n_near, near_body, 0)
    lax.fori_loop(0, n_groups, far_group, 0)
    lax.fori_loop(n_near + n_groups * FLASH_DIFF_UNROLL, n_k, far_body, 0)

    lp = lam_ref[...]
    lam = (jnp.exp(jnp.sum(lp[0:1] * lp[1:2], axis=1, keepdims=True))
           - jnp.exp(jnp.sum(lp[2:3] * lp[3:4], axis=1, keepdims=True)) + lambda_init)
    for hh in range(4):
        y = _finish_head(acc_scr[2 * hh]) - lam * _finish_head(acc_scr[2 * hh + 1])
        y = _rms_rows(y, subln_ref[...], HEAD_V) * (1.0 - lambda_init)
        o_ref[0, hh * HEAD_V:(hh + 1) * HEAD_V, :] = y.astype(BF16)


def _flash_diff(t5_flat, qt, k, vt, bias, lam_params, subln_col, lambda_init):
    b, hq, dq, s = qt.shape
    n_k, t = k.shape[2], k.shape[3]
    return pl.pallas_call(
        functools.partial(_flash_diff_kernel, lambda_init=lambda_init, n_k=n_k),
        grid=(b, s // t),
        in_specs=[
            pl.BlockSpec(memory_space=pltpu.SMEM),
            pl.BlockSpec((1, hq, dq, t), lambda bi, qi: (bi, 0, 0, qi)),
            pl.BlockSpec((1, 4, n_k, t, 2 * dq), lambda bi, qi: (bi, 0, 0, 0, 0)),
            pl.BlockSpec((1, 4, n_k, V_ROWS, t), lambda bi, qi: (bi, 0, 0, 0, 0)),
            pl.BlockSpec(bias.shape, lambda bi, qi: (0, 0, 0, 0), pipeline_mode=pl.Buffered(1)),
            pl.BlockSpec(lam_params.shape, lambda bi, qi: (0, 0)),
            pl.BlockSpec((HEAD_V, 1), lambda bi, qi: (0, 0)),
        ],
        out_specs=pl.BlockSpec((1, GROUP_W, t), lambda bi, qi: (bi, 0, qi)),
        out_shape=jax.ShapeDtypeStruct((b, GROUP_W, s), BF16),
        scratch_shapes=[pltpu.VMEM((hq, 2 * dq, t), BF16),
                        pltpu.VMEM((FLASH_SLOTS, t, t + FLASH_SKEW), F32), pltpu.VMEM((FLASH_SLOTS, 1, t), F32),
                        pltpu.VMEM((hq, 1, t), F32), pltpu.VMEM((hq, V_ROWS, t), F32)],
        compiler_params=pltpu.CompilerParams(
            dimension_semantics=("parallel", "parallel"), vmem_limit_bytes=VMEM_LIMIT),
        name="flash_diff",
    )(t5_flat, qt, k, vt, bias, lam_params, subln_col)


def _t5_bias_kernel(tab_ref, o_ref, *, t):
    d = pl.program_id(0) - 1
    rows = 16
    half = T5_BUCKETS // 2
    max_exact = half // 2

    def body(r, carry):
        kk = lax.broadcasted_iota(jnp.int32, (rows, t), 0) + r * rows
        qq = lax.broadcasted_iota(jnp.int32, (rows, t), 1)
        rel = d * t + kk - qq
        n = jnp.abs(rel)
        large = max_exact + (jnp.log(jnp.maximum(n, 1).astype(F32) / max_exact)
                             / math.log(T5_MAX_DIST / max_exact) * (half - max_exact)).astype(jnp.int32)
        large = jnp.minimum(large, half - 1)
        bucket = jnp.where(rel > 0, half, 0) + jnp.where(n < max_exact, n, large)
        accs = [jnp.zeros((rows, t), F32) for _ in range(4)]
        for bkt in range(T5_BUCKETS):
            hit = bucket == bkt
            accs = [jnp.where(hit, tab_ref[bkt * 4 + hh], accs[hh]) for hh in range(4)]
        for hh in range(4):
            o_ref[0, hh, pl.ds(pl.multiple_of(r * rows, rows), rows), :] = accs[hh] * LOG2E
        return carry

    lax.fori_loop(0, t // rows, body, 0)


def _t5_bias(t5_flat):
    t = FLASH_T
    return pl.pallas_call(
        functools.partial(_t5_bias_kernel, t=t),
        grid=(3,),
        in_specs=[pl.BlockSpec(memory_space=pltpu.SMEM)],
        out_specs=pl.BlockSpec((1, 4, t, t), lambda d: (d, 0, 0, 0)),
        out_shape=jax.ShapeDtypeStruct((3, 4, t, t), F32),
        name="t5_bias",
    )(t5_flat)


def _na_window(kind, j, i, grid_rows):
    r0 = (0, NA_Q_ROWS, grid_rows - NA_Q_ROWS)[kind]
    start = (0, 0, grid_rows - NA_K_ROWS)[kind]
    krow, qrow = start + j, r0 + i
    lo = min(max(qrow - NA_KH // 2, 0), grid_rows - NA_KH)
    return lo <= krow < lo + NA_KH, krow - qrow + NA_KH - 1


def _na_bias_kernel(rpb_ref, o_ref, t_scr, *, grid_rows):
    hh = pl.program_id(0)
    n_dr, n_dc = 2 * NA_KH - 1, 2 * NA_KW - 1
    shape = (GRID_W, NA_Q_ROWS * GRID_W)
    lane = lax.broadcasted_iota(jnp.int32, shape, 1)
    kc = lax.broadcasted_iota(jnp.int32, shape, 0)
    qc = lane % GRID_W
    grp = lane // GRID_W
    dcm = jnp.clip(kc - qc + NA_KW - 1, 0, n_dc - 1)
    qs = jnp.clip(qc - NA_KW // 2, 0, GRID_W - NA_KW)
    col_ok = (kc >= qs) & (kc < qs + NA_KW)

    for dr in range(n_dr):
        base = (hh * n_dr + dr) * n_dc
        t_scr[dr] = lax.fori_loop(
            0, n_dc, lambda dc, acc: jnp.where(dcm == dc, rpb_ref[base + dc], acc), jnp.zeros(shape, F32)) * LOG2E

    for kind in range(3):
        for j in range(NA_K_ROWS):
            wins = [_na_window(kind, j, i, grid_rows) for i in range(NA_Q_ROWS)]
            blk = jnp.full(shape, NEG_BIG, F32)
            for i, (inside, dr) in enumerate(wins):
                if inside:
                    blk = jnp.where((grp == i) & col_ok, t_scr[dr], blk)
            o_ref[kind, 0, j * GRID_W:(j + 1) * GRID_W, :] = blk


def _na_bias(rpb_flat, grid_rows):
    kt, qt = NA_K_ROWS * GRID_W, NA_Q_ROWS * GRID_W
    return pl.pallas_call(
        functools.partial(_na_bias_kernel, grid_rows=grid_rows),
        grid=(4,),
        in_specs=[pl.BlockSpec(memory_space=pltpu.SMEM)],
        out_specs=pl.BlockSpec((3, 1, kt, qt), lambda h: (0, h, 0, 0)),
        out_shape=jax.ShapeDtypeStruct((3, 4, kt, qt), F32),
        scratch_shapes=[pltpu.VMEM((2 * NA_KH - 1, GRID_W, qt), F32)],
        name="na_bias",
    )(rpb_flat)


def _na_kernel(q_ref, k0_ref, k1_ref, k2_ref, v0_ref, v1_ref, v2_ref, bias_ref, beta_ref, o_ref):
    qt = NA_Q_ROWS * GRID_W
    k_refs = (k0_ref, k1_ref, k2_ref)
    v_refs = (v0_ref, v1_ref, v2_ref)

    def logits(hh):
        q = q_ref[0, hh]
        return [_dot(k_refs[j][0, hh], q) + bias_ref[0, hh, j * qt:(j + 1) * qt, :] for j in range(3)]

    outs = []
    ss_next = logits(0)
    for hh in range(4):
        ss = ss_next
        if hh + 1 < 4:
            ss_next = logits(hh + 1)
        m = functools.reduce(jnp.maximum, [_col_max(s) for s in ss])
        acc = None
        for j in range(3):
            p = jnp.exp2(ss[j] - m).astype(BF16)
            pv = _dot(v_refs[j][0, hh, 0], p)
            acc = pv if acc is None else acc + pv
        outs.append(_finish_head(acc))
    y = jnp.concatenate(outs, axis=0)
    o_ref[0] = _rms_rows(y, beta_ref[...], GROUP_W).astype(BF16)


def _na(qt_arr, k, vt, bias, beta_col):
    b, nh, d, s = qt_arr.shape
    qt = NA_Q_ROWS * GRID_W
    kt = NA_K_ROWS * GRID_W
    n_t = s // qt
    n_win = kt // qt

    def win(j):
        return lambda bi, ti: jnp.clip(ti - 1, 0, n_t - n_win) + j

    def kind(bi, ti):
        return (jnp.where(ti == 0, 0, jnp.where(ti == n_t - 1, 2, 1)), 0, 0, 0)

    k_specs = [pl.BlockSpec((1, nh, qt, d), (lambda j: lambda bi, ti: (bi, 0, win(j)(bi, ti), 0))(j))
               for j in range(n_win)]
    per = vt.shape[4] // qt
    v_specs = [pl.BlockSpec((1, nh, 1, V_ROWS, qt),
                            (lambda j: lambda bi, ti: (bi, 0, win(j)(bi, ti) // per, 0, win(j)(bi, ti) % per))(j))
               for j in range(n_win)]
    return pl.pallas_call(
        _na_kernel,
        grid=(b, n_t),
        in_specs=[pl.BlockSpec((1, nh, d, qt), lambda bi, ti: (bi, 0, 0, ti))] + k_specs + v_specs + [
            pl.BlockSpec((1, nh, kt, qt), kind),
            pl.BlockSpec((GROUP_W, 1), lambda bi, ti: (0, 0)),
        ],
        out_specs=pl.BlockSpec((1, GROUP_W, qt), lambda bi, ti: (bi, 0, ti)),
        out_shape=jax.ShapeDtypeStruct((b, GROUP_W, s), BF16),
        compiler_params=pltpu.CompilerParams(
            dimension_semantics=("parallel", "parallel"), vmem_limit_bytes=VMEM_LIMIT),
        name="na_attn",
    )(qt_arr, k, k, k, vt, vt, vt, bias, beta_col)


def _rope_tables(pos, dim):
    inv = jnp.exp(-math.log(ROPE_THETA) * jnp.arange(0, dim, 2, dtype=F32) / dim)
    ang = pos.astype(F32)[:, None] * inv[None, :]
    return jnp.cos(ang).T, jnp.sin(ang).T


def _col(v):
    return v.reshape(-1, 1).astype(F32)


def _lambda_init(layer):
    return 0.8 - 0.6 * math.exp(-0.3 * layer)


def kernel(x, ffn1_norm, ffn1_w_gate, ffn1_w_up, ffn1_w_down, mix_norm, w_in, na_q_norm, na_k_norm, na_rpb, na_beta, mla_q_lat_norm, mla_w_uq, mla_kv_lat_norm, mla_w_ukv, mla_q_norm, mla_k_norm, mla_beta, diff_q_norm, diff_k_norm, diff_lambda, diff_subln, gqa_q_norm, gqa_k_norm, gqa_beta, w_out, ffn2_norm, ffn2_w_gate, ffn2_w_up, ffn2_w_down, final_norm, t5_bias):
    batch, seq, d = x.shape
    grid_rows = seq // GRID_W
    pos = jnp.arange(seq, dtype=jnp.int32)
    tabs = (*_rope_tables(pos, 32), *_rope_tables(pos // GRID_W, 32), *_rope_tables(pos % GRID_W, 32))
    t5_flat = t5_bias.reshape(-1).astype(F32)
    t5_tiles = _t5_bias(t5_flat)

    xf = x.reshape(batch * seq, d)
    for l in range(N_LAYERS):
        xf = _ffn(xf, ffn1_norm[l], ffn1_w_gate[l].astype(BF16), ffn1_w_up[l].astype(BF16),
                  ffn1_w_down[l].astype(BF16))
        cols = tuple(_col(v[l]) for v in (na_q_norm, na_k_norm, mla_q_lat_norm, mla_kv_lat_norm,
                                          mla_q_norm, mla_k_norm, diff_q_norm, diff_k_norm,
                                          gqa_q_norm, gqa_k_norm))
        mats = (mla_w_uq[l].T.astype(BF16), mla_w_ukv[l].T.astype(BF16))
        (qa, ka, va, qb, kb, vb, qc, kc, vc, qd, kd, vd) = _proj(
            xf, batch, seq, mix_norm[l], w_in[l].T.astype(BF16), cols, mats, tabs)
        n_k = seq // FLASH_T
        ka = jnp.swapaxes(ka, 2, 3)
        kb = jnp.swapaxes(kb, 2, 3).reshape(batch, 4, n_k, FLASH_T, 96)
        kc = jnp.swapaxes(kc.reshape(batch, 4, 64, seq), 2, 3).reshape(batch, 4, n_k, FLASH_T, 64)
        kd = jnp.swapaxes(kd, 2, 3).reshape(batch, 2, n_k, FLASH_T, 64)

        ya = _na(qa, ka, va, _na_bias(na_rpb[l].reshape(-1).astype(F32), grid_rows), _col(na_beta[l]))
        yb = _flash_plain(qb, kb, vb, _col(mla_beta[l]), (0, 1, 2, 3))
        yc = _flash_diff(t5_flat, qc, kc, vc, t5_tiles, diff_lambda[l].astype(F32), _col(diff_subln[l]),
                         _lambda_init(l))
        yd = _flash_plain(qd, kd, vd, _col(gqa_beta[l]), (0, 0, 1, 1))

        xf = _ffn(xf, ffn2_norm[l], ffn2_w_gate[l].astype(BF16), ffn2_w_up[l].astype(BF16),
                  ffn2_w_down[l].astype(BF16), mix=((ya, yb, yc, yd), w_out[l].astype(BF16)),
                  final_g=final_norm[l], seq=seq)
    return xf.reshape(batch, seq, d)
```

```python
import functools
import math

import jax
import jax.numpy as jnp
from jax import lax
from jax.experimental import pallas as pl
from jax.experimental.pallas import tpu as pltpu

F32 = jnp.float32
BF16 = jnp.bfloat16

EPS = 1e-6
NEG_BIG = -1e30
LOG2E = 1.4426950408889634
ROPE_THETA = 10000.0
GRID_W = 64
N_LAYERS = 2

HEAD_V = 64
V_ROWS = 80
GROUP_W = 256

NA_KH, NA_KW = 8, 16
NA_Q_ROWS = 4
NA_K_ROWS = 12
T5_BUCKETS = 32
T5_MAX_DIST = 128

FFN_TM = 1024
FFN_CHUNK = 1024
FLASH_T = 512
FLASH_AHEAD = 2
FLASH_SLOTS = 4
FLASH_UNROLL = 8
FLASH_DIFF_UNROLL = 4
PROJ_TM = FLASH_T

VMEM_LIMIT = 56 * 1024 * 1024


def _rms_rows(x, g_col, n):
    r = lax.rsqrt(jnp.sum(x * x, axis=0, keepdims=True) * (1.0 / n) + EPS)
    return (x * r) * g_col


def _rms_lanes(x, g_row):
    r = lax.rsqrt(jnp.mean(x * x, axis=-1, keepdims=True) + EPS)
    return (x * r) * g_row


def _dot(a, b):
    return jnp.dot(a, b, preferred_element_type=F32)


def _ffn_kernel(*refs, has_mix, has_final, chunks):
    x_ref, g_ref, wg_ref, wu_ref, wd_ref = refs[:5]
    rest = refs[5:]
    if has_mix:
        y_refs, wo_ref, rest = rest[:4], rest[4], rest[5:]
    if has_final:
        gf_ref, rest = rest[0], rest[1:]
    (o_ref,) = rest

    x = x_ref[...]
    if has_mix:
        for g in range(4):
            x = x + lax.dot_general(
                y_refs[g][0], wo_ref[g * GROUP_W:(g + 1) * GROUP_W, :],
                (((0,), (0,)), ((), ())), preferred_element_type=F32)
    o_ref[...] = x
    h = _rms_lanes(x, g_ref[...]).astype(BF16)
    acc = None
    for lo, hi in chunks:
        gate = _dot(h, wg_ref[:, lo:hi])
        up = _dot(h, wu_ref[:, lo:hi])
        a = (gate / (1.0 + jnp.exp(-gate))) * up
        part = _dot(a.astype(BF16), wd_ref[lo:hi, :])
        acc = part if acc is None else acc + part
    out = o_ref[...] + 0.5 * acc
    if has_final:
        out = _rms_lanes(out, gf_ref[...])
    o_ref[...] = out


def _ffn(x, g, wg, wu, wd, mix=None, final_g=None, seq=None):
    n, d = x.shape
    d_ff = wg.shape[1]
    tm = FFN_TM
    chunks = tuple((lo, min(lo + FFN_CHUNK, d_ff)) for lo in range(0, d_ff, FFN_CHUNK))
    once = pl.Buffered(1)
    in_specs = [
        pl.BlockSpec((tm, d), lambda i: (i, 0)),
        pl.BlockSpec((1, d), lambda i: (0, 0)),
        pl.BlockSpec((d, d_ff), lambda i: (0, 0), pipeline_mode=once),
        pl.BlockSpec((d, d_ff), lambda i: (0, 0), pipeline_mode=once),
        pl.BlockSpec((d_ff, d), lambda i: (0, 0), pipeline_mode=once),
    ]
    args = [x, g.reshape(1, d), wg, wu, wd]
    if mix is not None:
        ys, wo = mix
        nsb = seq // tm
        for y in ys:
            in_specs.append(pl.BlockSpec((1, GROUP_W, tm), lambda i: (i // nsb, 0, i % nsb)))
            args.append(y)
        in_specs.append(pl.BlockSpec(wo.shape, lambda i: (0, 0), pipeline_mode=once))
        args.append(wo)
    if final_g is not None:
        in_specs.append(pl.BlockSpec((1, d), lambda i: (0, 0)))
        args.append(final_g.reshape(1, d))
    return pl.pallas_call(
        functools.partial(_ffn_kernel, has_mix=mix is not None, has_final=final_g is not None, chunks=chunks),
        grid=(n // tm,),
        in_specs=in_specs,
        out_specs=pl.BlockSpec((tm, d), lambda i: (i, 0)),
        out_shape=jax.ShapeDtypeStruct((n, d), F32),
        compiler_params=pltpu.CompilerParams(
            dimension_semantics=("parallel",), vmem_limit_bytes=VMEM_LIMIT),
        name="ffn_mix" if mix is not None else "ffn",
    )(*args)


def _rope_rows(x, cos, sin):
    x1, x2 = x[:16], x[16:]
    return x1 * cos - x2 * sin, x1 * sin + x2 * cos


def _store_v(v_ref, hh, v):
    v_ref[0, hh, 0, 0:HEAD_V, :] = v.astype(BF16)
    v_ref[0, hh, 0, HEAD_V:V_ROWS, :] = jnp.ones((V_ROWS - HEAD_V, v.shape[1]), BF16)


def _proj_kernel(x_ref, gmix_ref, wint_ref,
                 naq_ref, nak_ref,
                 qlat_ref, wuqt_ref, kvlat_ref, wukvt_ref, mq_ref, mk_ref,
                 dq_ref, dk_ref, gq_ref, gk_ref,
                 cseq_ref, sseq_ref, crow_ref, srow_ref, ccol_ref, scol_ref,
                 qa_o, ka_o, va_o, qb_o, kb_o, vb_o, qc_o, kc_o, vc_o, qd_o, kd_o, vd_o):
    h = _rms_lanes(x_ref[...], gmix_ref[...]).astype(BF16)

    def proj(lo, hi):
        return lax.dot_general(wint_ref[lo:hi, :], h, (((1,), (1,)), ((), ())),
                               preferred_element_type=F32)

    pa = proj(0, 768)
    sa = LOG2E * 64 ** -0.5
    for hh in range(4):
        q = _rms_rows(pa[hh * 64:(hh + 1) * 64], naq_ref[...], 64) * sa
        k = _rms_rows(pa[256 + hh * 64:256 + (hh + 1) * 64], nak_ref[...], 64)
        qa_o[0, hh] = q.astype(BF16)
        ka_o[0, hh] = k.astype(BF16)
        _store_v(va_o, hh, pa[512 + hh * 64:512 + (hh + 1) * 64])

    pb = proj(768, 1184)
    cq = _rms_rows(pb[0:256], qlat_ref[...], 256).astype(BF16)
    qb = _dot(wuqt_ref[...], cq)
    ckv = _rms_rows(pb[256:384], kvlat_ref[...], 128).astype(BF16)
    kv = _dot(wukvt_ref[...], ckv)
    kr = pb[384:416]
    cseq, sseq = cseq_ref[...], sseq_ref[...]
    sb = LOG2E * 96 ** -0.5
    for hh in range(4):
        q = _rms_rows(qb[hh * 96:(hh + 1) * 96], mq_ref[...], 96) * sb
        r1, r2 = _rope_rows(q[64:96], cseq, sseq)
        qb_o[0, hh, 0:64, :] = q[0:64].astype(BF16)
        qb_o[0, hh, 64:80, :] = r1.astype(BF16)
        qb_o[0, hh, 80:96, :] = r2.astype(BF16)
        k = jnp.concatenate([kv[hh * 128:hh * 128 + 64], kr], axis=0)
        k = _rms_rows(k, mk_ref[...], 96)
        r1, r2 = _rope_rows(k[64:96], cseq, sseq)
        kb_o[0, hh, 0:64, :] = k[0:64].astype(BF16)
        kb_o[0, hh, 64:80, :] = r1.astype(BF16)
        kb_o[0, hh, 80:96, :] = r2.astype(BF16)
        _store_v(vb_o, hh, kv[hh * 128 + 64:hh * 128 + 128])

    pc = proj(1184, 1952)
    sc = LOG2E * 32 ** -0.5
    for j in range(8):
        q = _rms_rows(pc[j * 32:(j + 1) * 32], dq_ref[...], 32) * sc
        k = _rms_rows(pc[256 + j * 32:256 + (j + 1) * 32], dk_ref[...], 32)
        qc_o[0, j] = q.astype(BF16)
        kc_o[0, j] = k.astype(BF16)
    for hh in range(4):
        _store_v(vc_o, hh, pc[512 + hh * 64:512 + (hh + 1) * 64])

    pd = proj(1952, 2464)
    crow, srow, ccol, scol = crow_ref[...], srow_ref[...], ccol_ref[...], scol_ref[...]
    sd = LOG2E * 64 ** -0.5

    def axial(x, o_ref, hh):
        a1, a2 = _rope_rows(x[0:32], crow, srow)
        b1, b2 = _rope_rows(x[32:64], ccol, scol)
        o_ref[0, hh, 0:16, :] = a1.astype(BF16)
        o_ref[0, hh, 16:32, :] = a2.astype(BF16)
        o_ref[0, hh, 32:48, :] = b1.astype(BF16)
        o_ref[0, hh, 48:64, :] = b2.astype(BF16)

    for hh in range(4):
        axial(_rms_rows(pd[hh * 64:(hh + 1) * 64], gq_ref[...], 64) * sd, qd_o, hh)
    for hh in range(2):
        axial(_rms_rows(pd[256 + hh * 64:256 + (hh + 1) * 64], gk_ref[...], 64), kd_o, hh)
        _store_v(vd_o, hh, pd[384 + hh * 64:384 + (hh + 1) * 64])


def _proj(x, batch, seq, gmix, wint, cols, mats, tabs):
    n, d = x.shape
    tm = PROJ_TM
    nsb = seq // tm
    naq, nak, qlat, kvlat, mq, mk, dq, dk, gq, gk = cols
    wuqt, wukvt = mats

    def full(a):
        return pl.BlockSpec(a.shape, lambda i: (0,) * a.ndim)

    tab_spec = pl.BlockSpec((16, tm), lambda i: (0, i % nsb))
    args = [x, gmix.reshape(1, d), wint, naq, nak, qlat, wuqt, kvlat, wukvt, mq, mk, dq, dk, gq, gk, *tabs]
    in_specs = [pl.BlockSpec((tm, d), lambda i: (i, 0))] + [full(a) for a in args[1:15]] + [tab_spec] * 6

    def head_out(nh, rows):
        return (jax.ShapeDtypeStruct((batch, nh, rows, seq), BF16),
                pl.BlockSpec((1, nh, rows, tm), lambda i: (i // nsb, 0, 0, i % nsb)))

    def v_out(nh):
        return (jax.ShapeDtypeStruct((batch, nh, nsb, V_ROWS, tm), BF16),
                pl.BlockSpec((1, nh, 1, V_ROWS, tm), lambda i: (i // nsb, 0, i % nsb, 0, 0)))

    outs = [head_out(4, 64), head_out(4, 64), v_out(4),
            head_out(4, 96), head_out(4, 96), v_out(4),
            head_out(8, 32), head_out(8, 32), v_out(4),
            head_out(4, 64), head_out(2, 64), v_out(2)]
    return pl.pallas_call(
        _proj_kernel,
        grid=(n // tm,),
        in_specs=in_specs,
        out_specs=[o[1] for o in outs],
        out_shape=[o[0] for o in outs],
        compiler_params=pltpu.CompilerParams(
            dimension_semantics=("parallel",), vmem_limit_bytes=VMEM_LIMIT),
        name="mix_proj",
    )(*args)


def _col_max(s):
    parts = [s]
    while parts[0].shape[0] > 64:
        half = parts[0].shape[0] // 2
        parts = [jnp.maximum(p[:half], p[half:]) for p in parts]
    return jnp.max(parts[0], axis=0, keepdims=True)


def _produce(k_blk, q_blk, slot, s_scr, mc_scr, bias=None):
    s = _dot(k_blk, q_blk)
    if bias is not None:
        s = s + bias
    s_scr[slot] = s
    mc_scr[slot] = _col_max(s)


def _consume(s, m_cur, c, idx, v_blk, m_scr, acc_scr):
    m_prev = m_scr[idx]
    m_new = jnp.maximum(m_prev, m_cur + c)
    alpha = jnp.exp2(m_prev - m_new)
    p = jnp.exp2(s - (m_new - c)).astype(BF16)
    acc_scr[idx] = alpha * acc_scr[idx] + _dot(v_blk, p)
    m_scr[idx] = m_new


def _finish_head(acc):
    return acc[0:HEAD_V] / acc[HEAD_V:HEAD_V + 1]


def _flash_plain_kernel(q_ref, k_ref, v_ref, beta_ref, o_ref, s_scr, mc_scr, m_scr, acc_scr, *, kv_map, n_k):
    n_h = len(kv_map)
    m_scr[...] = jnp.full_like(m_scr, NEG_BIG)
    acc_scr[...] = jnp.zeros_like(acc_scr)

    def produce(i, hh):
        _produce(k_ref[0, kv_map[hh], i], q_ref[0, hh], hh % FLASH_SLOTS, s_scr, mc_scr)

    for hh in range(FLASH_AHEAD):
        produce(0, hh)

    def body(i, carry):
        nxt = jnp.minimum(i + 1, n_k - 1)
        for hh in range(n_h):
            ahead = hh + FLASH_AHEAD
            produce(i if ahead < n_h else nxt, ahead % n_h)
            slot = hh % FLASH_SLOTS
            _consume(s_scr[slot], mc_scr[slot], 0.0, hh, v_ref[0, kv_map[hh], i], m_scr, acc_scr)
        return carry

    lax.fori_loop(0, n_k, body, 0, unroll=FLASH_UNROLL)
    y = jnp.concatenate([_finish_head(acc_scr[hh]) for hh in range(n_h)], axis=0)
    o_ref[0] = _rms_rows(y, beta_ref[...], GROUP_W).astype(BF16)


def _flash_plain(qt, k, vt, beta_col, kv_map):
    b, hq, dq, s = qt.shape
    hk, n_k, t = k.shape[1], k.shape[2], k.shape[3]
    assert len(kv_map) % FLASH_SLOTS == 0
    return pl.pallas_call(
        functools.partial(_flash_plain_kernel, kv_map=kv_map, n_k=n_k),
        grid=(b, s // t),
        in_specs=[
            pl.BlockSpec((1, hq, dq, t), lambda bi, qi: (bi, 0, 0, qi)),
            pl.BlockSpec((1, hk, n_k, t, dq), lambda bi, qi: (bi, 0, 0, 0, 0)),
            pl.BlockSpec((1, hk, n_k, V_ROWS, t), lambda bi, qi: (bi, 0, 0, 0, 0)),
            pl.BlockSpec((GROUP_W, 1), lambda bi, qi: (0, 0)),
        ],
        out_specs=pl.BlockSpec((1, GROUP_W, t), lambda bi, qi: (bi, 0, qi)),
        out_shape=jax.ShapeDtypeStruct((b, GROUP_W, s), BF16),
        scratch_shapes=[pltpu.VMEM((FLASH_SLOTS, t, t), F32), pltpu.VMEM((FLASH_SLOTS, 1, t), F32),
                        pltpu.VMEM((hq, 1, t), F32), pltpu.VMEM((hq, V_ROWS, t), F32)],
        compiler_params=pltpu.CompilerParams(
            dimension_semantics=("parallel", "parallel"), vmem_limit_bytes=VMEM_LIMIT),
        name="flash_plain",
    )(qt, k, vt, beta_col)


def _flash_diff_kernel(t5_ref, q_ref, k_ref, v_ref, bias_ref, lam_ref, subln_ref, o_ref,
                       qpad_scr, s_scr, mc_scr, m_scr, acc_scr, *, lambda_init, n_k):
    qi = pl.program_id(1)
    n_m = 8
    dq = q_ref.shape[2]
    m_scr[...] = jnp.full_like(m_scr, NEG_BIG)
    acc_scr[...] = jnp.zeros_like(acc_scr)
    qpad_scr[...] = jnp.zeros_like(qpad_scr)
    for j in range(n_m):
        qpad_scr[j, (j % 2) * dq:(j % 2 + 1) * dq, :] = q_ref[0, j]

    def produce(i, j, bias=None):
        _produce(k_ref[0, j // 2, i], qpad_scr[j], j % FLASH_SLOTS, s_scr, mc_scr, bias)

    near_lo = jnp.maximum(qi - 1, 0)
    n_near = jnp.minimum(qi + 2, n_k) - near_lo
    far_left, far_right = T5_BUCKETS // 2 - 1, T5_BUCKETS - 1

    def tile_at(t):
        far = t - n_near
        return jnp.where(t < n_near, near_lo + t, jnp.where(far < near_lo, far, far + n_near))

    for j in range(FLASH_AHEAD):
        produce(tile_at(0), j)

    def visit(t, near):
        i = tile_at(t)
        nxt = tile_at(jnp.minimum(t + 1, n_k - 1))
        for j in range(n_m):
            hh = j // 2
            ahead = j + FLASH_AHEAD
            if ahead >= n_m:
                produce(nxt, ahead % n_m)
            elif near:
                produce(i, ahead, bias_ref[i - qi + 1, ahead // 2])
            else:
                produce(i, ahead)
            slot = j % FLASH_SLOTS
            if near and j < FLASH_AHEAD:
                s = s_scr[slot] + bias_ref[i - qi + 1, hh]
                _consume(s, _col_max(s), 0.0, j, v_ref[0, hh, i], m_scr, acc_scr)
            elif near:
                _consume(s_scr[slot], mc_scr[slot], 0.0, j, v_ref[0, hh, i], m_scr, acc_scr)
            else:
                c = jnp.where(i < qi, t5_ref[far_left * 4 + hh], t5_ref[far_right * 4 + hh]) * LOG2E
                _consume(s_scr[slot], mc_scr[slot], c, j, v_ref[0, hh, i], m_scr, acc_scr)

    def near_body(t, carry):
        visit(t, True)
        return carry

    def far_group(g, carry):
        for u in range(FLASH_DIFF_UNROLL):
            visit(n_near + g * FLASH_DIFF_UNROLL + u, False)
        return carry

    def far_body(t, carry):
        visit(t, False)
        return carry

    n_groups = (n_k - n_near) // FLASH_DIFF_UNROLL
    lax.fori_loop(0, n_near, near_body, 0)
    lax.fori_loop(0, n_groups, far_group, 0)
    lax.fori_loop(n_near + n_groups * FLASH_DIFF_UNROLL, n_k, far_body, 0)

    lp = lam_ref[...]
    lam = (jnp.exp(jnp.sum(lp[0:1] * lp[1:2], axis=1, keepdims=True))
           - jnp.exp(jnp.sum(lp[2:3] * lp[3:4], axis=1, keepdims=True)) + lambda_init)
    for hh in range(4):
        y = _finish_head(acc_scr[2 * hh]) - lam * _finish_head(acc_scr[2 * hh + 1])
        y = _rms_rows(y, subln_ref[...], HEAD_V) * (1.0 - lambda_init)
        o_ref[0, hh * HEAD_V:(hh + 1) * HEAD_V, :] = y.astype(BF16)


def _flash_diff(t5_flat, qt, k, vt, bias, lam_params, subln_col, lambda_init):
    b, hq, dq, s = qt.shape
    n_k, t = k.shape[2], k.shape[3]
    return pl.pallas_call(
        functools.partial(_flash_diff_kernel, lambda_init=lambda_init, n_k=n_k),
        grid=(b, s // t),
        in_specs=[
            pl.BlockSpec(memory_space=pltpu.SMEM),
            pl.BlockSpec((1, hq, dq, t), lambda bi, qi: (bi, 0, 0, qi)),
            pl.BlockSpec((1, 4, n_k, t, 2 * dq), lambda bi, qi: (bi, 0, 0, 0, 0)),
            pl.BlockSpec((1, 4, n_k, V_ROWS, t), lambda bi, qi: (bi, 0, 0, 0, 0)),
            pl.BlockSpec(bias.shape, lambda bi, qi: (0, 0, 0, 0), pipeline_mode=pl.Buffered(1)),
            pl.BlockSpec(lam_params.shape, lambda bi, qi: (0, 0)),
            pl.BlockSpec((HEAD_V, 1), lambda bi, qi: (0, 0)),
        ],
        out_specs=pl.BlockSpec((1, GROUP_W, t), lambda bi, qi: (bi, 0, qi)),
        out_shape=jax.ShapeDtypeStruct((b, GROUP_W, s), BF16),
        scratch_shapes=[pltpu.VMEM((hq, 2 * dq, t), BF16),
                        pltpu.VMEM((FLASH_SLOTS, t, t), F32), pltpu.VMEM((FLASH_SLOTS, 1, t), F32),
                        pltpu.VMEM((hq, 1, t), F32), pltpu.VMEM((hq, V_ROWS, t), F32)],
        compiler_params=pltpu.CompilerParams(
            dimension_semantics=("parallel", "parallel"), vmem_limit_bytes=VMEM_LIMIT),
        name="flash_diff",
    )(t5_flat, qt, k, vt, bias, lam_params, subln_col)


def _t5_bias_kernel(tab_ref, o_ref, *, t):
    d = pl.program_id(0) - 1
    rows = 16
    half = T5_BUCKETS // 2
    max_exact = half // 2

    def body(r, carry):
        kk = lax.broadcasted_iota(jnp.int32, (rows, t), 0) + r * rows
        qq = lax.broadcasted_iota(jnp.int32, (rows, t), 1)
        rel = d * t + kk - qq
        n = jnp.abs(rel)
        large = max_exact + (jnp.log(jnp.maximum(n, 1).astype(F32) / max_exact)
                             / math.log(T5_MAX_DIST / max_exact) * (half - max_exact)).astype(jnp.int32)
        large = jnp.minimum(large, half - 1)
        bucket = jnp.where(rel > 0, half, 0) + jnp.where(n < max_exact, n, large)
        accs = [jnp.zeros((rows, t), F32) for _ in range(4)]
        for bkt in range(T5_BUCKETS):
            hit = bucket == bkt
            accs = [jnp.where(hit, tab_ref[bkt * 4 + hh], accs[hh]) for hh in range(4)]
        for hh in range(4):
            o_ref[0, hh, pl.ds(pl.multiple_of(r * rows, rows), rows), :] = accs[hh] * LOG2E
        return carry

    lax.fori_loop(0, t // rows, body, 0)


def _t5_bias(t5_flat):
    t = FLASH_T
    return pl.pallas_call(
        functools.partial(_t5_bias_kernel, t=t),
        grid=(3,),
        in_specs=[pl.BlockSpec(memory_space=pltpu.SMEM)],
        out_specs=pl.BlockSpec((1, 4, t, t), lambda d: (d, 0, 0, 0)),
        out_shape=jax.ShapeDtypeStruct((3, 4, t, t), F32),
        name="t5_bias",
    )(t5_flat)


def _na_window(kind, j, i, grid_rows):
    r0 = (0, NA_Q_ROWS, grid_rows - NA_Q_ROWS)[kind]
    start = (0, 0, grid_rows - NA_K_ROWS)[kind]
    krow, qrow = start + j, r0 + i
    lo = min(max(qrow - NA_KH // 2, 0), grid_rows - NA_KH)
    return lo <= krow < lo + NA_KH, krow - qrow + NA_KH - 1


def _na_bias_kernel(rpb_ref, o_ref, t_scr, *, grid_rows):
    hh = pl.program_id(0)
    n_dr, n_dc = 2 * NA_KH - 1, 2 * NA_KW - 1
    shape = (GRID_W, NA_Q_ROWS * GRID_W)
    lane = lax.broadcasted_iota(jnp.int32, shape, 1)
    kc = lax.broadcasted_iota(jnp.int32, shape, 0)
    qc = lane % GRID_W
    grp = lane // GRID_W
    dcm = jnp.clip(kc - qc + NA_KW - 1, 0, n_dc - 1)
    qs = jnp.clip(qc - NA_KW // 2, 0, GRID_W - NA_KW)
    col_ok = (kc >= qs) & (kc < qs + NA_KW)

    for dr in range(n_dr):
        base = (hh * n_dr + dr) * n_dc
        t_scr[dr] = lax.fori_loop(
            0, n_dc, lambda dc, acc: jnp.where(dcm == dc, rpb_ref[base + dc], acc), jnp.zeros(shape, F32)) * LOG2E

    for kind in range(3):
        for j in range(NA_K_ROWS):
            wins = [_na_window(kind, j, i, grid_rows) for i in range(NA_Q_ROWS)]
            blk = jnp.full(shape, NEG_BIG, F32)
            for i, (inside, dr) in enumerate(wins):
                if inside:
                    blk = jnp.where((grp == i) & col_ok, t_scr[dr], blk)
            o_ref[kind, 0, j * GRID_W:(j + 1) * GRID_W, :] = blk


def _na_bias(rpb_flat, grid_rows):
    kt, qt = NA_K_ROWS * GRID_W, NA_Q_ROWS * GRID_W
    return pl.pallas_call(
        functools.partial(_na_bias_kernel, grid_rows=grid_rows),
        grid=(4,),
        in_specs=[pl.BlockSpec(memory_space=pltpu.SMEM)],
        out_specs=pl.BlockSpec((3, 1, kt, qt), lambda h: (0, h, 0, 0)),
        out_shape=jax.ShapeDtypeStruct((3, 4, kt, qt), F32),
        scratch_shapes=[pltpu.VMEM((2 * NA_KH - 1, GRID_W, qt), F32)],
        name="na_bias",
    )(rpb_flat)


def _na_kernel(q_ref, k0_ref, k1_ref, k2_ref, v0_ref, v1_ref, v2_ref, bias_ref, beta_ref, o_ref):
    qt = NA_Q_ROWS * GRID_W
    k_refs = (k0_ref, k1_ref, k2_ref)
    v_refs = (v0_ref, v1_ref, v2_ref)

    def logits(hh):
        q = q_ref[0, hh]
        return [_dot(k_refs[j][0, hh], q) + bias_ref[0, hh, j * qt:(j + 1) * qt, :] for j in range(3)]

    outs = []
    ss_next = logits(0)
    for hh in range(4):
        ss = ss_next
        if hh + 1 < 4:
            ss_next = logits(hh + 1)
        m = functools.reduce(jnp.maximum, [_col_max(s) for s in ss])
        acc = None
        for j in range(3):
            p = jnp.exp2(ss[j] - m).astype(BF16)
            pv = _dot(v_refs[j][0, hh, 0], p)
            acc = pv if acc is None else acc + pv
        outs.append(_finish_head(acc))
    y = jnp.concatenate(outs, axis=0)
    o_ref[0] = _rms_rows(y, beta_ref[...], GROUP_W).astype(BF16)


def _na(qt_arr, k, vt, bias, beta_col):
    b, nh, d, s = qt_arr.shape
    qt = NA_Q_ROWS * GRID_W
    kt = NA_K_ROWS * GRID_W
    n_t = s // qt
    n_win = kt // qt

    def win(j):
        return lambda bi, ti: jnp.clip(ti - 1, 0, n_t - n_win) + j

    def kind(bi, ti):
        return (jnp.where(ti == 0, 0, jnp.where(ti == n_t - 1, 2, 1)), 0, 0, 0)

    k_specs = [pl.BlockSpec((1, nh, qt, d), (lambda j: lambda bi, ti: (bi, 0, win(j)(bi, ti), 0))(j))
               for j in range(n_win)]
    per = vt.shape[4] // qt
    v_specs = [pl.BlockSpec((1, nh, 1, V_ROWS, qt),
                            (lambda j: lambda bi, ti: (bi, 0, win(j)(bi, ti) // per, 0, win(j)(bi, ti) % per))(j))
               for j in range(n_win)]
    return pl.pallas_call(
        _na_kernel,
        grid=(b, n_t),
        in_specs=[pl.BlockSpec((1, nh, d, qt), lambda bi, ti: (bi, 0, 0, ti))] + k_specs + v_specs + [
            pl.BlockSpec((1, nh, kt, qt), kind),
            pl.BlockSpec((GROUP_W, 1), lambda bi, ti: (0, 0)),
        ],
        out_specs=pl.BlockSpec((1, GROUP_W, qt), lambda bi, ti: (bi, 0, ti)),
        out_shape=jax.ShapeDtypeStruct((b, GROUP_W, s), BF16),
        compiler_params=pltpu.CompilerParams(
            dimension_semantics=("parallel", "parallel"), vmem_limit_bytes=VMEM_LIMIT),
        name="na_attn",
    )(qt_arr, k, k, k, vt, vt, vt, bias, beta_col)


def _rope_tables(pos, dim):
    inv = jnp.exp(-math.log(ROPE_THETA) * jnp.arange(0, dim, 2, dtype=F32) / dim)
    ang = pos.astype(F32)[:, None] * inv[None, :]
    return jnp.cos(ang).T, jnp.sin(ang).T


def _col(v):
    return v.reshape(-1, 1).astype(F32)


def _lambda_init(layer):
    return 0.8 - 0.6 * math.exp(-0.3 * layer)


def kernel(x, ffn1_norm, ffn1_w_gate, ffn1_w_up, ffn1_w_down, mix_norm, w_in, na_q_norm, na_k_norm, na_rpb, na_beta, mla_q_lat_norm, mla_w_uq, mla_kv_lat_norm, mla_w_ukv, mla_q_norm, mla_k_norm, mla_beta, diff_q_norm, diff_k_norm, diff_lambda, diff_subln, gqa_q_norm, gqa_k_norm, gqa_beta, w_out, ffn2_norm, ffn2_w_gate, ffn2_w_up, ffn2_w_down, final_norm, t5_bias):
    batch, seq, d = x.shape
    grid_rows = seq // GRID_W
    pos = jnp.arange(seq, dtype=jnp.int32)
    tabs = (*_rope_tables(pos, 32), *_rope_tables(pos // GRID_W, 32), *_rope_tables(pos % GRID_W, 32))
    t5_flat = t5_bias.reshape(-1).astype(F32)
    t5_tiles = _t5_bias(t5_flat)

    xf = x.reshape(batch * seq, d)
    for l in range(N_LAYERS):
        xf = _ffn(xf, ffn1_norm[l], ffn1_w_gate[l].astype(BF16), ffn1_w_up[l].astype(BF16),
                  ffn1_w_down[l].astype(BF16))
        cols = tuple(_col(v[l]) for v in (na_q_norm, na_k_norm, mla_q_lat_norm, mla_kv_lat_norm,
                                          mla_q_norm, mla_k_norm, diff_q_norm, diff_k_norm,
                                          gqa_q_norm, gqa_k_norm))
        mats = (mla_w_uq[l].T.astype(BF16), mla_w_ukv[l].T.astype(BF16))
        (qa, ka, va, qb, kb, vb, qc, kc, vc, qd, kd, vd) = _proj(
            xf, batch, seq, mix_norm[l], w_in[l].T.astype(BF16), cols, mats, tabs)
        n_k = seq // FLASH_T
        ka = jnp.swapaxes(ka, 2, 3)
        kb = jnp.swapaxes(kb, 2, 3).reshape(batch, 4, n_k, FLASH_T, 96)
        kc = jnp.swapaxes(kc.reshape(batch, 4, 64, seq), 2, 3).reshape(batch, 4, n_k, FLASH_T, 64)
        kd = jnp.swapaxes(kd, 2, 3).reshape(batch, 2, n_k, FLASH_T, 64)

        ya = _na(qa, ka, va, _na_bias(na_rpb[l].reshape(-1).astype(F32), grid_rows), _col(na_beta[l]))
        yb = _flash_plain(qb, kb, vb, _col(mla_beta[l]), (0, 1, 2, 3))
        yc = _flash_diff(t5_flat, qc, kc, vc, t5_tiles, diff_lambda[l].astype(F32), _col(diff_subln[l]),
                         _lambda_init(l))
        yd = _flash_plain(qd, kd, vd, _col(gqa_beta[l]), (0, 0, 1, 1))

        xf = _ffn(xf, ffn2_norm[l], ffn2_w_gate[l].astype(BF16), ffn2_w_up[l].astype(BF16),
                  ffn2_w_down[l].astype(BF16), mix=((ya, yb, yc, yd), w_out[l].astype(BF16)),
                  final_g=final_norm[l], seq=seq)
    return xf.reshape(batch, seq, d)
```

```python
import functools
import math

import jax
import jax.numpy as jnp
from jax import lax
from jax.experimental import pallas as pl
from jax.experimental.pallas import tpu as pltpu

F32 = jnp.float32
BF16 = jnp.bfloat16

EPS = 1e-6
NEG_BIG = -1e30
LOG2E = 1.4426950408889634
ROPE_THETA = 10000.0
GRID_W = 64
N_LAYERS = 2

HEAD_V = 64
V_ROWS = 80
GROUP_W = 256
A_ROWS, B_ROWS, C_ROWS, D_ROWS = (0, 768), (768, 1184), (1184, 1952), (1952, 2464)

NA_KH, NA_KW = 8, 16
NA_Q_ROWS = 4
NA_K_ROWS = 12
T5_BUCKETS = 32
T5_MAX_DIST = 128

FFN_TM = 1024
FFN_CHUNK = 1024
FLASH_T = 512
FLASH_AHEAD = 2
FLASH_SLOTS = 4
FLASH_UNROLL = 8
FLASH_DIFF_UNROLL = 4
PROJ_TM = FLASH_T

VMEM_LIMIT = 56 * 1024 * 1024


def _rms_rows(x, g_col, n):
    r = lax.rsqrt(jnp.sum(x * x, axis=0, keepdims=True) * (1.0 / n) + EPS)
    return (x * r) * g_col


def _rms_lanes(x, g_row):
    r = lax.rsqrt(jnp.mean(x * x, axis=-1, keepdims=True) + EPS)
    return (x * r) * g_row


def _dot(a, b):
    return jnp.dot(a, b, preferred_element_type=F32)


def _ffn_kernel(*refs, has_mix, has_final, chunks):
    x_ref, g_ref, wg_ref, wu_ref, wd_ref = refs[:5]
    rest = refs[5:]
    if has_mix:
        y_refs, wo_ref, rest = rest[:4], rest[4], rest[5:]
    if has_final:
        gf_ref, rest = rest[0], rest[1:]
    (o_ref,) = rest

    x = x_ref[...]
    if has_mix:
        for g in range(4):
            x = x + lax.dot_general(
                y_refs[g][0], wo_ref[g * GROUP_W:(g + 1) * GROUP_W, :],
                (((0,), (0,)), ((), ())), preferred_element_type=F32)
    o_ref[...] = x
    h = _rms_lanes(x, g_ref[...]).astype(BF16)
    acc = None
    for lo, hi in chunks:
        gate = _dot(h, wg_ref[:, lo:hi])
        up = _dot(h, wu_ref[:, lo:hi])
        a = (gate / (1.0 + jnp.exp(-gate))) * up
        part = _dot(a.astype(BF16), wd_ref[lo:hi, :])
        acc = part if acc is None else acc + part
    out = o_ref[...] + 0.5 * acc
    if has_final:
        out = _rms_lanes(out, gf_ref[...])
    o_ref[...] = out


def _ffn(x, g, wg, wu, wd, mix=None, final_g=None, seq=None):
    n, d = x.shape
    d_ff = wg.shape[1]
    tm = FFN_TM
    chunks = tuple((lo, min(lo + FFN_CHUNK, d_ff)) for lo in range(0, d_ff, FFN_CHUNK))
    once = pl.Buffered(1)
    in_specs = [
        pl.BlockSpec((tm, d), lambda i: (i, 0)),
        pl.BlockSpec((1, d), lambda i: (0, 0)),
        pl.BlockSpec((d, d_ff), lambda i: (0, 0), pipeline_mode=once),
        pl.BlockSpec((d, d_ff), lambda i: (0, 0), pipeline_mode=once),
        pl.BlockSpec((d_ff, d), lambda i: (0, 0), pipeline_mode=once),
    ]
    args = [x, g.reshape(1, d), wg, wu, wd]
    if mix is not None:
        ys, wo = mix
        nsb = seq // tm
        for y in ys:
            in_specs.append(pl.BlockSpec((1, GROUP_W, tm), lambda i: (i // nsb, 0, i % nsb)))
            args.append(y)
        in_specs.append(pl.BlockSpec(wo.shape, lambda i: (0, 0), pipeline_mode=once))
        args.append(wo)
    if final_g is not None:
        in_specs.append(pl.BlockSpec((1, d), lambda i: (0, 0)))
        args.append(final_g.reshape(1, d))
    return pl.pallas_call(
        functools.partial(_ffn_kernel, has_mix=mix is not None, has_final=final_g is not None, chunks=chunks),
        grid=(n // tm,),
        in_specs=in_specs,
        out_specs=pl.BlockSpec((tm, d), lambda i: (i, 0)),
        out_shape=jax.ShapeDtypeStruct((n, d), F32),
        compiler_params=pltpu.CompilerParams(
            dimension_semantics=("parallel",), vmem_limit_bytes=VMEM_LIMIT),
        name="ffn_mix" if mix is not None else "ffn",
    )(*args)


def _rope_rows(x, cos, sin):
    x1, x2 = x[:16], x[16:]
    return x1 * cos - x2 * sin, x1 * sin + x2 * cos


def _store_v(v_ref, hh, v):
    v_ref[0, hh, 0, 0:HEAD_V, :] = v.astype(BF16)
    v_ref[0, hh, 0, HEAD_V:V_ROWS, :] = jnp.ones((V_ROWS - HEAD_V, v.shape[1]), BF16)


def _proj_kernel(x_ref, gmix_ref, wint_ref,
                 naq_ref, nak_ref,
                 qlat_ref, wuqt_ref, kvlat_ref, wukvt_ref, mq_ref, mk_ref,
                 dq_ref, dk_ref, gq_ref, gk_ref,
                 cseq_ref, sseq_ref, crow_ref, srow_ref, ccol_ref, scol_ref,
                 qa_o, ka_o, va_o, qb_o, kb_o, vb_o, qc_o, kc_o, vc_o, qd_o, kd_o, vd_o):
    h = _rms_lanes(x_ref[...], gmix_ref[...]).astype(BF16)

    def proj(lo, hi):
        return lax.dot_general(wint_ref[lo:hi, :], h, (((1,), (1,)), ((), ())),
                               preferred_element_type=F32)

    pa = proj(*A_ROWS)
    sa = LOG2E * 64 ** -0.5
    for hh in range(4):
        q = _rms_rows(pa[hh * 64:(hh + 1) * 64], naq_ref[...], 64) * sa
        k = _rms_rows(pa[256 + hh * 64:256 + (hh + 1) * 64], nak_ref[...], 64)
        qa_o[0, hh] = q.astype(BF16)
        ka_o[0, hh] = k.astype(BF16)
        _store_v(va_o, hh, pa[512 + hh * 64:512 + (hh + 1) * 64])

    pb = proj(*B_ROWS)
    cq = _rms_rows(pb[0:256], qlat_ref[...], 256).astype(BF16)
    qb = _dot(wuqt_ref[...], cq)
    ckv = _rms_rows(pb[256:384], kvlat_ref[...], 128).astype(BF16)
    kv = _dot(wukvt_ref[...], ckv)
    kr = pb[384:416]
    cseq, sseq = cseq_ref[...], sseq_ref[...]
    sb = LOG2E * 96 ** -0.5
    for hh in range(4):
        q = _rms_rows(qb[hh * 96:(hh + 1) * 96], mq_ref[...], 96) * sb
        r1, r2 = _rope_rows(q[64:96], cseq, sseq)
        qb_o[0, hh, 0:64, :] = q[0:64].astype(BF16)
        qb_o[0, hh, 64:80, :] = r1.astype(BF16)
        qb_o[0, hh, 80:96, :] = r2.astype(BF16)
        k = jnp.concatenate([kv[hh * 128:hh * 128 + 64], kr], axis=0)
        k = _rms_rows(k, mk_ref[...], 96)
        r1, r2 = _rope_rows(k[64:96], cseq, sseq)
        kb_o[0, hh, 0:64, :] = k[0:64].astype(BF16)
        kb_o[0, hh, 64:80, :] = r1.astype(BF16)
        kb_o[0, hh, 80:96, :] = r2.astype(BF16)
        _store_v(vb_o, hh, kv[hh * 128 + 64:hh * 128 + 128])

    pc = proj(*C_ROWS)
    sc = LOG2E * 32 ** -0.5
    for j in range(8):
        q = _rms_rows(pc[j * 32:(j + 1) * 32], dq_ref[...], 32) * sc
        k = _rms_rows(pc[256 + j * 32:256 + (j + 1) * 32], dk_ref[...], 32)
        qc_o[0, j] = q.astype(BF16)
        kc_o[0, j] = k.astype(BF16)
    for hh in range(4):
        _store_v(vc_o, hh, pc[512 + hh * 64:512 + (hh + 1) * 64])

    pd = proj(*D_ROWS)
    crow, srow, ccol, scol = crow_ref[...], srow_ref[...], ccol_ref[...], scol_ref[...]
    sd = LOG2E * 64 ** -0.5

    def axial(x, o_ref, hh):
        a1, a2 = _rope_rows(x[0:32], crow, srow)
        b1, b2 = _rope_rows(x[32:64], ccol, scol)
        o_ref[0, hh, 0:16, :] = a1.astype(BF16)
        o_ref[0, hh, 16:32, :] = a2.astype(BF16)
        o_ref[0, hh, 32:48, :] = b1.astype(BF16)
        o_ref[0, hh, 48:64, :] = b2.astype(BF16)

    for hh in range(4):
        axial(_rms_rows(pd[hh * 64:(hh + 1) * 64], gq_ref[...], 64) * sd, qd_o, hh)
    for hh in range(2):
        axial(_rms_rows(pd[256 + hh * 64:256 + (hh + 1) * 64], gk_ref[...], 64), kd_o, hh)
        _store_v(vd_o, hh, pd[384 + hh * 64:384 + (hh + 1) * 64])


def _proj(x, batch, seq, gmix, wint, cols, mats, tabs):
    n, d = x.shape
    tm = PROJ_TM
    nsb = seq // tm
    naq, nak, qlat, kvlat, mq, mk, dq, dk, gq, gk = cols
    wuqt, wukvt = mats

    def full(a):
        return pl.BlockSpec(a.shape, lambda i: (0,) * a.ndim)

    tab_spec = pl.BlockSpec((16, tm), lambda i: (0, i % nsb))
    args = [x, gmix.reshape(1, d), wint, naq, nak, qlat, wuqt, kvlat, wukvt, mq, mk, dq, dk, gq, gk, *tabs]
    in_specs = [pl.BlockSpec((tm, d), lambda i: (i, 0))] + [full(a) for a in args[1:15]] + [tab_spec] * 6

    def head_out(nh, rows):
        return (jax.ShapeDtypeStruct((batch, nh, rows, seq), BF16),
                pl.BlockSpec((1, nh, rows, tm), lambda i: (i // nsb, 0, 0, i % nsb)))

    def v_out(nh):
        return (jax.ShapeDtypeStruct((batch, nh, nsb, V_ROWS, tm), BF16),
                pl.BlockSpec((1, nh, 1, V_ROWS, tm), lambda i: (i // nsb, 0, i % nsb, 0, 0)))

    outs = [head_out(4, 64), head_out(4, 64), v_out(4),
            head_out(4, 96), head_out(4, 96), v_out(4),
            head_out(8, 32), head_out(8, 32), v_out(4),
            head_out(4, 64), head_out(2, 64), v_out(2)]
    return pl.pallas_call(
        _proj_kernel,
        grid=(n // tm,),
        in_specs=in_specs,
        out_specs=[o[1] for o in outs],
        out_shape=[o[0] for o in outs],
        compiler_params=pltpu.CompilerParams(
            dimension_semantics=("parallel",), vmem_limit_bytes=VMEM_LIMIT),
        name="mix_proj",
    )(*args)


def _col_max(s):
    parts = [s]
    while parts[0].shape[0] > 64:
        half = parts[0].shape[0] // 2
        parts = [jnp.maximum(p[:half], p[half:]) for p in parts]
    return jnp.max(parts[0], axis=0, keepdims=True)


def _produce(k_blk, q_blk, slot, s_scr, mc_scr, bias=None):
    s = _dot(k_blk, q_blk)
    if bias is not None:
        s = s + bias
    s_scr[slot] = s
    mc_scr[slot] = _col_max(s)


def _consume(s, m_cur, c, idx, v_blk, m_scr, acc_scr):
    m_prev = m_scr[idx]
    m_new = jnp.maximum(m_prev, m_cur + c)
    alpha = jnp.exp2(m_prev - m_new)
    p = jnp.exp2(s - (m_new - c)).astype(BF16)
    acc_scr[idx] = alpha * acc_scr[idx] + _dot(v_blk, p)
    m_scr[idx] = m_new


def _finish_head(acc):
    return acc[0:HEAD_V] / acc[HEAD_V:HEAD_V + 1]


def _flash_plain_kernel(q_ref, k_ref, v_ref, beta_ref, o_ref, s_scr, mc_scr, m_scr, acc_scr, *, kv_map, n_k):
    n_h = len(kv_map)
    m_scr[...] = jnp.full_like(m_scr, NEG_BIG)
    acc_scr[...] = jnp.zeros_like(acc_scr)

    def produce(i, hh):
        _produce(k_ref[0, kv_map[hh], i], q_ref[0, hh], hh % FLASH_SLOTS, s_scr, mc_scr)

    for hh in range(FLASH_AHEAD):
        produce(0, hh)

    def body(i, carry):
        nxt = jnp.minimum(i + 1, n_k - 1)
        for hh in range(n_h):
            ahead = hh + FLASH_AHEAD
            produce(i if ahead < n_h else nxt, ahead % n_h)
            slot = hh % FLASH_SLOTS
            _consume(s_scr[slot], mc_scr[slot], 0.0, hh, v_ref[0, kv_map[hh], i], m_scr, acc_scr)
        return carry

    lax.fori_loop(0, n_k, body, 0, unroll=FLASH_UNROLL)
    y = jnp.concatenate([_finish_head(acc_scr[hh]) for hh in range(n_h)], axis=0)
    o_ref[0] = _rms_rows(y, beta_ref[...], GROUP_W).astype(BF16)


def _flash_plain(qt, k, vt, beta_col, kv_map):
    b, hq, dq, s = qt.shape
    hk, n_k, t = k.shape[1], k.shape[2], k.shape[3]
    assert len(kv_map) % FLASH_SLOTS == 0
    return pl.pallas_call(
        functools.partial(_flash_plain_kernel, kv_map=kv_map, n_k=n_k),
        grid=(b, s // t),
        in_specs=[
            pl.BlockSpec((1, hq, dq, t), lambda bi, qi: (bi, 0, 0, qi)),
            pl.BlockSpec((1, hk, n_k, t, dq), lambda bi, qi: (bi, 0, 0, 0, 0)),
            pl.BlockSpec((1, hk, n_k, V_ROWS, t), lambda bi, qi: (bi, 0, 0, 0, 0)),
            pl.BlockSpec((GROUP_W, 1), lambda bi, qi: (0, 0)),
        ],
        out_specs=pl.BlockSpec((1, GROUP_W, t), lambda bi, qi: (bi, 0, qi)),
        out_shape=jax.ShapeDtypeStruct((b, GROUP_W, s), BF16),
        scratch_shapes=[pltpu.VMEM((FLASH_SLOTS, t, t), F32), pltpu.VMEM((FLASH_SLOTS, 1, t), F32),
                        pltpu.VMEM((hq, 1, t), F32), pltpu.VMEM((hq, V_ROWS, t), F32)],
        compiler_params=pltpu.CompilerParams(
            dimension_semantics=("parallel", "parallel"), vmem_limit_bytes=VMEM_LIMIT),
        name="flash_plain",
    )(qt, k, vt, beta_col)


def _flash_diff_kernel(t5_ref, q_ref, k_ref, v_ref, bias_ref, lam_ref, subln_ref, o_ref,
                       qpad_scr, s_scr, mc_scr, m_scr, acc_scr, *, lambda_init, n_k):
    qi = pl.program_id(1)
    n_m = 8
    dq = q_ref.shape[2]
    m_scr[...] = jnp.full_like(m_scr, NEG_BIG)
    acc_scr[...] = jnp.zeros_like(acc_scr)
    qpad_scr[...] = jnp.zeros_like(qpad_scr)
    for j in range(n_m):
        qpad_scr[j, (j % 2) * dq:(j % 2 + 1) * dq, :] = q_ref[0, j]

    def produce(i, j, bias=None):
        _produce(k_ref[0, j // 2, i], qpad_scr[j], j % FLASH_SLOTS, s_scr, mc_scr, bias)

    near_lo = jnp.maximum(qi - 1, 0)
    n_near = jnp.minimum(qi + 2, n_k) - near_lo
    far_left, far_right = T5_BUCKETS // 2 - 1, T5_BUCKETS - 1

    def tile_at(t):
        far = t - n_near
        return jnp.where(t < n_near, near_lo + t, jnp.where(far < near_lo, far, far + n_near))

    for j in range(FLASH_AHEAD):
        produce(tile_at(0), j)

    def visit(t, near):
        i = tile_at(t)
        nxt = tile_at(jnp.minimum(t + 1, n_k - 1))
        for j in range(n_m):
            hh = j // 2
            ahead = j + FLASH_AHEAD
            if ahead >= n_m:
                produce(nxt, ahead % n_m)
            elif near:
                produce(i, ahead, bias_ref[i - qi + 1, ahead // 2])
            else:
                produce(i, ahead)
            slot = j % FLASH_SLOTS
            if near and j < FLASH_AHEAD:
                s = s_scr[slot] + bias_ref[i - qi + 1, hh]
                _consume(s, _col_max(s), 0.0, j, v_ref[0, hh, i], m_scr, acc_scr)
            elif near:
                _consume(s_scr[slot], mc_scr[slot], 0.0, j, v_ref[0, hh, i], m_scr, acc_scr)
            else:
                c = jnp.where(i < qi, t5_ref[far_left * 4 + hh], t5_ref[far_right * 4 + hh]) * LOG2E
                _consume(s_scr[slot], mc_scr[slot], c, j, v_ref[0, hh, i], m_scr, acc_scr)

    def near_body(t, carry):
        visit(t, True)
        return carry

    def far_group(g, carry):
        for u in range(FLASH_DIFF_UNROLL):
            visit(n_near + g * FLASH_DIFF_UNROLL + u, False)
        return carry

    def far_body(t, carry):
        visit(t, False)
        return carry

    n_groups = (n_k - n_near) // FLASH_DIFF_UNROLL
    lax.fori_loop(0, n_near, near_body, 0)
    lax.fori_loop(0, n_groups, far_group, 0)
    lax.fori_loop(n_near + n_groups * FLASH_DIFF_UNROLL, n_k, far_body, 0)

    lp = lam_ref[...]
    lam = (jnp.exp(jnp.sum(lp[0:1] * lp[1:2], axis=1, keepdims=True))
           - jnp.exp(jnp.sum(lp[2:3] * lp[3:4], axis=1, keepdims=True)) + lambda_init)
    for hh in range(4):
        y = _finish_head(acc_scr[2 * hh]) - lam * _finish_head(acc_scr[2 * hh + 1])
        y = _rms_rows(y, subln_ref[...], HEAD_V) * (1.0 - lambda_init)
        o_ref[0, hh * HEAD_V:(hh + 1) * HEAD_V, :] = y.astype(BF16)


def _flash_diff(t5_flat, qt, k, vt, bias, lam_params, subln_col, lambda_init):
    b, hq, dq, s = qt.shape
    n_k, t = k.shape[2], k.shape[3]
    return pl.pallas_call(
        functools.partial(_flash_diff_kernel, lambda_init=lambda_init, n_k=n_k),
        grid=(b, s // t),
        in_specs=[
            pl.BlockSpec(memory_space=pltpu.SMEM),
            pl.BlockSpec((1, hq, dq, t), lambda bi, qi: (bi, 0, 0, qi)),
            pl.BlockSpec((1, 4, n_k, t, 2 * dq), lambda bi, qi: (bi, 0, 0, 0, 0)),
            pl.BlockSpec((1, 4, n_k, V_ROWS, t), lambda bi, qi: (bi, 0, 0, 0, 0)),
            pl.BlockSpec(bias.shape, lambda bi, qi: (0, 0, 0, 0), pipeline_mode=pl.Buffered(1)),
            pl.BlockSpec(lam_params.shape, lambda bi, qi: (0, 0)),
            pl.BlockSpec((HEAD_V, 1), lambda bi, qi: (0, 0)),
        ],
        out_specs=pl.BlockSpec((1, GROUP_W, t), lambda bi, qi: (bi, 0, qi)),
        out_shape=jax.ShapeDtypeStruct((b, GROUP_W, s), BF16),
        scratch_shapes=[pltpu.VMEM((hq, 2 * dq, t), BF16),
                        pltpu.VMEM((FLASH_SLOTS, t, t), F32), pltpu.VMEM((FLASH_SLOTS, 1, t), F32),
                        pltpu.VMEM((hq, 1, t), F32), pltpu.VMEM((hq, V_ROWS, t), F32)],
        compiler_params=pltpu.CompilerParams(
            dimension_semantics=("parallel", "parallel"), vmem_limit_bytes=VMEM_LIMIT),
        name="flash_diff",
    )(t5_flat, qt, k, vt, bias, lam_params, subln_col)


def _t5_bias_kernel(tab_ref, o_ref, *, t):
    d = pl.program_id(0) - 1
    rows = 16
    half = T5_BUCKETS // 2
    max_exact = half // 2

    def body(r, carry):
        kk = lax.broadcasted_iota(jnp.int32, (rows, t), 0) + r * rows
        qq = lax.broadcasted_iota(jnp.int32, (rows, t), 1)
        rel = d * t + kk - qq
        n = jnp.abs(rel)
        large = max_exact + (jnp.log(jnp.maximum(n, 1).astype(F32) / max_exact)
                             / math.log(T5_MAX_DIST / max_exact) * (half - max_exact)).astype(jnp.int32)
        large = jnp.minimum(large, half - 1)
        bucket = jnp.where(rel > 0, half, 0) + jnp.where(n < max_exact, n, large)
        accs = [jnp.zeros((rows, t), F32) for _ in range(4)]
        for bkt in range(T5_BUCKETS):
            hit = bucket == bkt
            accs = [jnp.where(hit, tab_ref[bkt * 4 + hh], accs[hh]) for hh in range(4)]
        for hh in range(4):
            o_ref[0, hh, pl.ds(pl.multiple_of(r * rows, rows), rows), :] = accs[hh] * LOG2E
        return carry

    lax.fori_loop(0, t // rows, body, 0)


def _t5_bias(t5_flat):
    t = FLASH_T
    return pl.pallas_call(
        functools.partial(_t5_bias_kernel, t=t),
        grid=(3,),
        in_specs=[pl.BlockSpec(memory_space=pltpu.SMEM)],
        out_specs=pl.BlockSpec((1, 4, t, t), lambda d: (d, 0, 0, 0)),
        out_shape=jax.ShapeDtypeStruct((3, 4, t, t), F32),
        name="t5_bias",
    )(t5_flat)


def _na_window(kind, j, i, grid_rows):
    r0 = (0, NA_Q_ROWS, grid_rows - NA_Q_ROWS)[kind]
    start = (0, 0, grid_rows - NA_K_ROWS)[kind]
    krow, qrow = start + j, r0 + i
    lo = min(max(qrow - NA_KH // 2, 0), grid_rows - NA_KH)
    return lo <= krow < lo + NA_KH, krow - qrow + NA_KH - 1


def _na_bias_kernel(rpb_ref, o_ref, t_scr, *, grid_rows):
    hh = pl.program_id(0)
    n_dr, n_dc = 2 * NA_KH - 1, 2 * NA_KW - 1
    shape = (GRID_W, NA_Q_ROWS * GRID_W)
    lane = lax.broadcasted_iota(jnp.int32, shape, 1)
    kc = lax.broadcasted_iota(jnp.int32, shape, 0)
    qc = lane % GRID_W
    grp = lane // GRID_W
    dcm = jnp.clip(kc - qc + NA_KW - 1, 0, n_dc - 1)
    qs = jnp.clip(qc - NA_KW // 2, 0, GRID_W - NA_KW)
    col_ok = (kc >= qs) & (kc < qs + NA_KW)

    for dr in range(n_dr):
        base = (hh * n_dr + dr) * n_dc
        t_scr[dr] = lax.fori_loop(
            0, n_dc, lambda dc, acc: jnp.where(dcm == dc, rpb_ref[base + dc], acc), jnp.zeros(shape, F32)) * LOG2E

    for kind in range(3):
        for j in range(NA_K_ROWS):
            wins = [_na_window(kind, j, i, grid_rows) for i in range(NA_Q_ROWS)]
            blk = jnp.full(shape, NEG_BIG, F32)
            for i, (inside, dr) in enumerate(wins):
                if inside:
                    blk = jnp.where((grp == i) & col_ok, t_scr[dr], blk)
            o_ref[kind, 0, j * GRID_W:(j + 1) * GRID_W, :] = blk


def _na_bias(rpb_flat, grid_rows):
    kt, qt = NA_K_ROWS * GRID_W, NA_Q_ROWS * GRID_W
    return pl.pallas_call(
        functools.partial(_na_bias_kernel, grid_rows=grid_rows),
        grid=(4,),
        in_specs=[pl.BlockSpec(memory_space=pltpu.SMEM)],
        out_specs=pl.BlockSpec((3, 1, kt, qt), lambda h: (0, h, 0, 0)),
        out_shape=jax.ShapeDtypeStruct((3, 4, kt, qt), F32),
        scratch_shapes=[pltpu.VMEM((2 * NA_KH - 1, GRID_W, qt), F32)],
        name="na_bias",
    )(rpb_flat)


def _na_kernel(q_ref, k0_ref, k1_ref, k2_ref, v0_ref, v1_ref, v2_ref, bias_ref, beta_ref, o_ref):
    qt = NA_Q_ROWS * GRID_W
    k_refs = (k0_ref, k1_ref, k2_ref)
    v_refs = (v0_ref, v1_ref, v2_ref)

    def logits(hh):
        q = q_ref[0, hh]
        return [_dot(k_refs[j][0, hh], q) + bias_ref[0, hh, j * qt:(j + 1) * qt, :] for j in range(3)]

    outs = []
    ss_next = logits(0)
    for hh in range(4):
        ss = ss_next
        if hh + 1 < 4:
            ss_next = logits(hh + 1)
        m = functools.reduce(jnp.maximum, [_col_max(s) for s in ss])
        acc = None
        for j in range(3):
            p = jnp.exp2(ss[j] - m).astype(BF16)
            pv = _dot(v_refs[j][0, hh, 0], p)
            acc = pv if acc is None else acc + pv
        outs.append(_finish_head(acc))
    y = jnp.concatenate(outs, axis=0)
    o_ref[0] = _rms_rows(y, beta_ref[...], GROUP_W).astype(BF16)


def _na(qt_arr, k, vt, bias, beta_col):
    b, nh, d, s = qt_arr.shape
    qt = NA_Q_ROWS * GRID_W
    kt = NA_K_ROWS * GRID_W
    n_t = s // qt
    n_win = kt // qt

    def win(j):
        return lambda bi, ti: jnp.clip(ti - 1, 0, n_t - n_win) + j

    def kind(bi, ti):
        return (jnp.where(ti == 0, 0, jnp.where(ti == n_t - 1, 2, 1)), 0, 0, 0)

    k_specs = [pl.BlockSpec((1, nh, qt, d), (lambda j: lambda bi, ti: (bi, 0, win(j)(bi, ti), 0))(j))
               for j in range(n_win)]
    per = vt.shape[4] // qt
    v_specs = [pl.BlockSpec((1, nh, 1, V_ROWS, qt),
                            (lambda j: lambda bi, ti: (bi, 0, win(j)(bi, ti) // per, 0, win(j)(bi, ti) % per))(j))
               for j in range(n_win)]
    return pl.pallas_call(
        _na_kernel,
        grid=(b, n_t),
        in_specs=[pl.BlockSpec((1, nh, d, qt), lambda bi, ti: (bi, 0, 0, ti))] + k_specs + v_specs + [
            pl.BlockSpec((1, nh, kt, qt), kind),
            pl.BlockSpec((GROUP_W, 1), lambda bi, ti: (0, 0)),
        ],
        out_specs=pl.BlockSpec((1, GROUP_W, qt), lambda bi, ti: (bi, 0, ti)),
        out_shape=jax.ShapeDtypeStruct((b, GROUP_W, s), BF16),
        compiler_params=pltpu.CompilerParams(
            dimension_semantics=("parallel", "parallel"), vmem_limit_bytes=VMEM_LIMIT),
        name="na_attn",
    )(qt_arr, k, k, k, vt, vt, vt, bias, beta_col)


def _rope_tables(pos, dim):
    inv = jnp.exp(-math.log(ROPE_THETA) * jnp.arange(0, dim, 2, dtype=F32) / dim)
    ang = pos.astype(F32)[:, None] * inv[None, :]
    return jnp.cos(ang).T, jnp.sin(ang).T


def _col(v):
    return v.reshape(-1, 1).astype(F32)


def _lambda_init(layer):
    return 0.8 - 0.6 * math.exp(-0.3 * layer)


def kernel(x, ffn1_norm, ffn1_w_gate, ffn1_w_up, ffn1_w_down, mix_norm, w_in, na_q_norm, na_k_norm, na_rpb, na_beta, mla_q_lat_norm, mla_w_uq, mla_kv_lat_norm, mla_w_ukv, mla_q_norm, mla_k_norm, mla_beta, diff_q_norm, diff_k_norm, diff_lambda, diff_subln, gqa_q_norm, gqa_k_norm, gqa_beta, w_out, ffn2_norm, ffn2_w_gate, ffn2_w_up, ffn2_w_down, final_norm, t5_bias):
    batch, seq, d = x.shape
    grid_rows = seq // GRID_W
    pos = jnp.arange(seq, dtype=jnp.int32)
    tabs = (*_rope_tables(pos, 32), *_rope_tables(pos // GRID_W, 32), *_rope_tables(pos % GRID_W, 32))
    t5_flat = t5_bias.reshape(-1).astype(F32)
    t5_tiles = _t5_bias(t5_flat)

    xf = x.reshape(batch * seq, d)
    for l in range(N_LAYERS):
        xf = _ffn(xf, ffn1_norm[l], ffn1_w_gate[l].astype(BF16), ffn1_w_up[l].astype(BF16),
                  ffn1_w_down[l].astype(BF16))
        cols = tuple(_col(v[l]) for v in (na_q_norm, na_k_norm, mla_q_lat_norm, mla_kv_lat_norm,
                                          mla_q_norm, mla_k_norm, diff_q_norm, diff_k_norm,
                                          gqa_q_norm, gqa_k_norm))
        mats = (mla_w_uq[l].T.astype(BF16), mla_w_ukv[l].T.astype(BF16))
        (qa, ka, va, qb, kb, vb, qc, kc, vc, qd, kd, vd) = _proj(
            xf, batch, seq, mix_norm[l], w_in[l].T.astype(BF16), cols, mats, tabs)
        n_k = seq // FLASH_T
        ka = jnp.swapaxes(ka, 2, 3)
        kb = jnp.swapaxes(kb, 2, 3).reshape(batch, 4, n_k, FLASH_T, 96)
        kc = jnp.swapaxes(kc.reshape(batch, 4, 64, seq), 2, 3).reshape(batch, 4, n_k, FLASH_T, 64)
        kd = jnp.swapaxes(kd, 2, 3).reshape(batch, 2, n_k, FLASH_T, 64)

        ya = _na(qa, ka, va, _na_bias(na_rpb[l].reshape(-1).astype(F32), grid_rows), _col(na_beta[l]))
        yb = _flash_plain(qb, kb, vb, _col(mla_beta[l]), (0, 1, 2, 3))
        yc = _flash_diff(t5_flat, qc, kc, vc, t5_tiles, diff_lambda[l].astype(F32), _col(diff_subln[l]),
                         _lambda_init(l))
        yd = _flash_plain(qd, kd, vd, _col(gqa_beta[l]), (0, 0, 1, 1))

        xf = _ffn(xf, ffn2_norm[l], ffn2_w_gate[l].astype(BF16), ffn2_w_up[l].astype(BF16),
                  ffn2_w_down[l].astype(BF16), mix=((ya, yb, yc, yd), w_out[l].astype(BF16)),
                  final_g=final_norm[l], seq=seq)
    return xf.reshape(batch, seq, d)
```

```python
import functools
import math

import jax
import jax.numpy as jnp
from jax import lax
from jax.experimental import pallas as pl
from jax.experimental.pallas import tpu as pltpu

F32 = jnp.float32
BF16 = jnp.bfloat16

EPS = 1e-6
NEG_BIG = -1e30
LOG2E = 1.4426950408889634
ROPE_THETA = 10000.0
GRID_W = 64
N_LAYERS = 2

HEAD_V = 64
V_ROWS = 80
GROUP_W = 256
A_ROWS, B_ROWS, C_ROWS, D_ROWS = (0, 768), (768, 1184), (1184, 1952), (1952, 2464)

NA_KH, NA_KW = 8, 16
NA_Q_ROWS = 4
NA_K_ROWS = 12
T5_BUCKETS = 32
T5_MAX_DIST = 128

FFN_TM = 1024
FFN_CHUNK = 1024
FLASH_T = 512
FLASH_AHEAD = 2
FLASH_SLOTS = 4
FLASH_UNROLL = 8
FLASH_DIFF_UNROLL = 4
PROJ_TM = FLASH_T

VMEM_LIMIT = 56 * 1024 * 1024


def _rms_rows(x, g_col, n):
    r = lax.rsqrt(jnp.sum(x * x, axis=0, keepdims=True) * (1.0 / n) + EPS)
    return (x * r) * g_col


def _rms_lanes(x, g_row):
    r = lax.rsqrt(jnp.mean(x * x, axis=-1, keepdims=True) + EPS)
    return (x * r) * g_row


def _dot(a, b):
    return jnp.dot(a, b, preferred_element_type=F32)


def _ffn_kernel(*refs, has_mix, has_final, chunks):
    x_ref, g_ref, wg_ref, wu_ref, wd_ref = refs[:5]
    rest = refs[5:]
    if has_mix:
        y_refs, wo_ref, rest = rest[:4], rest[4], rest[5:]
    if has_final:
        gf_ref, rest = rest[0], rest[1:]
    (o_ref,) = rest

    x = x_ref[...]
    if has_mix:
        for g in range(4):
            x = x + lax.dot_general(
                y_refs[g][0], wo_ref[g * GROUP_W:(g + 1) * GROUP_W, :],
                (((0,), (0,)), ((), ())), preferred_element_type=F32)
    o_ref[...] = x
    h = _rms_lanes(x, g_ref[...]).astype(BF16)
    acc = None
    for lo, hi in chunks:
        gate = _dot(h, wg_ref[:, lo:hi])
        up = _dot(h, wu_ref[:, lo:hi])
        a = (gate / (1.0 + jnp.exp(-gate))) * up
        part = _dot(a.astype(BF16), wd_ref[lo:hi, :])
        acc = part if acc is None else acc + part
    out = o_ref[...] + 0.5 * acc
    if has_final:
        out = _rms_lanes(out, gf_ref[...])
    o_ref[...] = out


def _ffn(x, g, wg, wu, wd, mix=None, final_g=None, seq=None):
    n, d = x.shape
    d_ff = wg.shape[1]
    tm = FFN_TM
    chunks = tuple((lo, min(lo + FFN_CHUNK, d_ff)) for lo in range(0, d_ff, FFN_CHUNK))
    once = pl.Buffered(1)
    in_specs = [
        pl.BlockSpec((tm, d), lambda i: (i, 0)),
        pl.BlockSpec((1, d), lambda i: (0, 0)),
        pl.BlockSpec((d, d_ff), lambda i: (0, 0), pipeline_mode=once),
        pl.BlockSpec((d, d_ff), lambda i: (0, 0), pipeline_mode=once),
        pl.BlockSpec((d_ff, d), lambda i: (0, 0), pipeline_mode=once),
    ]
    args = [x, g.reshape(1, d), wg, wu, wd]
    if mix is not None:
        ys, wo = mix
        nsb = seq // tm
        for y in ys:
            in_specs.append(pl.BlockSpec((1, GROUP_W, tm), lambda i: (i // nsb, 0, i % nsb)))
            args.append(y)
        in_specs.append(pl.BlockSpec(wo.shape, lambda i: (0, 0), pipeline_mode=once))
        args.append(wo)
    if final_g is not None:
        in_specs.append(pl.BlockSpec((1, d), lambda i: (0, 0)))
        args.append(final_g.reshape(1, d))
    return pl.pallas_call(
        functools.partial(_ffn_kernel, has_mix=mix is not None, has_final=final_g is not None, chunks=chunks),
        grid=(n // tm,),
        in_specs=in_specs,
        out_specs=pl.BlockSpec((tm, d), lambda i: (i, 0)),
        out_shape=jax.ShapeDtypeStruct((n, d), F32),
        compiler_params=pltpu.CompilerParams(
            dimension_semantics=("parallel",), vmem_limit_bytes=VMEM_LIMIT),
        name="ffn_mix" if mix is not None else "ffn",
    )(*args)


def _rope_rows(x, cos, sin):
    x1, x2 = x[:16], x[16:]
    return x1 * cos - x2 * sin, x1 * sin + x2 * cos


def _store_v(v_ref, hh, v):
    v_ref[0, hh, 0, 0:HEAD_V, :] = v.astype(BF16)
    v_ref[0, hh, 0, HEAD_V:V_ROWS, :] = jnp.ones((V_ROWS - HEAD_V, v.shape[1]), BF16)


def _proj_kernel(x_ref, gmix_ref, wint_ref,
                 naq_ref, nak_ref,
                 qlat_ref, wuqt_ref, kvlat_ref, wukvt_ref, mq_ref, mk_ref,
                 dq_ref, dk_ref, gq_ref, gk_ref,
                 cseq_ref, sseq_ref, crow_ref, srow_ref, ccol_ref, scol_ref,
                 qa_o, ka_o, va_o, qb_o, kb_o, vb_o, qc_o, kc_o, vc_o, qd_o, kd_o, vd_o):
    h = _rms_lanes(x_ref[...], gmix_ref[...]).astype(BF16)

    def proj(lo, hi):
        return lax.dot_general(wint_ref[lo:hi, :], h, (((1,), (1,)), ((), ())),
                               preferred_element_type=F32)

    pa = proj(*A_ROWS)
    sa = LOG2E * 64 ** -0.5
    for hh in range(4):
        q = _rms_rows(pa[hh * 64:(hh + 1) * 64], naq_ref[...], 64) * sa
        k = _rms_rows(pa[256 + hh * 64:256 + (hh + 1) * 64], nak_ref[...], 64)
        qa_o[0, hh] = q.astype(BF16)
        ka_o[0, hh] = k.astype(BF16)
        _store_v(va_o, hh, pa[512 + hh * 64:512 + (hh + 1) * 64])

    pb = proj(*B_ROWS)
    cq = _rms_rows(pb[0:256], qlat_ref[...], 256).astype(BF16)
    qb = _dot(wuqt_ref[...], cq)
    ckv = _rms_rows(pb[256:384], kvlat_ref[...], 128).astype(BF16)
    kv = _dot(wukvt_ref[...], ckv)
    kr = pb[384:416]
    cseq, sseq = cseq_ref[...], sseq_ref[...]
    sb = LOG2E * 96 ** -0.5
    for hh in range(4):
        q = _rms_rows(qb[hh * 96:(hh + 1) * 96], mq_ref[...], 96) * sb
        r1, r2 = _rope_rows(q[64:96], cseq, sseq)
        qb_o[0, hh, 0:64, :] = q[0:64].astype(BF16)
        qb_o[0, hh, 64:80, :] = r1.astype(BF16)
        qb_o[0, hh, 80:96, :] = r2.astype(BF16)
        k = jnp.concatenate([kv[hh * 128:hh * 128 + 64], kr], axis=0)
        k = _rms_rows(k, mk_ref[...], 96)
        r1, r2 = _rope_rows(k[64:96], cseq, sseq)
        kb_o[0, hh, 0:64, :] = k[0:64].astype(BF16)
        kb_o[0, hh, 64:80, :] = r1.astype(BF16)
        kb_o[0, hh, 80:96, :] = r2.astype(BF16)
        _store_v(vb_o, hh, kv[hh * 128 + 64:hh * 128 + 128])

    pc = proj(*C_ROWS)
    sc = LOG2E * 32 ** -0.5
    for j in range(8):
        q = _rms_rows(pc[j * 32:(j + 1) * 32], dq_ref[...], 32) * sc
        k = _rms_rows(pc[256 + j * 32:256 + (j + 1) * 32], dk_ref[...], 32)
        qc_o[0, j] = q.astype(BF16)
        kc_o[0, j] = k.astype(BF16)
    for hh in range(4):
        _store_v(vc_o, hh, pc[512 + hh * 64:512 + (hh + 1) * 64])

    pd = proj(*D_ROWS)
    crow, srow, ccol, scol = crow_ref[...], srow_ref[...], ccol_ref[...], scol_ref[...]
    sd = LOG2E * 64 ** -0.5

    def axial(x, o_ref, hh):
        a1, a2 = _rope_rows(x[0:32], crow, srow)
        b1, b2 = _rope_rows(x[32:64], ccol, scol)
        o_ref[0, hh, 0:16, :] = a1.astype(BF16)
        o_ref[0, hh, 16:32, :] = a2.astype(BF16)
        o_ref[0, hh, 32:48, :] = b1.astype(BF16)
        o_ref[0, hh, 48:64, :] = b2.astype(BF16)

    for hh in range(4):
        axial(_rms_rows(pd[hh * 64:(hh + 1) * 64], gq_ref[...], 64) * sd, qd_o, hh)
    for hh in range(2):
        axial(_rms_rows(pd[256 + hh * 64:256 + (hh + 1) * 64], gk_ref[...], 64), kd_o, hh)
        _store_v(vd_o, hh, pd[384 + hh * 64:384 + (hh + 1) * 64])


def _proj(x, batch, seq, gmix, wint, cols, mats, tabs):
    n, d = x.shape
    tm = PROJ_TM
    nsb = seq // tm
    naq, nak, qlat, kvlat, mq, mk, dq, dk, gq, gk = cols
    wuqt, wukvt = mats

    def full(a):
        return pl.BlockSpec(a.shape, lambda i: (0,) * a.ndim)

    tab_spec = pl.BlockSpec((16, tm), lambda i: (0, i % nsb))
    args = [x, gmix.reshape(1, d), wint, naq, nak, qlat, wuqt, kvlat, wukvt, mq, mk, dq, dk, gq, gk, *tabs]
    in_specs = [pl.BlockSpec((tm, d), lambda i: (i, 0))] + [full(a) for a in args[1:15]] + [tab_spec] * 6

    def head_out(nh, rows):
        return (jax.ShapeDtypeStruct((batch, nh, rows, seq), BF16),
                pl.BlockSpec((1, nh, rows, tm), lambda i: (i // nsb, 0, 0, i % nsb)))

    def v_out(nh):
        return (jax.ShapeDtypeStruct((batch, nh, nsb, V_ROWS, tm), BF16),
                pl.BlockSpec((1, nh, 1, V_ROWS, tm), lambda i: (i // nsb, 0, i % nsb, 0, 0)))

    outs = [head_out(4, 64), head_out(4, 64), v_out(4),
            head_out(4, 96), head_out(4, 96), v_out(4),
            head_out(8, 32), head_out(8, 32), v_out(4),
            head_out(4, 64), head_out(2, 64), v_out(2)]
    return pl.pallas_call(
        _proj_kernel,
        grid=(n // tm,),
        in_specs=in_specs,
        out_specs=[o[1] for o in outs],
        out_shape=[o[0] for o in outs],
        compiler_params=pltpu.CompilerParams(
            dimension_semantics=("parallel",), vmem_limit_bytes=VMEM_LIMIT),
        name="mix_proj",
    )(*args)


def _col_max(s):
    parts = [s]
    while parts[0].shape[0] > 64:
        half = parts[0].shape[0] // 2
        parts = [jnp.maximum(p[:half], p[half:]) for p in parts]
    return jnp.max(parts[0], axis=0, keepdims=True)


def _produce(k_blk, q_blk, slot, s_scr, mc_scr, bias=None):
    s = _dot(k_blk, q_blk)
    if bias is not None:
        s = s + bias
    s_scr[slot] = s
    mc_scr[slot] = _col_max(s)


def _consume(s, m_cur, c, idx, v_blk, m_scr, acc_scr):
    m_prev = m_scr[idx]
    m_new = jnp.maximum(m_prev, m_cur + c)
    alpha = jnp.exp2(m_prev - m_new)
    p = jnp.exp2(s - (m_new - c)).astype(BF16)
    acc_scr[idx] = alpha * acc_scr[idx] + _dot(v_blk, p)
    m_scr[idx] = m_new


def _finish_head(acc):
    return acc[0:HEAD_V] / acc[HEAD_V:HEAD_V + 1]


def _flash_plain_kernel(q_ref, k_ref, v_ref, beta_ref, o_ref, s_scr, mc_scr, m_scr, acc_scr, *, kv_map, n_k):
    n_h = len(kv_map)
    m_scr[...] = jnp.full_like(m_scr, NEG_BIG)
    acc_scr[...] = jnp.zeros_like(acc_scr)

    def produce(i, hh):
        _produce(k_ref[0, kv_map[hh], i], q_ref[0, hh], hh % FLASH_SLOTS, s_scr, mc_scr)

    for hh in range(FLASH_AHEAD):
        produce(0, hh)

    def body(i, carry):
        nxt = jnp.minimum(i + 1, n_k - 1)
        for hh in range(n_h):
            ahead = hh + FLASH_AHEAD
            produce(i if ahead < n_h else nxt, ahead % n_h)
            slot = hh % FLASH_SLOTS
            _consume(s_scr[slot], mc_scr[slot], 0.0, hh, v_ref[0, kv_map[hh], i], m_scr, acc_scr)
        return carry

    lax.fori_loop(0, n_k, body, 0, unroll=FLASH_UNROLL)
    y = jnp.concatenate([_finish_head(acc_scr[hh]) for hh in range(n_h)], axis=0)
    o_ref[0] = _rms_rows(y, beta_ref[...], GROUP_W).astype(BF16)


def _flash_plain(qt, k, vt, beta_col, kv_map):
    b, hq, dq, s = qt.shape
    hk, n_k, t = k.shape[1], k.shape[2], k.shape[3]
    assert len(kv_map) % FLASH_SLOTS == 0
    return pl.pallas_call(
        functools.partial(_flash_plain_kernel, kv_map=kv_map, n_k=n_k),
        grid=(b, s // t),
        in_specs=[
            pl.BlockSpec((1, hq, dq, t), lambda bi, qi: (bi, 0, 0, qi)),
            pl.BlockSpec((1, hk, n_k, t, dq), lambda bi, qi: (bi, 0, 0, 0, 0)),
            pl.BlockSpec((1, hk, n_k, V_ROWS, t), lambda bi, qi: (bi, 0, 0, 0, 0)),
            pl.BlockSpec((GROUP_W, 1), lambda bi, qi: (0, 0)),
        ],
        out_specs=pl.BlockSpec((1, GROUP_W, t), lambda bi, qi: (bi, 0, qi)),
        out_shape=jax.ShapeDtypeStruct((b, GROUP_W, s), BF16),
        scratch_shapes=[pltpu.VMEM((FLASH_SLOTS, t, t), F32), pltpu.VMEM((FLASH_SLOTS, 1, t), F32),
                        pltpu.VMEM((hq, 1, t), F32), pltpu.VMEM((hq, V_ROWS, t), F32)],
        compiler_params=pltpu.CompilerParams(
            dimension_semantics=("parallel", "parallel"), vmem_limit_bytes=VMEM_LIMIT),
        name="flash_plain",
    )(qt, k, vt, beta_col)


def _flash_diff_kernel(t5_ref, q_ref, k_ref, v_ref, bias_ref, lam_ref, subln_ref, o_ref,
                       qpad_scr, s_scr, mc_scr, m_scr, acc_scr, *, lambda_init, n_k):
    qi = pl.program_id(1)
    n_m = 8
    dq = q_ref.shape[2]
    m_scr[...] = jnp.full_like(m_scr, NEG_BIG)
    acc_scr[...] = jnp.zeros_like(acc_scr)
    qpad_scr[...] = jnp.zeros_like(qpad_scr)
    for j in range(n_m):
        qpad_scr[j, (j % 2) * dq:(j % 2 + 1) * dq, :] = q_ref[0, j]

    def produce(i, j, bias=None):
        _produce(k_ref[0, j // 2, i], qpad_scr[j], j % FLASH_SLOTS, s_scr, mc_scr, bias)

    near_lo = jnp.maximum(qi - 1, 0)
    n_near = jnp.minimum(qi + 2, n_k) - near_lo
    far_left, far_right = T5_BUCKETS // 2 - 1, T5_BUCKETS - 1

    def tile_at(t):
        far = t - n_near
        return jnp.where(t < n_near, near_lo + t, jnp.where(far < near_lo, far, far + n_near))

    for j in range(FLASH_AHEAD):
        produce(tile_at(0), j, bias_ref[tile_at(0) - qi + 1, j // 2])

    def visit(t, near):
        i = tile_at(t)
        nxt = tile_at(jnp.minimum(t + 1, n_k - 1))
        next_near = (t + 1 < n_near).astype(F32)
        next_off = jnp.clip(nxt - qi + 1, 0, 2)
        for j in range(n_m):
            hh = j // 2
            ahead = j + FLASH_AHEAD
            if not near:
                produce(i if ahead < n_m else nxt, ahead % n_m)
            elif ahead < n_m:
                produce(i, ahead, bias_ref[i - qi + 1, ahead // 2])
            else:
                produce(nxt, ahead % n_m, bias_ref[next_off, (ahead % n_m) // 2] * next_near)
            slot = j % FLASH_SLOTS
            if near:
                _consume(s_scr[slot], mc_scr[slot], 0.0, j, v_ref[0, hh, i], m_scr, acc_scr)
            else:
                c = jnp.where(i < qi, t5_ref[far_left * 4 + hh], t5_ref[far_right * 4 + hh]) * LOG2E
                _consume(s_scr[slot], mc_scr[slot], c, j, v_ref[0, hh, i], m_scr, acc_scr)

    def near_body(t, carry):
        visit(t, True)
        return carry

    def far_group(g, carry):
        for u in range(FLASH_DIFF_UNROLL):
            visit(n_near + g * FLASH_DIFF_UNROLL + u, False)
        return carry

    def far_body(t, carry):
        visit(t, False)
        return carry

    n_groups = (n_k - n_near) // FLASH_DIFF_UNROLL
    lax.fori_loop(0, n_near, near_body, 0)
    lax.fori_loop(0, n_groups, far_group, 0)
    lax.fori_loop(n_near + n_groups * FLASH_DIFF_UNROLL, n_k, far_body, 0)

    lp = lam_ref[...]
    lam = (jnp.exp(jnp.sum(lp[0:1] * lp[1:2], axis=1, keepdims=True))
           - jnp.exp(jnp.sum(lp[2:3] * lp[3:4], axis=1, keepdims=True)) + lambda_init)
    for hh in range(4):
        y = _finish_head(acc_scr[2 * hh]) - lam * _finish_head(acc_scr[2 * hh + 1])
        y = _rms_rows(y, subln_ref[...], HEAD_V) * (1.0 - lambda_init)
        o_ref[0, hh * HEAD_V:(hh + 1) * HEAD_V, :] = y.astype(BF16)


def _flash_diff(t5_flat, qt, k, vt, bias, lam_params, subln_col, lambda_init):
    b, hq, dq, s = qt.shape
    n_k, t = k.shape[2], k.shape[3]
    return pl.pallas_call(
        functools.partial(_flash_diff_kernel, lambda_init=lambda_init, n_k=n_k),
        grid=(b, s // t),
        in_specs=[
            pl.BlockSpec(memory_space=pltpu.SMEM),
            pl.BlockSpec((1, hq, dq, t), lambda bi, qi: (bi, 0, 0, qi)),
            pl.BlockSpec((1, 4, n_k, t, 2 * dq), lambda bi, qi: (bi, 0, 0, 0, 0)),
            pl.BlockSpec((1, 4, n_k, V_ROWS, t), lambda bi, qi: (bi, 0, 0, 0, 0)),
            pl.BlockSpec(bias.shape, lambda bi, qi: (0, 0, 0, 0), pipeline_mode=pl.Buffered(1)),
            pl.BlockSpec(lam_params.shape, lambda bi, qi: (0, 0)),
            pl.BlockSpec((HEAD_V, 1), lambda bi, qi: (0, 0)),
        ],
        out_specs=pl.BlockSpec((1, GROUP_W, t), lambda bi, qi: (bi, 0, qi)),
        out_shape=jax.ShapeDtypeStruct((b, GROUP_W, s), BF16),
        scratch_shapes=[pltpu.VMEM((hq, 2 * dq, t), BF16),
                        pltpu.VMEM((FLASH_SLOTS, t, t), F32), pltpu.VMEM((FLASH_SLOTS, 1, t), F32),
                        pltpu.VMEM((hq, 1, t), F32), pltpu.VMEM((hq, V_ROWS, t), F32)],
        compiler_params=pltpu.CompilerParams(
            dimension_semantics=("parallel", "parallel"), vmem_limit_bytes=VMEM_LIMIT),
        name="flash_diff",
    )(t5_flat, qt, k, vt, bias, lam_params, subln_col)


def _t5_bias_kernel(tab_ref, o_ref, *, t):
    d = pl.program_id(0) - 1
    rows = 16
    half = T5_BUCKETS // 2
    max_exact = half // 2

    def body(r, carry):
        kk = lax.broadcasted_iota(jnp.int32, (rows, t), 0) + r * rows
        qq = lax.broadcasted_iota(jnp.int32, (rows, t), 1)
        rel = d * t + kk - qq
        n = jnp.abs(rel)
        large = max_exact + (jnp.log(jnp.maximum(n, 1).astype(F32) / max_exact)
                             / math.log(T5_MAX_DIST / max_exact) * (half - max_exact)).astype(jnp.int32)
        large = jnp.minimum(large, half - 1)
        bucket = jnp.where(rel > 0, half, 0) + jnp.where(n < max_exact, n, large)
        accs = [jnp.zeros((rows, t), F32) for _ in range(4)]
        for bkt in range(T5_BUCKETS):
            hit = bucket == bkt
            accs = [jnp.where(hit, tab_ref[bkt * 4 + hh], accs[hh]) for hh in range(4)]
        for hh in range(4):
            o_ref[0, hh, pl.ds(pl.multiple_of(r * rows, rows), rows), :] = accs[hh] * LOG2E
        return carry

    lax.fori_loop(0, t // rows, body, 0)


def _t5_bias(t5_flat):
    t = FLASH_T
    return pl.pallas_call(
        functools.partial(_t5_bias_kernel, t=t),
        grid=(3,),
        in_specs=[pl.BlockSpec(memory_space=pltpu.SMEM)],
        out_specs=pl.BlockSpec((1, 4, t, t), lambda d: (d, 0, 0, 0)),
        out_shape=jax.ShapeDtypeStruct((3, 4, t, t), F32),
        name="t5_bias",
    )(t5_flat)


def _na_window(kind, j, i, grid_rows):
    r0 = (0, NA_Q_ROWS, grid_rows - NA_Q_ROWS)[kind]
    start = (0, 0, grid_rows - NA_K_ROWS)[kind]
    krow, qrow = start + j, r0 + i
    lo = min(max(qrow - NA_KH // 2, 0), grid_rows - NA_KH)
    return lo <= krow < lo + NA_KH, krow - qrow + NA_KH - 1


def _na_bias_kernel(rpb_ref, o_ref, t_scr, *, grid_rows):
    hh = pl.program_id(0)
    n_dr, n_dc = 2 * NA_KH - 1, 2 * NA_KW - 1
    shape = (GRID_W, NA_Q_ROWS * GRID_W)
    lane = lax.broadcasted_iota(jnp.int32, shape, 1)
    kc = lax.broadcasted_iota(jnp.int32, shape, 0)
    qc = lane % GRID_W
    grp = lane // GRID_W
    dcm = jnp.clip(kc - qc + NA_KW - 1, 0, n_dc - 1)
    qs = jnp.clip(qc - NA_KW // 2, 0, GRID_W - NA_KW)
    col_ok = (kc >= qs) & (kc < qs + NA_KW)

    for dr in range(n_dr):
        base = (hh * n_dr + dr) * n_dc
        t_scr[dr] = lax.fori_loop(
            0, n_dc, lambda dc, acc: jnp.where(dcm == dc, rpb_ref[base + dc], acc), jnp.zeros(shape, F32)) * LOG2E

    for kind in range(3):
        for j in range(NA_K_ROWS):
            wins = [_na_window(kind, j, i, grid_rows) for i in range(NA_Q_ROWS)]
            blk = jnp.full(shape, NEG_BIG, F32)
            for i, (inside, dr) in enumerate(wins):
                if inside:
                    blk = jnp.where((grp == i) & col_ok, t_scr[dr], blk)
            o_ref[kind, 0, j * GRID_W:(j + 1) * GRID_W, :] = blk


def _na_bias(rpb_flat, grid_rows):
    kt, qt = NA_K_ROWS * GRID_W, NA_Q_ROWS * GRID_W
    return pl.pallas_call(
        functools.partial(_na_bias_kernel, grid_rows=grid_rows),
        grid=(4,),
        in_specs=[pl.BlockSpec(memory_space=pltpu.SMEM)],
        out_specs=pl.BlockSpec((3, 1, kt, qt), lambda h: (0, h, 0, 0)),
        out_shape=jax.ShapeDtypeStruct((3, 4, kt, qt), F32),
        scratch_shapes=[pltpu.VMEM((2 * NA_KH - 1, GRID_W, qt), F32)],
        name="na_bias",
    )(rpb_flat)


def _na_kernel(q_ref, k0_ref, k1_ref, k2_ref, v0_ref, v1_ref, v2_ref, bias_ref, beta_ref, o_ref):
    qt = NA_Q_ROWS * GRID_W
    k_refs = (k0_ref, k1_ref, k2_ref)
    v_refs = (v0_ref, v1_ref, v2_ref)

    def logits(hh):
        q = q_ref[0, hh]
        return [_dot(k_refs[j][0, hh], q) + bias_ref[0, hh, j * qt:(j + 1) * qt, :] for j in range(3)]

    outs = []
    ss_next = logits(0)
    for hh in range(4):
        ss = ss_next
        if hh + 1 < 4:
            ss_next = logits(hh + 1)
        m = functools.reduce(jnp.maximum, [_col_max(s) for s in ss])
        acc = None
        for j in range(3):
            p = jnp.exp2(ss[j] - m).astype(BF16)
            pv = _dot(v_refs[j][0, hh, 0], p)
            acc = pv if acc is None else acc + pv
        outs.append(_finish_head(acc))
    y = jnp.concatenate(outs, axis=0)
    o_ref[0] = _rms_rows(y, beta_ref[...], GROUP_W).astype(BF16)


def _na(qt_arr, k, vt, bias, beta_col):
    b, nh, d, s = qt_arr.shape
    qt = NA_Q_ROWS * GRID_W
    kt = NA_K_ROWS * GRID_W
    n_t = s // qt
    n_win = kt // qt

    def win(j):
        return lambda bi, ti: jnp.clip(ti - 1, 0, n_t - n_win) + j

    def kind(bi, ti):
        return (jnp.where(ti == 0, 0, jnp.where(ti == n_t - 1, 2, 1)), 0, 0, 0)

    k_specs = [pl.BlockSpec((1, nh, qt, d), (lambda j: lambda bi, ti: (bi, 0, win(j)(bi, ti), 0))(j))
               for j in range(n_win)]
    per = vt.shape[4] // qt
    v_specs = [pl.BlockSpec((1, nh, 1, V_ROWS, qt),
                            (lambda j: lambda bi, ti: (bi, 0, win(j)(bi, ti) // per, 0, win(j)(bi, ti) % per))(j))
               for j in range(n_win)]
    return pl.pallas_call(
        _na_kernel,
        grid=(b, n_t),
        in_specs=[pl.BlockSpec((1, nh, d, qt), lambda bi, ti: (bi, 0, 0, ti))] + k_specs + v_specs + [
            pl.BlockSpec((1, nh, kt, qt), kind),
            pl.BlockSpec((GROUP_W, 1), lambda bi, ti: (0, 0)),
        ],
        out_specs=pl.BlockSpec((1, GROUP_W, qt), lambda bi, ti: (bi, 0, ti)),
        out_shape=jax.ShapeDtypeStruct((b, GROUP_W, s), BF16),
        compiler_params=pltpu.CompilerParams(
            dimension_semantics=("parallel", "parallel"), vmem_limit_bytes=VMEM_LIMIT),
        name="na_attn",
    )(qt_arr, k, k, k, vt, vt, vt, bias, beta_col)


def _rope_tables(pos, dim):
    inv = jnp.exp(-math.log(ROPE_THETA) * jnp.arange(0, dim, 2, dtype=F32) / dim)
    ang = pos.astype(F32)[:, None] * inv[None, :]
    return jnp.cos(ang).T, jnp.sin(ang).T


def _col(v):
    return v.reshape(-1, 1).astype(F32)


def _lambda_init(layer):
    return 0.8 - 0.6 * math.exp(-0.3 * layer)


def kernel(x, ffn1_norm, ffn1_w_gate, ffn1_w_up, ffn1_w_down, mix_norm, w_in, na_q_norm, na_k_norm, na_rpb, na_beta, mla_q_lat_norm, mla_w_uq, mla_kv_lat_norm, mla_w_ukv, mla_q_norm, mla_k_norm, mla_beta, diff_q_norm, diff_k_norm, diff_lambda, diff_subln, gqa_q_norm, gqa_k_norm, gqa_beta, w_out, ffn2_norm, ffn2_w_gate, ffn2_w_up, ffn2_w_down, final_norm, t5_bias):
    batch, seq, d = x.shape
    grid_rows = seq // GRID_W
    pos = jnp.arange(seq, dtype=jnp.int32)
    tabs = (*_rope_tables(pos, 32), *_rope_tables(pos // GRID_W, 32), *_rope_tables(pos % GRID_W, 32))
    t5_flat = t5_bias.reshape(-1).astype(F32)
    t5_tiles = _t5_bias(t5_flat)

    xf = x.reshape(batch * seq, d)
    for l in range(N_LAYERS):
        xf = _ffn(xf, ffn1_norm[l], ffn1_w_gate[l].astype(BF16), ffn1_w_up[l].astype(BF16),
                  ffn1_w_down[l].astype(BF16))
        cols = tuple(_col(v[l]) for v in (na_q_norm, na_k_norm, mla_q_lat_norm, mla_kv_lat_norm,
                                          mla_q_norm, mla_k_norm, diff_q_norm, diff_k_norm,
                                          gqa_q_norm, gqa_k_norm))
        mats = (mla_w_uq[l].T.astype(BF16), mla_w_ukv[l].T.astype(BF16))
        (qa, ka, va, qb, kb, vb, qc, kc, vc, qd, kd, vd) = _proj(
            xf, batch, seq, mix_norm[l], w_in[l].T.astype(BF16), cols, mats, tabs)
        n_k = seq // FLASH_T
        ka = jnp.swapaxes(ka, 2, 3)
        kb = jnp.swapaxes(kb, 2, 3).reshape(batch, 4, n_k, FLASH_T, 96)
        kc = jnp.swapaxes(kc.reshape(batch, 4, 64, seq), 2, 3).reshape(batch, 4, n_k, FLASH_T, 64)
        kd = jnp.swapaxes(kd, 2, 3).reshape(batch, 2, n_k, FLASH_T, 64)

        ya = _na(qa, ka, va, _na_bias(na_rpb[l].reshape(-1).astype(F32), grid_rows), _col(na_beta[l]))
        yb = _flash_plain(qb, kb, vb, _col(mla_beta[l]), (0, 1, 2, 3))
        yc = _flash_diff(t5_flat, qc, kc, vc, t5_tiles, diff_lambda[l].astype(F32), _col(diff_subln[l]),
                         _lambda_init(l))
        yd = _flash_plain(qd, kd, vd, _col(gqa_beta[l]), (0, 0, 1, 1))

        xf = _ffn(xf, ffn2_norm[l], ffn2_w_gate[l].astype(BF16), ffn2_w_up[l].astype(BF16),
                  ffn2_w_down[l].astype(BF16), mix=((ya, yb, yc, yd), w_out[l].astype(BF16)),
                  final_g=final_norm[l], seq=seq)
    return xf.reshape(batch, seq, d)
```

```python
import functools
import math

import jax
import jax.numpy as jnp
from jax import lax
from jax.experimental import pallas as pl
from jax.experimental.pallas import tpu as pltpu

F32 = jnp.float32
BF16 = jnp.bfloat16

EPS = 1e-6
NEG_BIG = -1e30
LOG2E = 1.4426950408889634
ROPE_THETA = 10000.0
GRID_W = 64
N_LAYERS = 2

HEAD_V = 64
V_ROWS = 80
GROUP_W = 256
A_ROWS, B_ROWS, C_ROWS, D_ROWS = (0, 768), (768, 1184), (1184, 1952), (1952, 2464)

NA_KH, NA_KW = 8, 16
NA_Q_ROWS = 4
NA_K_ROWS = 12
T5_BUCKETS = 32
T5_MAX_DIST = 128

FFN_TM = 1024
FFN_CHUNK = 1024
FLASH_T = 512
FLASH_AHEAD = 2
FLASH_SLOTS = 4
FLASH_UNROLL = 8
FLASH_DIFF_UNROLL = 4
PROJ_TM = FLASH_T

VMEM_LIMIT = 56 * 1024 * 1024


def _rms_rows(x, g_col, n):
    r = lax.rsqrt(jnp.sum(x * x, axis=0, keepdims=True) * (1.0 / n) + EPS)
    return (x * r) * g_col


def _rms_lanes(x, g_row):
    r = lax.rsqrt(jnp.mean(x * x, axis=-1, keepdims=True) + EPS)
    return (x * r) * g_row


def _dot(a, b):
    return jnp.dot(a, b, preferred_element_type=F32)


def _ffn_kernel(*refs, has_mix, has_final, chunks):
    x_ref, g_ref, wg_ref, wu_ref, wd_ref = refs[:5]
    rest = refs[5:]
    if has_mix:
        y_refs, wo_ref, rest = rest[:4], rest[4], rest[5:]
    if has_final:
        gf_ref, rest = rest[0], rest[1:]
    (o_ref,) = rest

    x = x_ref[...]
    if has_mix:
        for g in range(4):
            x = x + lax.dot_general(
                y_refs[g][0], wo_ref[g * GROUP_W:(g + 1) * GROUP_W, :],
                (((0,), (0,)), ((), ())), preferred_element_type=F32)
    o_ref[...] = x
    h = _rms_lanes(x, g_ref[...]).astype(BF16)
    acc = None
    for lo, hi in chunks:
        gate = _dot(h, wg_ref[:, lo:hi])
        up = _dot(h, wu_ref[:, lo:hi])
        a = (gate / (1.0 + jnp.exp(-gate))) * up
        part = _dot(a.astype(BF16), wd_ref[lo:hi, :])
        acc = part if acc is None else acc + part
    out = o_ref[...] + 0.5 * acc
    if has_final:
        out = _rms_lanes(out, gf_ref[...])
    o_ref[...] = out


def _ffn(x, g, wg, wu, wd, mix=None, final_g=None, seq=None):
    n, d = x.shape
    d_ff = wg.shape[1]
    tm = FFN_TM
    chunks = tuple((lo, min(lo + FFN_CHUNK, d_ff)) for lo in range(0, d_ff, FFN_CHUNK))
    once = pl.Buffered(1)
    in_specs = [
        pl.BlockSpec((tm, d), lambda i: (i, 0)),
        pl.BlockSpec((1, d), lambda i: (0, 0)),
        pl.BlockSpec((d, d_ff), lambda i: (0, 0), pipeline_mode=once),
        pl.BlockSpec((d, d_ff), lambda i: (0, 0), pipeline_mode=once),
        pl.BlockSpec((d_ff, d), lambda i: (0, 0), pipeline_mode=once),
    ]
    args = [x, g.reshape(1, d), wg, wu, wd]
    if mix is not None:
        ys, wo = mix
        nsb = seq // tm
        for y in ys:
            in_specs.append(pl.BlockSpec((1, GROUP_W, tm), lambda i: (i // nsb, 0, i % nsb)))
            args.append(y)
        in_specs.append(pl.BlockSpec(wo.shape, lambda i: (0, 0), pipeline_mode=once))
        args.append(wo)
    if final_g is not None:
        in_specs.append(pl.BlockSpec((1, d), lambda i: (0, 0)))
        args.append(final_g.reshape(1, d))
    return pl.pallas_call(
        functools.partial(_ffn_kernel, has_mix=mix is not None, has_final=final_g is not None, chunks=chunks),
        grid=(n // tm,),
        in_specs=in_specs,
        out_specs=pl.BlockSpec((tm, d), lambda i: (i, 0)),
        out_shape=jax.ShapeDtypeStruct((n, d), F32),
        compiler_params=pltpu.CompilerParams(
            dimension_semantics=("parallel",), vmem_limit_bytes=VMEM_LIMIT),
        name="ffn_mix" if mix is not None else "ffn",
    )(*args)


def _rope_rows(x, cos, sin):
    x1, x2 = x[:16], x[16:]
    return x1 * cos - x2 * sin, x1 * sin + x2 * cos


def _store_v(v_ref, hh, v):
    v_ref[0, hh, 0, 0:HEAD_V, :] = v.astype(BF16)
    v_ref[0, hh, 0, HEAD_V:V_ROWS, :] = jnp.ones((V_ROWS - HEAD_V, v.shape[1]), BF16)


def _proj_kernel(x_ref, gmix_ref, wint_ref,
                 naq_ref, nak_ref,
                 qlat_ref, wuqt_ref, kvlat_ref, wukvt_ref, mq_ref, mk_ref,
                 dq_ref, dk_ref, gq_ref, gk_ref,
                 cseq_ref, sseq_ref, crow_ref, srow_ref, ccol_ref, scol_ref,
                 qa_o, ka_o, va_o, qb_o, kb_o, vb_o, qc_o, kc_o, vc_o, qd_o, kd_o, vd_o):
    h = _rms_lanes(x_ref[...], gmix_ref[...]).astype(BF16)

    def proj(lo, hi):
        return lax.dot_general(wint_ref[lo:hi, :], h, (((1,), (1,)), ((), ())),
                               preferred_element_type=F32)

    pa = proj(*A_ROWS)
    sa = LOG2E * 64 ** -0.5
    for hh in range(4):
        q = _rms_rows(pa[hh * 64:(hh + 1) * 64], naq_ref[...], 64) * sa
        k = _rms_rows(pa[256 + hh * 64:256 + (hh + 1) * 64], nak_ref[...], 64)
        qa_o[0, hh] = q.astype(BF16)
        ka_o[0, hh] = k.astype(BF16)
        _store_v(va_o, hh, pa[512 + hh * 64:512 + (hh + 1) * 64])

    pb = proj(*B_ROWS)
    cq = _rms_rows(pb[0:256], qlat_ref[...], 256).astype(BF16)
    qb = _dot(wuqt_ref[...], cq)
    ckv = _rms_rows(pb[256:384], kvlat_ref[...], 128).astype(BF16)
    kv = _dot(wukvt_ref[...], ckv)
    kr = pb[384:416]
    cseq, sseq = cseq_ref[...], sseq_ref[...]
    sb = LOG2E * 96 ** -0.5
    for hh in range(4):
        q = _rms_rows(qb[hh * 96:(hh + 1) * 96], mq_ref[...], 96) * sb
        r1, r2 = _rope_rows(q[64:96], cseq, sseq)
        qb_o[0, hh, 0:64, :] = q[0:64].astype(BF16)
        qb_o[0, hh, 64:80, :] = r1.astype(BF16)
        qb_o[0, hh, 80:96, :] = r2.astype(BF16)
        k = jnp.concatenate([kv[hh * 128:hh * 128 + 64], kr], axis=0)
        k = _rms_rows(k, mk_ref[...], 96)
        r1, r2 = _rope_rows(k[64:96], cseq, sseq)
        kb_o[0, hh, 0:64, :] = k[0:64].astype(BF16)
        kb_o[0, hh, 64:80, :] = r1.astype(BF16)
        kb_o[0, hh, 80:96, :] = r2.astype(BF16)
        _store_v(vb_o, hh, kv[hh * 128 + 64:hh * 128 + 128])

    pc = proj(*C_ROWS)
    sc = LOG2E * 32 ** -0.5
    for j in range(8):
        q = _rms_rows(pc[j * 32:(j + 1) * 32], dq_ref[...], 32) * sc
        k = _rms_rows(pc[256 + j * 32:256 + (j + 1) * 32], dk_ref[...], 32)
        qc_o[0, j] = q.astype(BF16)
        kc_o[0, j] = k.astype(BF16)
    for hh in range(4):
        _store_v(vc_o, hh, pc[512 + hh * 64:512 + (hh + 1) * 64])

    pd = proj(*D_ROWS)
    crow, srow, ccol, scol = crow_ref[...], srow_ref[...], ccol_ref[...], scol_ref[...]
    sd = LOG2E * 64 ** -0.5

    def axial(x, o_ref, hh):
        a1, a2 = _rope_rows(x[0:32], crow, srow)
        b1, b2 = _rope_rows(x[32:64], ccol, scol)
        o_ref[0, hh, 0:16, :] = a1.astype(BF16)
        o_ref[0, hh, 16:32, :] = a2.astype(BF16)
        o_ref[0, hh, 32:48, :] = b1.astype(BF16)
        o_ref[0, hh, 48:64, :] = b2.astype(BF16)

    for hh in range(4):
        axial(_rms_rows(pd[hh * 64:(hh + 1) * 64], gq_ref[...], 64) * sd, qd_o, hh)
    for hh in range(2):
        axial(_rms_rows(pd[256 + hh * 64:256 + (hh + 1) * 64], gk_ref[...], 64), kd_o, hh)
        _store_v(vd_o, hh, pd[384 + hh * 64:384 + (hh + 1) * 64])


def _proj(x, batch, seq, gmix, wint, cols, mats, tabs):
    n, d = x.shape
    tm = PROJ_TM
    nsb = seq // tm
    naq, nak, qlat, kvlat, mq, mk, dq, dk, gq, gk = cols
    wuqt, wukvt = mats

    def full(a):
        return pl.BlockSpec(a.shape, lambda i: (0,) * a.ndim)

    tab_spec = pl.BlockSpec((16, tm), lambda i: (0, i % nsb))
    args = [x, gmix.reshape(1, d), wint, naq, nak, qlat, wuqt, kvlat, wukvt, mq, mk, dq, dk, gq, gk, *tabs]
    in_specs = [pl.BlockSpec((tm, d), lambda i: (i, 0))] + [full(a) for a in args[1:15]] + [tab_spec] * 6

    def head_out(nh, rows):
        return (jax.ShapeDtypeStruct((batch, nh, rows, seq), BF16),
                pl.BlockSpec((1, nh, rows, tm), lambda i: (i // nsb, 0, 0, i % nsb)))

    def v_out(nh):
        return (jax.ShapeDtypeStruct((batch, nh, nsb, V_ROWS, tm), BF16),
                pl.BlockSpec((1, nh, 1, V_ROWS, tm), lambda i: (i // nsb, 0, i % nsb, 0, 0)))

    outs = [head_out(4, 64), head_out(4, 64), v_out(4),
            head_out(4, 96), head_out(4, 96), v_out(4),
            head_out(8, 32), head_out(8, 32), v_out(4),
            head_out(4, 64), head_out(2, 64), v_out(2)]
    return pl.pallas_call(
        _proj_kernel,
        grid=(n // tm,),
        in_specs=in_specs,
        out_specs=[o[1] for o in outs],
        out_shape=[o[0] for o in outs],
        compiler_params=pltpu.CompilerParams(
            dimension_semantics=("parallel",), vmem_limit_bytes=VMEM_LIMIT),
        name="mix_proj",
    )(*args)


def _col_max(s):
    parts = [s]
    while parts[0].shape[0] > 64:
        half = parts[0].shape[0] // 2
        parts = [jnp.maximum(p[:half], p[half:]) for p in parts]
    return jnp.max(parts[0], axis=0, keepdims=True)


def _produce(k_blk, q_blk, slot, s_scr, mc_scr, bias=None):
    s = _dot(k_blk, q_blk)
    if bias is not None:
        s = s + bias
    s_scr[slot] = s
    mc_scr[slot] = _col_max(s)


def _consume(s, m_cur, c, idx, v_blk, m_scr, acc_scr):
    m_prev = m_scr[idx]
    m_new = jnp.maximum(m_prev, m_cur + c)
    alpha = jnp.exp2(m_prev - m_new)
    p = jnp.exp2(s - (m_new - c)).astype(BF16)
    acc_scr[idx] = alpha * acc_scr[idx] + _dot(v_blk, p)
    m_scr[idx] = m_new


def _finish_head(acc):
    return acc[0:HEAD_V] / acc[HEAD_V:HEAD_V + 1]


def _flash_plain_kernel(q_ref, k_ref, v_ref, beta_ref, o_ref, s_scr, mc_scr, m_scr, acc_scr, *, kv_map, n_k):
    n_h = len(kv_map)
    m_scr[...] = jnp.full_like(m_scr, NEG_BIG)
    acc_scr[...] = jnp.zeros_like(acc_scr)

    def produce(i, hh):
        _produce(k_ref[0, kv_map[hh], i], q_ref[0, hh], hh % FLASH_SLOTS, s_scr, mc_scr)

    for hh in range(FLASH_AHEAD):
        produce(0, hh)

    def body(i, carry):
        nxt = jnp.minimum(i + 1, n_k - 1)
        for hh in range(n_h):
            ahead = hh + FLASH_AHEAD
            produce(i if ahead < n_h else nxt, ahead % n_h)
            slot = hh % FLASH_SLOTS
            _consume(s_scr[slot], mc_scr[slot], 0.0, hh, v_ref[0, kv_map[hh], i], m_scr, acc_scr)
        return carry

    lax.fori_loop(0, n_k, body, 0, unroll=FLASH_UNROLL)
    y = jnp.concatenate([_finish_head(acc_scr[hh]) for hh in range(n_h)], axis=0)
    o_ref[0] = _rms_rows(y, beta_ref[...], GROUP_W).astype(BF16)


def _flash_plain(qt, k, vt, beta_col, kv_map):
    b, hq, dq, s = qt.shape
    hk, n_k, t = k.shape[1], k.shape[2], k.shape[3]
    assert len(kv_map) % FLASH_SLOTS == 0
    return pl.pallas_call(
        functools.partial(_flash_plain_kernel, kv_map=kv_map, n_k=n_k),
        grid=(b, s // t),
        in_specs=[
            pl.BlockSpec((1, hq, dq, t), lambda bi, qi: (bi, 0, 0, qi)),
            pl.BlockSpec((1, hk, n_k, t, dq), lambda bi, qi: (bi, 0, 0, 0, 0)),
            pl.BlockSpec((1, hk, n_k, V_ROWS, t), lambda bi, qi: (bi, 0, 0, 0, 0)),
            pl.BlockSpec((GROUP_W, 1), lambda bi, qi: (0, 0)),
        ],
        out_specs=pl.BlockSpec((1, GROUP_W, t), lambda bi, qi: (bi, 0, qi)),
        out_shape=jax.ShapeDtypeStruct((b, GROUP_W, s), BF16),
        scratch_shapes=[pltpu.VMEM((FLASH_SLOTS, t, t), F32), pltpu.VMEM((FLASH_SLOTS, 1, t), F32),
                        pltpu.VMEM((hq, 1, t), F32), pltpu.VMEM((hq, V_ROWS, t), F32)],
        compiler_params=pltpu.CompilerParams(
            dimension_semantics=("parallel", "parallel"), vmem_limit_bytes=VMEM_LIMIT),
        name="flash_plain",
    )(qt, k, vt, beta_col)


def _flash_diff_kernel(t5_ref, q_ref, k_ref, v_ref, bias_ref, lam_ref, subln_ref, o_ref,
                       qpad_scr, s_scr, mc_scr, m_scr, acc_scr, *, lambda_init, n_k):
    qi = pl.program_id(1)
    n_m = 8
    dq = q_ref.shape[2]
    m_scr[...] = jnp.full_like(m_scr, NEG_BIG)
    acc_scr[...] = jnp.zeros_like(acc_scr)
    qpad_scr[...] = jnp.zeros_like(qpad_scr)
    for j in range(n_m):
        qpad_scr[j, (j % 2) * dq:(j % 2 + 1) * dq, :] = q_ref[0, j]

    def produce(i, j, bias=None):
        _produce(k_ref[0, j // 2, i], qpad_scr[j], j % FLASH_SLOTS, s_scr, mc_scr, bias)

    tq = q_ref.shape[3]
    far_left, far_right = T5_BUCKETS // 2 - 1, T5_BUCKETS - 1
    has_left, has_right = qi >= 1, qi <= n_k - 2
    left = jnp.where(has_left, qi - 1, 2)
    right = jnp.where(has_right, qi + 1, n_k - 3)
    w0 = jnp.clip(qi - 1, 0, n_k - 3)
    n_rest = n_k - 3

    def rest(t):
        return jnp.where(t < w0, t, t + 3)

    patch_max = 2.0 * LOG2E * lax.fori_loop(
        0, T5_BUCKETS * 4, lambda e, m: jnp.maximum(m, jnp.abs(t5_ref[e])), 0.0)

    for j in range(FLASH_AHEAD):
        produce(qi, j)

    def visit(i, nxt, mode, flag=None):
        for j in range(n_m):
            hh = j // 2
            ahead = j + FLASH_AHEAD
            if ahead >= n_m:
                produce(nxt, ahead % n_m)
            elif mode == "diag":
                produce(i, ahead, bias_ref[1, ahead // 2])
            else:
                produce(i, ahead)
            slot = j % FLASH_SLOTS
            if mode == "diag" and j < FLASH_AHEAD:
                s = s_scr[slot] + bias_ref[1, hh]
                _consume(s, _col_max(s), 0.0, j, v_ref[0, hh, i], m_scr, acc_scr)
                continue
            if mode == "diag":
                _consume(s_scr[slot], mc_scr[slot], 0.0, j, v_ref[0, hh, i], m_scr, acc_scr)
                continue
            c = jnp.where(i < qi, t5_ref[far_left * 4 + hh], t5_ref[far_right * 4 + hh]) * LOG2E
            m_cur = mc_scr[slot]
            if mode in ("left", "right"):
                side, bucket = (0, far_left) if mode == "left" else (2, far_right)
                rows = slice(tq - T5_MAX_DIST, tq) if mode == "left" else slice(0, T5_MAX_DIST)
                cols = slice(0, T5_MAX_DIST) if mode == "left" else slice(tq - T5_MAX_DIST, tq)
                corner = (bias_ref[side, hh, rows, cols] - t5_ref[bucket * 4 + hh] * LOG2E) * flag
                s_scr[slot, rows, cols] = s_scr[slot, rows, cols] + corner
                m_cur = m_cur + patch_max * flag
            _consume(s_scr[slot], m_cur, c, j, v_ref[0, hh, i], m_scr, acc_scr)

    visit(qi, left, "diag")
    visit(left, right, "left", has_left.astype(F32))
    visit(right, rest(0), "right", has_right.astype(F32))

    def far_group(g, carry):
        for u in range(FLASH_DIFF_UNROLL):
            t = g * FLASH_DIFF_UNROLL + u
            visit(rest(t), rest(jnp.minimum(t + 1, n_rest - 1)), "far")
        return carry

    def far_body(t, carry):
        visit(rest(t), rest(jnp.minimum(t + 1, n_rest - 1)), "far")
        return carry

    n_groups = n_rest // FLASH_DIFF_UNROLL
    lax.fori_loop(0, n_groups, far_group, 0)
    lax.fori_loop(n_groups * FLASH_DIFF_UNROLL, n_rest, far_body, 0)

    lp = lam_ref[...]
    lam = (jnp.exp(jnp.sum(lp[0:1] * lp[1:2], axis=1, keepdims=True))
           - jnp.exp(jnp.sum(lp[2:3] * lp[3:4], axis=1, keepdims=True)) + lambda_init)
    for hh in range(4):
        y = _finish_head(acc_scr[2 * hh]) - lam * _finish_head(acc_scr[2 * hh + 1])
        y = _rms_rows(y, subln_ref[...], HEAD_V) * (1.0 - lambda_init)
        o_ref[0, hh * HEAD_V:(hh + 1) * HEAD_V, :] = y.astype(BF16)


def _flash_diff(t5_flat, qt, k, vt, bias, lam_params, subln_col, lambda_init):
    b, hq, dq, s = qt.shape
    n_k, t = k.shape[2], k.shape[3]
    return pl.pallas_call(
        functools.partial(_flash_diff_kernel, lambda_init=lambda_init, n_k=n_k),
        grid=(b, s // t),
        in_specs=[
            pl.BlockSpec(memory_space=pltpu.SMEM),
            pl.BlockSpec((1, hq, dq, t), lambda bi, qi: (bi, 0, 0, qi)),
            pl.BlockSpec((1, 4, n_k, t, 2 * dq), lambda bi, qi: (bi, 0, 0, 0, 0)),
            pl.BlockSpec((1, 4, n_k, V_ROWS, t), lambda bi, qi: (bi, 0, 0, 0, 0)),
            pl.BlockSpec(bias.shape, lambda bi, qi: (0, 0, 0, 0), pipeline_mode=pl.Buffered(1)),
            pl.BlockSpec(lam_params.shape, lambda bi, qi: (0, 0)),
            pl.BlockSpec((HEAD_V, 1), lambda bi, qi: (0, 0)),
        ],
        out_specs=pl.BlockSpec((1, GROUP_W, t), lambda bi, qi: (bi, 0, qi)),
        out_shape=jax.ShapeDtypeStruct((b, GROUP_W, s), BF16),
        scratch_shapes=[pltpu.VMEM((hq, 2 * dq, t), BF16),
                        pltpu.VMEM((FLASH_SLOTS, t, t), F32), pltpu.VMEM((FLASH_SLOTS, 1, t), F32),
                        pltpu.VMEM((hq, 1, t), F32), pltpu.VMEM((hq, V_ROWS, t), F32)],
        compiler_params=pltpu.CompilerParams(
            dimension_semantics=("parallel", "parallel"), vmem_limit_bytes=VMEM_LIMIT),
        name="flash_diff",
    )(t5_flat, qt, k, vt, bias, lam_params, subln_col)


def _t5_bias_kernel(tab_ref, o_ref, *, t):
    d = pl.program_id(0) - 1
    rows = 16
    half = T5_BUCKETS // 2
    max_exact = half // 2

    def body(r, carry):
        kk = lax.broadcasted_iota(jnp.int32, (rows, t), 0) + r * rows
        qq = lax.broadcasted_iota(jnp.int32, (rows, t), 1)
        rel = d * t + kk - qq
        n = jnp.abs(rel)
        large = max_exact + (jnp.log(jnp.maximum(n, 1).astype(F32) / max_exact)
                             / math.log(T5_MAX_DIST / max_exact) * (half - max_exact)).astype(jnp.int32)
        large = jnp.minimum(large, half - 1)
        bucket = jnp.where(rel > 0, half, 0) + jnp.where(n < max_exact, n, large)
        accs = [jnp.zeros((rows, t), F32) for _ in range(4)]
        for bkt in range(T5_BUCKETS):
            hit = bucket == bkt
            accs = [jnp.where(hit, tab_ref[bkt * 4 + hh], accs[hh]) for hh in range(4)]
        for hh in range(4):
            o_ref[0, hh, pl.ds(pl.multiple_of(r * rows, rows), rows), :] = accs[hh] * LOG2E
        return carry

    lax.fori_loop(0, t // rows, body, 0)


def _t5_bias(t5_flat):
    t = FLASH_T
    return pl.pallas_call(
        functools.partial(_t5_bias_kernel, t=t),
        grid=(3,),
        in_specs=[pl.BlockSpec(memory_space=pltpu.SMEM)],
        out_specs=pl.BlockSpec((1, 4, t, t), lambda d: (d, 0, 0, 0)),
        out_shape=jax.ShapeDtypeStruct((3, 4, t, t), F32),
        name="t5_bias",
    )(t5_flat)


def _na_window(kind, j, i, grid_rows):
    r0 = (0, NA_Q_ROWS, grid_rows - NA_Q_ROWS)[kind]
    start = (0, 0, grid_rows - NA_K_ROWS)[kind]
    krow, qrow = start + j, r0 + i
    lo = min(max(qrow - NA_KH // 2, 0), grid_rows - NA_KH)
    return lo <= krow < lo + NA_KH, krow - qrow + NA_KH - 1


def _na_bias_kernel(rpb_ref, o_ref, t_scr, *, grid_rows):
    hh = pl.program_id(0)
    n_dr, n_dc = 2 * NA_KH - 1, 2 * NA_KW - 1
    shape = (GRID_W, NA_Q_ROWS * GRID_W)
    lane = lax.broadcasted_iota(jnp.int32, shape, 1)
    kc = lax.broadcasted_iota(jnp.int32, shape, 0)
    qc = lane % GRID_W
    grp = lane // GRID_W
    dcm = jnp.clip(kc - qc + NA_KW - 1, 0, n_dc - 1)
    qs = jnp.clip(qc - NA_KW // 2, 0, GRID_W - NA_KW)
    col_ok = (kc >= qs) & (kc < qs + NA_KW)

    for dr in range(n_dr):
        base = (hh * n_dr + dr) * n_dc
        t_scr[dr] = lax.fori_loop(
            0, n_dc, lambda dc, acc: jnp.where(dcm == dc, rpb_ref[base + dc], acc), jnp.zeros(shape, F32)) * LOG2E

    for kind in range(3):
        for j in range(NA_K_ROWS):
            wins = [_na_window(kind, j, i, grid_rows) for i in range(NA_Q_ROWS)]
            blk = jnp.full(shape, NEG_BIG, F32)
            for i, (inside, dr) in enumerate(wins):
                if inside:
                    blk = jnp.where((grp == i) & col_ok, t_scr[dr], blk)
            o_ref[kind, 0, j * GRID_W:(j + 1) * GRID_W, :] = blk


def _na_bias(rpb_flat, grid_rows):
    kt, qt = NA_K_ROWS * GRID_W, NA_Q_ROWS * GRID_W
    return pl.pallas_call(
        functools.partial(_na_bias_kernel, grid_rows=grid_rows),
        grid=(4,),
        in_specs=[pl.BlockSpec(memory_space=pltpu.SMEM)],
        out_specs=pl.BlockSpec((3, 1, kt, qt), lambda h: (0, h, 0, 0)),
        out_shape=jax.ShapeDtypeStruct((3, 4, kt, qt), F32),
        scratch_shapes=[pltpu.VMEM((2 * NA_KH - 1, GRID_W, qt), F32)],
        name="na_bias",
    )(rpb_flat)


def _na_kernel(q_ref, k0_ref, k1_ref, k2_ref, v0_ref, v1_ref, v2_ref, bias_ref, beta_ref, o_ref):
    qt = NA_Q_ROWS * GRID_W
    k_refs = (k0_ref, k1_ref, k2_ref)
    v_refs = (v0_ref, v1_ref, v2_ref)

    def logits(hh):
        q = q_ref[0, hh]
        return [_dot(k_refs[j][0, hh], q) + bias_ref[0, hh, j * qt:(j + 1) * qt, :] for j in range(3)]

    outs = []
    ss_next = logits(0)
    for hh in range(4):
        ss = ss_next
        if hh + 1 < 4:
            ss_next = logits(hh + 1)
        m = functools.reduce(jnp.maximum, [_col_max(s) for s in ss])
        acc = None
        for j in range(3):
            p = jnp.exp2(ss[j] - m).astype(BF16)
            pv = _dot(v_refs[j][0, hh, 0], p)
            acc = pv if acc is None else acc + pv
        outs.append(_finish_head(acc))
    y = jnp.concatenate(outs, axis=0)
    o_ref[0] = _rms_rows(y, beta_ref[...], GROUP_W).astype(BF16)


def _na(qt_arr, k, vt, bias, beta_col):
    b, nh, d, s = qt_arr.shape
    qt = NA_Q_ROWS * GRID_W
    kt = NA_K_ROWS * GRID_W
    n_t = s // qt
    n_win = kt // qt

    def win(j):
        return lambda bi, ti: jnp.clip(ti - 1, 0, n_t - n_win) + j

    def kind(bi, ti):
        return (jnp.where(ti == 0, 0, jnp.where(ti == n_t - 1, 2, 1)), 0, 0, 0)

    k_specs = [pl.BlockSpec((1, nh, qt, d), (lambda j: lambda bi, ti: (bi, 0, win(j)(bi, ti), 0))(j))
               for j in range(n_win)]
    per = vt.shape[4] // qt
    v_specs = [pl.BlockSpec((1, nh, 1, V_ROWS, qt),
                            (lambda j: lambda bi, ti: (bi, 0, win(j)(bi, ti) // per, 0, win(j)(bi, ti) % per))(j))
               for j in range(n_win)]
    return pl.pallas_call(
        _na_kernel,
        grid=(b, n_t),
        in_specs=[pl.BlockSpec((1, nh, d, qt), lambda bi, ti: (bi, 0, 0, ti))] + k_specs + v_specs + [
            pl.BlockSpec((1, nh, kt, qt), kind),
            pl.BlockSpec((GROUP_W, 1), lambda bi, ti: (0, 0)),
        ],
        out_specs=pl.BlockSpec((1, GROUP_W, qt), lambda bi, ti: (bi, 0, ti)),
        out_shape=jax.ShapeDtypeStruct((b, GROUP_W, s), BF16),
        compiler_params=pltpu.CompilerParams(
            dimension_semantics=("parallel", "parallel"), vmem_limit_bytes=VMEM_LIMIT),
        name="na_attn",
    )(qt_arr, k, k, k, vt, vt, vt, bias, beta_col)


def _rope_tables(pos, dim):
    inv = jnp.exp(-math.log(ROPE_THETA) * jnp.arange(0, dim, 2, dtype=F32) / dim)
    ang = pos.astype(F32)[:, None] * inv[None, :]
    return jnp.cos(ang).T, jnp.sin(ang).T


def _col(v):
    return v.reshape(-1, 1).astype(F32)


def _lambda_init(layer):
    return 0.8 - 0.6 * math.exp(-0.3 * layer)


def kernel(x, ffn1_norm, ffn1_w_gate, ffn1_w_up, ffn1_w_down, mix_norm, w_in, na_q_norm, na_k_norm, na_rpb, na_beta, mla_q_lat_norm, mla_w_uq, mla_kv_lat_norm, mla_w_ukv, mla_q_norm, mla_k_norm, mla_beta, diff_q_norm, diff_k_norm, diff_lambda, diff_subln, gqa_q_norm, gqa_k_norm, gqa_beta, w_out, ffn2_norm, ffn2_w_gate, ffn2_w_up, ffn2_w_down, final_norm, t5_bias):
    batch, seq, d = x.shape
    grid_rows = seq // GRID_W
    pos = jnp.arange(seq, dtype=jnp.int32)
    tabs = (*_rope_tables(pos, 32), *_rope_tables(pos // GRID_W, 32), *_rope_tables(pos % GRID_W, 32))
    t5_flat = t5_bias.reshape(-1).astype(F32)
    t5_tiles = _t5_bias(t5_flat)

    xf = x.reshape(batch * seq, d)
    for l in range(N_LAYERS):
        xf = _ffn(xf, ffn1_norm[l], ffn1_w_gate[l].astype(BF16), ffn1_w_up[l].astype(BF16),
                  ffn1_w_down[l].astype(BF16))
        cols = tuple(_col(v[l]) for v in (na_q_norm, na_k_norm, mla_q_lat_norm, mla_kv_lat_norm,
                                          mla_q_norm, mla_k_norm, diff_q_norm, diff_k_norm,
                                          gqa_q_norm, gqa_k_norm))
        mats = (mla_w_uq[l].T.astype(BF16), mla_w_ukv[l].T.astype(BF16))
        (qa, ka, va, qb, kb, vb, qc, kc, vc, qd, kd, vd) = _proj(
            xf, batch, seq, mix_norm[l], w_in[l].T.astype(BF16), cols, mats, tabs)
        n_k = seq // FLASH_T
        ka = jnp.swapaxes(ka, 2, 3)
        kb = jnp.swapaxes(kb, 2, 3).reshape(batch, 4, n_k, FLASH_T, 96)
        kc = jnp.swapaxes(kc.reshape(batch, 4, 64, seq), 2, 3).reshape(batch, 4, n_k, FLASH_T, 64)
        kd = jnp.swapaxes(kd, 2, 3).reshape(batch, 2, n_k, FLASH_T, 64)

        ya = _na(qa, ka, va, _na_bias(na_rpb[l].reshape(-1).astype(F32), grid_rows), _col(na_beta[l]))
        yb = _flash_plain(qb, kb, vb, _col(mla_beta[l]), (0, 1, 2, 3))
        yc = _flash_diff(t5_flat, qc, kc, vc, t5_tiles, diff_lambda[l].astype(F32), _col(diff_subln[l]),
                         _lambda_init(l))
        yd = _flash_plain(qd, kd, vd, _col(gqa_beta[l]), (0, 0, 1, 1))

        xf = _ffn(xf, ffn2_norm[l], ffn2_w_gate[l].astype(BF16), ffn2_w_up[l].astype(BF16),
                  ffn2_w_down[l].astype(BF16), mix=((ya, yb, yc, yd), w_out[l].astype(BF16)),
                  final_g=final_norm[l], seq=seq)
    return xf.reshape(batch, seq, d)
```

```python
import functools
import math

import jax
import jax.numpy as jnp
from jax import lax
from jax.experimental import pallas as pl
from jax.experimental.pallas import tpu as pltpu

F32 = jnp.float32
BF16 = jnp.bfloat16

EPS = 1e-6
NEG_BIG = -1e30
LOG2E = 1.4426950408889634
ROPE_THETA = 10000.0
GRID_W = 64
N_LAYERS = 2

HEAD_V = 64
V_ROWS = 80
GROUP_W = 256
A_ROWS, B_ROWS, C_ROWS, D_ROWS = (0, 768), (768, 1184), (1184, 1952), (1952, 2464)

NA_KH, NA_KW = 8, 16
NA_Q_ROWS = 4
NA_K_ROWS = 12
T5_BUCKETS = 32
T5_MAX_DIST = 128

FFN_TM = 1024
FFN_CHUNK = 1024
FLASH_T = 512
FLASH_AHEAD = 2
FLASH_SLOTS = 4
FLASH_UNROLL = 8
FLASH_DIFF_UNROLL = 4
PROJ_TM = FLASH_T

VMEM_LIMIT = 56 * 1024 * 1024


def _rms_rows(x, g_col, n):
    r = lax.rsqrt(jnp.sum(x * x, axis=0, keepdims=True) * (1.0 / n) + EPS)
    return (x * r) * g_col


def _rms_lanes(x, g_row):
    r = lax.rsqrt(jnp.mean(x * x, axis=-1, keepdims=True) + EPS)
    return (x * r) * g_row


def _dot(a, b):
    return jnp.dot(a, b, preferred_element_type=F32)


def _ffn_kernel(*refs, has_mix, has_final, chunks):
    x_ref, g_ref, wg_ref, wu_ref, wd_ref = refs[:5]
    rest = refs[5:]
    if has_mix:
        y_refs, wo_ref, rest = rest[:4], rest[4], rest[5:]
    if has_final:
        gf_ref, rest = rest[0], rest[1:]
    (o_ref,) = rest

    x = x_ref[...]
    if has_mix:
        for g in range(4):
            x = x + lax.dot_general(
                y_refs[g][0], wo_ref[g * GROUP_W:(g + 1) * GROUP_W, :],
                (((0,), (0,)), ((), ())), preferred_element_type=F32)
    o_ref[...] = x
    h = _rms_lanes(x, g_ref[...]).astype(BF16)
    acc = None
    for lo, hi in chunks:
        gate = _dot(h, wg_ref[:, lo:hi])
        up = _dot(h, wu_ref[:, lo:hi])
        a = (gate / (1.0 + jnp.exp(-gate))) * up
        part = _dot(a.astype(BF16), wd_ref[lo:hi, :])
        acc = part if acc is None else acc + part
    out = o_ref[...] + 0.5 * acc
    if has_final:
        out = _rms_lanes(out, gf_ref[...])
    o_ref[...] = out


def _ffn(x, g, wg, wu, wd, mix=None, final_g=None, seq=None):
    n, d = x.shape
    d_ff = wg.shape[1]
    tm = FFN_TM
    chunks = tuple((lo, min(lo + FFN_CHUNK, d_ff)) for lo in range(0, d_ff, FFN_CHUNK))
    once = pl.Buffered(1)
    in_specs = [
        pl.BlockSpec((tm, d), lambda i: (i, 0)),
        pl.BlockSpec((1, d), lambda i: (0, 0)),
        pl.BlockSpec((d, d_ff), lambda i: (0, 0), pipeline_mode=once),
        pl.BlockSpec((d, d_ff), lambda i: (0, 0), pipeline_mode=once),
        pl.BlockSpec((d_ff, d), lambda i: (0, 0), pipeline_mode=once),
    ]
    args = [x, g.reshape(1, d), wg, wu, wd]
    if mix is not None:
        ys, wo = mix
        nsb = seq // tm
        for y in ys:
            in_specs.append(pl.BlockSpec((1, GROUP_W, tm), lambda i: (i // nsb, 0, i % nsb)))
            args.append(y)
        in_specs.append(pl.BlockSpec(wo.shape, lambda i: (0, 0), pipeline_mode=once))
        args.append(wo)
    if final_g is not None:
        in_specs.append(pl.BlockSpec((1, d), lambda i: (0, 0)))
        args.append(final_g.reshape(1, d))
    return pl.pallas_call(
        functools.partial(_ffn_kernel, has_mix=mix is not None, has_final=final_g is not None, chunks=chunks),
        grid=(n // tm,),
        in_specs=in_specs,
        out_specs=pl.BlockSpec((tm, d), lambda i: (i, 0)),
        out_shape=jax.ShapeDtypeStruct((n, d), F32),
        compiler_params=pltpu.CompilerParams(
            dimension_semantics=("parallel",), vmem_limit_bytes=VMEM_LIMIT),
        name="ffn_mix" if mix is not None else "ffn",
    )(*args)


def _rope_rows(x, cos, sin):
    x1, x2 = x[:16], x[16:]
    return x1 * cos - x2 * sin, x1 * sin + x2 * cos


def _store_v(v_ref, hh, v):
    v_ref[0, hh, 0, 0:HEAD_V, :] = v.astype(BF16)
    v_ref[0, hh, 0, HEAD_V:V_ROWS, :] = jnp.ones((V_ROWS - HEAD_V, v.shape[1]), BF16)


def _proj_kernel(x_ref, gmix_ref, wint_ref,
                 naq_ref, nak_ref,
                 qlat_ref, wuqt_ref, kvlat_ref, wukvt_ref, mq_ref, mk_ref,
                 dq_ref, dk_ref, gq_ref, gk_ref,
                 cseq_ref, sseq_ref, crow_ref, srow_ref, ccol_ref, scol_ref,
                 qa_o, ka_o, va_o, qb_o, kb_o, vb_o, qc_o, kc_o, vc_o, qd_o, kd_o, vd_o):
    h = _rms_lanes(x_ref[...], gmix_ref[...]).astype(BF16)

    def proj(lo, hi):
        return lax.dot_general(wint_ref[lo:hi, :], h, (((1,), (1,)), ((), ())),
                               preferred_element_type=F32)

    pa = proj(*A_ROWS)
    sa = LOG2E * 64 ** -0.5
    for hh in range(4):
        q = _rms_rows(pa[hh * 64:(hh + 1) * 64], naq_ref[...], 64) * sa
        k = _rms_rows(pa[256 + hh * 64:256 + (hh + 1) * 64], nak_ref[...], 64)
        qa_o[0, hh] = q.astype(BF16)
        ka_o[0, hh] = k.astype(BF16)
        _store_v(va_o, hh, pa[512 + hh * 64:512 + (hh + 1) * 64])

    pb = proj(*B_ROWS)
    cq = _rms_rows(pb[0:256], qlat_ref[...], 256).astype(BF16)
    qb = _dot(wuqt_ref[...], cq)
    ckv = _rms_rows(pb[256:384], kvlat_ref[...], 128).astype(BF16)
    kv = _dot(wukvt_ref[...], ckv)
    kr = pb[384:416]
    cseq, sseq = cseq_ref[...], sseq_ref[...]
    sb = LOG2E * 96 ** -0.5
    for hh in range(4):
        q = _rms_rows(qb[hh * 96:(hh + 1) * 96], mq_ref[...], 96) * sb
        r1, r2 = _rope_rows(q[64:96], cseq, sseq)
        qb_o[0, hh, 0:64, :] = q[0:64].astype(BF16)
        qb_o[0, hh, 64:80, :] = r1.astype(BF16)
        qb_o[0, hh, 80:96, :] = r2.astype(BF16)
        k = jnp.concatenate([kv[hh * 128:hh * 128 + 64], kr], axis=0)
        k = _rms_rows(k, mk_ref[...], 96)
        r1, r2 = _rope_rows(k[64:96], cseq, sseq)
        kb_o[0, hh, 0:64, :] = k[0:64].astype(BF16)
        kb_o[0, hh, 64:80, :] = r1.astype(BF16)
        kb_o[0, hh, 80:96, :] = r2.astype(BF16)
        _store_v(vb_o, hh, kv[hh * 128 + 64:hh * 128 + 128])

    pc = proj(*C_ROWS)
    sc = LOG2E * 32 ** -0.5
    for j in range(8):
        q = _rms_rows(pc[j * 32:(j + 1) * 32], dq_ref[...], 32) * sc
        k = _rms_rows(pc[256 + j * 32:256 + (j + 1) * 32], dk_ref[...], 32)
        qc_o[0, j] = q.astype(BF16)
        kc_o[0, j] = k.astype(BF16)
    for hh in range(4):
        _store_v(vc_o, hh, pc[512 + hh * 64:512 + (hh + 1) * 64])

    pd = proj(*D_ROWS)
    crow, srow, ccol, scol = crow_ref[...], srow_ref[...], ccol_ref[...], scol_ref[...]
    sd = LOG2E * 64 ** -0.5

    def axial(x, o_ref, hh):
        a1, a2 = _rope_rows(x[0:32], crow, srow)
        b1, b2 = _rope_rows(x[32:64], ccol, scol)
        o_ref[0, hh, 0:16, :] = a1.astype(BF16)
        o_ref[0, hh, 16:32, :] = a2.astype(BF16)
        o_ref[0, hh, 32:48, :] = b1.astype(BF16)
        o_ref[0, hh, 48:64, :] = b2.astype(BF16)

    for hh in range(4):
        axial(_rms_rows(pd[hh * 64:(hh + 1) * 64], gq_ref[...], 64) * sd, qd_o, hh)
    for hh in range(2):
        axial(_rms_rows(pd[256 + hh * 64:256 + (hh + 1) * 64], gk_ref[...], 64), kd_o, hh)
        _store_v(vd_o, hh, pd[384 + hh * 64:384 + (hh + 1) * 64])


def _proj(x, batch, seq, gmix, wint, cols, mats, tabs):
    n, d = x.shape
    tm = PROJ_TM
    nsb = seq // tm
    naq, nak, qlat, kvlat, mq, mk, dq, dk, gq, gk = cols
    wuqt, wukvt = mats

    def full(a):
        return pl.BlockSpec(a.shape, lambda i: (0,) * a.ndim)

    tab_spec = pl.BlockSpec((16, tm), lambda i: (0, i % nsb))
    args = [x, gmix.reshape(1, d), wint, naq, nak, qlat, wuqt, kvlat, wukvt, mq, mk, dq, dk, gq, gk, *tabs]
    in_specs = [pl.BlockSpec((tm, d), lambda i: (i, 0))] + [full(a) for a in args[1:15]] + [tab_spec] * 6

    def head_out(nh, rows):
        return (jax.ShapeDtypeStruct((batch, nh, rows, seq), BF16),
                pl.BlockSpec((1, nh, rows, tm), lambda i: (i // nsb, 0, 0, i % nsb)))

    def v_out(nh):
        return (jax.ShapeDtypeStruct((batch, nh, nsb, V_ROWS, tm), BF16),
                pl.BlockSpec((1, nh, 1, V_ROWS, tm), lambda i: (i // nsb, 0, i % nsb, 0, 0)))

    outs = [head_out(4, 64), head_out(4, 64), v_out(4),
            head_out(4, 96), head_out(4, 96), v_out(4),
            head_out(8, 32), head_out(8, 32), v_out(4),
            head_out(4, 64), head_out(2, 64), v_out(2)]
    return pl.pallas_call(
        _proj_kernel,
        grid=(n // tm,),
        in_specs=in_specs,
        out_specs=[o[1] for o in outs],
        out_shape=[o[0] for o in outs],
        compiler_params=pltpu.CompilerParams(
            dimension_semantics=("parallel",), vmem_limit_bytes=VMEM_LIMIT),
        name="mix_proj",
    )(*args)


def _col_max(s):
    parts = [s]
    while parts[0].shape[0] > 64:
        half = parts[0].shape[0] // 2
        parts = [jnp.maximum(p[:half], p[half:]) for p in parts]
    return jnp.max(parts[0], axis=0, keepdims=True)


def _produce(k_blk, q_blk, slot, s_scr, mc_scr, bias=None):
    s = _dot(k_blk, q_blk)
    if bias is not None:
        s = s + bias
    s_scr[slot] = s
    mc_scr[slot] = _col_max(s)


def _consume(s, m_cur, c, idx, v_blk, m_scr, acc_scr):
    m_prev = m_scr[idx]
    m_new = jnp.maximum(m_prev, m_cur + c)
    alpha = jnp.exp2(m_prev - m_new)
    p = jnp.exp2(s - (m_new - c)).astype(BF16)
    acc_scr[idx] = alpha * acc_scr[idx] + _dot(v_blk, p)
    m_scr[idx] = m_new


def _finish_head(acc):
    return acc[0:HEAD_V] / acc[HEAD_V:HEAD_V + 1]


def _flash_plain_kernel(q_ref, k_ref, v_ref, beta_ref, o_ref, s_scr, mc_scr, m_scr, acc_scr, *, kv_map, n_k):
    n_h = len(kv_map)
    m_scr[...] = jnp.full_like(m_scr, NEG_BIG)
    acc_scr[...] = jnp.zeros_like(acc_scr)

    def produce(i, hh):
        _produce(k_ref[0, kv_map[hh], i], q_ref[0, hh], hh % FLASH_SLOTS, s_scr, mc_scr)

    for hh in range(FLASH_AHEAD):
        produce(0, hh)

    def body(i, carry):
        nxt = jnp.minimum(i + 1, n_k - 1)
        for hh in range(n_h):
            ahead = hh + FLASH_AHEAD
            produce(i if ahead < n_h else nxt, ahead % n_h)
            slot = hh % FLASH_SLOTS
            _consume(s_scr[slot], mc_scr[slot], 0.0, hh, v_ref[0, kv_map[hh], i], m_scr, acc_scr)
        return carry

    lax.fori_loop(0, n_k, body, 0, unroll=FLASH_UNROLL)
    y = jnp.concatenate([_finish_head(acc_scr[hh]) for hh in range(n_h)], axis=0)
    o_ref[0] = _rms_rows(y, beta_ref[...], GROUP_W).astype(BF16)


def _flash_plain(qt, k, vt, beta_col, kv_map):
    b, hq, dq, s = qt.shape
    hk, n_k, t = k.shape[1], k.shape[2], k.shape[3]
    assert len(kv_map) % FLASH_SLOTS == 0
    return pl.pallas_call(
        functools.partial(_flash_plain_kernel, kv_map=kv_map, n_k=n_k),
        grid=(b, s // t),
        in_specs=[
            pl.BlockSpec((1, hq, dq, t), lambda bi, qi: (bi, 0, 0, qi)),
            pl.BlockSpec((1, hk, n_k, t, dq), lambda bi, qi: (bi, 0, 0, 0, 0)),
            pl.BlockSpec((1, hk, n_k, V_ROWS, t), lambda bi, qi: (bi, 0, 0, 0, 0)),
            pl.BlockSpec((GROUP_W, 1), lambda bi, qi: (0, 0)),
        ],
        out_specs=pl.BlockSpec((1, GROUP_W, t), lambda bi, qi: (bi, 0, qi)),
        out_shape=jax.ShapeDtypeStruct((b, GROUP_W, s), BF16),
        scratch_shapes=[pltpu.VMEM((FLASH_SLOTS, t, t), F32), pltpu.VMEM((FLASH_SLOTS, 1, t), F32),
                        pltpu.VMEM((hq, 1, t), F32), pltpu.VMEM((hq, V_ROWS, t), F32)],
        compiler_params=pltpu.CompilerParams(
            dimension_semantics=("parallel", "parallel"), vmem_limit_bytes=VMEM_LIMIT),
        name="flash_plain",
    )(qt, k, vt, beta_col)


def _flash_diff_kernel(t5_ref, q_ref, k_ref, v_ref, diag_ref, corner_ref, lam_ref, subln_ref, o_ref,
                       qpad_scr, s_scr, mc_scr, m_scr, acc_scr, *, lambda_init, n_k):
    qi = pl.program_id(1)
    n_m = 8
    dq = q_ref.shape[2]
    m_scr[...] = jnp.full_like(m_scr, NEG_BIG)
    acc_scr[...] = jnp.zeros_like(acc_scr)
    qpad_scr[...] = jnp.zeros_like(qpad_scr)
    for j in range(n_m):
        qpad_scr[j, (j % 2) * dq:(j % 2 + 1) * dq, :] = q_ref[0, j]

    def produce(i, j, bias=None):
        _produce(k_ref[0, j // 2, i], qpad_scr[j], j % FLASH_SLOTS, s_scr, mc_scr, bias)

    tq = q_ref.shape[3]
    far_left, far_right = T5_BUCKETS // 2 - 1, T5_BUCKETS - 1
    has_left, has_right = qi >= 1, qi <= n_k - 2
    left = jnp.where(has_left, qi - 1, 2)
    right = jnp.where(has_right, qi + 1, n_k - 3)
    w0 = jnp.clip(qi - 1, 0, n_k - 3)
    n_rest = n_k - 3

    def rest(t):
        return jnp.where(t < w0, t, t + 3)

    patch_max = 2.0 * LOG2E * lax.fori_loop(
        0, T5_BUCKETS * 4, lambda e, m: jnp.maximum(m, jnp.abs(t5_ref[e])), 0.0)

    for j in range(FLASH_AHEAD):
        produce(qi, j, diag_ref[j // 2])

    def visit(i, nxt, mode, flag=None):
        for j in range(n_m):
            hh = j // 2
            ahead = j + FLASH_AHEAD
            if ahead >= n_m:
                produce(nxt, ahead % n_m)
            elif mode == "diag":
                produce(i, ahead, diag_ref[ahead // 2])
            else:
                produce(i, ahead)
            slot = j % FLASH_SLOTS
            if mode == "diag":
                _consume(s_scr[slot], mc_scr[slot], 0.0, j, v_ref[0, hh, i], m_scr, acc_scr)
                continue
            c = jnp.where(i < qi, t5_ref[far_left * 4 + hh], t5_ref[far_right * 4 + hh]) * LOG2E
            m_cur = mc_scr[slot]
            if mode in ("left", "right"):
                side, bucket = (0, far_left) if mode == "left" else (1, far_right)
                rows = slice(tq - T5_MAX_DIST, tq) if mode == "left" else slice(0, T5_MAX_DIST)
                cols = slice(0, T5_MAX_DIST) if mode == "left" else slice(tq - T5_MAX_DIST, tq)
                corner = (corner_ref[side, hh] - t5_ref[bucket * 4 + hh] * LOG2E) * flag
                s_scr[slot, rows, cols] = s_scr[slot, rows, cols] + corner
                m_cur = m_cur + patch_max * flag
            _consume(s_scr[slot], m_cur, c, j, v_ref[0, hh, i], m_scr, acc_scr)

    visit(qi, left, "diag")
    visit(left, right, "left", has_left.astype(F32))
    visit(right, rest(0), "right", has_right.astype(F32))

    def far_group(g, carry):
        for u in range(FLASH_DIFF_UNROLL):
            t = g * FLASH_DIFF_UNROLL + u
            visit(rest(t), rest(jnp.minimum(t + 1, n_rest - 1)), "far")
        return carry

    def far_body(t, carry):
        visit(rest(t), rest(jnp.minimum(t + 1, n_rest - 1)), "far")
        return carry

    n_groups = n_rest // FLASH_DIFF_UNROLL
    lax.fori_loop(0, n_groups, far_group, 0)
    lax.fori_loop(n_groups * FLASH_DIFF_UNROLL, n_rest, far_body, 0)

    lp = lam_ref[...]
    lam = (jnp.exp(jnp.sum(lp[0:1] * lp[1:2], axis=1, keepdims=True))
           - jnp.exp(jnp.sum(lp[2:3] * lp[3:4], axis=1, keepdims=True)) + lambda_init)
    for hh in range(4):
        y = _finish_head(acc_scr[2 * hh]) - lam * _finish_head(acc_scr[2 * hh + 1])
        y = _rms_rows(y, subln_ref[...], HEAD_V) * (1.0 - lambda_init)
        o_ref[0, hh * HEAD_V:(hh + 1) * HEAD_V, :] = y.astype(BF16)


def _flash_diff(t5_flat, qt, k, vt, bias_tiles, lam_params, subln_col, lambda_init):
    diag, corner = bias_tiles
    b, hq, dq, s = qt.shape
    n_k, t = k.shape[2], k.shape[3]
    return pl.pallas_call(
        functools.partial(_flash_diff_kernel, lambda_init=lambda_init, n_k=n_k),
        grid=(b, s // t),
        in_specs=[
            pl.BlockSpec(memory_space=pltpu.SMEM),
            pl.BlockSpec((1, hq, dq, t), lambda bi, qi: (bi, 0, 0, qi)),
            pl.BlockSpec((1, 4, n_k, t, 2 * dq), lambda bi, qi: (bi, 0, 0, 0, 0)),
            pl.BlockSpec((1, 4, n_k, V_ROWS, t), lambda bi, qi: (bi, 0, 0, 0, 0)),
            pl.BlockSpec(diag.shape, lambda bi, qi: (0, 0, 0), pipeline_mode=pl.Buffered(1)),
            pl.BlockSpec(corner.shape, lambda bi, qi: (0, 0, 0, 0)),
            pl.BlockSpec(lam_params.shape, lambda bi, qi: (0, 0)),
            pl.BlockSpec((HEAD_V, 1), lambda bi, qi: (0, 0)),
        ],
        out_specs=pl.BlockSpec((1, GROUP_W, t), lambda bi, qi: (bi, 0, qi)),
        out_shape=jax.ShapeDtypeStruct((b, GROUP_W, s), BF16),
        scratch_shapes=[pltpu.VMEM((hq, 2 * dq, t), BF16),
                        pltpu.VMEM((FLASH_SLOTS, t, t), F32), pltpu.VMEM((FLASH_SLOTS, 1, t), F32),
                        pltpu.VMEM((hq, 1, t), F32), pltpu.VMEM((hq, V_ROWS, t), F32)],
        compiler_params=pltpu.CompilerParams(
            dimension_semantics=("parallel", "parallel"), vmem_limit_bytes=VMEM_LIMIT),
        name="flash_diff",
    )(t5_flat, qt, k, vt, diag, corner, lam_params, subln_col)


def _t5_bias_of(rel, tab_ref):
    half = T5_BUCKETS // 2
    max_exact = half // 2
    n = jnp.abs(rel)
    large = max_exact + (jnp.log(jnp.maximum(n, 1).astype(F32) / max_exact)
                         / math.log(T5_MAX_DIST / max_exact) * (half - max_exact)).astype(jnp.int32)
    large = jnp.minimum(large, half - 1)
    bucket = jnp.where(rel > 0, half, 0) + jnp.where(n < max_exact, n, large)
    accs = [jnp.zeros(rel.shape, F32) for _ in range(4)]
    for bkt in range(T5_BUCKETS):
        hit = bucket == bkt
        accs = [jnp.where(hit, tab_ref[bkt * 4 + hh], accs[hh]) for hh in range(4)]
    return [a * LOG2E for a in accs]


def _t5_bias_kernel(tab_ref, diag_ref, corner_ref, *, t):
    step = pl.program_id(0)
    rows = 16
    c = T5_MAX_DIST

    @pl.when(step == 0)
    def _():
        def body(r, carry):
            kk = lax.broadcasted_iota(jnp.int32, (rows, t), 0) + r * rows
            qq = lax.broadcasted_iota(jnp.int32, (rows, t), 1)
            for hh, b in enumerate(_t5_bias_of(kk - qq, tab_ref)):
                diag_ref[hh, pl.ds(pl.multiple_of(r * rows, rows), rows), :] = b
            return carry

        lax.fori_loop(0, t // rows, body, 0)

    @pl.when(step > 0)
    def _():
        base = jnp.where(step == 1, -c, c)

        def body(r, carry):
            kk = lax.broadcasted_iota(jnp.int32, (rows, c), 0) + r * rows
            qq = lax.broadcasted_iota(jnp.int32, (rows, c), 1)
            for hh, b in enumerate(_t5_bias_of(base + kk - qq, tab_ref)):
                corner_ref[0, hh, pl.ds(pl.multiple_of(r * rows, rows), rows), :] = b
            return carry

        lax.fori_loop(0, c // rows, body, 0)


def _t5_bias(t5_flat):
    t, c = FLASH_T, T5_MAX_DIST
    return pl.pallas_call(
        functools.partial(_t5_bias_kernel, t=t),
        grid=(3,),
        in_specs=[pl.BlockSpec(memory_space=pltpu.SMEM)],
        out_specs=[pl.BlockSpec((4, t, t), lambda d: (0, 0, 0)),
                   pl.BlockSpec((1, 4, c, c), lambda d: (jnp.maximum(d - 1, 0), 0, 0, 0))],
        out_shape=[jax.ShapeDtypeStruct((4, t, t), F32), jax.ShapeDtypeStruct((2, 4, c, c), F32)],
        compiler_params=pltpu.CompilerParams(dimension_semantics=("arbitrary",)),
        name="t5_bias",
    )(t5_flat)


def _na_window(kind, j, i, grid_rows):
    r0 = (0, NA_Q_ROWS, grid_rows - NA_Q_ROWS)[kind]
    start = (0, 0, grid_rows - NA_K_ROWS)[kind]
    krow, qrow = start + j, r0 + i
    lo = min(max(qrow - NA_KH // 2, 0), grid_rows - NA_KH)
    return lo <= krow < lo + NA_KH, krow - qrow + NA_KH - 1


def _na_bias_kernel(rpb_ref, o_ref, t_scr, *, grid_rows):
    hh = pl.program_id(0)
    n_dr, n_dc = 2 * NA_KH - 1, 2 * NA_KW - 1
    shape = (GRID_W, NA_Q_ROWS * GRID_W)
    lane = lax.broadcasted_iota(jnp.int32, shape, 1)
    kc = lax.broadcasted_iota(jnp.int32, shape, 0)
    qc = lane % GRID_W
    grp = lane // GRID_W
    dcm = jnp.clip(kc - qc + NA_KW - 1, 0, n_dc - 1)
    qs = jnp.clip(qc - NA_KW // 2, 0, GRID_W - NA_KW)
    col_ok = (kc >= qs) & (kc < qs + NA_KW)

    for dr in range(n_dr):
        base = (hh * n_dr + dr) * n_dc
        t_scr[dr] = lax.fori_loop(
            0, n_dc, lambda dc, acc: jnp.where(dcm == dc, rpb_ref[base + dc], acc), jnp.zeros(shape, F32)) * LOG2E

    for kind in range(3):
        for j in range(NA_K_ROWS):
            wins = [_na_window(kind, j, i, grid_rows) for i in range(NA_Q_ROWS)]
            blk = jnp.full(shape, NEG_BIG, F32)
            for i, (inside, dr) in enumerate(wins):
                if inside:
                    blk = jnp.where((grp == i) & col_ok, t_scr[dr], blk)
            o_ref[kind, 0, j * GRID_W:(j + 1) * GRID_W, :] = blk


def _na_bias(rpb_flat, grid_rows):
    kt, qt = NA_K_ROWS * GRID_W, NA_Q_ROWS * GRID_W
    return pl.pallas_call(
        functools.partial(_na_bias_kernel, grid_rows=grid_rows),
        grid=(4,),
        in_specs=[pl.BlockSpec(memory_space=pltpu.SMEM)],
        out_specs=pl.BlockSpec((3, 1, kt, qt), lambda h: (0, h, 0, 0)),
        out_shape=jax.ShapeDtypeStruct((3, 4, kt, qt), F32),
        scratch_shapes=[pltpu.VMEM((2 * NA_KH - 1, GRID_W, qt), F32)],
        name="na_bias",
    )(rpb_flat)


def _na_kernel(q_ref, k0_ref, k1_ref, k2_ref, v0_ref, v1_ref, v2_ref, bias_ref, beta_ref, o_ref):
    qt = NA_Q_ROWS * GRID_W
    k_refs = (k0_ref, k1_ref, k2_ref)
    v_refs = (v0_ref, v1_ref, v2_ref)

    def logits(hh):
        q = q_ref[0, hh]
        return [_dot(k_refs[j][0, hh], q) + bias_ref[0, hh, j * qt:(j + 1) * qt, :] for j in range(3)]

    outs = []
    ss_next = logits(0)
    for hh in range(4):
        ss = ss_next
        if hh + 1 < 4:
            ss_next = logits(hh + 1)
        m = functools.reduce(jnp.maximum, [_col_max(s) for s in ss])
        acc = None
        for j in range(3):
            p = jnp.exp2(ss[j] - m).astype(BF16)
            pv = _dot(v_refs[j][0, hh, 0], p)
            acc = pv if acc is None else acc + pv
        outs.append(_finish_head(acc))
    y = jnp.concatenate(outs, axis=0)
    o_ref[0] = _rms_rows(y, beta_ref[...], GROUP_W).astype(BF16)


def _na(qt_arr, k, vt, bias, beta_col):
    b, nh, d, s = qt_arr.shape
    qt = NA_Q_ROWS * GRID_W
    kt = NA_K_ROWS * GRID_W
    n_t = s // qt
    n_win = kt // qt

    def win(j):
        return lambda bi, ti: jnp.clip(ti - 1, 0, n_t - n_win) + j

    def kind(bi, ti):
        return (jnp.where(ti == 0, 0, jnp.where(ti == n_t - 1, 2, 1)), 0, 0, 0)

    k_specs = [pl.BlockSpec((1, nh, qt, d), (lambda j: lambda bi, ti: (bi, 0, win(j)(bi, ti), 0))(j))
               for j in range(n_win)]
    per = vt.shape[4] // qt
    v_specs = [pl.BlockSpec((1, nh, 1, V_ROWS, qt),
                            (lambda j: lambda bi, ti: (bi, 0, win(j)(bi, ti) // per, 0, win(j)(bi, ti) % per))(j))
               for j in range(n_win)]
    return pl.pallas_call(
        _na_kernel,
        grid=(b, n_t),
        in_specs=[pl.BlockSpec((1, nh, d, qt), lambda bi, ti: (bi, 0, 0, ti))] + k_specs + v_specs + [
            pl.BlockSpec((1, nh, kt, qt), kind),
            pl.BlockSpec((GROUP_W, 1), lambda bi, ti: (0, 0)),
        ],
        out_specs=pl.BlockSpec((1, GROUP_W, qt), lambda bi, ti: (bi, 0, ti)),
        out_shape=jax.ShapeDtypeStruct((b, GROUP_W, s), BF16),
        compiler_params=pltpu.CompilerParams(
            dimension_semantics=("parallel", "parallel"), vmem_limit_bytes=VMEM_LIMIT),
        name="na_attn",
    )(qt_arr, k, k, k, vt, vt, vt, bias, beta_col)


def _rope_tables(pos, dim):
    inv = jnp.exp(-math.log(ROPE_THETA) * jnp.arange(0, dim, 2, dtype=F32) / dim)
    ang = pos.astype(F32)[:, None] * inv[None, :]
    return jnp.cos(ang).T, jnp.sin(ang).T


def _col(v):
    return v.reshape(-1, 1).astype(F32)


def _lambda_init(layer):
    return 0.8 - 0.6 * math.exp(-0.3 * layer)


def kernel(x, ffn1_norm, ffn1_w_gate, ffn1_w_up, ffn1_w_down, mix_norm, w_in, na_q_norm, na_k_norm, na_rpb, na_beta, mla_q_lat_norm, mla_w_uq, mla_kv_lat_norm, mla_w_ukv, mla_q_norm, mla_k_norm, mla_beta, diff_q_norm, diff_k_norm, diff_lambda, diff_subln, gqa_q_norm, gqa_k_norm, gqa_beta, w_out, ffn2_norm, ffn2_w_gate, ffn2_w_up, ffn2_w_down, final_norm, t5_bias):
    batch, seq, d = x.shape
    grid_rows = seq // GRID_W
    pos = jnp.arange(seq, dtype=jnp.int32)
    tabs = (*_rope_tables(pos, 32), *_rope_tables(pos // GRID_W, 32), *_rope_tables(pos % GRID_W, 32))
    t5_flat = t5_bias.reshape(-1).astype(F32)
    t5_tiles = _t5_bias(t5_flat)

    xf = x.reshape(batch * seq, d)
    for l in range(N_LAYERS):
        xf = _ffn(xf, ffn1_norm[l], ffn1_w_gate[l].astype(BF16), ffn1_w_up[l].astype(BF16),
                  ffn1_w_down[l].astype(BF16))
        cols = tuple(_col(v[l]) for v in (na_q_norm, na_k_norm, mla_q_lat_norm, mla_kv_lat_norm,
                                          mla_q_norm, mla_k_norm, diff_q_norm, diff_k_norm,
                                          gqa_q_norm, gqa_k_norm))
        mats = (mla_w_uq[l].T.astype(BF16), mla_w_ukv[l].T.astype(BF16))
        (qa, ka, va, qb, kb, vb, qc, kc, vc, qd, kd, vd) = _proj(
            xf, batch, seq, mix_norm[l], w_in[l].T.astype(BF16), cols, mats, tabs)
        n_k = seq // FLASH_T
        ka = jnp.swapaxes(ka, 2, 3)
        kb = jnp.swapaxes(kb, 2, 3).reshape(batch, 4, n_k, FLASH_T, 96)
        kc = jnp.swapaxes(kc.reshape(batch, 4, 64, seq), 2, 3).reshape(batch, 4, n_k, FLASH_T, 64)
        kd = jnp.swapaxes(kd, 2, 3).reshape(batch, 2, n_k, FLASH_T, 64)

        ya = _na(qa, ka, va, _na_bias(na_rpb[l].reshape(-1).astype(F32), grid_rows), _col(na_beta[l]))
        yb = _flash_plain(qb, kb, vb, _col(mla_beta[l]), (0, 1, 2, 3))
        yc = _flash_diff(t5_flat, qc, kc, vc, t5_tiles, diff_lambda[l].astype(F32), _col(diff_subln[l]),
                         _lambda_init(l))
        yd = _flash_plain(qd, kd, vd, _col(gqa_beta[l]), (0, 0, 1, 1))

        xf = _ffn(xf, ffn2_norm[l], ffn2_w_gate[l].astype(BF16), ffn2_w_up[l].astype(BF16),
                  ffn2_w_down[l].astype(BF16), mix=((ya, yb, yc, yd), w_out[l].astype(BF16)),
                  final_g=final_norm[l], seq=seq)
    return xf.reshape(batch, seq, d)
```

```python
import functools
import math

import jax
import jax.numpy as jnp
from jax import lax
from jax.experimental import pallas as pl
from jax.experimental.pallas import tpu as pltpu

F32 = jnp.float32
BF16 = jnp.bfloat16

EPS = 1e-6
NEG_BIG = -1e30
LOG2E = 1.4426950408889634
ROPE_THETA = 10000.0
GRID_W = 64
N_LAYERS = 2

HEAD_V = 64
V_ROWS = 80
GROUP_W = 256
A_ROWS, B_ROWS, C_ROWS, D_ROWS = (0, 768), (768, 1184), (1184, 1952), (1952, 2464)

NA_KH, NA_KW = 8, 16
NA_Q_ROWS = 4
NA_K_ROWS = 12
T5_BUCKETS = 32
T5_MAX_DIST = 128

FFN_TM = 1024
FFN_CHUNK = 1024
FLASH_T = 512
FLASH_AHEAD = 2
FLASH_SLOTS = 4
FLASH_UNROLL = 8
FLASH_DIFF_UNROLL = 4
PROJ_TM = FLASH_T

VMEM_LIMIT = 56 * 1024 * 1024


def _rms_rows(x, g_col, n):
    r = lax.rsqrt(jnp.sum(x * x, axis=0, keepdims=True) * (1.0 / n) + EPS)
    return (x * r) * g_col


def _rms_lanes(x, g_row):
    r = lax.rsqrt(jnp.mean(x * x, axis=-1, keepdims=True) + EPS)
    return (x * r) * g_row


def _dot(a, b):
    return jnp.dot(a, b, preferred_element_type=F32)


def _ffn_kernel(*refs, has_mix, has_final, chunks):
    x_ref, g_ref, wg_ref, wu_ref, wd_ref = refs[:5]
    rest = refs[5:]
    if has_mix:
        y_refs, wo_ref, rest = rest[:4], rest[4], rest[5:]
    if has_final:
        gf_ref, rest = rest[0], rest[1:]
    (o_ref,) = rest

    x = x_ref[...]
    if has_mix:
        for g in range(4):
            x = x + lax.dot_general(
                y_refs[g][0], wo_ref[g * GROUP_W:(g + 1) * GROUP_W, :],
                (((0,), (0,)), ((), ())), preferred_element_type=F32)
    o_ref[...] = x
    h = _rms_lanes(x, g_ref[...]).astype(BF16)
    acc = None
    for lo, hi in chunks:
        gate = _dot(h, wg_ref[:, lo:hi])
        up = _dot(h, wu_ref[:, lo:hi])
        a = (gate / (1.0 + jnp.exp(-gate))) * up
        part = _dot(a.astype(BF16), wd_ref[lo:hi, :])
        acc = part if acc is None else acc + part
    out = o_ref[...] + 0.5 * acc
    if has_final:
        out = _rms_lanes(out, gf_ref[...])
    o_ref[...] = out


def _ffn(x, g, wg, wu, wd, mix=None, final_g=None, seq=None):
    n, d = x.shape
    d_ff = wg.shape[1]
    tm = FFN_TM
    chunks = tuple((lo, min(lo + FFN_CHUNK, d_ff)) for lo in range(0, d_ff, FFN_CHUNK))
    once = pl.Buffered(1)
    in_specs = [
        pl.BlockSpec((tm, d), lambda i: (i, 0)),
        pl.BlockSpec((1, d), lambda i: (0, 0)),
        pl.BlockSpec((d, d_ff), lambda i: (0, 0), pipeline_mode=once),
        pl.BlockSpec((d, d_ff), lambda i: (0, 0), pipeline_mode=once),
        pl.BlockSpec((d_ff, d), lambda i: (0, 0), pipeline_mode=once),
    ]
    args = [x, g.reshape(1, d), wg, wu, wd]
    if mix is not None:
        ys, wo = mix
        nsb = seq // tm
        for y in ys:
            in_specs.append(pl.BlockSpec((1, GROUP_W, tm), lambda i: (i // nsb, 0, i % nsb)))
            args.append(y)
        in_specs.append(pl.BlockSpec(wo.shape, lambda i: (0, 0), pipeline_mode=once))
        args.append(wo)
    if final_g is not None:
        in_specs.append(pl.BlockSpec((1, d), lambda i: (0, 0)))
        args.append(final_g.reshape(1, d))
    return pl.pallas_call(
        functools.partial(_ffn_kernel, has_mix=mix is not None, has_final=final_g is not None, chunks=chunks),
        grid=(n // tm,),
        in_specs=in_specs,
        out_specs=pl.BlockSpec((tm, d), lambda i: (i, 0)),
        out_shape=jax.ShapeDtypeStruct((n, d), F32),
        compiler_params=pltpu.CompilerParams(
            dimension_semantics=("parallel",), vmem_limit_bytes=VMEM_LIMIT),
        name="ffn_mix" if mix is not None else "ffn",
    )(*args)


def _rope_rows(x, cos, sin):
    x1, x2 = x[:16], x[16:]
    return x1 * cos - x2 * sin, x1 * sin + x2 * cos


def _store_v(v_ref, hh, v):
    v_ref[0, hh, 0, 0:HEAD_V, :] = v.astype(BF16)
    v_ref[0, hh, 0, HEAD_V:V_ROWS, :] = jnp.ones((V_ROWS - HEAD_V, v.shape[1]), BF16)


def _proj_kernel(x_ref, gmix_ref, wint_ref,
                 naq_ref, nak_ref,
                 qlat_ref, wuqt_ref, kvlat_ref, wukvt_ref, mq_ref, mk_ref,
                 dq_ref, dk_ref, gq_ref, gk_ref,
                 cseq_ref, sseq_ref, crow_ref, srow_ref, ccol_ref, scol_ref,
                 qa_o, ka_o, va_o, qb_o, kb_o, vb_o, qc_o, kc_o, vc_o, qd_o, kd_o, vd_o):
    h = _rms_lanes(x_ref[...], gmix_ref[...]).astype(BF16)

    def proj(lo, hi):
        return lax.dot_general(wint_ref[lo:hi, :], h, (((1,), (1,)), ((), ())),
                               preferred_element_type=F32)

    pa = proj(*A_ROWS)
    sa = LOG2E * 64 ** -0.5
    for hh in range(4):
        q = _rms_rows(pa[hh * 64:(hh + 1) * 64], naq_ref[...], 64) * sa
        k = _rms_rows(pa[256 + hh * 64:256 + (hh + 1) * 64], nak_ref[...], 64)
        qa_o[0, hh] = q.astype(BF16)
        ka_o[0, hh] = k.astype(BF16)
        _store_v(va_o, hh, pa[512 + hh * 64:512 + (hh + 1) * 64])

    pb = proj(*B_ROWS)
    cq = _rms_rows(pb[0:256], qlat_ref[...], 256).astype(BF16)
    qb = _dot(wuqt_ref[...], cq)
    ckv = _rms_rows(pb[256:384], kvlat_ref[...], 128).astype(BF16)
    kv = _dot(wukvt_ref[...], ckv)
    kr = pb[384:416]
    cseq, sseq = cseq_ref[...], sseq_ref[...]
    sb = LOG2E * 96 ** -0.5
    for hh in range(4):
        q = _rms_rows(qb[hh * 96:(hh + 1) * 96], mq_ref[...], 96) * sb
        r1, r2 = _rope_rows(q[64:96], cseq, sseq)
        qb_o[0, hh, 0:64, :] = q[0:64].astype(BF16)
        qb_o[0, hh, 64:80, :] = r1.astype(BF16)
        qb_o[0, hh, 80:96, :] = r2.astype(BF16)
        k = jnp.concatenate([kv[hh * 128:hh * 128 + 64], kr], axis=0)
        k = _rms_rows(k, mk_ref[...], 96)
        r1, r2 = _rope_rows(k[64:96], cseq, sseq)
        kb_o[0, hh, 0:64, :] = k[0:64].astype(BF16)
        kb_o[0, hh, 64:80, :] = r1.astype(BF16)
        kb_o[0, hh, 80:96, :] = r2.astype(BF16)
        _store_v(vb_o, hh, kv[hh * 128 + 64:hh * 128 + 128])

    pc = proj(*C_ROWS)
    sc = LOG2E * 32 ** -0.5
    for j in range(8):
        q = _rms_rows(pc[j * 32:(j + 1) * 32], dq_ref[...], 32) * sc
        k = _rms_rows(pc[256 + j * 32:256 + (j + 1) * 32], dk_ref[...], 32)
        qc_o[0, j] = q.astype(BF16)
        kc_o[0, j] = k.astype(BF16)
    for hh in range(4):
        _store_v(vc_o, hh, pc[512 + hh * 64:512 + (hh + 1) * 64])

    pd = proj(*D_ROWS)
    crow, srow, ccol, scol = crow_ref[...], srow_ref[...], ccol_ref[...], scol_ref[...]
    sd = LOG2E * 64 ** -0.5

    def axial(x, o_ref, hh):
        a1, a2 = _rope_rows(x[0:32], crow, srow)
        b1, b2 = _rope_rows(x[32:64], ccol, scol)
        o_ref[0, hh, 0:16, :] = a1.astype(BF16)
        o_ref[0, hh, 16:32, :] = a2.astype(BF16)
        o_ref[0, hh, 32:48, :] = b1.astype(BF16)
        o_ref[0, hh, 48:64, :] = b2.astype(BF16)

    for hh in range(4):
        axial(_rms_rows(pd[hh * 64:(hh + 1) * 64], gq_ref[...], 64) * sd, qd_o, hh)
    for hh in range(2):
        axial(_rms_rows(pd[256 + hh * 64:256 + (hh + 1) * 64], gk_ref[...], 64), kd_o, hh)
        _store_v(vd_o, hh, pd[384 + hh * 64:384 + (hh + 1) * 64])


def _proj(x, batch, seq, gmix, wint, cols, mats, tabs):
    n, d = x.shape
    tm = PROJ_TM
    nsb = seq // tm
    naq, nak, qlat, kvlat, mq, mk, dq, dk, gq, gk = cols
    wuqt, wukvt = mats

    def full(a):
        return pl.BlockSpec(a.shape, lambda i: (0,) * a.ndim)

    tab_spec = pl.BlockSpec((16, tm), lambda i: (0, i % nsb))
    args = [x, gmix.reshape(1, d), wint, naq, nak, qlat, wuqt, kvlat, wukvt, mq, mk, dq, dk, gq, gk, *tabs]
    in_specs = [pl.BlockSpec((tm, d), lambda i: (i, 0))] + [full(a) for a in args[1:15]] + [tab_spec] * 6

    def head_out(nh, rows):
        return (jax.ShapeDtypeStruct((batch, nh, rows, seq), BF16),
                pl.BlockSpec((1, nh, rows, tm), lambda i: (i // nsb, 0, 0, i % nsb)))

    def v_out(nh):
        return (jax.ShapeDtypeStruct((batch, nh, nsb, V_ROWS, tm), BF16),
                pl.BlockSpec((1, nh, 1, V_ROWS, tm), lambda i: (i // nsb, 0, i % nsb, 0, 0)))

    outs = [head_out(4, 64), head_out(4, 64), v_out(4),
            head_out(4, 96), head_out(4, 96), v_out(4),
            head_out(8, 32), head_out(8, 32), v_out(4),
            head_out(4, 64), head_out(2, 64), v_out(2)]
    return pl.pallas_call(
        _proj_kernel,
        grid=(n // tm,),
        in_specs=in_specs,
        out_specs=[o[1] for o in outs],
        out_shape=[o[0] for o in outs],
        compiler_params=pltpu.CompilerParams(
            dimension_semantics=("parallel",), vmem_limit_bytes=VMEM_LIMIT),
        name="mix_proj",
    )(*args)


def _col_max(s):
    parts = [s]
    while parts[0].shape[0] > 64:
        half = parts[0].shape[0] // 2
        parts = [jnp.maximum(p[:half], p[half:]) for p in parts]
    return jnp.max(parts[0], axis=0, keepdims=True)


def _produce(k_blk, q_blk, slot, s_scr, mc_scr, bias=None):
    s = _dot(k_blk, q_blk)
    if bias is not None:
        s = s + bias
    s_scr[slot] = s
    mc_scr[slot] = _col_max(s)


def _consume(s, m_cur, c, idx, v_blk, m_scr, acc_scr):
    m_prev = m_scr[idx]
    m_new = jnp.maximum(m_prev, m_cur + c)
    alpha = jnp.exp2(m_prev - m_new)
    p = jnp.exp2(s - (m_new - c)).astype(BF16)
    acc_scr[idx] = alpha * acc_scr[idx] + _dot(v_blk, p)
    m_scr[idx] = m_new


def _finish_head(acc):
    return acc[0:HEAD_V] / acc[HEAD_V:HEAD_V + 1]


def _flash_plain_kernel(q_ref, k_ref, v_ref, beta_ref, o_ref, s_scr, mc_scr, m_scr, acc_scr, *, kv_map, n_k):
    n_h = len(kv_map)
    m_scr[...] = jnp.full_like(m_scr, NEG_BIG)
    acc_scr[...] = jnp.zeros_like(acc_scr)

    def produce(i, hh):
        _produce(k_ref[0, kv_map[hh], i], q_ref[0, hh], hh % FLASH_SLOTS, s_scr, mc_scr)

    for hh in range(FLASH_AHEAD):
        produce(0, hh)

    def body(i, carry):
        nxt = jnp.minimum(i + 1, n_k - 1)
        for hh in range(n_h):
            ahead = hh + FLASH_AHEAD
            produce(i if ahead < n_h else nxt, ahead % n_h)
            slot = hh % FLASH_SLOTS
            _consume(s_scr[slot], mc_scr[slot], 0.0, hh, v_ref[0, kv_map[hh], i], m_scr, acc_scr)
        return carry

    lax.fori_loop(0, n_k, body, 0, unroll=FLASH_UNROLL)
    y = jnp.concatenate([_finish_head(acc_scr[hh]) for hh in range(n_h)], axis=0)
    o_ref[0] = _rms_rows(y, beta_ref[...], GROUP_W).astype(BF16)


def _flash_plain(qt, k, vt, beta_col, kv_map):
    b, hq, dq, s = qt.shape
    hk, n_k, t = k.shape[1], k.shape[2], k.shape[3]
    assert len(kv_map) % FLASH_SLOTS == 0
    return pl.pallas_call(
        functools.partial(_flash_plain_kernel, kv_map=kv_map, n_k=n_k),
        grid=(b, s // t),
        in_specs=[
            pl.BlockSpec((1, hq, dq, t), lambda bi, qi: (bi, 0, 0, qi)),
            pl.BlockSpec((1, hk, n_k, t, dq), lambda bi, qi: (bi, 0, 0, 0, 0)),
            pl.BlockSpec((1, hk, n_k, V_ROWS, t), lambda bi, qi: (bi, 0, 0, 0, 0)),
            pl.BlockSpec((GROUP_W, 1), lambda bi, qi: (0, 0)),
        ],
        out_specs=pl.BlockSpec((1, GROUP_W, t), lambda bi, qi: (bi, 0, qi)),
        out_shape=jax.ShapeDtypeStruct((b, GROUP_W, s), BF16),
        scratch_shapes=[pltpu.VMEM((FLASH_SLOTS, t, t), F32), pltpu.VMEM((FLASH_SLOTS, 1, t), F32),
                        pltpu.VMEM((hq, 1, t), F32), pltpu.VMEM((hq, V_ROWS, t), F32)],
        compiler_params=pltpu.CompilerParams(
            dimension_semantics=("parallel", "parallel"), vmem_limit_bytes=VMEM_LIMIT),
        name="flash_plain",
    )(qt, k, vt, beta_col)


def _flash_diff_kernel(t5_ref, q_ref, k_ref, v_ref, diag_ref, corner_ref, span_ref, lam_ref, subln_ref, o_ref,
                       qpad_scr, s_scr, mc_scr, m_scr, acc_scr, *, lambda_init, n_k):
    qi = pl.program_id(1)
    n_m = 8
    dq = q_ref.shape[2]
    m_scr[...] = jnp.full_like(m_scr, NEG_BIG)
    acc_scr[...] = jnp.zeros_like(acc_scr)
    qpad_scr[...] = jnp.zeros_like(qpad_scr)
    for j in range(n_m):
        qpad_scr[j, (j % 2) * dq:(j % 2 + 1) * dq, :] = q_ref[0, j]

    def produce(i, j, bias=None):
        _produce(k_ref[0, j // 2, i], qpad_scr[j], j % FLASH_SLOTS, s_scr, mc_scr, bias)

    tq = q_ref.shape[3]
    far_left, far_right = T5_BUCKETS // 2 - 1, T5_BUCKETS - 1
    has_left, has_right = qi >= 1, qi <= n_k - 2
    left = jnp.where(has_left, qi - 1, 2)
    right = jnp.where(has_right, qi + 1, n_k - 3)
    w0 = jnp.clip(qi - 1, 0, n_k - 3)
    n_rest = n_k - 3

    def rest(t):
        return jnp.where(t < w0, t, t + 3)

    patch_max = span_ref[0:1, 0:1]

    for j in range(FLASH_AHEAD):
        produce(qi, j, diag_ref[j // 2])

    def visit(i, nxt, mode, flag=None):
        for j in range(n_m):
            hh = j // 2
            ahead = j + FLASH_AHEAD
            if ahead >= n_m:
                produce(nxt, ahead % n_m)
            elif mode == "diag":
                produce(i, ahead, diag_ref[ahead // 2])
            else:
                produce(i, ahead)
            slot = j % FLASH_SLOTS
            if mode == "diag":
                _consume(s_scr[slot], mc_scr[slot], 0.0, j, v_ref[0, hh, i], m_scr, acc_scr)
                continue
            c = jnp.where(i < qi, t5_ref[far_left * 4 + hh], t5_ref[far_right * 4 + hh]) * LOG2E
            m_cur = mc_scr[slot]
            if mode in ("left", "right"):
                side, bucket = (0, far_left) if mode == "left" else (1, far_right)
                rows = slice(tq - T5_MAX_DIST, tq) if mode == "left" else slice(0, T5_MAX_DIST)
                cols = slice(0, T5_MAX_DIST) if mode == "left" else slice(tq - T5_MAX_DIST, tq)
                corner = (corner_ref[side, hh] - t5_ref[bucket * 4 + hh] * LOG2E) * flag
                s_scr[slot, rows, cols] = s_scr[slot, rows, cols] + corner
                m_cur = m_cur + patch_max * flag
            _consume(s_scr[slot], m_cur, c, j, v_ref[0, hh, i], m_scr, acc_scr)

    visit(qi, left, "diag")
    visit(left, right, "left", has_left.astype(F32))
    visit(right, rest(0), "right", has_right.astype(F32))

    def far_group(g, carry):
        for u in range(FLASH_DIFF_UNROLL):
            t = g * FLASH_DIFF_UNROLL + u
            visit(rest(t), rest(jnp.minimum(t + 1, n_rest - 1)), "far")
        return carry

    def far_body(t, carry):
        visit(rest(t), rest(jnp.minimum(t + 1, n_rest - 1)), "far")
        return carry

    n_groups = n_rest // FLASH_DIFF_UNROLL
    lax.fori_loop(0, n_groups, far_group, 0)
    lax.fori_loop(n_groups * FLASH_DIFF_UNROLL, n_rest, far_body, 0)

    lp = lam_ref[...]
    lam = (jnp.exp(jnp.sum(lp[0:1] * lp[1:2], axis=1, keepdims=True))
           - jnp.exp(jnp.sum(lp[2:3] * lp[3:4], axis=1, keepdims=True)) + lambda_init)
    for hh in range(4):
        y = _finish_head(acc_scr[2 * hh]) - lam * _finish_head(acc_scr[2 * hh + 1])
        y = _rms_rows(y, subln_ref[...], HEAD_V) * (1.0 - lambda_init)
        o_ref[0, hh * HEAD_V:(hh + 1) * HEAD_V, :] = y.astype(BF16)


def _flash_diff(t5_flat, qt, k, vt, bias_tiles, lam_params, subln_col, lambda_init):
    diag, corner, span = bias_tiles
    assert k.shape[2] >= 4, "the diagonal window (3 tiles) plus at least one other key tile"
    b, hq, dq, s = qt.shape
    n_k, t = k.shape[2], k.shape[3]
    return pl.pallas_call(
        functools.partial(_flash_diff_kernel, lambda_init=lambda_init, n_k=n_k),
        grid=(b, s // t),
        in_specs=[
            pl.BlockSpec(memory_space=pltpu.SMEM),
            pl.BlockSpec((1, hq, dq, t), lambda bi, qi: (bi, 0, 0, qi)),
            pl.BlockSpec((1, 4, n_k, t, 2 * dq), lambda bi, qi: (bi, 0, 0, 0, 0)),
            pl.BlockSpec((1, 4, n_k, V_ROWS, t), lambda bi, qi: (bi, 0, 0, 0, 0)),
            pl.BlockSpec(diag.shape, lambda bi, qi: (0, 0, 0), pipeline_mode=pl.Buffered(1)),
            pl.BlockSpec(corner.shape, lambda bi, qi: (0, 0, 0, 0)),
            pl.BlockSpec(span.shape, lambda bi, qi: (0, 0)),
            pl.BlockSpec(lam_params.shape, lambda bi, qi: (0, 0)),
            pl.BlockSpec((HEAD_V, 1), lambda bi, qi: (0, 0)),
        ],
        out_specs=pl.BlockSpec((1, GROUP_W, t), lambda bi, qi: (bi, 0, qi)),
        out_shape=jax.ShapeDtypeStruct((b, GROUP_W, s), BF16),
        scratch_shapes=[pltpu.VMEM((hq, 2 * dq, t), BF16),
                        pltpu.VMEM((FLASH_SLOTS, t, t), F32), pltpu.VMEM((FLASH_SLOTS, 1, t), F32),
                        pltpu.VMEM((hq, 1, t), F32), pltpu.VMEM((hq, V_ROWS, t), F32)],
        compiler_params=pltpu.CompilerParams(
            dimension_semantics=("parallel", "parallel"), vmem_limit_bytes=VMEM_LIMIT),
        name="flash_diff",
    )(t5_flat, qt, k, vt, diag, corner, span, lam_params, subln_col)


def _t5_bias_of(rel, tab_ref):
    half = T5_BUCKETS // 2
    max_exact = half // 2
    n = jnp.abs(rel)
    large = max_exact + (jnp.log(jnp.maximum(n, 1).astype(F32) / max_exact)
                         / math.log(T5_MAX_DIST / max_exact) * (half - max_exact)).astype(jnp.int32)
    large = jnp.minimum(large, half - 1)
    bucket = jnp.where(rel > 0, half, 0) + jnp.where(n < max_exact, n, large)
    accs = [jnp.zeros(rel.shape, F32) for _ in range(4)]
    for bkt in range(T5_BUCKETS):
        hit = bucket == bkt
        accs = [jnp.where(hit, tab_ref[bkt * 4 + hh], accs[hh]) for hh in range(4)]
    return [a * LOG2E for a in accs]


def _t5_bias_kernel(tab_ref, diag_ref, corner_ref, span_ref, *, t):
    step = pl.program_id(0)
    rows = 16
    c = T5_MAX_DIST

    @pl.when(step == 0)
    def _():
        top = lax.fori_loop(0, T5_BUCKETS * 4, lambda e, m: jnp.maximum(m, jnp.abs(tab_ref[e])), 0.0)
        span_ref[...] = jnp.full(span_ref.shape, 2.0 * LOG2E * top, F32)

        def body(r, carry):
            kk = lax.broadcasted_iota(jnp.int32, (rows, t), 0) + r * rows
            qq = lax.broadcasted_iota(jnp.int32, (rows, t), 1)
            for hh, b in enumerate(_t5_bias_of(kk - qq, tab_ref)):
                diag_ref[hh, pl.ds(pl.multiple_of(r * rows, rows), rows), :] = b
            return carry

        lax.fori_loop(0, t // rows, body, 0)

    @pl.when(step > 0)
    def _():
        base = jnp.where(step == 1, -c, c)

        def body(r, carry):
            kk = lax.broadcasted_iota(jnp.int32, (rows, c), 0) + r * rows
            qq = lax.broadcasted_iota(jnp.int32, (rows, c), 1)
            for hh, b in enumerate(_t5_bias_of(base + kk - qq, tab_ref)):
                corner_ref[0, hh, pl.ds(pl.multiple_of(r * rows, rows), rows), :] = b
            return carry

        lax.fori_loop(0, c // rows, body, 0)


def _t5_bias(t5_flat):
    t, c = FLASH_T, T5_MAX_DIST
    return pl.pallas_call(
        functools.partial(_t5_bias_kernel, t=t),
        grid=(3,),
        in_specs=[pl.BlockSpec(memory_space=pltpu.SMEM)],
        out_specs=[pl.BlockSpec((4, t, t), lambda d: (0, 0, 0)),
                   pl.BlockSpec((1, 4, c, c), lambda d: (jnp.maximum(d - 1, 0), 0, 0, 0)),
                   pl.BlockSpec((8, 128), lambda d: (0, 0))],
        out_shape=[jax.ShapeDtypeStruct((4, t, t), F32), jax.ShapeDtypeStruct((2, 4, c, c), F32),
                   jax.ShapeDtypeStruct((8, 128), F32)],
        compiler_params=pltpu.CompilerParams(dimension_semantics=("arbitrary",)),
        name="t5_bias",
    )(t5_flat)


def _na_window(kind, j, i, grid_rows):
    r0 = (0, NA_Q_ROWS, grid_rows - NA_Q_ROWS)[kind]
    start = (0, 0, grid_rows - NA_K_ROWS)[kind]
    krow, qrow = start + j, r0 + i
    lo = min(max(qrow - NA_KH // 2, 0), grid_rows - NA_KH)
    return lo <= krow < lo + NA_KH, krow - qrow + NA_KH - 1


def _na_bias_kernel(rpb_ref, o_ref, t_scr, *, grid_rows):
    hh = pl.program_id(0)
    n_dr, n_dc = 2 * NA_KH - 1, 2 * NA_KW - 1
    shape = (GRID_W, NA_Q_ROWS * GRID_W)
    lane = lax.broadcasted_iota(jnp.int32, shape, 1)
    kc = lax.broadcasted_iota(jnp.int32, shape, 0)
    qc = lane % GRID_W
    grp = lane // GRID_W
    dcm = jnp.clip(kc - qc + NA_KW - 1, 0, n_dc - 1)
    qs = jnp.clip(qc - NA_KW // 2, 0, GRID_W - NA_KW)
    col_ok = (kc >= qs) & (kc < qs + NA_KW)

    for dr in range(n_dr):
        base = (hh * n_dr + dr) * n_dc
        t_scr[dr] = lax.fori_loop(
            0, n_dc, lambda dc, acc: jnp.where(dcm == dc, rpb_ref[base + dc], acc), jnp.zeros(shape, F32)) * LOG2E

    for kind in range(3):
        for j in range(NA_K_ROWS):
            wins = [_na_window(kind, j, i, grid_rows) for i in range(NA_Q_ROWS)]
            blk = jnp.full(shape, NEG_BIG, F32)
            for i, (inside, dr) in enumerate(wins):
                if inside:
                    blk = jnp.where((grp == i) & col_ok, t_scr[dr], blk)
            o_ref[kind, 0, j * GRID_W:(j + 1) * GRID_W, :] = blk


def _na_bias(rpb_flat, grid_rows):
    kt, qt = NA_K_ROWS * GRID_W, NA_Q_ROWS * GRID_W
    return pl.pallas_call(
        functools.partial(_na_bias_kernel, grid_rows=grid_rows),
        grid=(4,),
        in_specs=[pl.BlockSpec(memory_space=pltpu.SMEM)],
        out_specs=pl.BlockSpec((3, 1, kt, qt), lambda h: (0, h, 0, 0)),
        out_shape=jax.ShapeDtypeStruct((3, 4, kt, qt), F32),
        scratch_shapes=[pltpu.VMEM((2 * NA_KH - 1, GRID_W, qt), F32)],
        name="na_bias",
    )(rpb_flat)


def _na_kernel(q_ref, k0_ref, k1_ref, k2_ref, v0_ref, v1_ref, v2_ref, bias_ref, beta_ref, o_ref):
    qt = NA_Q_ROWS * GRID_W
    k_refs = (k0_ref, k1_ref, k2_ref)
    v_refs = (v0_ref, v1_ref, v2_ref)

    def logits(hh):
        q = q_ref[0, hh]
        return [_dot(k_refs[j][0, hh], q) + bias_ref[0, hh, j * qt:(j + 1) * qt, :] for j in range(3)]

    outs = []
    ss_next = logits(0)
    for hh in range(4):
        ss = ss_next
        if hh + 1 < 4:
            ss_next = logits(hh + 1)
        m = functools.reduce(jnp.maximum, [_col_max(s) for s in ss])
        acc = None
        for j in range(3):
            p = jnp.exp2(ss[j] - m).astype(BF16)
            pv = _dot(v_refs[j][0, hh, 0], p)
            acc = pv if acc is None else acc + pv
        outs.append(_finish_head(acc))
    y = jnp.concatenate(outs, axis=0)
    o_ref[0] = _rms_rows(y, beta_ref[...], GROUP_W).astype(BF16)


def _na(qt_arr, k, vt, bias, beta_col):
    b, nh, d, s = qt_arr.shape
    qt = NA_Q_ROWS * GRID_W
    kt = NA_K_ROWS * GRID_W
    n_t = s // qt
    n_win = kt // qt

    def win(j):
        return lambda bi, ti: jnp.clip(ti - 1, 0, n_t - n_win) + j

    def kind(bi, ti):
        return (jnp.where(ti == 0, 0, jnp.where(ti == n_t - 1, 2, 1)), 0, 0, 0)

    k_specs = [pl.BlockSpec((1, nh, qt, d), (lambda j: lambda bi, ti: (bi, 0, win(j)(bi, ti), 0))(j))
               for j in range(n_win)]
    per = vt.shape[4] // qt
    v_specs = [pl.BlockSpec((1, nh, 1, V_ROWS, qt),
                            (lambda j: lambda bi, ti: (bi, 0, win(j)(bi, ti) // per, 0, win(j)(bi, ti) % per))(j))
               for j in range(n_win)]
    return pl.pallas_call(
        _na_kernel,
        grid=(b, n_t),
        in_specs=[pl.BlockSpec((1, nh, d, qt), lambda bi, ti: (bi, 0, 0, ti))] + k_specs + v_specs + [
            pl.BlockSpec((1, nh, kt, qt), kind),
            pl.BlockSpec((GROUP_W, 1), lambda bi, ti: (0, 0)),
        ],
        out_specs=pl.BlockSpec((1, GROUP_W, qt), lambda bi, ti: (bi, 0, ti)),
        out_shape=jax.ShapeDtypeStruct((b, GROUP_W, s), BF16),
        compiler_params=pltpu.CompilerParams(
            dimension_semantics=("parallel", "parallel"), vmem_limit_bytes=VMEM_LIMIT),
        name="na_attn",
    )(qt_arr, k, k, k, vt, vt, vt, bias, beta_col)


def _rope_tables(pos, dim):
    inv = jnp.exp(-math.log(ROPE_THETA) * jnp.arange(0, dim, 2, dtype=F32) / dim)
    ang = pos.astype(F32)[:, None] * inv[None, :]
    return jnp.cos(ang).T, jnp.sin(ang).T


def _col(v):
    return v.reshape(-1, 1).astype(F32)


def _lambda_init(layer):
    return 0.8 - 0.6 * math.exp(-0.3 * layer)


def kernel(x, ffn1_norm, ffn1_w_gate, ffn1_w_up, ffn1_w_down, mix_norm, w_in, na_q_norm, na_k_norm, na_rpb, na_beta, mla_q_lat_norm, mla_w_uq, mla_kv_lat_norm, mla_w_ukv, mla_q_norm, mla_k_norm, mla_beta, diff_q_norm, diff_k_norm, diff_lambda, diff_subln, gqa_q_norm, gqa_k_norm, gqa_beta, w_out, ffn2_norm, ffn2_w_gate, ffn2_w_up, ffn2_w_down, final_norm, t5_bias):
    batch, seq, d = x.shape
    grid_rows = seq // GRID_W
    pos = jnp.arange(seq, dtype=jnp.int32)
    tabs = (*_rope_tables(pos, 32), *_rope_tables(pos // GRID_W, 32), *_rope_tables(pos % GRID_W, 32))
    t5_flat = t5_bias.reshape(-1).astype(F32)
    t5_tiles = _t5_bias(t5_flat)

    xf = x.reshape(batch * seq, d)
    for l in range(N_LAYERS):
        xf = _ffn(xf, ffn1_norm[l], ffn1_w_gate[l].astype(BF16), ffn1_w_up[l].astype(BF16),
                  ffn1_w_down[l].astype(BF16))
        cols = tuple(_col(v[l]) for v in (na_q_norm, na_k_norm, mla_q_lat_norm, mla_kv_lat_norm,
                                          mla_q_norm, mla_k_norm, diff_q_norm, diff_k_norm,
                                          gqa_q_norm, gqa_k_norm))
        mats = (mla_w_uq[l].T.astype(BF16), mla_w_ukv[l].T.astype(BF16))
        (qa, ka, va, qb, kb, vb, qc, kc, vc, qd, kd, vd) = _proj(
            xf, batch, seq, mix_norm[l], w_in[l].T.astype(BF16), cols, mats, tabs)
        n_k = seq // FLASH_T
        ka = jnp.swapaxes(ka, 2, 3)
        kb = jnp.swapaxes(kb, 2, 3).reshape(batch, 4, n_k, FLASH_T, 96)
        kc = jnp.swapaxes(kc.reshape(batch, 4, 64, seq), 2, 3).reshape(batch, 4, n_k, FLASH_T, 64)
        kd = jnp.swapaxes(kd, 2, 3).reshape(batch, 2, n_k, FLASH_T, 64)

        ya = _na(qa, ka, va, _na_bias(na_rpb[l].reshape(-1).astype(F32), grid_rows), _col(na_beta[l]))
        yb = _flash_plain(qb, kb, vb, _col(mla_beta[l]), (0, 1, 2, 3))
        yc = _flash_diff(t5_flat, qc, kc, vc, t5_tiles, diff_lambda[l].astype(F32), _col(diff_subln[l]),
                         _lambda_init(l))
        yd = _flash_plain(qd, kd, vd, _col(gqa_beta[l]), (0, 0, 1, 1))

        xf = _ffn(xf, ffn2_norm[l], ffn2_w_gate[l].astype(BF16), ffn2_w_up[l].astype(BF16),
                  ffn2_w_down[l].astype(BF16), mix=((ya, yb, yc, yd), w_out[l].astype(BF16)),
                  final_g=final_norm[l], seq=seq)
    return xf.reshape(batch, seq, d)
```

```python
import functools
import math

import jax
import jax.numpy as jnp
from jax import lax
from jax.experimental import pallas as pl
from jax.experimental.pallas import tpu as pltpu

F32 = jnp.float32
BF16 = jnp.bfloat16

EPS = 1e-6
NEG_BIG = -1e30
LOG2E = 1.4426950408889634
ROPE_THETA = 10000.0
GRID_W = 64
N_LAYERS = 2

HEAD_V = 64
V_ROWS = 80
GROUP_W = 256
A_ROWS, B_ROWS, C_ROWS, D_ROWS = (0, 768), (768, 1184), (1184, 1952), (1952, 2464)

NA_KH, NA_KW = 8, 16
NA_Q_ROWS = 4
NA_K_ROWS = 12
T5_BUCKETS = 32
T5_MAX_DIST = 128

FFN_TM = 1024
FFN_CHUNK = 1024
FLASH_T = 512
FLASH_AHEAD = 2
FLASH_SLOTS = 4
FLASH_UNROLL = 8
FLASH_DIFF_UNROLL = 4
PROJ_TM = FLASH_T

VMEM_LIMIT = 56 * 1024 * 1024


def _rms_rows(x, g_col, n):
    r = lax.rsqrt(jnp.sum(x * x, axis=0, keepdims=True) * (1.0 / n) + EPS)
    return (x * r) * g_col


def _rms_lanes(x, g_row):
    r = lax.rsqrt(jnp.mean(x * x, axis=-1, keepdims=True) + EPS)
    return (x * r) * g_row


def _dot(a, b):
    return jnp.dot(a, b, preferred_element_type=F32)


def _ffn_kernel(*refs, has_mix, has_final, chunks):
    x_ref, g_ref, wg_ref, wu_ref, wd_ref = refs[:5]
    rest = refs[5:]
    if has_mix:
        y_refs, wo_ref, rest = rest[:4], rest[4], rest[5:]
    if has_final:
        gf_ref, rest = rest[0], rest[1:]
    (o_ref,) = rest

    x = x_ref[...]
    if has_mix:
        for g in range(4):
            x = x + lax.dot_general(
                y_refs[g][0], wo_ref[g * GROUP_W:(g + 1) * GROUP_W, :],
                (((0,), (0,)), ((), ())), preferred_element_type=F32)
    o_ref[...] = x
    h = _rms_lanes(x, g_ref[...]).astype(BF16)
    acc = None
    for lo, hi in chunks:
        gate = _dot(h, wg_ref[:, lo:hi])
        up = _dot(h, wu_ref[:, lo:hi])
        a = (gate / (1.0 + jnp.exp(-gate))) * up
        part = _dot(a.astype(BF16), wd_ref[lo:hi, :])
        acc = part if acc is None else acc + part
    out = o_ref[...] + 0.5 * acc
    if has_final:
        out = _rms_lanes(out, gf_ref[...])
    o_ref[...] = out


def _ffn(x, g, wg, wu, wd, mix=None, final_g=None, seq=None):
    n, d = x.shape
    d_ff = wg.shape[1]
    tm = FFN_TM
    chunks = tuple((lo, min(lo + FFN_CHUNK, d_ff)) for lo in range(0, d_ff, FFN_CHUNK))
    once = pl.Buffered(1)
    in_specs = [
        pl.BlockSpec((tm, d), lambda i: (i, 0)),
        pl.BlockSpec((1, d), lambda i: (0, 0)),
        pl.BlockSpec((d, d_ff), lambda i: (0, 0), pipeline_mode=once),
        pl.BlockSpec((d, d_ff), lambda i: (0, 0), pipeline_mode=once),
        pl.BlockSpec((d_ff, d), lambda i: (0, 0), pipeline_mode=once),
    ]
    args = [x, g.reshape(1, d), wg, wu, wd]
    if mix is not None:
        ys, wo = mix
        nsb = seq // tm
        for y in ys:
            in_specs.append(pl.BlockSpec((1, GROUP_W, tm), lambda i: (i // nsb, 0, i % nsb)))
            args.append(y)
        in_specs.append(pl.BlockSpec(wo.shape, lambda i: (0, 0), pipeline_mode=once))
        args.append(wo)
    if final_g is not None:
        in_specs.append(pl.BlockSpec((1, d), lambda i: (0, 0)))
        args.append(final_g.reshape(1, d))
    return pl.pallas_call(
        functools.partial(_ffn_kernel, has_mix=mix is not None, has_final=final_g is not None, chunks=chunks),
        grid=(n // tm,),
        in_specs=in_specs,
        out_specs=pl.BlockSpec((tm, d), lambda i: (i, 0)),
        out_shape=jax.ShapeDtypeStruct((n, d), F32),
        compiler_params=pltpu.CompilerParams(
            dimension_semantics=("parallel",), vmem_limit_bytes=VMEM_LIMIT),
        name="ffn_mix" if mix is not None else "ffn",
    )(*args)


def _rope_rows(x, cos, sin):
    x1, x2 = x[:16], x[16:]
    return x1 * cos - x2 * sin, x1 * sin + x2 * cos


def _store_v(v_ref, hh, v):
    v_ref[0, hh, 0, 0:HEAD_V, :] = v.astype(BF16)
    v_ref[0, hh, 0, HEAD_V:V_ROWS, :] = jnp.ones((V_ROWS - HEAD_V, v.shape[1]), BF16)


def _proj_kernel(x_ref, gmix_ref, wint_ref,
                 naq_ref, nak_ref,
                 qlat_ref, wuqt_ref, kvlat_ref, wukvt_ref, mq_ref, mk_ref,
                 dq_ref, dk_ref, gq_ref, gk_ref,
                 cseq_ref, sseq_ref, crow_ref, srow_ref, ccol_ref, scol_ref,
                 qa_o, ka_o, va_o, qb_o, kb_o, vb_o, qc_o, kc_o, vc_o, qd_o, kd_o, vd_o):
    h = _rms_lanes(x_ref[...], gmix_ref[...]).astype(BF16)

    def proj(lo, hi):
        return lax.dot_general(wint_ref[lo:hi, :], h, (((1,), (1,)), ((), ())),
                               preferred_element_type=F32)

    pa = proj(*A_ROWS)
    sa = LOG2E * 64 ** -0.5
    for hh in range(4):
        q = _rms_rows(pa[hh * 64:(hh + 1) * 64], naq_ref[...], 64) * sa
        k = _rms_rows(pa[256 + hh * 64:256 + (hh + 1) * 64], nak_ref[...], 64)
        qa_o[0, hh] = q.astype(BF16)
        ka_o[0, hh] = k.astype(BF16)
        _store_v(va_o, hh, pa[512 + hh * 64:512 + (hh + 1) * 64])

    pb = proj(*B_ROWS)
    cq = _rms_rows(pb[0:256], qlat_ref[...], 256).astype(BF16)
    qb = _dot(wuqt_ref[...], cq)
    ckv = _rms_rows(pb[256:384], kvlat_ref[...], 128).astype(BF16)
    kv = _dot(wukvt_ref[...], ckv)
    kr = pb[384:416]
    cseq, sseq = cseq_ref[...], sseq_ref[...]
    sb = LOG2E * 96 ** -0.5
    for hh in range(4):
        q = _rms_rows(qb[hh * 96:(hh + 1) * 96], mq_ref[...], 96) * sb
        r1, r2 = _rope_rows(q[64:96], cseq, sseq)
        qb_o[0, hh, 0:64, :] = q[0:64].astype(BF16)
        qb_o[0, hh, 64:80, :] = r1.astype(BF16)
        qb_o[0, hh, 80:96, :] = r2.astype(BF16)
        k = jnp.concatenate([kv[hh * 128:hh * 128 + 64], kr], axis=0)
        k = _rms_rows(k, mk_ref[...], 96)
        r1, r2 = _rope_rows(k[64:96], cseq, sseq)
        kb_o[0, hh, 0:64, :] = k[0:64].astype(BF16)
        kb_o[0, hh, 64:80, :] = r1.astype(BF16)
        kb_o[0, hh, 80:96, :] = r2.astype(BF16)
        _store_v(vb_o, hh, kv[hh * 128 + 64:hh * 128 + 128])

    pc = proj(*C_ROWS)
    sc = LOG2E * 32 ** -0.5
    for j in range(8):
        q = _rms_rows(pc[j * 32:(j + 1) * 32], dq_ref[...], 32) * sc
        k = _rms_rows(pc[256 + j * 32:256 + (j + 1) * 32], dk_ref[...], 32)
        qc_o[0, j] = q.astype(BF16)
        kc_o[0, j] = k.astype(BF16)
    for hh in range(4):
        _store_v(vc_o, hh, pc[512 + hh * 64:512 + (hh + 1) * 64])

    pd = proj(*D_ROWS)
    crow, srow, ccol, scol = crow_ref[...], srow_ref[...], ccol_ref[...], scol_ref[...]
    sd = LOG2E * 64 ** -0.5

    def axial(x, o_ref, hh):
        a1, a2 = _rope_rows(x[0:32], crow, srow)
        b1, b2 = _rope_rows(x[32:64], ccol, scol)
        o_ref[0, hh, 0:16, :] = a1.astype(BF16)
        o_ref[0, hh, 16:32, :] = a2.astype(BF16)
        o_ref[0, hh, 32:48, :] = b1.astype(BF16)
        o_ref[0, hh, 48:64, :] = b2.astype(BF16)

    for hh in range(4):
        axial(_rms_rows(pd[hh * 64:(hh + 1) * 64], gq_ref[...], 64) * sd, qd_o, hh)
    for hh in range(2):
        axial(_rms_rows(pd[256 + hh * 64:256 + (hh + 1) * 64], gk_ref[...], 64), kd_o, hh)
        _store_v(vd_o, hh, pd[384 + hh * 64:384 + (hh + 1) * 64])


def _proj(x, batch, seq, gmix, wint, cols, mats, tabs):
    n, d = x.shape
    tm = PROJ_TM
    nsb = seq // tm
    naq, nak, qlat, kvlat, mq, mk, dq, dk, gq, gk = cols
    wuqt, wukvt = mats

    def full(a):
        return pl.BlockSpec(a.shape, lambda i: (0,) * a.ndim)

    tab_spec = pl.BlockSpec((16, tm), lambda i: (0, i % nsb))
    args = [x, gmix.reshape(1, d), wint, naq, nak, qlat, wuqt, kvlat, wukvt, mq, mk, dq, dk, gq, gk, *tabs]
    in_specs = [pl.BlockSpec((tm, d), lambda i: (i, 0))] + [full(a) for a in args[1:15]] + [tab_spec] * 6

    def head_out(nh, rows):
        return (jax.ShapeDtypeStruct((batch, nh, rows, seq), BF16),
                pl.BlockSpec((1, nh, rows, tm), lambda i: (i // nsb, 0, 0, i % nsb)))

    def v_out(nh):
        return (jax.ShapeDtypeStruct((batch, nh, nsb, V_ROWS, tm), BF16),
                pl.BlockSpec((1, nh, 1, V_ROWS, tm), lambda i: (i // nsb, 0, i % nsb, 0, 0)))

    outs = [head_out(4, 64), head_out(4, 64), v_out(4),
            head_out(4, 96), head_out(4, 96), v_out(4),
            head_out(8, 32), head_out(8, 32), v_out(4),
            head_out(4, 64), head_out(2, 64), v_out(2)]
    return pl.pallas_call(
        _proj_kernel,
        grid=(n // tm,),
        in_specs=in_specs,
        out_specs=[o[1] for o in outs],
        out_shape=[o[0] for o in outs],
        compiler_params=pltpu.CompilerParams(
            dimension_semantics=("parallel",), vmem_limit_bytes=VMEM_LIMIT),
        name="mix_proj",
    )(*args)


def _col_max(s):
    parts = [s]
    while parts[0].shape[0] > 64:
        half = parts[0].shape[0] // 2
        parts = [jnp.maximum(p[:half], p[half:]) for p in parts]
    return jnp.max(parts[0], axis=0, keepdims=True)


def _produce(k_blk, q_blk, slot, s_scr, mc_scr, bias=None):
    s = _dot(k_blk, q_blk)
    if bias is not None:
        s = s + bias
    s_scr[slot] = s
    mc_scr[slot] = _col_max(s)


def _consume(s, m_cur, c, idx, v_blk, m_scr, acc_scr):
    m_prev = m_scr[idx]
    m_new = jnp.maximum(m_prev, m_cur + c)
    alpha = jnp.exp2(m_prev - m_new)
    p = jnp.exp2(s - (m_new - c)).astype(BF16)
    acc_scr[idx] = alpha * acc_scr[idx] + _dot(v_blk, p)
    m_scr[idx] = m_new


def _finish_head(acc):
    return acc[0:HEAD_V] / acc[HEAD_V:HEAD_V + 1]


def _flash_plain_kernel(q_ref, k_ref, v_ref, beta_ref, o_ref, s_scr, mc_scr, m_scr, acc_scr, *, kv_map, n_k):
    n_h = len(kv_map)
    m_scr[...] = jnp.full_like(m_scr, NEG_BIG)
    acc_scr[...] = jnp.zeros_like(acc_scr)

    def produce(i, hh):
        _produce(k_ref[0, kv_map[hh], i], q_ref[0, hh], hh % FLASH_SLOTS, s_scr, mc_scr)

    for hh in range(FLASH_AHEAD):
        produce(0, hh)

    def body(i, carry):
        nxt = jnp.minimum(i + 1, n_k - 1)
        for hh in range(n_h):
            ahead = hh + FLASH_AHEAD
            produce(i if ahead < n_h else nxt, ahead % n_h)
            slot = hh % FLASH_SLOTS
            _consume(s_scr[slot], mc_scr[slot], 0.0, hh, v_ref[0, kv_map[hh], i], m_scr, acc_scr)
        return carry

    lax.fori_loop(0, n_k, body, 0, unroll=FLASH_UNROLL)
    y = jnp.concatenate([_finish_head(acc_scr[hh]) for hh in range(n_h)], axis=0)
    o_ref[0] = _rms_rows(y, beta_ref[...], GROUP_W).astype(BF16)


def _flash_plain(qt, k, vt, beta_col, kv_map):
    b, hq, dq, s = qt.shape
    hk, n_k, t = k.shape[1], k.shape[2], k.shape[3]
    assert len(kv_map) % FLASH_SLOTS == 0
    return pl.pallas_call(
        functools.partial(_flash_plain_kernel, kv_map=kv_map, n_k=n_k),
        grid=(b, s // t),
        in_specs=[
            pl.BlockSpec((1, hq, dq, t), lambda bi, qi: (bi, 0, 0, qi)),
            pl.BlockSpec((1, hk, n_k, t, dq), lambda bi, qi: (bi, 0, 0, 0, 0)),
            pl.BlockSpec((1, hk, n_k, V_ROWS, t), lambda bi, qi: (bi, 0, 0, 0, 0)),
            pl.BlockSpec((GROUP_W, 1), lambda bi, qi: (0, 0)),
        ],
        out_specs=pl.BlockSpec((1, GROUP_W, t), lambda bi, qi: (bi, 0, qi)),
        out_shape=jax.ShapeDtypeStruct((b, GROUP_W, s), BF16),
        scratch_shapes=[pltpu.VMEM((FLASH_SLOTS, t, t), F32), pltpu.VMEM((FLASH_SLOTS, 1, t), F32),
                        pltpu.VMEM((hq, 1, t), F32), pltpu.VMEM((hq, V_ROWS, t), F32)],
        compiler_params=pltpu.CompilerParams(
            dimension_semantics=("parallel", "parallel"), vmem_limit_bytes=VMEM_LIMIT),
        name="flash_plain",
    )(qt, k, vt, beta_col)


def _flash_diff_kernel(t5_ref, q_ref, k_ref, v_ref, diag_ref, corner_ref, span_ref, lam_ref, subln_ref, o_ref,
                       qpad_scr, s_scr, mc_scr, m_scr, acc_scr, *, lambda_init, n_k):
    qi = pl.program_id(1)
    n_m = 8
    dq = q_ref.shape[2]
    m_scr[...] = jnp.full_like(m_scr, NEG_BIG)
    acc_scr[...] = jnp.zeros_like(acc_scr)
    qpad_scr[...] = jnp.zeros_like(qpad_scr)
    for j in range(n_m):
        qpad_scr[j, (j % 2) * dq:(j % 2 + 1) * dq, :] = q_ref[0, j]

    def produce(i, j, bias=None):
        _produce(k_ref[0, j // 2, i], qpad_scr[j], j % FLASH_SLOTS, s_scr, mc_scr, bias)

    tq = q_ref.shape[3]
    far_left, far_right = T5_BUCKETS // 2 - 1, T5_BUCKETS - 1
    has_left, has_right = qi >= 1, qi <= n_k - 2
    left = jnp.where(has_left, qi - 1, 2)
    right = jnp.where(has_right, qi + 1, n_k - 3)
    w0 = jnp.clip(qi - 1, 0, n_k - 3)
    n_rest = n_k - 3

    def rest(t):
        return jnp.where(t < w0, t, t + 3)

    patch_max = span_ref[0:1, 0:1]

    for j in range(FLASH_AHEAD):
        produce(qi, j, diag_ref[j // 2])

    def visit(i, nxt, mode, flag=None):
        for j in range(n_m):
            hh = j // 2
            ahead = j + FLASH_AHEAD
            if ahead >= n_m:
                produce(nxt, ahead % n_m)
            elif mode == "diag":
                produce(i, ahead, diag_ref[ahead // 2])
            else:
                produce(i, ahead)
            slot = j % FLASH_SLOTS
            if mode == "diag":
                _consume(s_scr[slot], mc_scr[slot], 0.0, j, v_ref[0, hh, i], m_scr, acc_scr)
                continue
            c = jnp.where(i < qi, t5_ref[far_left * 4 + hh], t5_ref[far_right * 4 + hh]) * LOG2E
            m_cur = mc_scr[slot]
            if mode in ("left", "right"):
                side, bucket = (0, far_left) if mode == "left" else (1, far_right)
                rows = slice(tq - T5_MAX_DIST, tq) if mode == "left" else slice(0, T5_MAX_DIST)
                cols = slice(0, T5_MAX_DIST) if mode == "left" else slice(tq - T5_MAX_DIST, tq)
                corner = (corner_ref[side, hh] - t5_ref[bucket * 4 + hh] * LOG2E) * flag
                s_scr[slot, rows, cols] = s_scr[slot, rows, cols] + corner
                m_cur = m_cur + patch_max * flag
            _consume(s_scr[slot], m_cur, c, j, v_ref[0, hh, i], m_scr, acc_scr)

    visit(qi, left, "diag")
    visit(left, right, "left", has_left.astype(F32))
    visit(right, rest(0), "right", has_right.astype(F32))

    def far_group(g, carry):
        for u in range(FLASH_DIFF_UNROLL):
            t = g * FLASH_DIFF_UNROLL + u
            visit(rest(t), rest(jnp.minimum(t + 1, n_rest - 1)), "far")
        return carry

    def far_body(t, carry):
        visit(rest(t), rest(jnp.minimum(t + 1, n_rest - 1)), "far")
        return carry

    n_groups = n_rest // FLASH_DIFF_UNROLL
    lax.fori_loop(0, n_groups, far_group, 0)
    lax.fori_loop(n_groups * FLASH_DIFF_UNROLL, n_rest, far_body, 0)

    lp = lam_ref[...]
    lam = (jnp.exp(jnp.sum(lp[0:1] * lp[1:2], axis=1, keepdims=True))
           - jnp.exp(jnp.sum(lp[2:3] * lp[3:4], axis=1, keepdims=True)) + lambda_init)
    for hh in range(4):
        y = _finish_head(acc_scr[2 * hh]) - lam * _finish_head(acc_scr[2 * hh + 1])
        y = _rms_rows(y, subln_ref[...], HEAD_V) * (1.0 - lambda_init)
        o_ref[0, hh * HEAD_V:(hh + 1) * HEAD_V, :] = y.astype(BF16)


def _flash_diff(t5_flat, qt, k, vt, bias_tiles, lam_params, subln_col, lambda_init):
    diag, corner, span = bias_tiles
    assert k.shape[2] >= 4, "the diagonal window (3 tiles) plus at least one other key tile"
    b, hq, dq, s = qt.shape
    n_k, t = k.shape[2], k.shape[3]
    return pl.pallas_call(
        functools.partial(_flash_diff_kernel, lambda_init=lambda_init, n_k=n_k),
        grid=(b, s // t),
        in_specs=[
            pl.BlockSpec(memory_space=pltpu.SMEM),
            pl.BlockSpec((1, hq, dq, t), lambda bi, qi: (bi, 0, 0, qi)),
            pl.BlockSpec((1, 4, n_k, t, 2 * dq), lambda bi, qi: (bi, 0, 0, 0, 0)),
            pl.BlockSpec((1, 4, n_k, V_ROWS, t), lambda bi, qi: (bi, 0, 0, 0, 0)),
            pl.BlockSpec(diag.shape, lambda bi, qi: (0, 0, 0), pipeline_mode=pl.Buffered(1)),
            pl.BlockSpec(corner.shape, lambda bi, qi: (0, 0, 0, 0)),
            pl.BlockSpec(span.shape, lambda bi, qi: (0, 0)),
            pl.BlockSpec(lam_params.shape, lambda bi, qi: (0, 0)),
            pl.BlockSpec((HEAD_V, 1), lambda bi, qi: (0, 0)),
        ],
        out_specs=pl.BlockSpec((1, GROUP_W, t), lambda bi, qi: (bi, 0, qi)),
        out_shape=jax.ShapeDtypeStruct((b, GROUP_W, s), BF16),
        scratch_shapes=[pltpu.VMEM((hq, 2 * dq, t), BF16),
                        pltpu.VMEM((FLASH_SLOTS, t, t), F32), pltpu.VMEM((FLASH_SLOTS, 1, t), F32),
                        pltpu.VMEM((hq, 1, t), F32), pltpu.VMEM((hq, V_ROWS, t), F32)],
        compiler_params=pltpu.CompilerParams(
            dimension_semantics=("parallel", "parallel"), vmem_limit_bytes=VMEM_LIMIT),
        name="flash_diff",
    )(t5_flat, qt, k, vt, diag, corner, span, lam_params, subln_col)


def _t5_bias_of(rel, tab_ref):
    half = T5_BUCKETS // 2
    max_exact = half // 2
    n = jnp.abs(rel)
    large = max_exact + (jnp.log(jnp.maximum(n, 1).astype(F32) / max_exact)
                         / math.log(T5_MAX_DIST / max_exact) * (half - max_exact)).astype(jnp.int32)
    large = jnp.minimum(large, half - 1)
    bucket = jnp.where(rel > 0, half, 0) + jnp.where(n < max_exact, n, large)
    accs = [jnp.zeros(rel.shape, F32) for _ in range(4)]
    for bkt in range(T5_BUCKETS):
        hit = bucket == bkt
        accs = [jnp.where(hit, tab_ref[bkt * 4 + hh], accs[hh]) for hh in range(4)]
    return [a * LOG2E for a in accs]


def _t5_bias_kernel(tab_ref, diag_ref, corner_ref, span_ref, *, t):
    step = pl.program_id(0)
    rows = 16
    c = T5_MAX_DIST

    @pl.when(step == 0)
    def _():
        top = lax.fori_loop(0, T5_BUCKETS * 4, lambda e, m: jnp.maximum(m, jnp.abs(tab_ref[e])), 0.0)
        span_ref[...] = jnp.full(span_ref.shape, 2.0 * LOG2E * top, F32)

        def body(r, carry):
            kk = lax.broadcasted_iota(jnp.int32, (rows, t), 0) + r * rows
            qq = lax.broadcasted_iota(jnp.int32, (rows, t), 1)
            for hh, b in enumerate(_t5_bias_of(kk - qq, tab_ref)):
                diag_ref[hh, pl.ds(pl.multiple_of(r * rows, rows), rows), :] = b
            return carry

        lax.fori_loop(0, t // rows, body, 0)

    @pl.when(step > 0)
    def _():
        base = jnp.where(step == 1, -c, c)

        def body(r, carry):
            kk = lax.broadcasted_iota(jnp.int32, (rows, c), 0) + r * rows
            qq = lax.broadcasted_iota(jnp.int32, (rows, c), 1)
            for hh, b in enumerate(_t5_bias_of(base + kk - qq, tab_ref)):
                corner_ref[0, hh, pl.ds(pl.multiple_of(r * rows, rows), rows), :] = b
            return carry

        lax.fori_loop(0, c // rows, body, 0)


def _t5_bias(t5_flat):
    t, c = FLASH_T, T5_MAX_DIST
    return pl.pallas_call(
        functools.partial(_t5_bias_kernel, t=t),
        grid=(3,),
        in_specs=[pl.BlockSpec(memory_space=pltpu.SMEM)],
        out_specs=[pl.BlockSpec((4, t, t), lambda d: (0, 0, 0)),
                   pl.BlockSpec((1, 4, c, c), lambda d: (jnp.maximum(d - 1, 0), 0, 0, 0)),
                   pl.BlockSpec((8, 128), lambda d: (0, 0))],
        out_shape=[jax.ShapeDtypeStruct((4, t, t), F32), jax.ShapeDtypeStruct((2, 4, c, c), F32),
                   jax.ShapeDtypeStruct((8, 128), F32)],
        compiler_params=pltpu.CompilerParams(dimension_semantics=("arbitrary",)),
        name="t5_bias",
    )(t5_flat)


def _na_window(kind, j, i, grid_rows):
    r0 = (0, NA_Q_ROWS, grid_rows - NA_Q_ROWS)[kind]
    start = (0, 0, grid_rows - NA_K_ROWS)[kind]
    krow, qrow = start + j, r0 + i
    lo = min(max(qrow - NA_KH // 2, 0), grid_rows - NA_KH)
    return lo <= krow < lo + NA_KH, krow - qrow + NA_KH - 1


def _na_bias_kernel(rpb_ref, o_ref, t_scr, *, grid_rows):
    hh = pl.program_id(0)
    n_dr, n_dc = 2 * NA_KH - 1, 2 * NA_KW - 1
    shape = (GRID_W, NA_Q_ROWS * GRID_W)
    lane = lax.broadcasted_iota(jnp.int32, shape, 1)
    kc = lax.broadcasted_iota(jnp.int32, shape, 0)
    qc = lane % GRID_W
    grp = lane // GRID_W
    dcm = jnp.clip(kc - qc + NA_KW - 1, 0, n_dc - 1)
    qs = jnp.clip(qc - NA_KW // 2, 0, GRID_W - NA_KW)
    col_ok = (kc >= qs) & (kc < qs + NA_KW)

    for dr in range(n_dr):
        base = (hh * n_dr + dr) * n_dc
        acc = jnp.zeros(shape, F32)
        for dc in range(n_dc):
            acc = jnp.where(dcm == dc, rpb_ref[base + dc], acc)
        t_scr[dr] = acc * LOG2E

    for kind in range(3):
        for j in range(NA_K_ROWS):
            wins = [_na_window(kind, j, i, grid_rows) for i in range(NA_Q_ROWS)]
            blk = jnp.full(shape, NEG_BIG, F32)
            for i, (inside, dr) in enumerate(wins):
                if inside:
                    blk = jnp.where((grp == i) & col_ok, t_scr[dr], blk)
            o_ref[kind, 0, j * GRID_W:(j + 1) * GRID_W, :] = blk


def _na_bias(rpb_flat, grid_rows):
    kt, qt = NA_K_ROWS * GRID_W, NA_Q_ROWS * GRID_W
    return pl.pallas_call(
        functools.partial(_na_bias_kernel, grid_rows=grid_rows),
        grid=(4,),
        in_specs=[pl.BlockSpec(memory_space=pltpu.SMEM)],
        out_specs=pl.BlockSpec((3, 1, kt, qt), lambda h: (0, h, 0, 0)),
        out_shape=jax.ShapeDtypeStruct((3, 4, kt, qt), F32),
        scratch_shapes=[pltpu.VMEM((2 * NA_KH - 1, GRID_W, qt), F32)],
        name="na_bias",
    )(rpb_flat)


def _na_kernel(q_ref, k0_ref, k1_ref, k2_ref, v0_ref, v1_ref, v2_ref, bias_ref, beta_ref, o_ref):
    qt = NA_Q_ROWS * GRID_W
    k_refs = (k0_ref, k1_ref, k2_ref)
    v_refs = (v0_ref, v1_ref, v2_ref)

    def logits(hh):
        q = q_ref[0, hh]
        return [_dot(k_refs[j][0, hh], q) + bias_ref[0, hh, j * qt:(j + 1) * qt, :] for j in range(3)]

    outs = []
    ss_next = logits(0)
    for hh in range(4):
        ss = ss_next
        if hh + 1 < 4:
            ss_next = logits(hh + 1)
        m = functools.reduce(jnp.maximum, [_col_max(s) for s in ss])
        acc = None
        for j in range(3):
            p = jnp.exp2(ss[j] - m).astype(BF16)
            pv = _dot(v_refs[j][0, hh, 0], p)
            acc = pv if acc is None else acc + pv
        outs.append(_finish_head(acc))
    y = jnp.concatenate(outs, axis=0)
    o_ref[0] = _rms_rows(y, beta_ref[...], GROUP_W).astype(BF16)


def _na(qt_arr, k, vt, bias, beta_col):
    b, nh, d, s = qt_arr.shape
    qt = NA_Q_ROWS * GRID_W
    kt = NA_K_ROWS * GRID_W
    n_t = s // qt
    n_win = kt // qt

    def win(j):
        return lambda bi, ti: jnp.clip(ti - 1, 0, n_t - n_win) + j

    def kind(bi, ti):
        return (jnp.where(ti == 0, 0, jnp.where(ti == n_t - 1, 2, 1)), 0, 0, 0)

    k_specs = [pl.BlockSpec((1, nh, qt, d), (lambda j: lambda bi, ti: (bi, 0, win(j)(bi, ti), 0))(j))
               for j in range(n_win)]
    per = vt.shape[4] // qt
    v_specs = [pl.BlockSpec((1, nh, 1, V_ROWS, qt),
                            (lambda j: lambda bi, ti: (bi, 0, win(j)(bi, ti) // per, 0, win(j)(bi, ti) % per))(j))
               for j in range(n_win)]
    return pl.pallas_call(
        _na_kernel,
        grid=(b, n_t),
        in_specs=[pl.BlockSpec((1, nh, d, qt), lambda bi, ti: (bi, 0, 0, ti))] + k_specs + v_specs + [
            pl.BlockSpec((1, nh, kt, qt), kind),
            pl.BlockSpec((GROUP_W, 1), lambda bi, ti: (0, 0)),
        ],
        out_specs=pl.BlockSpec((1, GROUP_W, qt), lambda bi, ti: (bi, 0, ti)),
        out_shape=jax.ShapeDtypeStruct((b, GROUP_W, s), BF16),
        compiler_params=pltpu.CompilerParams(
            dimension_semantics=("parallel", "parallel"), vmem_limit_bytes=VMEM_LIMIT),
        name="na_attn",
    )(qt_arr, k, k, k, vt, vt, vt, bias, beta_col)


def _rope_tables(pos, dim):
    inv = jnp.exp(-math.log(ROPE_THETA) * jnp.arange(0, dim, 2, dtype=F32) / dim)
    ang = pos.astype(F32)[:, None] * inv[None, :]
    return jnp.cos(ang).T, jnp.sin(ang).T


def _col(v):
    return v.reshape(-1, 1).astype(F32)


def _lambda_init(layer):
    return 0.8 - 0.6 * math.exp(-0.3 * layer)


def kernel(x, ffn1_norm, ffn1_w_gate, ffn1_w_up, ffn1_w_down, mix_norm, w_in, na_q_norm, na_k_norm, na_rpb, na_beta, mla_q_lat_norm, mla_w_uq, mla_kv_lat_norm, mla_w_ukv, mla_q_norm, mla_k_norm, mla_beta, diff_q_norm, diff_k_norm, diff_lambda, diff_subln, gqa_q_norm, gqa_k_norm, gqa_beta, w_out, ffn2_norm, ffn2_w_gate, ffn2_w_up, ffn2_w_down, final_norm, t5_bias):
    batch, seq, d = x.shape
    grid_rows = seq // GRID_W
    pos = jnp.arange(seq, dtype=jnp.int32)
    tabs = (*_rope_tables(pos, 32), *_rope_tables(pos // GRID_W, 32), *_rope_tables(pos % GRID_W, 32))
    t5_flat = t5_bias.reshape(-1).astype(F32)
    t5_tiles = _t5_bias(t5_flat)

    xf = x.reshape(batch * seq, d)
    for l in range(N_LAYERS):
        xf = _ffn(xf, ffn1_norm[l], ffn1_w_gate[l].astype(BF16), ffn1_w_up[l].astype(BF16),
                  ffn1_w_down[l].astype(BF16))
        cols = tuple(_col(v[l]) for v in (na_q_norm, na_k_norm, mla_q_lat_norm, mla_kv_lat_norm,
                                          mla_q_norm, mla_k_norm, diff_q_norm, diff_k_norm,
                                          gqa_q_norm, gqa_k_norm))
        mats = (mla_w_uq[l].T.astype(BF16), mla_w_ukv[l].T.astype(BF16))
        (qa, ka, va, qb, kb, vb, qc, kc, vc, qd, kd, vd) = _proj(
            xf, batch, seq, mix_norm[l], w_in[l].T.astype(BF16), cols, mats, tabs)
        n_k = seq // FLASH_T
        ka = jnp.swapaxes(ka, 2, 3)
        kb = jnp.swapaxes(kb, 2, 3).reshape(batch, 4, n_k, FLASH_T, 96)
        kc = jnp.swapaxes(kc.reshape(batch, 4, 64, seq), 2, 3).reshape(batch, 4, n_k, FLASH_T, 64)
        kd = jnp.swapaxes(kd, 2, 3).reshape(batch, 2, n_k, FLASH_T, 64)

        ya = _na(qa, ka, va, _na_bias(na_rpb[l].reshape(-1).astype(F32), grid_rows), _col(na_beta[l]))
        yb = _flash_plain(qb, kb, vb, _col(mla_beta[l]), (0, 1, 2, 3))
        yc = _flash_diff(t5_flat, qc, kc, vc, t5_tiles, diff_lambda[l].astype(F32), _col(diff_subln[l]),
                         _lambda_init(l))
        yd = _flash_plain(qd, kd, vd, _col(gqa_beta[l]), (0, 0, 1, 1))

        xf = _ffn(xf, ffn2_norm[l], ffn2_w_gate[l].astype(BF16), ffn2_w_up[l].astype(BF16),
                  ffn2_w_down[l].astype(BF16), mix=((ya, yb, yc, yd), w_out[l].astype(BF16)),
                  final_g=final_norm[l], seq=seq)
    return xf.reshape(batch, seq, d)
```

```python
import functools
import math

import jax
import jax.numpy as jnp
from jax import lax
from jax.experimental import pallas as pl
from jax.experimental.pallas import tpu as pltpu

F32 = jnp.float32
BF16 = jnp.bfloat16

EPS = 1e-6
NEG_BIG = -1e30
LOG2E = 1.4426950408889634
ROPE_THETA = 10000.0
GRID_W = 64
N_LAYERS = 2

HEAD_V = 64
V_ROWS = 80
GROUP_W = 256
A_ROWS, B_ROWS, C_ROWS, D_ROWS = (0, 768), (768, 1184), (1184, 1952), (1952, 2464)

NA_KH, NA_KW = 8, 16
NA_Q_ROWS = 4
NA_K_ROWS = 12
T5_BUCKETS = 32
T5_MAX_DIST = 128

FFN_TM = 1024
FFN_CHUNK = 1024
FLASH_T = 512
FLASH_AHEAD = 2
FLASH_SLOTS = 4
FLASH_UNROLL = 8
FLASH_DIFF_UNROLL = 4
PROJ_TM = FLASH_T
PROJ_SPLIT = 2

VMEM_LIMIT = 56 * 1024 * 1024


def _rms_rows(x, g_col, n):
    r = lax.rsqrt(jnp.sum(x * x, axis=0, keepdims=True) * (1.0 / n) + EPS)
    return (x * r) * g_col


def _rms_lanes(x, g_row):
    r = lax.rsqrt(jnp.mean(x * x, axis=-1, keepdims=True) + EPS)
    return (x * r) * g_row


def _dot(a, b):
    return jnp.dot(a, b, preferred_element_type=F32)


def _ffn_kernel(*refs, has_mix, has_final, chunks):
    x_ref, g_ref, wg_ref, wu_ref, wd_ref = refs[:5]
    rest = refs[5:]
    if has_mix:
        y_refs, wo_ref, rest = rest[:4], rest[4], rest[5:]
    if has_final:
        gf_ref, rest = rest[0], rest[1:]
    (o_ref,) = rest

    x = x_ref[...]
    if has_mix:
        for g in range(4):
            x = x + lax.dot_general(
                y_refs[g][0], wo_ref[g * GROUP_W:(g + 1) * GROUP_W, :],
                (((0,), (0,)), ((), ())), preferred_element_type=F32)
    o_ref[...] = x
    h = _rms_lanes(x, g_ref[...]).astype(BF16)
    acc = None
    for lo, hi in chunks:
        gate = _dot(h, wg_ref[:, lo:hi])
        up = _dot(h, wu_ref[:, lo:hi])
        a = (gate / (1.0 + jnp.exp(-gate))) * up
        part = _dot(a.astype(BF16), wd_ref[lo:hi, :])
        acc = part if acc is None else acc + part
    out = o_ref[...] + 0.5 * acc
    if has_final:
        out = _rms_lanes(out, gf_ref[...])
    o_ref[...] = out


def _ffn(x, g, wg, wu, wd, mix=None, final_g=None, seq=None):
    n, d = x.shape
    d_ff = wg.shape[1]
    tm = FFN_TM
    chunks = tuple((lo, min(lo + FFN_CHUNK, d_ff)) for lo in range(0, d_ff, FFN_CHUNK))
    once = pl.Buffered(1)
    in_specs = [
        pl.BlockSpec((tm, d), lambda i: (i, 0)),
        pl.BlockSpec((1, d), lambda i: (0, 0)),
        pl.BlockSpec((d, d_ff), lambda i: (0, 0), pipeline_mode=once),
        pl.BlockSpec((d, d_ff), lambda i: (0, 0), pipeline_mode=once),
        pl.BlockSpec((d_ff, d), lambda i: (0, 0), pipeline_mode=once),
    ]
    args = [x, g.reshape(1, d), wg, wu, wd]
    if mix is not None:
        ys, wo = mix
        nsb = seq // tm
        for y in ys:
            in_specs.append(pl.BlockSpec((1, GROUP_W, tm), lambda i: (i // nsb, 0, i % nsb)))
            args.append(y)
        in_specs.append(pl.BlockSpec(wo.shape, lambda i: (0, 0), pipeline_mode=once))
        args.append(wo)
    if final_g is not None:
        in_specs.append(pl.BlockSpec((1, d), lambda i: (0, 0)))
        args.append(final_g.reshape(1, d))
    return pl.pallas_call(
        functools.partial(_ffn_kernel, has_mix=mix is not None, has_final=final_g is not None, chunks=chunks),
        grid=(n // tm,),
        in_specs=in_specs,
        out_specs=pl.BlockSpec((tm, d), lambda i: (i, 0)),
        out_shape=jax.ShapeDtypeStruct((n, d), F32),
        compiler_params=pltpu.CompilerParams(
            dimension_semantics=("parallel",), vmem_limit_bytes=VMEM_LIMIT),
        name="ffn_mix" if mix is not None else "ffn",
    )(*args)


def _rope_rows(x, cos, sin):
    x1, x2 = x[:16], x[16:]
    return x1 * cos - x2 * sin, x1 * sin + x2 * cos


def _store_v(v_ref, hh, v, ls):
    v_ref[0, hh, 0, 0:HEAD_V, ls] = v.astype(BF16)
    v_ref[0, hh, 0, HEAD_V:V_ROWS, ls] = jnp.ones((V_ROWS - HEAD_V, v.shape[1]), BF16)


def _proj_kernel(x_ref, gmix_ref, wint_ref,
                 naq_ref, nak_ref,
                 qlat_ref, wuqt_ref, kvlat_ref, wukvt_ref, mq_ref, mk_ref,
                 dq_ref, dk_ref, gq_ref, gk_ref,
                 cseq_ref, sseq_ref, crow_ref, srow_ref, ccol_ref, scol_ref,
                 qa_o, ka_o, va_o, qb_o, kb_o, vb_o, qc_o, kc_o, vc_o, qd_o, kd_o, vd_o):
    width = x_ref.shape[0] // PROJ_SPLIT
    for part in range(PROJ_SPLIT):
        ls = slice(part * width, (part + 1) * width)
        _proj_part(x_ref[ls, :], ls, gmix_ref, wint_ref, naq_ref, nak_ref, qlat_ref, wuqt_ref, kvlat_ref, wukvt_ref,
                   mq_ref, mk_ref, dq_ref, dk_ref, gq_ref, gk_ref,
                   (cseq_ref[:, ls], sseq_ref[:, ls], crow_ref[:, ls], srow_ref[:, ls], ccol_ref[:, ls], scol_ref[:, ls]),
                   (qa_o, ka_o, va_o, qb_o, kb_o, vb_o, qc_o, kc_o, vc_o, qd_o, kd_o, vd_o))


def _proj_part(x, ls, gmix_ref, wint_ref, naq_ref, nak_ref, qlat_ref, wuqt_ref, kvlat_ref, wukvt_ref,
               mq_ref, mk_ref, dq_ref, dk_ref, gq_ref, gk_ref, tabs, outs):
    cseq, sseq, crow, srow, ccol, scol = tabs
    qa_o, ka_o, va_o, qb_o, kb_o, vb_o, qc_o, kc_o, vc_o, qd_o, kd_o, vd_o = outs
    h = _rms_lanes(x, gmix_ref[...]).astype(BF16)

    def proj(lo, hi):
        return lax.dot_general(wint_ref[lo:hi, :], h, (((1,), (1,)), ((), ())),
                               preferred_element_type=F32)

    pa = proj(*A_ROWS)
    sa = LOG2E * 64 ** -0.5
    for hh in range(4):
        q = _rms_rows(pa[hh * 64:(hh + 1) * 64], naq_ref[...], 64) * sa
        k = _rms_rows(pa[256 + hh * 64:256 + (hh + 1) * 64], nak_ref[...], 64)
        qa_o[0, hh, :, ls] = q.astype(BF16)
        ka_o[0, hh, :, ls] = k.astype(BF16)
        _store_v(va_o, hh, pa[512 + hh * 64:512 + (hh + 1) * 64], ls)

    pb = proj(*B_ROWS)
    cq = _rms_rows(pb[0:256], qlat_ref[...], 256).astype(BF16)
    qb = _dot(wuqt_ref[...], cq)
    ckv = _rms_rows(pb[256:384], kvlat_ref[...], 128).astype(BF16)
    kv = _dot(wukvt_ref[...], ckv)
    kr = pb[384:416]
    sb = LOG2E * 96 ** -0.5
    for hh in range(4):
        q = _rms_rows(qb[hh * 96:(hh + 1) * 96], mq_ref[...], 96) * sb
        r1, r2 = _rope_rows(q[64:96], cseq, sseq)
        qb_o[0, hh, 0:64, ls] = q[0:64].astype(BF16)
        qb_o[0, hh, 64:80, ls] = r1.astype(BF16)
        qb_o[0, hh, 80:96, ls] = r2.astype(BF16)
        k = jnp.concatenate([kv[hh * 128:hh * 128 + 64], kr], axis=0)
        k = _rms_rows(k, mk_ref[...], 96)
        r1, r2 = _rope_rows(k[64:96], cseq, sseq)
        kb_o[0, hh, 0:64, ls] = k[0:64].astype(BF16)
        kb_o[0, hh, 64:80, ls] = r1.astype(BF16)
        kb_o[0, hh, 80:96, ls] = r2.astype(BF16)
        _store_v(vb_o, hh, kv[hh * 128 + 64:hh * 128 + 128], ls)

    pc = proj(*C_ROWS)
    sc = LOG2E * 32 ** -0.5
    for j in range(8):
        q = _rms_rows(pc[j * 32:(j + 1) * 32], dq_ref[...], 32) * sc
        k = _rms_rows(pc[256 + j * 32:256 + (j + 1) * 32], dk_ref[...], 32)
        qc_o[0, j, :, ls] = q.astype(BF16)
        kc_o[0, j, :, ls] = k.astype(BF16)
    for hh in range(4):
        _store_v(vc_o, hh, pc[512 + hh * 64:512 + (hh + 1) * 64], ls)

    pd = proj(*D_ROWS)
    sd = LOG2E * 64 ** -0.5

    def axial(x, o_ref, hh):
        a1, a2 = _rope_rows(x[0:32], crow, srow)
        b1, b2 = _rope_rows(x[32:64], ccol, scol)
        o_ref[0, hh, 0:16, ls] = a1.astype(BF16)
        o_ref[0, hh, 16:32, ls] = a2.astype(BF16)
        o_ref[0, hh, 32:48, ls] = b1.astype(BF16)
        o_ref[0, hh, 48:64, ls] = b2.astype(BF16)

    for hh in range(4):
        axial(_rms_rows(pd[hh * 64:(hh + 1) * 64], gq_ref[...], 64) * sd, qd_o, hh)
    for hh in range(2):
        axial(_rms_rows(pd[256 + hh * 64:256 + (hh + 1) * 64], gk_ref[...], 64), kd_o, hh)
        _store_v(vd_o, hh, pd[384 + hh * 64:384 + (hh + 1) * 64], ls)


def _proj(x, batch, seq, gmix, wint, cols, mats, tabs):
    n, d = x.shape
    tm = PROJ_TM
    nsb = seq // tm
    naq, nak, qlat, kvlat, mq, mk, dq, dk, gq, gk = cols
    wuqt, wukvt = mats

    def full(a):
        return pl.BlockSpec(a.shape, lambda i: (0,) * a.ndim)

    tab_spec = pl.BlockSpec((16, tm), lambda i: (0, i % nsb))
    args = [x, gmix.reshape(1, d), wint, naq, nak, qlat, wuqt, kvlat, wukvt, mq, mk, dq, dk, gq, gk, *tabs]
    in_specs = [pl.BlockSpec((tm, d), lambda i: (i, 0))] + [full(a) for a in args[1:15]] + [tab_spec] * 6

    def head_out(nh, rows):
        return (jax.ShapeDtypeStruct((batch, nh, rows, seq), BF16),
                pl.BlockSpec((1, nh, rows, tm), lambda i: (i // nsb, 0, 0, i % nsb)))

    def v_out(nh):
        return (jax.ShapeDtypeStruct((batch, nh, nsb, V_ROWS, tm), BF16),
                pl.BlockSpec((1, nh, 1, V_ROWS, tm), lambda i: (i // nsb, 0, i % nsb, 0, 0)))

    outs = [head_out(4, 64), head_out(4, 64), v_out(4),
            head_out(4, 96), head_out(4, 96), v_out(4),
            head_out(8, 32), head_out(8, 32), v_out(4),
            head_out(4, 64), head_out(2, 64), v_out(2)]
    return pl.pallas_call(
        _proj_kernel,
        grid=(n // tm,),
        in_specs=in_specs,
        out_specs=[o[1] for o in outs],
        out_shape=[o[0] for o in outs],
        compiler_params=pltpu.CompilerParams(
            dimension_semantics=("parallel",), vmem_limit_bytes=VMEM_LIMIT),
        name="mix_proj",
    )(*args)


def _col_max(s):
    parts = [s]
    while parts[0].shape[0] > 64:
        half = parts[0].shape[0] // 2
        parts = [jnp.maximum(p[:half], p[half:]) for p in parts]
    return jnp.max(parts[0], axis=0, keepdims=True)


def _produce(k_blk, q_blk, slot, s_scr, mc_scr, bias=None):
    s = _dot(k_blk, q_blk)
    if bias is not None:
        s = s + bias
    s_scr[slot] = s
    mc_scr[slot] = _col_max(s)


def _consume(s, m_cur, c, idx, v_blk, m_scr, acc_scr):
    m_prev = m_scr[idx]
    m_new = jnp.maximum(m_prev, m_cur + c)
    alpha = jnp.exp2(m_prev - m_new)
    p = jnp.exp2(s - (m_new - c)).astype(BF16)
    acc_scr[idx] = alpha * acc_scr[idx] + _dot(v_blk, p)
    m_scr[idx] = m_new


def _finish_head(acc):
    return acc[0:HEAD_V] / acc[HEAD_V:HEAD_V + 1]


def _flash_plain_kernel(q_ref, k_ref, v_ref, beta_ref, o_ref, s_scr, mc_scr, m_scr, acc_scr, *, kv_map, n_k):
    n_h = len(kv_map)
    m_scr[...] = jnp.full_like(m_scr, NEG_BIG)
    acc_scr[...] = jnp.zeros_like(acc_scr)

    def produce(i, hh):
        _produce(k_ref[0, kv_map[hh], i], q_ref[0, hh], hh % FLASH_SLOTS, s_scr, mc_scr)

    for hh in range(FLASH_AHEAD):
        produce(0, hh)

    def body(i, carry):
        nxt = jnp.minimum(i + 1, n_k - 1)
        for hh in range(n_h):
            ahead = hh + FLASH_AHEAD
            produce(i if ahead < n_h else nxt, ahead % n_h)
            slot = hh % FLASH_SLOTS
            _consume(s_scr[slot], mc_scr[slot], 0.0, hh, v_ref[0, kv_map[hh], i], m_scr, acc_scr)
        return carry

    lax.fori_loop(0, n_k, body, 0, unroll=FLASH_UNROLL)
    y = jnp.concatenate([_finish_head(acc_scr[hh]) for hh in range(n_h)], axis=0)
    o_ref[0] = _rms_rows(y, beta_ref[...], GROUP_W).astype(BF16)


def _flash_plain(qt, k, vt, beta_col, kv_map):
    b, hq, dq, s = qt.shape
    hk, n_k, t = k.shape[1], k.shape[2], k.shape[3]
    assert len(kv_map) % FLASH_SLOTS == 0
    return pl.pallas_call(
        functools.partial(_flash_plain_kernel, kv_map=kv_map, n_k=n_k),
        grid=(b, s // t),
        in_specs=[
            pl.BlockSpec((1, hq, dq, t), lambda bi, qi: (bi, 0, 0, qi)),
            pl.BlockSpec((1, hk, n_k, t, dq), lambda bi, qi: (bi, 0, 0, 0, 0)),
            pl.BlockSpec((1, hk, n_k, V_ROWS, t), lambda bi, qi: (bi, 0, 0, 0, 0)),
            pl.BlockSpec((GROUP_W, 1), lambda bi, qi: (0, 0)),
        ],
        out_specs=pl.BlockSpec((1, GROUP_W, t), lambda bi, qi: (bi, 0, qi)),
        out_shape=jax.ShapeDtypeStruct((b, GROUP_W, s), BF16),
        scratch_shapes=[pltpu.VMEM((FLASH_SLOTS, t, t), F32), pltpu.VMEM((FLASH_SLOTS, 1, t), F32),
                        pltpu.VMEM((hq, 1, t), F32), pltpu.VMEM((hq, V_ROWS, t), F32)],
        compiler_params=pltpu.CompilerParams(
            dimension_semantics=("parallel", "parallel"), vmem_limit_bytes=VMEM_LIMIT),
        name="flash_plain",
    )(qt, k, vt, beta_col)


def _flash_diff_kernel(t5_ref, q_ref, k_ref, v_ref, diag_ref, corner_ref, span_ref, lam_ref, subln_ref, o_ref,
                       qpad_scr, s_scr, mc_scr, m_scr, acc_scr, *, lambda_init, n_k):
    qi = pl.program_id(1)
    n_m = 8
    dq = q_ref.shape[2]
    m_scr[...] = jnp.full_like(m_scr, NEG_BIG)
    acc_scr[...] = jnp.zeros_like(acc_scr)
    qpad_scr[...] = jnp.zeros_like(qpad_scr)
    for j in range(n_m):
        qpad_scr[j, (j % 2) * dq:(j % 2 + 1) * dq, :] = q_ref[0, j]

    def produce(i, j, bias=None):
        _produce(k_ref[0, j // 2, i], qpad_scr[j], j % FLASH_SLOTS, s_scr, mc_scr, bias)

    tq = q_ref.shape[3]
    far_left, far_right = T5_BUCKETS // 2 - 1, T5_BUCKETS - 1
    has_left, has_right = qi >= 1, qi <= n_k - 2
    left = jnp.where(has_left, qi - 1, 2)
    right = jnp.where(has_right, qi + 1, n_k - 3)
    w0 = jnp.clip(qi - 1, 0, n_k - 3)
    n_rest = n_k - 3

    def rest(t):
        return jnp.where(t < w0, t, t + 3)

    patch_max = span_ref[0:1, 0:1]

    for j in range(FLASH_AHEAD):
        produce(qi, j, diag_ref[j // 2])

    def visit(i, nxt, mode, flag=None):
        for j in range(n_m):
            hh = j // 2
            ahead = j + FLASH_AHEAD
            if ahead >= n_m:
                produce(nxt, ahead % n_m)
            elif mode == "diag":
                produce(i, ahead, diag_ref[ahead // 2])
            else:
                produce(i, ahead)
            slot = j % FLASH_SLOTS
            if mode == "diag":
                _consume(s_scr[slot], mc_scr[slot], 0.0, j, v_ref[0, hh, i], m_scr, acc_scr)
                continue
            c = jnp.where(i < qi, t5_ref[far_left * 4 + hh], t5_ref[far_right * 4 + hh]) * LOG2E
            m_cur = mc_scr[slot]
            if mode in ("left", "right"):
                side, bucket = (0, far_left) if mode == "left" else (1, far_right)
                rows = slice(tq - T5_MAX_DIST, tq) if mode == "left" else slice(0, T5_MAX_DIST)
                cols = slice(0, T5_MAX_DIST) if mode == "left" else slice(tq - T5_MAX_DIST, tq)
                corner = (corner_ref[side, hh] - t5_ref[bucket * 4 + hh] * LOG2E) * flag
                s_scr[slot, rows, cols] = s_scr[slot, rows, cols] + corner
                m_cur = m_cur + patch_max * flag
            _consume(s_scr[slot], m_cur, c, j, v_ref[0, hh, i], m_scr, acc_scr)

    visit(qi, left, "diag")
    visit(left, right, "left", has_left.astype(F32))
    visit(right, rest(0), "right", has_right.astype(F32))

    def far_group(g, carry):
        for u in range(FLASH_DIFF_UNROLL):
            t = g * FLASH_DIFF_UNROLL + u
            visit(rest(t), rest(jnp.minimum(t + 1, n_rest - 1)), "far")
        return carry

    def far_body(t, carry):
        visit(rest(t), rest(jnp.minimum(t + 1, n_rest - 1)), "far")
        return carry

    n_groups = n_rest // FLASH_DIFF_UNROLL
    lax.fori_loop(0, n_groups, far_group, 0)
    lax.fori_loop(n_groups * FLASH_DIFF_UNROLL, n_rest, far_body, 0)

    lp = lam_ref[...]
    lam = (jnp.exp(jnp.sum(lp[0:1] * lp[1:2], axis=1, keepdims=True))
           - jnp.exp(jnp.sum(lp[2:3] * lp[3:4], axis=1, keepdims=True)) + lambda_init)
    for hh in range(4):
        y = _finish_head(acc_scr[2 * hh]) - lam * _finish_head(acc_scr[2 * hh + 1])
        y = _rms_rows(y, subln_ref[...], HEAD_V) * (1.0 - lambda_init)
        o_ref[0, hh * HEAD_V:(hh + 1) * HEAD_V, :] = y.astype(BF16)


def _flash_diff(t5_flat, qt, k, vt, bias_tiles, lam_params, subln_col, lambda_init):
    diag, corner, span = bias_tiles
    assert k.shape[2] >= 4, "the diagonal window (3 tiles) plus at least one other key tile"
    b, hq, dq, s = qt.shape
    n_k, t = k.shape[2], k.shape[3]
    return pl.pallas_call(
        functools.partial(_flash_diff_kernel, lambda_init=lambda_init, n_k=n_k),
        grid=(b, s // t),
        in_specs=[
            pl.BlockSpec(memory_space=pltpu.SMEM),
            pl.BlockSpec((1, hq, dq, t), lambda bi, qi: (bi, 0, 0, qi)),
            pl.BlockSpec((1, 4, n_k, t, 2 * dq), lambda bi, qi: (bi, 0, 0, 0, 0)),
            pl.BlockSpec((1, 4, n_k, V_ROWS, t), lambda bi, qi: (bi, 0, 0, 0, 0)),
            pl.BlockSpec(diag.shape, lambda bi, qi: (0, 0, 0), pipeline_mode=pl.Buffered(1)),
            pl.BlockSpec(corner.shape, lambda bi, qi: (0, 0, 0, 0)),
            pl.BlockSpec(span.shape, lambda bi, qi: (0, 0)),
            pl.BlockSpec(lam_params.shape, lambda bi, qi: (0, 0)),
            pl.BlockSpec((HEAD_V, 1), lambda bi, qi: (0, 0)),
        ],
        out_specs=pl.BlockSpec((1, GROUP_W, t), lambda bi, qi: (bi, 0, qi)),
        out_shape=jax.ShapeDtypeStruct((b, GROUP_W, s), BF16),
        scratch_shapes=[pltpu.VMEM((hq, 2 * dq, t), BF16),
                        pltpu.VMEM((FLASH_SLOTS, t, t), F32), pltpu.VMEM((FLASH_SLOTS, 1, t), F32),
                        pltpu.VMEM((hq, 1, t), F32), pltpu.VMEM((hq, V_ROWS, t), F32)],
        compiler_params=pltpu.CompilerParams(
            dimension_semantics=("parallel", "parallel"), vmem_limit_bytes=VMEM_LIMIT),
        name="flash_diff",
    )(t5_flat, qt, k, vt, diag, corner, span, lam_params, subln_col)


def _t5_bias_of(rel, tab_ref):
    half = T5_BUCKETS // 2
    max_exact = half // 2
    n = jnp.abs(rel)
    large = max_exact + (jnp.log(jnp.maximum(n, 1).astype(F32) / max_exact)
                         / math.log(T5_MAX_DIST / max_exact) * (half - max_exact)).astype(jnp.int32)
    large = jnp.minimum(large, half - 1)
    bucket = jnp.where(rel > 0, half, 0) + jnp.where(n < max_exact, n, large)
    accs = [jnp.zeros(rel.shape, F32) for _ in range(4)]
    for bkt in range(T5_BUCKETS):
        hit = bucket == bkt
        accs = [jnp.where(hit, tab_ref[bkt * 4 + hh], accs[hh]) for hh in range(4)]
    return [a * LOG2E for a in accs]


def _t5_bias_kernel(tab_ref, diag_ref, corner_ref, span_ref, *, t):
    step = pl.program_id(0)
    rows = 16
    c = T5_MAX_DIST

    @pl.when(step == 0)
    def _():
        top = lax.fori_loop(0, T5_BUCKETS * 4, lambda e, m: jnp.maximum(m, jnp.abs(tab_ref[e])), 0.0)
        span_ref[...] = jnp.full(span_ref.shape, 2.0 * LOG2E * top, F32)

        def body(r, carry):
            kk = lax.broadcasted_iota(jnp.int32, (rows, t), 0) + r * rows
            qq = lax.broadcasted_iota(jnp.int32, (rows, t), 1)
            for hh, b in enumerate(_t5_bias_of(kk - qq, tab_ref)):
                diag_ref[hh, pl.ds(pl.multiple_of(r * rows, rows), rows), :] = b
            return carry

        lax.fori_loop(0, t // rows, body, 0)

    @pl.when(step > 0)
    def _():
        base = jnp.where(step == 1, -c, c)

        def body(r, carry):
            kk = lax.broadcasted_iota(jnp.int32, (rows, c), 0) + r * rows
            qq = lax.broadcasted_iota(jnp.int32, (rows, c), 1)
            for hh, b in enumerate(_t5_bias_of(base + kk - qq, tab_ref)):
                corner_ref[0, hh, pl.ds(pl.multiple_of(r * rows, rows), rows), :] = b
            return carry

        lax.fori_loop(0, c // rows, body, 0)


def _t5_bias(t5_flat):
    t, c = FLASH_T, T5_MAX_DIST
    return pl.pallas_call(
        functools.partial(_t5_bias_kernel, t=t),
        grid=(3,),
        in_specs=[pl.BlockSpec(memory_space=pltpu.SMEM)],
        out_specs=[pl.BlockSpec((4, t, t), lambda d: (0, 0, 0)),
                   pl.BlockSpec((1, 4, c, c), lambda d: (jnp.maximum(d - 1, 0), 0, 0, 0)),
                   pl.BlockSpec((8, 128), lambda d: (0, 0))],
        out_shape=[jax.ShapeDtypeStruct((4, t, t), F32), jax.ShapeDtypeStruct((2, 4, c, c), F32),
                   jax.ShapeDtypeStruct((8, 128), F32)],
        compiler_params=pltpu.CompilerParams(dimension_semantics=("arbitrary",)),
        name="t5_bias",
    )(t5_flat)


def _na_window(kind, j, i, grid_rows):
    r0 = (0, NA_Q_ROWS, grid_rows - NA_Q_ROWS)[kind]
    start = (0, 0, grid_rows - NA_K_ROWS)[kind]
    krow, qrow = start + j, r0 + i
    lo = min(max(qrow - NA_KH // 2, 0), grid_rows - NA_KH)
    return lo <= krow < lo + NA_KH, krow - qrow + NA_KH - 1


def _na_bias_kernel(rpb_ref, o_ref, t_scr, *, grid_rows):
    hh = pl.program_id(0)
    n_dr, n_dc = 2 * NA_KH - 1, 2 * NA_KW - 1
    shape = (GRID_W, NA_Q_ROWS * GRID_W)
    lane = lax.broadcasted_iota(jnp.int32, shape, 1)
    kc = lax.broadcasted_iota(jnp.int32, shape, 0)
    qc = lane % GRID_W
    grp = lane // GRID_W
    dcm = jnp.clip(kc - qc + NA_KW - 1, 0, n_dc - 1)
    qs = jnp.clip(qc - NA_KW // 2, 0, GRID_W - NA_KW)
    col_ok = (kc >= qs) & (kc < qs + NA_KW)

    for dr in range(n_dr):
        base = (hh * n_dr + dr) * n_dc
        acc = jnp.zeros(shape, F32)
        for dc in range(n_dc):
            acc = jnp.where(dcm == dc, rpb_ref[base + dc], acc)
        t_scr[dr] = acc * LOG2E

    for kind in range(3):
        for j in range(NA_K_ROWS):
            wins = [_na_window(kind, j, i, grid_rows) for i in range(NA_Q_ROWS)]
            blk = jnp.full(shape, NEG_BIG, F32)
            for i, (inside, dr) in enumerate(wins):
                if inside:
                    blk = jnp.where((grp == i) & col_ok, t_scr[dr], blk)
            o_ref[kind, 0, j * GRID_W:(j + 1) * GRID_W, :] = blk


def _na_bias(rpb_flat, grid_rows):
    kt, qt = NA_K_ROWS * GRID_W, NA_Q_ROWS * GRID_W
    return pl.pallas_call(
        functools.partial(_na_bias_kernel, grid_rows=grid_rows),
        grid=(4,),
        in_specs=[pl.BlockSpec(memory_space=pltpu.SMEM)],
        out_specs=pl.BlockSpec((3, 1, kt, qt), lambda h: (0, h, 0, 0)),
        out_shape=jax.ShapeDtypeStruct((3, 4, kt, qt), F32),
        scratch_shapes=[pltpu.VMEM((2 * NA_KH - 1, GRID_W, qt), F32)],
        name="na_bias",
    )(rpb_flat)


def _na_kernel(q_ref, k0_ref, k1_ref, k2_ref, v0_ref, v1_ref, v2_ref, bias_ref, beta_ref, o_ref):
    qt = NA_Q_ROWS * GRID_W
    k_refs = (k0_ref, k1_ref, k2_ref)
    v_refs = (v0_ref, v1_ref, v2_ref)

    def logits(hh):
        q = q_ref[0, hh]
        return [_dot(k_refs[j][0, hh], q) + bias_ref[0, hh, j * qt:(j + 1) * qt, :] for j in range(3)]

    outs = []
    ss_next = logits(0)
    for hh in range(4):
        ss = ss_next
        if hh + 1 < 4:
            ss_next = logits(hh + 1)
        m = functools.reduce(jnp.maximum, [_col_max(s) for s in ss])
        acc = None
        for j in range(3):
            p = jnp.exp2(ss[j] - m).astype(BF16)
            pv = _dot(v_refs[j][0, hh, 0], p)
            acc = pv if acc is None else acc + pv
        outs.append(_finish_head(acc))
    y = jnp.concatenate(outs, axis=0)
    o_ref[0] = _rms_rows(y, beta_ref[...], GROUP_W).astype(BF16)


def _na(qt_arr, k, vt, bias, beta_col):
    b, nh, d, s = qt_arr.shape
    qt = NA_Q_ROWS * GRID_W
    kt = NA_K_ROWS * GRID_W
    n_t = s // qt
    n_win = kt // qt

    def win(j):
        return lambda bi, ti: jnp.clip(ti - 1, 0, n_t - n_win) + j

    def kind(bi, ti):
        return (jnp.where(ti == 0, 0, jnp.where(ti == n_t - 1, 2, 1)), 0, 0, 0)

    k_specs = [pl.BlockSpec((1, nh, qt, d), (lambda j: lambda bi, ti: (bi, 0, win(j)(bi, ti), 0))(j))
               for j in range(n_win)]
    per = vt.shape[4] // qt
    v_specs = [pl.BlockSpec((1, nh, 1, V_ROWS, qt),
                            (lambda j: lambda bi, ti: (bi, 0, win(j)(bi, ti) // per, 0, win(j)(bi, ti) % per))(j))
               for j in range(n_win)]
    return pl.pallas_call(
        _na_kernel,
        grid=(b, n_t),
        in_specs=[pl.BlockSpec((1, nh, d, qt), lambda bi, ti: (bi, 0, 0, ti))] + k_specs + v_specs + [
            pl.BlockSpec((1, nh, kt, qt), kind),
            pl.BlockSpec((GROUP_W, 1), lambda bi, ti: (0, 0)),
        ],
        out_specs=pl.BlockSpec((1, GROUP_W, qt), lambda bi, ti: (bi, 0, ti)),
        out_shape=jax.ShapeDtypeStruct((b, GROUP_W, s), BF16),
        compiler_params=pltpu.CompilerParams(
            dimension_semantics=("parallel", "parallel"), vmem_limit_bytes=VMEM_LIMIT),
        name="na_attn",
    )(qt_arr, k, k, k, vt, vt, vt, bias, beta_col)


def _rope_tables(pos, dim):
    inv = jnp.exp(-math.log(ROPE_THETA) * jnp.arange(0, dim, 2, dtype=F32) / dim)
    ang = pos.astype(F32)[:, None] * inv[None, :]
    return jnp.cos(ang).T, jnp.sin(ang).T


def _col(v):
    return v.reshape(-1, 1).astype(F32)


def _lambda_init(layer):
    return 0.8 - 0.6 * math.exp(-0.3 * layer)


def kernel(x, ffn1_norm, ffn1_w_gate, ffn1_w_up, ffn1_w_down, mix_norm, w_in, na_q_norm, na_k_norm, na_rpb, na_beta, mla_q_lat_norm, mla_w_uq, mla_kv_lat_norm, mla_w_ukv, mla_q_norm, mla_k_norm, mla_beta, diff_q_norm, diff_k_norm, diff_lambda, diff_subln, gqa_q_norm, gqa_k_norm, gqa_beta, w_out, ffn2_norm, ffn2_w_gate, ffn2_w_up, ffn2_w_down, final_norm, t5_bias):
    batch, seq, d = x.shape
    grid_rows = seq // GRID_W
    pos = jnp.arange(seq, dtype=jnp.int32)
    tabs = (*_rope_tables(pos, 32), *_rope_tables(pos // GRID_W, 32), *_rope_tables(pos % GRID_W, 32))
    t5_flat = t5_bias.reshape(-1).astype(F32)
    t5_tiles = _t5_bias(t5_flat)

    xf = x.reshape(batch * seq, d)
    for l in range(N_LAYERS):
        xf = _ffn(xf, ffn1_norm[l], ffn1_w_gate[l].astype(BF16), ffn1_w_up[l].astype(BF16),
                  ffn1_w_down[l].astype(BF16))
        cols = tuple(_col(v[l]) for v in (na_q_norm, na_k_norm, mla_q_lat_norm, mla_kv_lat_norm,
                                          mla_q_norm, mla_k_norm, diff_q_norm, diff_k_norm,
                                          gqa_q_norm, gqa_k_norm))
        mats = (mla_w_uq[l].T.astype(BF16), mla_w_ukv[l].T.astype(BF16))
        (qa, ka, va, qb, kb, vb, qc, kc, vc, qd, kd, vd) = _proj(
            xf, batch, seq, mix_norm[l], w_in[l].T.astype(BF16), cols, mats, tabs)
        n_k = seq // FLASH_T
        ka = jnp.swapaxes(ka, 2, 3)
        kb = jnp.swapaxes(kb, 2, 3).reshape(batch, 4, n_k, FLASH_T, 96)
        kc = jnp.swapaxes(kc.reshape(batch, 4, 64, seq), 2, 3).reshape(batch, 4, n_k, FLASH_T, 64)
        kd = jnp.swapaxes(kd, 2, 3).reshape(batch, 2, n_k, FLASH_T, 64)

        ya = _na(qa, ka, va, _na_bias(na_rpb[l].reshape(-1).astype(F32), grid_rows), _col(na_beta[l]))
        yb = _flash_plain(qb, kb, vb, _col(mla_beta[l]), (0, 1, 2, 3))
        yc = _flash_diff(t5_flat, qc, kc, vc, t5_tiles, diff_lambda[l].astype(F32), _col(diff_subln[l]),
                         _lambda_init(l))
        yd = _flash_plain(qd, kd, vd, _col(gqa_beta[l]), (0, 0, 1, 1))

        xf = _ffn(xf, ffn2_norm[l], ffn2_w_gate[l].astype(BF16), ffn2_w_up[l].astype(BF16),
                  ffn2_w_down[l].astype(BF16), mix=((ya, yb, yc, yd), w_out[l].astype(BF16)),
                  final_g=final_norm[l], seq=seq)
    return xf.reshape(batch, seq, d)
```
